```python
import jax, jax.numpy as jnp
from jax import lax
import numpy as np

D_MODEL = 1024
BATCH = 4
SEQ = 4096
DEPTH = 1

N_META = 16
BLOCK = 128
LEAD = BLOCK
FOX_HD = 64
FOX_HEADS = D_MODEL // FOX_HD
FOX_W = FOX_HEADS * FOX_HD
RET_HEADS = 4
RET_DK = D_MODEL // (2 * RET_HEADS)
RET_DV = 2 * RET_DK
RET_QK = RET_HEADS * RET_DK
RET_V = RET_HEADS * RET_DV
D_FF = ((8 * D_MODEL // 3 + 127) // 128) * 128
IN_SIZES = (FOX_W, FOX_W, FOX_W, FOX_HEADS, RET_QK, RET_QK, RET_V, RET_V, D_MODEL, D_MODEL)
N_IN = sum(IN_SIZES)
IN_SPLITS = [int(s) for s in np.cumsum(IN_SIZES)[:-1]]
EPS = 1e-6
GN_EPS = 1e-5
ROPE_BASE = 10000.0
FORGET_BIAS_INIT = 3.0
NEG = -1e30

kernel_name = "fox_retnet_macaron_hybrid"


def rmsnorm(x, g):
    xf = x.astype(jnp.float32)
    y = xf * lax.rsqrt(jnp.mean(xf * xf, axis=-1, keepdims=True) + EPS)
    return (y * g.astype(jnp.float32)).astype(x.dtype)


def swiglu(x, w_in, w_out):
    a, b = jnp.split(x @ w_in, 2, axis=-1)
    return (jax.nn.silu(a) * b) @ w_out


def rotary(x, pos):
    half = x.shape[-1] // 2
    inv = ROPE_BASE ** (-jnp.arange(half, dtype=jnp.float32) / half)
    ang = pos[:, None] * inv[None, :]
    cos, sin = jnp.cos(ang)[None, :, None, :], jnp.sin(ang)[None, :, None, :]
    xf = x.astype(jnp.float32)
    x1, x2 = xf[..., :half], xf[..., half:]
    return jnp.concatenate([x1 * cos - x2 * sin, x2 * cos + x1 * sin], axis=-1)


def forgetting_attention(q, k, v, logf, valid):
    B, H, P, hd = q.shape
    nb = P // BLOCK
    c = jnp.cumsum(logf, axis=-1)
    kpos = jnp.arange(P)
    scale = hd ** -0.5

    def block(i):
        s0 = i * BLOCK
        qb = lax.dynamic_slice_in_dim(q, s0, BLOCK, axis=2)
        cb = lax.dynamic_slice_in_dim(c, s0, BLOCK, axis=2)
        logits = jnp.einsum('bhqd,bhkd->bhqk', qb, k) * scale + cb[..., None] - c[:, :, None, :]
        qpos = s0 + jnp.arange(BLOCK)
        allowed = (kpos[None, :] <= qpos[:, None]) & (valid[None, :] | (kpos[None, :] == qpos[:, None]))
        p = jax.nn.softmax(jnp.where(allowed[None, None], logits, NEG), axis=-1)
        return jnp.einsum('bhqk,bhkd->bhqd', p, v)

    out = lax.map(block, jnp.arange(nb))
    return out.transpose(1, 2, 0, 3, 4).reshape(B, H, P, hd)


def retention(q, k, v, log_gamma):
    B, H, P, dk = q.shape
    dv = v.shape[-1]
    nc = P // BLOCK
    qc = q.reshape(B, H, nc, BLOCK, dk)
    kc = k.reshape(B, H, nc, BLOCK, dk)
    vc = v.reshape(B, H, nc, BLOCK, dv)
    n = jnp.arange(BLOCK, dtype=jnp.float32)
    diff = n[:, None] - n[None, :]
    lg = log_gamma[:, None, None]
    decay = jnp.where(diff >= 0, jnp.exp(lg * jnp.maximum(diff, 0.0)), 0.0)
    scores = jnp.einsum('bhcnd,bhcmd->bhcnm', qc, kc) * decay[None, :, None]
    out = jnp.einsum('bhcnm,bhcme->bhcne', scores, vc)
    zeta = jnp.exp(log_gamma[:, None] * (BLOCK - 1 - n)[None, :])
    kv = jnp.einsum('bhcmd,bhcme->cbhde', kc * zeta[None, :, None, :, None], vc)
    chunk_decay = jnp.exp(log_gamma * BLOCK)[None, :, None, None]

    def step(R, kv_c):
        return chunk_decay * R + kv_c, R

    _, r_prev = lax.scan(step, jnp.zeros((B, H, dk, dv), jnp.float32), kv)
    xi = jnp.exp(log_gamma[:, None] * (n + 1.0)[None, :])
    out = out + jnp.einsum('bhcnd,cbhde->bhcne', qc * xi[None, :, None, :, None], r_prev)
    return out.reshape(B, H, P, dv)


def head_groupnorm(y, g):
    mu = jnp.mean(y, axis=-1, keepdims=True)
    var = jnp.mean(jnp.square(y - mu), axis=-1, keepdims=True)
    yn = (y - mu) * lax.rsqrt(var + GN_EPS)
    B, H, P, dv = y.shape
    return yn.transpose(0, 2, 1, 3).reshape(B, P, H * dv) * g.astype(jnp.float32)


def hybrid_layer(h, valid, pos, log_gamma, norm_ffn1, w_ffn1_in, w_ffn1_out, norm_mix, w_in,
                 b_forget, b_gate, fox_q_norm, fox_k_norm, w_o_fox, ret_gn, w_o_ret, w_out,
                 norm_ffn2, w_ffn2_in, w_ffn2_out):
    dt = h.dtype
    B, P, _ = h.shape
    h = h + 0.5 * swiglu(rmsnorm(h, norm_ffn1), w_ffn1_in, w_ffn1_out)
    u = rmsnorm(h, norm_mix)
    fq, fk, fv, ff, rq, rk, rv, rg, ga, gb = jnp.split(u @ w_in, IN_SPLITS, axis=-1)
    vmask = valid[None, :, None, None].astype(jnp.float32)

    def fox_heads(t):
        return t.reshape(B, P, FOX_HEADS, FOX_HD)
    q_a = rmsnorm(fox_heads(fq), fox_q_norm).astype(jnp.float32).transpose(0, 2, 1, 3)
    k_a = rmsnorm(fox_heads(fk), fox_k_norm).astype(jnp.float32).transpose(0, 2, 1, 3)
    v_a = fox_heads(fv).astype(jnp.float32).transpose(0, 2, 1, 3)
    logf = jax.nn.log_sigmoid(ff.astype(jnp.float32) + b_forget.astype(jnp.float32))
    logf = jnp.where(valid[None, :, None], logf, 0.0).transpose(0, 2, 1)
    y_a = forgetting_attention(q_a, k_a, v_a, logf, valid)
    y_a = y_a.transpose(0, 2, 1, 3).reshape(B, P, FOX_W).astype(dt)

    q_b = rotary(rq.reshape(B, P, RET_HEADS, RET_DK), pos).transpose(0, 2, 1, 3)
    k_b = (rotary(rk.reshape(B, P, RET_HEADS, RET_DK), pos) * (RET_DK ** -0.5) * vmask).transpose(0, 2, 1, 3)
    v_b = (rv.reshape(B, P, RET_HEADS, RET_DV).astype(jnp.float32) * vmask).transpose(0, 2, 1, 3)
    y_b = head_groupnorm(retention(q_b, k_b, v_b, log_gamma), ret_gn)
    y_b = (jax.nn.silu(rg.astype(jnp.float32)) * y_b).astype(dt)

    g_a = jax.nn.sigmoid(ga + b_gate[:D_MODEL])
    g_b = jax.nn.sigmoid(gb + b_gate[D_MODEL:])
    mixed = g_a * (y_a @ w_o_fox) + g_b * (y_b @ w_o_ret)
    h = h + mixed @ w_out

    h = h + 0.5 * swiglu(rmsnorm(h, norm_ffn2), w_ffn2_in, w_ffn2_out)
    return h


def setup_inputs(seed: int = 0) -> dict:
    key = jax.random.key(seed)
    ks = jax.random.split(key, 20)
    f32 = jnp.float32

    def w(k, shape, fan_in):
        return jax.random.normal(k, shape, f32) * fan_in ** -0.5

    def gain(k, shape):
        return 1.0 + 0.01 * jax.random.normal(k, shape, f32)

    L = DEPTH
    return {
        "x": jax.random.normal(ks[0], (BATCH, SEQ, D_MODEL), f32),
        "meta_tokens": jax.random.normal(ks[1], (N_META, D_MODEL), f32),
        "norm_ffn1": gain(ks[2], (L, D_MODEL)),
        "w_ffn1_in": w(ks[3], (L, D_MODEL, 2 * D_FF), D_MODEL),
        "w_ffn1_out": w(ks[4], (L, D_FF, D_MODEL), D_FF),
        "norm_mix": gain(ks[5], (L, D_MODEL)),
        "w_in": w(ks[6], (L, D_MODEL, N_IN), D_MODEL),
        "b_forget": FORGET_BIAS_INIT + 0.1 * jax.random.normal(ks[7], (L, FOX_HEADS), f32),
        "b_gate": 0.01 * jax.random.normal(ks[8], (L, 2 * D_MODEL), f32),
        "fox_q_norm": gain(ks[9], (L, FOX_HD)),
        "fox_k_norm": gain(ks[10], (L, FOX_HD)),
        "w_o_fox": w(ks[11], (L, FOX_W, D_MODEL), FOX_W),
        "ret_gn": gain(ks[12], (L, RET_V)),
        "w_o_ret": w(ks[13], (L, RET_V, D_MODEL), RET_V),
        "w_out": w(ks[14], (L, D_MODEL, D_MODEL), D_MODEL),
        "norm_ffn2": gain(ks[15], (L, D_MODEL)),
        "w_ffn2_in": w(ks[16], (L, D_MODEL, 2 * D_FF), D_MODEL),
        "w_ffn2_out": w(ks[17], (L, D_FF, D_MODEL), D_FF),
    }


def reference(x, meta_tokens, norm_ffn1, w_ffn1_in, w_ffn1_out, norm_mix, w_in, b_forget, b_gate,
              fox_q_norm, fox_k_norm, w_o_fox, ret_gn, w_o_ret, w_out, norm_ffn2, w_ffn2_in,
              w_ffn2_out):
    B, S, D = x.shape
    n_empty = LEAD - N_META
    h = jnp.concatenate([
        jnp.zeros((B, n_empty, D), x.dtype),
        jnp.broadcast_to(meta_tokens.astype(x.dtype)[None], (B, N_META, D)),
        x], axis=1)
    P = h.shape[1]
    idx = jnp.arange(P)
    valid = idx >= n_empty
    pos = (idx - n_empty).astype(jnp.float32)
    log_gamma = jnp.log1p(-jnp.exp2(-5.0 - jnp.arange(RET_HEADS, dtype=jnp.float32)))
    for l in range(DEPTH):
        h = hybrid_layer(h, valid, pos, log_gamma, norm_ffn1[l], w_ffn1_in[l], w_ffn1_out[l],
                         norm_mix[l], w_in[l], b_forget[l], b_gate[l], fox_q_norm[l], fox_k_norm[l],
                         w_o_fox[l], ret_gn[l], w_o_ret[l], w_out[l], norm_ffn2[l], w_ffn2_in[l],
                         w_ffn2_out[l])
    return h[:, LEAD:]
```

```python
import functools
import math

import jax
import jax.numpy as jnp
import numpy as np
from jax import lax
from jax.experimental import pallas as pl
from jax.experimental.pallas import tpu as pltpu

F32 = jnp.float32
BF16 = jnp.bfloat16

D_MODEL = 1024
N_META = 16
BLOCK = 128
N_EMPTY = BLOCK - N_META
FOX_HD = 64
FOX_HEADS = D_MODEL // FOX_HD
FOX_PAIRS = FOX_HEADS // 2
RET_HEADS = 4
RET_DK = D_MODEL // (2 * RET_HEADS)
RET_DV = 2 * RET_DK
RET_QK = RET_HEADS * RET_DK
D_FF = ((8 * D_MODEL // 3 + 127) // 128) * 128
EPS = 1e-6
GN_EPS = 1e-5
ROPE_BASE = 10000.0
NEG = -1e30
LOG2E = math.log2(math.e)

V7X_LANES = 128
V7X_MXU_DIM = 256
V7X_VMEM_BYTES = 64 * 2**20
VMEM_LIMIT = V7X_VMEM_BYTES - 8 * 2**20

ROW_TILE = 512
FF_CHUNK = V7X_MXU_DIM
N_FF_CHUNKS = D_FF // FF_CHUNK
ATT_BLOCK = 512


def _dot(a, b):
    return jnp.dot(a, b, preferred_element_type=F32)


def _dot_nt(a, b):
    return lax.dot_general(a, b, (((1,), (1,)), ((), ())), preferred_element_type=F32)


def _dot_tn(a, b):
    return lax.dot_general(a, b, (((0,), (0,)), ((), ())), preferred_element_type=F32)


def _const_spec(shape):
    zeros = (0,) * len(shape)
    return pl.BlockSpec(shape, lambda *_: zeros, pipeline_mode=pl.Buffered(1))


def _params(semantics):
    return pltpu.CompilerParams(dimension_semantics=semantics, vmem_limit_bytes=VMEM_LIMIT)


def _rms_rows(x, g):
    ms = jnp.mean(x * x, axis=-1, keepdims=True)
    return x * lax.rsqrt(ms + EPS) * g


def _swiglu(xn, win_ref, wout_ref):
    acc = None
    for c in range(N_FF_CHUNKS):
        a = _dot(xn, win_ref[0, c])
        b = _dot(xn, win_ref[1, c])
        hm = (a * jax.nn.sigmoid(a) * b).astype(BF16)
        d = _dot(hm, wout_ref[c])
        acc = d if acc is None else acc + d
    return acc


def _ffn_body(x_ref, g_ref, win_ref, wout_ref, o_ref):
    x = x_ref[...]
    xn = _rms_rows(x, g_ref[...]).astype(BF16)
    o_ref[...] = x + 0.5 * _swiglu(xn, win_ref, wout_ref)


def _ffn(x, g, win, wout, tm):
    rows = x.shape[0]
    row_spec = pl.BlockSpec((tm, D_MODEL), lambda i: (i, 0))
    return pl.pallas_call(
        _ffn_body,
        grid=(rows // tm,),
        in_specs=[row_spec, _const_spec(g.shape), _const_spec(win.shape), _const_spec(wout.shape)],
        out_specs=row_spec,
        out_shape=jax.ShapeDtypeStruct((rows, D_MODEL), F32),
        compiler_params=_params(("arbitrary",)),
        name="ffn",
    )(x, g, win, wout)


def _group_mean_sq(x, gmat_ref):
    sq = (x * x).astype(BF16)
    w = V7X_MXU_DIM
    parts = [_dot(sq[:, i * w:(i + 1) * w], gmat_ref[...]) for i in range(D_MODEL // w)]
    return jnp.concatenate(parts, axis=1)


def _rotary_tile(x, cos, sin_signed):
    return x * cos + pltpu.roll(x, RET_DK // 2, axis=1) * sin_signed


def _proj_body(h_ref, gmix_ref, wf_ref, wff_ref, wr_ref, wg_ref, gmat_ref, gq_ref, gk_ref,
               bf_ref, bg_ref, cos_ref, sin_ref, tri_ref,
               q_ref, k_ref, v_ref, c_ref, rq_ref, rk_ref, rv_ref, sg_ref, ga_ref, gb_ref,
               carry_ref, *, lead, tiles_per_seq):
    tm = h_ref.shape[0]
    u = _rms_rows(h_ref[...], gmix_ref[...]).astype(BF16)
    if lead:
        row = lax.broadcasted_iota(jnp.int32, (tm, 1), 0)
        valid = row >= N_EMPTY
        vmask = valid.astype(F32)

    fq = _dot(u, wf_ref[0])
    qn = fq * lax.rsqrt(_group_mean_sq(fq, gmat_ref) + EPS) * gq_ref[...]
    q_ref[...] = (qn * (FOX_HD ** -0.5 * LOG2E)).astype(BF16)
    fk = _dot(u, wf_ref[1])
    kn = fk * lax.rsqrt(_group_mean_sq(fk, gmat_ref) + EPS) * gk_ref[...]
    k_ref[...] = kn.astype(BF16)
    v_ref[...] = _dot(u, wf_ref[2]).astype(BF16)

    z = _dot(u, wff_ref[...]) + bf_ref[...]
    logf = -(jnp.maximum(-z, 0.0) + jnp.log1p(jnp.exp(-jnp.abs(z))))
    if lead:
        logf = jnp.where(valid, logf, 0.0)
    p1 = logf.astype(BF16)
    r1 = logf - p1.astype(F32)
    p2 = r1.astype(BF16)
    p3 = (r1 - p2.astype(F32)).astype(BF16)
    tri = tri_ref[...]
    cum = _dot(tri, p1) + _dot(tri, p2) + _dot(tri, p3)
    if lead:
        c = -cum
    else:
        @pl.when(pl.program_id(0) % tiles_per_seq == 0)
        def _():
            carry_ref[...] = jnp.zeros_like(carry_ref)
        c = cum + carry_ref[...]
        carry_ref[...] = c[tm - 1:tm, :]
    c_ref[...] = (c * LOG2E)[:, :FOX_HEADS]

    cos = cos_ref[...]
    sin = sin_ref[...]
    rq = _dot(u, wr_ref[:, 0:RET_QK])
    rk = _dot(u, wr_ref[:, RET_QK:2 * RET_QK])
    for hh in range(RET_HEADS):
        sl = slice(hh * RET_DK, (hh + 1) * RET_DK)
        rq_ref[:, sl] = _rotary_tile(rq[:, sl], cos, sin).astype(BF16)
        kt = _rotary_tile(rk[:, sl], cos, sin) * (RET_DK ** -0.5)
        if lead:
            kt = kt * vmask
        rk_ref[:, sl] = kt.astype(BF16)
    rv = _dot(u, wr_ref[:, 2 * RET_QK:2 * RET_QK + D_MODEL])
    if lead:
        rv = rv * vmask
    rv_ref[...] = rv.astype(BF16)
    rg = _dot(u, wr_ref[:, 2 * RET_QK + D_MODEL:])
    sg_ref[...] = (rg * jax.nn.sigmoid(rg)).astype(BF16)

    bg = bg_ref[...]
    ga_ref[...] = jax.nn.sigmoid(_dot(u, wg_ref[:, :D_MODEL]) + bg[:, :D_MODEL]).astype(BF16)
    gb_ref[...] = jax.nn.sigmoid(_dot(u, wg_ref[:, D_MODEL:]) + bg[:, D_MODEL:]).astype(BF16)


def _proj(h, consts, cos, sin, tri, tm, lead, tiles_per_seq):
    rows = h.shape[0]
    grid = rows // tm

    def rows_spec(width):
        return pl.BlockSpec((tm, width), lambda i: (i, 0))

    pos_spec = pl.BlockSpec((tm, RET_DK), lambda i: (i % tiles_per_seq, 0))
    out_widths = [D_MODEL, D_MODEL, D_MODEL, FOX_HEADS, RET_QK, RET_QK, D_MODEL, D_MODEL,
                  D_MODEL, D_MODEL]
    out_dtypes = [BF16, BF16, BF16, F32, BF16, BF16, BF16, BF16, BF16, BF16]
    return pl.pallas_call(
        functools.partial(_proj_body, lead=lead, tiles_per_seq=tiles_per_seq),
        grid=(grid,),
        in_specs=[rows_spec(D_MODEL)] + [_const_spec(a.shape) for a in consts]
        + [pos_spec, pos_spec, _const_spec(tri.shape)],
        out_specs=[rows_spec(w) for w in out_widths],
        out_shape=[jax.ShapeDtypeStruct((rows, w), dt) for w, dt in zip(out_widths, out_dtypes)],
        scratch_shapes=[pltpu.VMEM((1, V7X_LANES), F32)],
        compiler_params=_params(("arbitrary",)),
        name="proj_lead" if lead else "proj",
    )(h, *consts, cos, sin, tri)


def _fox_body(q_ref, k_ref, v_ref, ck_ref, kl_ref, vl_ref, cl_ref, o_ref, m_ref, l_ref, acc_ref):
    tq = q_ref.shape[0]
    i = pl.program_id(2)
    lane = lax.broadcasted_iota(jnp.int32, (1, V7X_LANES), 1)
    first = lane < FOX_HD
    q = q_ref[...]
    zero = jnp.zeros_like(q)
    q_heads = (jnp.where(first, q, zero), jnp.where(first, zero, q))

    m_ref[...] = jnp.full_like(m_ref, NEG)
    l_ref[...] = jnp.zeros_like(l_ref)
    acc_ref[...] = jnp.zeros_like(acc_ref)

    def block(k, v, ck, allowed):
        pvs, alphas = [], []
        for hh in range(2):
            s = _dot_nt(q_heads[hh], k) - ck[hh:hh + 1, :]
            if allowed is not None:
                s = jnp.where(allowed, s, NEG)
            m_old = m_ref[hh]
            m_new = jnp.maximum(m_old, jnp.max(s, axis=1, keepdims=True))
            p = jnp.exp2(s - m_new)
            alpha = jnp.exp2(m_old - m_new)
            l_ref[hh] = alpha * l_ref[hh] + jnp.sum(p, axis=1, keepdims=True)
            m_ref[hh] = m_new
            pvs.append(_dot(p.astype(BF16), v))
            alphas.append(alpha)
        acc_ref[...] = (acc_ref[...] * jnp.where(first, alphas[0], alphas[1])
                        + jnp.where(first, pvs[0], pvs[1]))

    lead_col = lax.broadcasted_iota(jnp.int32, (tq, BLOCK), 1)
    block(kl_ref[...], vl_ref[...], cl_ref[...], lead_col >= N_EMPTY)

    def full_block(j, carry):
        ks = pl.multiple_of(j * tq, tq)
        block(k_ref[pl.ds(ks, tq), :], v_ref[pl.ds(ks, tq), :], ck_ref[:, pl.ds(ks, tq)], None)
        return carry

    lax.fori_loop(0, i, full_block, 0)

    ks = pl.multiple_of(i * tq, tq)
    r = lax.broadcasted_iota(jnp.int32, (tq, tq), 0)
    c = lax.broadcasted_iota(jnp.int32, (tq, tq), 1)
    block(k_ref[pl.ds(ks, tq), :], v_ref[pl.ds(ks, tq), :], ck_ref[:, pl.ds(ks, tq)], c <= r)

    inv = jnp.where(first, 1.0 / l_ref[0], 1.0 / l_ref[1])
    o_ref[...] = (acc_ref[...] * inv).astype(o_ref.dtype)


def _fox(q, k, v, ck, kl, vl, cl, batch, seq):
    tq = ATT_BLOCK
    w = V7X_LANES
    return pl.pallas_call(
        _fox_body,
        grid=(batch, FOX_PAIRS, seq // tq),
        in_specs=[
            pl.BlockSpec((None, tq, w), lambda b, j, i: (b, i, j)),
            pl.BlockSpec((None, seq, w), lambda b, j, i: (b, 0, j)),
            pl.BlockSpec((None, seq, w), lambda b, j, i: (b, 0, j)),
            pl.BlockSpec((None, None, 2, seq), lambda b, j, i: (b, j, 0, 0)),
            pl.BlockSpec((BLOCK, w), lambda b, j, i: (0, j)),
            pl.BlockSpec((BLOCK, w), lambda b, j, i: (0, j)),
            pl.BlockSpec((None, 2, BLOCK), lambda b, j, i: (j, 0, 0)),
        ],
        out_specs=pl.BlockSpec((None, tq, w), lambda b, j, i: (b, i, j)),
        out_shape=jax.ShapeDtypeStruct((batch, seq, D_MODEL), BF16),
        scratch_shapes=[pltpu.VMEM((2, tq, 1), F32), pltpu.VMEM((2, tq, 1), F32),
                        pltpu.VMEM((tq, w), F32)],
        compiler_params=_params(("arbitrary", "arbitrary", "arbitrary")),
        name="fox",
    )(q, k, v, ck, kl, vl, cl)


def _ret_body(q_ref, k_ref, v_ref, sg_ref, kl_ref, vl_ref, gn_ref, decay_ref, zeta_ref, xi_ref,
              cd_ref, o_ref, state_ref):
    n_chunks = q_ref.shape[0] // BLOCK
    decay = decay_ref[...]
    zeta = zeta_ref[...]
    xi = xi_ref[...]
    cd = cd_ref[...]
    gn = gn_ref[...]

    state_ref[...] = _dot_tn((kl_ref[...].astype(F32) * zeta).astype(BF16), vl_ref[...])

    def chunk(c, carry):
        rs = pl.multiple_of(c * BLOCK, BLOCK)
        q = q_ref[pl.ds(rs, BLOCK), :]
        k = k_ref[pl.ds(rs, BLOCK), :]
        v = v_ref[pl.ds(rs, BLOCK), :]
        state = state_ref[...]
        scores = (_dot_nt(q, k) * decay).astype(BF16)
        o = _dot(scores, v) + _dot((q.astype(F32) * xi).astype(BF16), state.astype(BF16))
        kv = _dot_tn((k.astype(F32) * zeta).astype(BF16), v)
        state_ref[...] = cd * state + kv
        mu = jnp.mean(o, axis=-1, keepdims=True)
        d = o - mu
        var = jnp.mean(d * d, axis=-1, keepdims=True)
        yn = d * lax.rsqrt(var + GN_EPS) * gn
        o_ref[pl.ds(rs, BLOCK), :] = (sg_ref[pl.ds(rs, BLOCK), :].astype(F32) * yn).astype(o_ref.dtype)
        return carry

    lax.fori_loop(0, n_chunks, chunk, 0)


def _ret(rq, rk, rv, sg, rkl, rvl, gn, tables, batch, seq):
    decay, zeta, xi, cd = tables
    head_qk = pl.BlockSpec((None, seq, RET_DK), lambda b, h: (b, 0, h))
    head_v = pl.BlockSpec((None, seq, RET_DV), lambda b, h: (b, 0, h))
    table = pl.BlockSpec((None, BLOCK, BLOCK), lambda b, h: (h, 0, 0))
    return pl.pallas_call(
        _ret_body,
        grid=(batch, RET_HEADS),
        in_specs=[head_qk, head_qk, head_v, head_v,
                  pl.BlockSpec((BLOCK, RET_DK), lambda b, h: (0, h)),
                  pl.BlockSpec((BLOCK, RET_DV), lambda b, h: (0, h)),
                  pl.BlockSpec((1, RET_DV), lambda b, h: (0, h)),
                  table, table, table,
                  pl.BlockSpec((None, 1, RET_DV), lambda b, h: (h, 0, 0))],
        out_specs=head_v,
        out_shape=jax.ShapeDtypeStruct((batch, seq, D_MODEL), BF16),
        scratch_shapes=[pltpu.VMEM((RET_DK, RET_DV), F32)],
        compiler_params=_params(("arbitrary", "arbitrary")),
        name="ret",
    )(rq, rk, rv, sg, rkl, rvl, gn, decay, zeta, xi, cd)


def _out_body(ya_ref, yb_ref, ga_ref, gb_ref, h_ref, wa_ref, wb_ref, wo_ref, g_ref, win_ref,
              wout_ref, o_ref):
    mixed = (ga_ref[...].astype(F32) * _dot(ya_ref[...], wa_ref[...])
             + gb_ref[...].astype(F32) * _dot(yb_ref[...], wb_ref[...]))
    h = h_ref[...] + _dot(mixed.astype(BF16), wo_ref[...])
    xn = _rms_rows(h, g_ref[...]).astype(BF16)
    o_ref[...] = h + 0.5 * _swiglu(xn, win_ref, wout_ref)


def _out(ya, yb, ga, gb, h, consts, tm):
    rows = h.shape[0]
    row_spec = pl.BlockSpec((tm, D_MODEL), lambda i: (i, 0))
    return pl.pallas_call(
        _out_body,
        grid=(rows // tm,),
        in_specs=[row_spec] * 5 + [_const_spec(a.shape) for a in consts],
        out_specs=row_spec,
        out_shape=jax.ShapeDtypeStruct((rows, D_MODEL), F32),
        compiler_params=_params(("arbitrary",)),
        name="out",
    )(ya, yb, ga, gb, h, *consts)


def _ffn_weights(w_in, w_out):
    win = w_in.astype(BF16).reshape(D_MODEL, 2, N_FF_CHUNKS, FF_CHUNK).transpose(1, 2, 0, 3)
    wout = w_out.astype(BF16).reshape(N_FF_CHUNKS, FF_CHUNK, D_MODEL)
    return win, wout


def _position_tables(seq):
    half = RET_DK // 2
    pos = jnp.arange(BLOCK + seq, dtype=F32) - N_EMPTY
    inv = ROPE_BASE ** (-jnp.arange(half, dtype=F32) / half)
    ang = pos[:, None] * inv[None, :]
    cos = jnp.concatenate([jnp.cos(ang), jnp.cos(ang)], axis=1)
    sin = jnp.concatenate([-jnp.sin(ang), jnp.sin(ang)], axis=1)
    return cos, sin


def _retention_tables():
    log_gamma = jnp.log1p(-jnp.exp2(-5.0 - jnp.arange(RET_HEADS, dtype=F32)))
    n = jnp.arange(BLOCK, dtype=F32)
    diff = n[:, None] - n[None, :]
    lg = log_gamma[:, None, None]
    decay = jnp.where(diff >= 0, jnp.exp(lg * jnp.maximum(diff, 0.0)), 0.0)
    zeta = jnp.exp(log_gamma[:, None] * (BLOCK - 1 - n)[None, :])
    xi = jnp.exp(log_gamma[:, None] * (n + 1.0)[None, :])
    cd = jnp.exp(log_gamma * BLOCK)
    lanes = (RET_HEADS, BLOCK, RET_DK)
    return (decay, jnp.broadcast_to(zeta[:, :, None], lanes), jnp.broadcast_to(xi[:, :, None], lanes),
            jnp.broadcast_to(cd[:, None, None], (RET_HEADS, 1, RET_DV)))


def kernel(x, meta_tokens, norm_ffn1, w_ffn1_in, w_ffn1_out, norm_mix, w_in, b_forget, b_gate,
           fox_q_norm, fox_k_norm, w_o_fox, ret_gn, w_o_ret, w_out, norm_ffn2, w_ffn2_in,
           w_ffn2_out):
    batch, seq, d = x.shape
    assert d == D_MODEL and seq % ATT_BLOCK == 0 and seq % ROW_TILE == 0
    assert norm_ffn1.shape[0] == 1, "one layer"
    tiles_per_seq = seq // ROW_TILE

    win1, wout1 = _ffn_weights(w_ffn1_in[0], w_ffn1_out[0])
    win2, wout2 = _ffn_weights(w_ffn2_in[0], w_ffn2_out[0])
    wi = w_in[0].astype(BF16)
    o_ff = 3 * D_MODEL
    o_r = o_ff + FOX_HEADS
    o_g = o_r + 2 * RET_QK + 2 * D_MODEL
    wf = wi[:, :o_ff].reshape(D_MODEL, 3, D_MODEL).transpose(1, 0, 2)
    wff = jnp.pad(wi[:, o_ff:o_r], ((0, 0), (0, V7X_LANES - FOX_HEADS)))
    wr = wi[:, o_r:o_g]
    wg = wi[:, o_g:]
    grp = np.arange(V7X_MXU_DIM) // FOX_HD
    gmat = jnp.asarray((grp[:, None] == grp[None, :]) / FOX_HD, dtype=BF16)
    gq = jnp.tile(fox_q_norm[0], FOX_HEADS)[None, :]
    gk = jnp.tile(fox_k_norm[0], FOX_HEADS)[None, :]
    bf = jnp.pad(b_forget[0], (0, V7X_LANES - FOX_HEADS))[None, :]
    bg = b_gate[0][None, :]
    proj_consts = (norm_mix, wf, wff, wr, wg, gmat, gq, gk, bf, bg)
    cos, sin = _position_tables(seq)
    idx = np.arange(ROW_TILE)
    tri_incl = jnp.asarray(idx[None, :] <= idx[:, None], dtype=BF16)
    idx = np.arange(BLOCK)
    tri_after = jnp.asarray(idx[None, :] > idx[:, None], dtype=BF16)

    lead = jnp.concatenate([jnp.zeros((N_EMPTY, d), x.dtype), meta_tokens.astype(x.dtype)], axis=0)
    xr = x.reshape(batch * seq, d)

    h1 = _ffn(xr, norm_ffn1, win1, wout1, ROW_TILE)
    h1l = _ffn(lead, norm_ffn1, win1, wout1, BLOCK)

    q, k, v, c, rq, rk, rv, sg, ga, gb = _proj(
        h1, proj_consts, cos[BLOCK:], sin[BLOCK:], tri_incl, ROW_TILE, False, tiles_per_seq)
    _, kl, vl, cl, _, rkl, rvl, _, _, _ = _proj(
        h1l, proj_consts, cos[:BLOCK], sin[:BLOCK], tri_after, BLOCK, True, 1)

    def b3(a):
        return a.reshape(batch, seq, a.shape[-1])

    ck = b3(c).transpose(0, 2, 1).reshape(batch, FOX_PAIRS, 2, seq)
    clt = cl.T.reshape(FOX_PAIRS, 2, BLOCK)
    ya = _fox(b3(q), b3(k), b3(v), ck, kl, vl, clt, batch, seq)
    yb = _ret(b3(rq), b3(rk), b3(rv), b3(sg), rkl, rvl, ret_gn, _retention_tables(), batch, seq)

    out_consts = (w_o_fox[0].astype(BF16), w_o_ret[0].astype(BF16), w_out[0].astype(BF16),
                  norm_ffn2, win2, wout2)
    out = _out(ya.reshape(batch * seq, d), yb.reshape(batch * seq, d), ga, gb, h1, out_consts,
               ROW_TILE)
    return out.reshape(batch, seq, d)
```

```python
import functools
import math

import jax
import jax.numpy as jnp
import numpy as np
from jax import lax
from jax.experimental import pallas as pl
from jax.experimental.pallas import tpu as pltpu

F32 = jnp.float32
BF16 = jnp.bfloat16

D_MODEL = 1024
N_META = 16
BLOCK = 128
N_EMPTY = BLOCK - N_META
FOX_HD = 64
FOX_HEADS = D_MODEL // FOX_HD
FOX_PAIRS = FOX_HEADS // 2
RET_HEADS = 4
RET_DK = D_MODEL // (2 * RET_HEADS)
RET_DV = 2 * RET_DK
RET_QK = RET_HEADS * RET_DK
D_FF = ((8 * D_MODEL // 3 + 127) // 128) * 128
EPS = 1e-6
GN_EPS = 1e-5
ROPE_BASE = 10000.0
NEG = -1e30
LOG2E = math.log2(math.e)
SAFE_LOGIT_BOUND = 30.0

V7X_LANES = 128
V7X_MXU_DIM = 256
V7X_VMEM_BYTES = 64 * 2**20
VMEM_LIMIT = V7X_VMEM_BYTES - 8 * 2**20

ROW_TILE = 512
FF_CHUNK = V7X_MXU_DIM
N_FF_CHUNKS = D_FF // FF_CHUNK
ATT_BLOCK = 512


def _dot(a, b):
    return jnp.dot(a, b, preferred_element_type=F32)


def _dot_nt(a, b):
    return lax.dot_general(a, b, (((1,), (1,)), ((), ())), preferred_element_type=F32)


def _dot_tn(a, b):
    return lax.dot_general(a, b, (((0,), (0,)), ((), ())), preferred_element_type=F32)


def _const_spec(shape):
    zeros = (0,) * len(shape)
    return pl.BlockSpec(shape, lambda *_: zeros, pipeline_mode=pl.Buffered(1))


def _params(semantics):
    return pltpu.CompilerParams(dimension_semantics=semantics, vmem_limit_bytes=VMEM_LIMIT)


def _rms_rows(x, g):
    ms = jnp.mean(x * x, axis=-1, keepdims=True)
    return x * lax.rsqrt(ms + EPS) * g


def _swiglu(xn, win_ref, wout_ref):
    acc = None
    for c in range(N_FF_CHUNKS):
        a = _dot(xn, win_ref[0, c])
        b = _dot(xn, win_ref[1, c])
        hm = (a * jax.nn.sigmoid(a) * b).astype(BF16)
        d = _dot(hm, wout_ref[c])
        acc = d if acc is None else acc + d
    return acc


def _ffn_body(x_ref, g_ref, win_ref, wout_ref, o_ref):
    x = x_ref[...]
    xn = _rms_rows(x, g_ref[...]).astype(BF16)
    o_ref[...] = x + 0.5 * _swiglu(xn, win_ref, wout_ref)


def _ffn(x, g, win, wout, tm):
    rows = x.shape[0]
    row_spec = pl.BlockSpec((tm, D_MODEL), lambda i: (i, 0))
    return pl.pallas_call(
        _ffn_body,
        grid=(rows // tm,),
        in_specs=[row_spec, _const_spec(g.shape), _const_spec(win.shape), _const_spec(wout.shape)],
        out_specs=row_spec,
        out_shape=jax.ShapeDtypeStruct((rows, D_MODEL), F32),
        compiler_params=_params(("arbitrary",)),
        name="ffn",
    )(x, g, win, wout)


def _group_mean_sq(x, gmat_ref):
    sq = (x * x).astype(BF16)
    w = V7X_MXU_DIM
    parts = [_dot(sq[:, i * w:(i + 1) * w], gmat_ref[...]) for i in range(D_MODEL // w)]
    return jnp.concatenate(parts, axis=1)


def _rotary_tile(x, cos, sin_signed):
    return x * cos + pltpu.roll(x, RET_DK // 2, axis=1) * sin_signed


def _proj_body(h_ref, gmix_ref, wf_ref, wff_ref, wr_ref, wg_ref, gmat_ref, gq_ref, gk_ref,
               bf_ref, bg_ref, cos_ref, sin_ref, tri_ref,
               q_ref, k_ref, v_ref, c_ref, rq_ref, rk_ref, rv_ref, sg_ref, ga_ref, gb_ref,
               carry_ref, *, lead, tiles_per_seq):
    tm = h_ref.shape[0]
    u = _rms_rows(h_ref[...], gmix_ref[...]).astype(BF16)
    if lead:
        row = lax.broadcasted_iota(jnp.int32, (tm, 1), 0)
        valid = row >= N_EMPTY
        vmask = valid.astype(F32)

    fq = _dot(u, wf_ref[0])
    qn = fq * lax.rsqrt(_group_mean_sq(fq, gmat_ref) + EPS) * gq_ref[...]
    q_ref[...] = (qn * (FOX_HD ** -0.5 * LOG2E)).astype(BF16)
    fk = _dot(u, wf_ref[1])
    kn = fk * lax.rsqrt(_group_mean_sq(fk, gmat_ref) + EPS) * gk_ref[...]
    k_ref[...] = kn.astype(BF16)
    fv = _dot(u, wf_ref[2]).astype(BF16)
    ones = jnp.ones((tm, V7X_LANES), BF16)
    for j in range(FOX_PAIRS):
        v_ref[:, 2 * j * V7X_LANES:(2 * j + 1) * V7X_LANES] = fv[:, j * V7X_LANES:(j + 1) * V7X_LANES]
        v_ref[:, (2 * j + 1) * V7X_LANES:(2 * j + 2) * V7X_LANES] = ones

    z = _dot(u, wff_ref[...]) + bf_ref[...]
    logf = -(jnp.maximum(-z, 0.0) + jnp.log1p(jnp.exp(-jnp.abs(z))))
    if lead:
        logf = jnp.where(valid, logf, 0.0)
    p1 = logf.astype(BF16)
    r1 = logf - p1.astype(F32)
    p2 = r1.astype(BF16)
    p3 = (r1 - p2.astype(F32)).astype(BF16)
    tri = tri_ref[...]
    cum = _dot(tri, p1) + _dot(tri, p2) + _dot(tri, p3)
    if lead:
        c = -cum
    else:
        @pl.when(pl.program_id(0) % tiles_per_seq == 0)
        def _():
            carry_ref[...] = jnp.zeros_like(carry_ref)
        c = cum + carry_ref[...]
        carry_ref[...] = c[tm - 1:tm, :]
    c_ref[...] = (c * LOG2E)[:, :FOX_HEADS]

    cos = cos_ref[...]
    sin = sin_ref[...]
    rq = _dot(u, wr_ref[:, 0:RET_QK])
    rk = _dot(u, wr_ref[:, RET_QK:2 * RET_QK])
    for hh in range(RET_HEADS):
        sl = slice(hh * RET_DK, (hh + 1) * RET_DK)
        rq_ref[:, sl] = _rotary_tile(rq[:, sl], cos, sin).astype(BF16)
        kt = _rotary_tile(rk[:, sl], cos, sin) * (RET_DK ** -0.5)
        if lead:
            kt = kt * vmask
        rk_ref[:, sl] = kt.astype(BF16)
    rv = _dot(u, wr_ref[:, 2 * RET_QK:2 * RET_QK + D_MODEL])
    if lead:
        rv = rv * vmask
    rv_ref[...] = rv.astype(BF16)
    rg = _dot(u, wr_ref[:, 2 * RET_QK + D_MODEL:])
    sg_ref[...] = (rg * jax.nn.sigmoid(rg)).astype(BF16)

    bg = bg_ref[...]
    ga_ref[...] = jax.nn.sigmoid(_dot(u, wg_ref[:, :D_MODEL]) + bg[:, :D_MODEL]).astype(BF16)
    gb_ref[...] = jax.nn.sigmoid(_dot(u, wg_ref[:, D_MODEL:]) + bg[:, D_MODEL:]).astype(BF16)


def _proj(h, consts, cos, sin, tri, tm, lead, tiles_per_seq):
    rows = h.shape[0]
    grid = rows // tm

    def rows_spec(width):
        return pl.BlockSpec((tm, width), lambda i: (i, 0))

    pos_spec = pl.BlockSpec((tm, RET_DK), lambda i: (i % tiles_per_seq, 0))
    out_widths = [D_MODEL, D_MODEL, 2 * D_MODEL, FOX_HEADS, RET_QK, RET_QK, D_MODEL, D_MODEL,
                  D_MODEL, D_MODEL]
    out_dtypes = [BF16, BF16, BF16, F32, BF16, BF16, BF16, BF16, BF16, BF16]
    return pl.pallas_call(
        functools.partial(_proj_body, lead=lead, tiles_per_seq=tiles_per_seq),
        grid=(grid,),
        in_specs=[rows_spec(D_MODEL)] + [_const_spec(a.shape) for a in consts]
        + [pos_spec, pos_spec, _const_spec(tri.shape)],
        out_specs=[rows_spec(w) for w in out_widths],
        out_shape=[jax.ShapeDtypeStruct((rows, w), dt) for w, dt in zip(out_widths, out_dtypes)],
        scratch_shapes=[pltpu.VMEM((1, V7X_LANES), F32)],
        compiler_params=_params(("arbitrary",)),
        name="proj_lead" if lead else "proj",
    )(h, *consts, cos, sin, tri)


def _fox_body(stab_ref, q_ref, cq_ref, k_ref, v_ref, ck_ref, kl_ref, vl_ref, cl_ref, o_ref,
              m_ref, acc_ref):
    tq = q_ref.shape[0]
    i = pl.program_id(2)
    lane = lax.broadcasted_iota(jnp.int32, (1, 2 * V7X_LANES), 1)
    first2 = (lane % V7X_LANES) < FOX_HD
    first = first2[:, :V7X_LANES]
    q = q_ref[...]
    zero = jnp.zeros_like(q)
    q_heads = (jnp.where(first, q, zero), jnp.where(first, zero, q))
    lead_ok = lax.broadcasted_iota(jnp.int32, (tq, BLOCK), 1) >= N_EMPTY
    row = lax.broadcasted_iota(jnp.int32, (tq, tq), 0)
    col = lax.broadcasted_iota(jnp.int32, (tq, tq), 1)
    causal = col <= row
    acc_ref[...] = jnp.zeros_like(acc_ref)

    def run(block):
        block(kl_ref[...], vl_ref[...], cl_ref[...], lead_ok)

        def full_block(j, carry):
            ks = pl.multiple_of(j * tq, tq)
            block(k_ref[pl.ds(ks, tq), :], v_ref[pl.ds(ks, tq), :], ck_ref[:, pl.ds(ks, tq)], None)
            return carry

        lax.fori_loop(0, i, full_block, 0)
        ks = pl.multiple_of(i * tq, tq)
        block(k_ref[pl.ds(ks, tq), :], v_ref[pl.ds(ks, tq), :], ck_ref[:, pl.ds(ks, tq)], causal)

    use_bound = stab_ref[0] > 0.5

    @pl.when(use_bound)
    def _():
        head = lax.broadcasted_iota(jnp.int32, (1, FOX_HEADS), 1) - 2 * pl.program_id(1)
        cq_blk = cq_ref[...]
        cq = [jnp.sum(jnp.where(head == hh, cq_blk, 0.0), axis=1, keepdims=True) - stab_ref[1]
              for hh in range(2)]

        def block(k, v, ck, allowed):
            pvs = []
            for hh in range(2):
                s = _dot_nt(q_heads[hh], k) + cq[hh] - ck[hh:hh + 1, :]
                if allowed is not None:
                    s = jnp.where(allowed, s, NEG)
                pvs.append(_dot(jnp.exp2(s).astype(BF16), v))
            acc_ref[...] += jnp.where(first2, pvs[0], pvs[1])

        run(block)

    @pl.when(jnp.logical_not(use_bound))
    def _():
        m_ref[...] = jnp.full_like(m_ref, NEG)

        def block(k, v, ck, allowed):
            pvs, alphas = [], []
            for hh in range(2):
                s = _dot_nt(q_heads[hh], k) - ck[hh:hh + 1, :]
                if allowed is not None:
                    s = jnp.where(allowed, s, NEG)
                m_old = m_ref[hh]
                m_new = jnp.maximum(m_old, jnp.max(s, axis=1, keepdims=True))
                m_ref[hh] = m_new
                pvs.append(_dot(jnp.exp2(s - m_new).astype(BF16), v))
                alphas.append(jnp.exp2(m_old - m_new))
            acc_ref[...] = (acc_ref[...] * jnp.where(first2, alphas[0], alphas[1])
                            + jnp.where(first2, pvs[0], pvs[1]))

        run(block)

    acc = acc_ref[...]
    o_ref[...] = (acc[:, :V7X_LANES] / acc[:, V7X_LANES:]).astype(o_ref.dtype)


def _fox(stab, q, cq, k, v, ck, kl, vl, cl, batch, seq):
    tq = ATT_BLOCK
    w = V7X_LANES
    return pl.pallas_call(
        _fox_body,
        grid=(batch, FOX_PAIRS, seq // tq),
        in_specs=[
            pl.BlockSpec(memory_space=pltpu.SMEM),
            pl.BlockSpec((None, tq, w), lambda b, j, i: (b, i, j)),
            pl.BlockSpec((None, tq, FOX_HEADS), lambda b, j, i: (b, i, 0)),
            pl.BlockSpec((None, seq, w), lambda b, j, i: (b, 0, j)),
            pl.BlockSpec((None, seq, 2 * w), lambda b, j, i: (b, 0, j)),
            pl.BlockSpec((None, None, 2, seq), lambda b, j, i: (b, j, 0, 0)),
            pl.BlockSpec((BLOCK, w), lambda b, j, i: (0, j)),
            pl.BlockSpec((BLOCK, 2 * w), lambda b, j, i: (0, j)),
            pl.BlockSpec((None, 2, BLOCK), lambda b, j, i: (j, 0, 0)),
        ],
        out_specs=pl.BlockSpec((None, tq, w), lambda b, j, i: (b, i, j)),
        out_shape=jax.ShapeDtypeStruct((batch, seq, D_MODEL), BF16),
        scratch_shapes=[pltpu.VMEM((2, tq, 1), F32), pltpu.VMEM((tq, 2 * w), F32)],
        compiler_params=_params(("arbitrary", "arbitrary", "arbitrary")),
        name="fox",
    )(stab, q, cq, k, v, ck, kl, vl, cl)


def _ret_body(q_ref, k_ref, v_ref, sg_ref, kl_ref, vl_ref, gn_ref, decay_ref, zeta_ref, xi_ref,
              cd_ref, o_ref, state_ref):
    n_chunks = q_ref.shape[0] // BLOCK
    decay = decay_ref[...]
    zeta = zeta_ref[...]
    xi = xi_ref[...]
    cd = cd_ref[...]
    gn = gn_ref[...]

    state_ref[...] = _dot_tn((kl_ref[...].astype(F32) * zeta).astype(BF16), vl_ref[...])

    def chunk(c, carry):
        rs = pl.multiple_of(c * BLOCK, BLOCK)
        q = q_ref[pl.ds(rs, BLOCK), :]
        k = k_ref[pl.ds(rs, BLOCK), :]
        v = v_ref[pl.ds(rs, BLOCK), :]
        state = state_ref[...]
        scores = (_dot_nt(q, k) * decay).astype(BF16)
        o = _dot(scores, v) + _dot((q.astype(F32) * xi).astype(BF16), state.astype(BF16))
        kv = _dot_tn((k.astype(F32) * zeta).astype(BF16), v)
        state_ref[...] = cd * state + kv
        mu = jnp.mean(o, axis=-1, keepdims=True)
        d = o - mu
        var = jnp.mean(d * d, axis=-1, keepdims=True)
        yn = d * lax.rsqrt(var + GN_EPS) * gn
        o_ref[pl.ds(rs, BLOCK), :] = (sg_ref[pl.ds(rs, BLOCK), :].astype(F32) * yn).astype(o_ref.dtype)
        return carry

    lax.fori_loop(0, n_chunks, chunk, 0)


def _ret(rq, rk, rv, sg, rkl, rvl, gn, tables, batch, seq):
    decay, zeta, xi, cd = tables
    head_qk = pl.BlockSpec((None, seq, RET_DK), lambda b, h: (b, 0, h))
    head_v = pl.BlockSpec((None, seq, RET_DV), lambda b, h: (b, 0, h))
    table = pl.BlockSpec((None, BLOCK, BLOCK), lambda b, h: (h, 0, 0))
    return pl.pallas_call(
        _ret_body,
        grid=(batch, RET_HEADS),
        in_specs=[head_qk, head_qk, head_v, head_v,
                  pl.BlockSpec((BLOCK, RET_DK), lambda b, h: (0, h)),
                  pl.BlockSpec((BLOCK, RET_DV), lambda b, h: (0, h)),
                  pl.BlockSpec((1, RET_DV), lambda b, h: (0, h)),
                  table, table, table,
                  pl.BlockSpec((None, 1, RET_DV), lambda b, h: (h, 0, 0))],
        out_specs=head_v,
        out_shape=jax.ShapeDtypeStruct((batch, seq, D_MODEL), BF16),
        scratch_shapes=[pltpu.VMEM((RET_DK, RET_DV), F32)],
        compiler_params=_params(("arbitrary", "arbitrary")),
        name="ret",
    )(rq, rk, rv, sg, rkl, rvl, gn, decay, zeta, xi, cd)


def _out_body(ya_ref, yb_ref, ga_ref, gb_ref, h_ref, wa_ref, wb_ref, wo_ref, g_ref, win_ref,
              wout_ref, o_ref):
    mixed = (ga_ref[...].astype(F32) * _dot(ya_ref[...], wa_ref[...])
             + gb_ref[...].astype(F32) * _dot(yb_ref[...], wb_ref[...]))
    h = h_ref[...] + _dot(mixed.astype(BF16), wo_ref[...])
    xn = _rms_rows(h, g_ref[...]).astype(BF16)
    o_ref[...] = h + 0.5 * _swiglu(xn, win_ref, wout_ref)


def _out(ya, yb, ga, gb, h, consts, tm):
    rows = h.shape[0]
    row_spec = pl.BlockSpec((tm, D_MODEL), lambda i: (i, 0))
    return pl.pallas_call(
        _out_body,
        grid=(rows // tm,),
        in_specs=[row_spec] * 5 + [_const_spec(a.shape) for a in consts],
        out_specs=row_spec,
        out_shape=jax.ShapeDtypeStruct((rows, D_MODEL), F32),
        compiler_params=_params(("arbitrary",)),
        name="out",
    )(ya, yb, ga, gb, h, *consts)


def _ffn_weights(w_in, w_out):
    win = w_in.astype(BF16).reshape(D_MODEL, 2, N_FF_CHUNKS, FF_CHUNK).transpose(1, 2, 0, 3)
    wout = w_out.astype(BF16).reshape(N_FF_CHUNKS, FF_CHUNK, D_MODEL)
    return win, wout


def _position_tables(seq):
    half = RET_DK // 2
    pos = jnp.arange(BLOCK + seq, dtype=F32) - N_EMPTY
    inv = ROPE_BASE ** (-jnp.arange(half, dtype=F32) / half)
    ang = pos[:, None] * inv[None, :]
    cos = jnp.concatenate([jnp.cos(ang), jnp.cos(ang)], axis=1)
    sin = jnp.concatenate([-jnp.sin(ang), jnp.sin(ang)], axis=1)
    return cos, sin


def _retention_tables():
    log_gamma = jnp.log1p(-jnp.exp2(-5.0 - jnp.arange(RET_HEADS, dtype=F32)))
    n = jnp.arange(BLOCK, dtype=F32)
    diff = n[:, None] - n[None, :]
    lg = log_gamma[:, None, None]
    decay = jnp.where(diff >= 0, jnp.exp(lg * jnp.maximum(diff, 0.0)), 0.0)
    zeta = jnp.exp(log_gamma[:, None] * (BLOCK - 1 - n)[None, :])
    xi = jnp.exp(log_gamma[:, None] * (n + 1.0)[None, :])
    cd = jnp.exp(log_gamma * BLOCK)
    lanes = (RET_HEADS, BLOCK, RET_DK)
    return (decay, jnp.broadcast_to(zeta[:, :, None], lanes), jnp.broadcast_to(xi[:, :, None], lanes),
            jnp.broadcast_to(cd[:, None, None], (RET_HEADS, 1, RET_DV)))


def kernel(x, meta_tokens, norm_ffn1, w_ffn1_in, w_ffn1_out, norm_mix, w_in, b_forget, b_gate,
           fox_q_norm, fox_k_norm, w_o_fox, ret_gn, w_o_ret, w_out, norm_ffn2, w_ffn2_in,
           w_ffn2_out):
    batch, seq, d = x.shape
    assert d == D_MODEL and seq % ATT_BLOCK == 0 and seq % ROW_TILE == 0
    assert norm_ffn1.shape[0] == 1, "one layer"
    tiles_per_seq = seq // ROW_TILE

    win1, wout1 = _ffn_weights(w_ffn1_in[0], w_ffn1_out[0])
    win2, wout2 = _ffn_weights(w_ffn2_in[0], w_ffn2_out[0])
    wi = w_in[0].astype(BF16)
    o_ff = 3 * D_MODEL
    o_r = o_ff + FOX_HEADS
    o_g = o_r + 2 * RET_QK + 2 * D_MODEL
    wf = wi[:, :o_ff].reshape(D_MODEL, 3, D_MODEL).transpose(1, 0, 2)
    wff = jnp.pad(wi[:, o_ff:o_r], ((0, 0), (0, V7X_LANES - FOX_HEADS)))
    wr = wi[:, o_r:o_g]
    wg = wi[:, o_g:]
    grp = np.arange(V7X_MXU_DIM) // FOX_HD
    gmat = jnp.asarray((grp[:, None] == grp[None, :]) / FOX_HD, dtype=BF16)
    gq = jnp.tile(fox_q_norm[0], FOX_HEADS)[None, :]
    gk = jnp.tile(fox_k_norm[0], FOX_HEADS)[None, :]
    bf = jnp.pad(b_forget[0], (0, V7X_LANES - FOX_HEADS))[None, :]
    bg = b_gate[0][None, :]
    proj_consts = (norm_mix, wf, wff, wr, wg, gmat, gq, gk, bf, bg)
    cos, sin = _position_tables(seq)
    idx = np.arange(ROW_TILE)
    tri_incl = jnp.asarray(idx[None, :] <= idx[:, None], dtype=BF16)
    idx = np.arange(BLOCK)
    tri_after = jnp.asarray(idx[None, :] > idx[:, None], dtype=BF16)

    lead = jnp.concatenate([jnp.zeros((N_EMPTY, d), x.dtype), meta_tokens.astype(x.dtype)], axis=0)
    xr = x.reshape(batch * seq, d)

    h1 = _ffn(xr, norm_ffn1, win1, wout1, ROW_TILE)
    h1l = _ffn(lead, norm_ffn1, win1, wout1, BLOCK)

    q, k, v, c, rq, rk, rv, sg, ga, gb = _proj(
        h1, proj_consts, cos[BLOCK:], sin[BLOCK:], tri_incl, ROW_TILE, False, tiles_per_seq)
    _, kl, vl, cl, _, rkl, rvl, _, _, _ = _proj(
        h1l, proj_consts, cos[:BLOCK], sin[:BLOCK], tri_after, BLOCK, True, 1)

    def b3(a):
        return a.reshape(batch, seq, a.shape[-1])

    ck = b3(c).transpose(0, 2, 1).reshape(batch, FOX_PAIRS, 2, seq)
    clt = cl.T.reshape(FOX_PAIRS, 2, BLOCK)
    bound = FOX_HD ** 0.5 * jnp.max(jnp.abs(fox_q_norm[0])) * jnp.max(jnp.abs(fox_k_norm[0]))
    stab = jnp.stack([(bound <= SAFE_LOGIT_BOUND).astype(F32), bound * LOG2E])
    ya = _fox(stab, b3(q), b3(c), b3(k), b3(v), ck, kl, vl, clt, batch, seq)
    yb = _ret(b3(rq), b3(rk), b3(rv), b3(sg), rkl, rvl, ret_gn, _retention_tables(), batch, seq)

    out_consts = (w_o_fox[0].astype(BF16), w_o_ret[0].astype(BF16), w_out[0].astype(BF16),
                  norm_ffn2, win2, wout2)
    out = _out(ya.reshape(batch * seq, d), yb.reshape(batch * seq, d), ga, gb, h1, out_consts,
               ROW_TILE)
    return out.reshape(batch, seq, d)
```

```python
import functools
import math

import jax
import jax.numpy as jnp
import numpy as np
from jax import lax
from jax.experimental import pallas as pl
from jax.experimental.pallas import tpu as pltpu

F32 = jnp.float32
BF16 = jnp.bfloat16

D_MODEL = 1024
N_META = 16
BLOCK = 128
N_EMPTY = BLOCK - N_META
FOX_HD = 64
FOX_HEADS = D_MODEL // FOX_HD
FOX_PAIRS = FOX_HEADS // 2
RET_HEADS = 4
RET_DK = D_MODEL // (2 * RET_HEADS)
RET_DV = 2 * RET_DK
RET_QK = RET_HEADS * RET_DK
D_FF = ((8 * D_MODEL // 3 + 127) // 128) * 128
EPS = 1e-6
GN_EPS = 1e-5
ROPE_BASE = 10000.0
NEG = -1e30
LOG2E = math.log2(math.e)
SAFE_LOGIT_BOUND = 30.0

V7X_LANES = 128
V7X_MXU_DIM = 256
V7X_VMEM_BYTES = 64 * 2**20
VMEM_LIMIT = V7X_VMEM_BYTES - 8 * 2**20

ROW_TILE = 512
FF_CHUNK = V7X_MXU_DIM
N_FF_CHUNKS = D_FF // FF_CHUNK
ATT_BLOCK = 1024


def _dot(a, b):
    return jnp.dot(a, b, preferred_element_type=F32)


def _dot_nt(a, b):
    return lax.dot_general(a, b, (((1,), (1,)), ((), ())), preferred_element_type=F32)


def _dot_tn(a, b):
    return lax.dot_general(a, b, (((0,), (0,)), ((), ())), preferred_element_type=F32)


def _const_spec(shape):
    zeros = (0,) * len(shape)
    return pl.BlockSpec(shape, lambda *_: zeros, pipeline_mode=pl.Buffered(1))


def _params(semantics):
    return pltpu.CompilerParams(dimension_semantics=semantics, vmem_limit_bytes=VMEM_LIMIT)


def _rms_rows(x, g):
    ms = jnp.mean(x * x, axis=-1, keepdims=True)
    return x * lax.rsqrt(ms + EPS) * g


def _swiglu(xn, win_ref, wout_ref):
    acc = None
    for c in range(N_FF_CHUNKS):
        a = _dot(xn, win_ref[0, c])
        b = _dot(xn, win_ref[1, c])
        hm = (a * jax.nn.sigmoid(a) * b).astype(BF16)
        d = _dot(hm, wout_ref[c])
        acc = d if acc is None else acc + d
    return acc


def _ffn_body(x_ref, g_ref, win_ref, wout_ref, o_ref):
    x = x_ref[...]
    xn = _rms_rows(x, g_ref[...]).astype(BF16)
    o_ref[...] = x + 0.5 * _swiglu(xn, win_ref, wout_ref)


def _ffn(x, g, win, wout, tm):
    rows = x.shape[0]
    row_spec = pl.BlockSpec((tm, D_MODEL), lambda i: (i, 0))
    return pl.pallas_call(
        _ffn_body,
        grid=(rows // tm,),
        in_specs=[row_spec, _const_spec(g.shape), _const_spec(win.shape), _const_spec(wout.shape)],
        out_specs=row_spec,
        out_shape=jax.ShapeDtypeStruct((rows, D_MODEL), F32),
        compiler_params=_params(("arbitrary",)),
        name="ffn",
    )(x, g, win, wout)


def _group_mean_sq(x, gmat_ref):
    sq = (x * x).astype(BF16)
    w = V7X_MXU_DIM
    parts = [_dot(sq[:, i * w:(i + 1) * w], gmat_ref[...]) for i in range(D_MODEL // w)]
    return jnp.concatenate(parts, axis=1)


def _rotary_tile(x, cos, sin_signed):
    return x * cos + pltpu.roll(x, RET_DK // 2, axis=1) * sin_signed


def _proj_body(h_ref, gmix_ref, wf_ref, wff_ref, wr_ref, wg_ref, gmat_ref, gq_ref, gk_ref,
               bf_ref, bg_ref, cos_ref, sin_ref, tri_ref,
               q_ref, k_ref, v_ref, c_ref, rq_ref, rk_ref, rv_ref, sg_ref, ga_ref, gb_ref,
               carry_ref, *, lead, tiles_per_seq):
    tm = h_ref.shape[0]
    u = _rms_rows(h_ref[...], gmix_ref[...]).astype(BF16)
    if lead:
        row = lax.broadcasted_iota(jnp.int32, (tm, 1), 0)
        valid = row >= N_EMPTY
        vmask = valid.astype(F32)

    fq = _dot(u, wf_ref[0])
    qn = fq * lax.rsqrt(_group_mean_sq(fq, gmat_ref) + EPS) * gq_ref[...]
    q_ref[...] = (qn * (FOX_HD ** -0.5 * LOG2E)).astype(BF16)
    fk = _dot(u, wf_ref[1])
    kn = fk * lax.rsqrt(_group_mean_sq(fk, gmat_ref) + EPS) * gk_ref[...]
    k_ref[...] = kn.astype(BF16)
    fv = _dot(u, wf_ref[2]).astype(BF16)
    ones = jnp.ones((tm, V7X_LANES), BF16)
    for j in range(FOX_PAIRS):
        v_ref[:, 2 * j * V7X_LANES:(2 * j + 1) * V7X_LANES] = fv[:, j * V7X_LANES:(j + 1) * V7X_LANES]
        v_ref[:, (2 * j + 1) * V7X_LANES:(2 * j + 2) * V7X_LANES] = ones

    z = _dot(u, wff_ref[...]) + bf_ref[...]
    logf = -(jnp.maximum(-z, 0.0) + jnp.log1p(jnp.exp(-jnp.abs(z))))
    if lead:
        logf = jnp.where(valid, logf, 0.0)
    p1 = logf.astype(BF16)
    r1 = logf - p1.astype(F32)
    p2 = r1.astype(BF16)
    p3 = (r1 - p2.astype(F32)).astype(BF16)
    tri = tri_ref[...]
    cum = _dot(tri, p1) + _dot(tri, p2) + _dot(tri, p3)
    if lead:
        c = -cum
    else:
        @pl.when(pl.program_id(0) % tiles_per_seq == 0)
        def _():
            carry_ref[...] = jnp.zeros_like(carry_ref)
        c = cum + carry_ref[...]
        carry_ref[...] = c[tm - 1:tm, :]
    c_ref[...] = (c * LOG2E)[:, :FOX_HEADS]

    cos = cos_ref[...]
    sin = sin_ref[...]
    rq = _dot(u, wr_ref[:, 0:RET_QK])
    rk = _dot(u, wr_ref[:, RET_QK:2 * RET_QK])
    for hh in range(RET_HEADS):
        sl = slice(hh * RET_DK, (hh + 1) * RET_DK)
        rq_ref[:, sl] = _rotary_tile(rq[:, sl], cos, sin).astype(BF16)
        kt = _rotary_tile(rk[:, sl], cos, sin) * (RET_DK ** -0.5)
        if lead:
            kt = kt * vmask
        rk_ref[:, sl] = kt.astype(BF16)
    rv = _dot(u, wr_ref[:, 2 * RET_QK:2 * RET_QK + D_MODEL])
    if lead:
        rv = rv * vmask
    rv_ref[...] = rv.astype(BF16)
    rg = _dot(u, wr_ref[:, 2 * RET_QK + D_MODEL:])
    sg_ref[...] = (rg * jax.nn.sigmoid(rg)).astype(BF16)

    bg = bg_ref[...]
    ga_ref[...] = jax.nn.sigmoid(_dot(u, wg_ref[:, :D_MODEL]) + bg[:, :D_MODEL]).astype(BF16)
    gb_ref[...] = jax.nn.sigmoid(_dot(u, wg_ref[:, D_MODEL:]) + bg[:, D_MODEL:]).astype(BF16)


def _proj(h, consts, cos, sin, tri, tm, lead, tiles_per_seq):
    rows = h.shape[0]
    grid = rows // tm

    def rows_spec(width):
        return pl.BlockSpec((tm, width), lambda i: (i, 0))

    pos_spec = pl.BlockSpec((tm, RET_DK), lambda i: (i % tiles_per_seq, 0))
    out_widths = [D_MODEL, D_MODEL, 2 * D_MODEL, FOX_HEADS, RET_QK, RET_QK, D_MODEL, D_MODEL,
                  D_MODEL, D_MODEL]
    out_dtypes = [BF16, BF16, BF16, F32, BF16, BF16, BF16, BF16, BF16, BF16]
    return pl.pallas_call(
        functools.partial(_proj_body, lead=lead, tiles_per_seq=tiles_per_seq),
        grid=(grid,),
        in_specs=[rows_spec(D_MODEL)] + [_const_spec(a.shape) for a in consts]
        + [pos_spec, pos_spec, _const_spec(tri.shape)],
        out_specs=[rows_spec(w) for w in out_widths],
        out_shape=[jax.ShapeDtypeStruct((rows, w), dt) for w, dt in zip(out_widths, out_dtypes)],
        scratch_shapes=[pltpu.VMEM((1, V7X_LANES), F32)],
        compiler_params=_params(("arbitrary",)),
        name="proj_lead" if lead else "proj",
    )(h, *consts, cos, sin, tri)


def _fox_body(stab_ref, q_ref, cq_ref, k_ref, v_ref, ck_ref, kl_ref, vl_ref, cl_ref, o_ref,
              m_ref, acc_ref, e_ref):
    tq = q_ref.shape[0]
    tk = tq // 2
    n_full = 2 * pl.program_id(2)
    lane = lax.broadcasted_iota(jnp.int32, (1, 2 * V7X_LANES), 1)
    first2 = (lane % V7X_LANES) < FOX_HD
    first = first2[:, :V7X_LANES]
    q = q_ref[...]
    zero = jnp.zeros_like(q)
    q_heads = (jnp.where(first, q, zero), jnp.where(first, zero, q))
    lead_ok = lax.broadcasted_iota(jnp.int32, (tq, BLOCK), 1) >= N_EMPTY
    row = lax.broadcasted_iota(jnp.int32, (tq, tk), 0)
    col = lax.broadcasted_iota(jnp.int32, (tq, tk), 1)
    causal = col <= row
    causal_half = causal[:tk]
    acc_ref[...] = jnp.zeros_like(acc_ref)

    def keys(j):
        ks = pl.multiple_of(j * tk, tk)
        return k_ref[pl.ds(ks, tk), :], ck_ref[:, pl.ds(ks, tk)]

    def values(j):
        return v_ref[pl.ds(pl.multiple_of(j * tk, tk), tk), :]

    use_bound = stab_ref[0] > 0.5

    @pl.when(use_bound)
    def _():
        head = lax.broadcasted_iota(jnp.int32, (1, FOX_HEADS), 1) - 2 * pl.program_id(1)
        cq_blk = cq_ref[...]
        cq = [jnp.sum(jnp.where(head == hh, cq_blk, 0.0), axis=1, keepdims=True) - stab_ref[1]
              for hh in range(2)]

        def exponents(k, ck, r0=0):
            return [_dot_nt(q_heads[hh][r0:], k) + cq[hh][r0:] - ck[hh:hh + 1, :] for hh in range(2)]

        def accumulate(e, v, allowed, r0=0):
            if allowed is not None:
                e = [jnp.where(allowed, x, NEG) for x in e]
            pvs = [_dot(jnp.exp2(x).astype(BF16), v) for x in e]
            acc_ref[r0:, :] += jnp.where(first2, pvs[0], pvs[1])

        e_lead = exponents(kl_ref[...], cl_ref[...])
        e0 = exponents(*keys(0))
        e_ref[0] = e0[0]
        e_ref[1] = e0[1]
        accumulate(e_lead, vl_ref[...], lead_ok)

        def step(j, carry):
            e_cur = [e_ref[0], e_ref[1]]
            e_next = exponents(*keys(j + 1))
            accumulate(e_cur, values(j), None)
            e_ref[0] = e_next[0]
            e_ref[1] = e_next[1]
            return carry

        lax.fori_loop(0, n_full, step, 0)
        e_last = exponents(*keys(n_full + 1), r0=tk)
        accumulate([e_ref[0], e_ref[1]], values(n_full), causal)
        accumulate(e_last, values(n_full + 1), causal_half, r0=tk)

    @pl.when(jnp.logical_not(use_bound))
    def _():
        m_ref[...] = jnp.full_like(m_ref, NEG)

        def block(k, v, ck, allowed):
            pvs, alphas = [], []
            for hh in range(2):
                s = _dot_nt(q_heads[hh], k) - ck[hh:hh + 1, :]
                if allowed is not None:
                    s = jnp.where(allowed, s, NEG)
                m_old = m_ref[hh]
                m_new = jnp.maximum(m_old, jnp.max(s, axis=1, keepdims=True))
                m_ref[hh] = m_new
                pvs.append(_dot(jnp.exp2(s - m_new).astype(BF16), v))
                alphas.append(jnp.exp2(m_old - m_new))
            acc_ref[...] = (acc_ref[...] * jnp.where(first2, alphas[0], alphas[1])
                            + jnp.where(first2, pvs[0], pvs[1]))

        block(kl_ref[...], vl_ref[...], cl_ref[...], lead_ok)

        def full_block(j, carry):
            block(keys(j)[0], values(j), keys(j)[1], None)
            return carry

        lax.fori_loop(0, n_full, full_block, 0)
        block(keys(n_full)[0], values(n_full), keys(n_full)[1], causal)
        block(keys(n_full + 1)[0], values(n_full + 1), keys(n_full + 1)[1], col + tk <= row)

    acc = acc_ref[...]
    o_ref[...] = (acc[:, :V7X_LANES] / acc[:, V7X_LANES:]).astype(o_ref.dtype)


def _fox(stab, q, cq, k, v, ck, kl, vl, cl, batch, seq):
    tq = ATT_BLOCK
    w = V7X_LANES
    return pl.pallas_call(
        _fox_body,
        grid=(batch, FOX_PAIRS, seq // tq),
        in_specs=[
            pl.BlockSpec(memory_space=pltpu.SMEM),
            pl.BlockSpec((None, tq, w), lambda b, j, i: (b, i, j)),
            pl.BlockSpec((None, tq, FOX_HEADS), lambda b, j, i: (b, i, 0)),
            pl.BlockSpec((None, seq, w), lambda b, j, i: (b, 0, j)),
            pl.BlockSpec((None, seq, 2 * w), lambda b, j, i: (b, 0, j)),
            pl.BlockSpec((None, None, 2, seq), lambda b, j, i: (b, j, 0, 0)),
            pl.BlockSpec((BLOCK, w), lambda b, j, i: (0, j)),
            pl.BlockSpec((BLOCK, 2 * w), lambda b, j, i: (0, j)),
            pl.BlockSpec((None, 2, BLOCK), lambda b, j, i: (j, 0, 0)),
        ],
        out_specs=pl.BlockSpec((None, tq, w), lambda b, j, i: (b, i, j)),
        out_shape=jax.ShapeDtypeStruct((batch, seq, D_MODEL), BF16),
        scratch_shapes=[pltpu.VMEM((2, tq, 1), F32), pltpu.VMEM((tq, 2 * w), F32),
                        pltpu.VMEM((2, tq, tq // 2), F32)],
        compiler_params=_params(("arbitrary", "arbitrary", "arbitrary")),
        name="fox",
    )(stab, q, cq, k, v, ck, kl, vl, cl)


def _ret_body(q_ref, k_ref, v_ref, sg_ref, kl_ref, vl_ref, gn_ref, decay_ref, zeta_ref, xi_ref,
              cd_ref, o_ref, state_ref):
    n_chunks = q_ref.shape[0] // BLOCK
    decay = decay_ref[...]
    zeta = zeta_ref[...]
    xi = xi_ref[...]
    cd = cd_ref[...]
    gn = gn_ref[...]

    state_ref[...] = _dot_tn((kl_ref[...].astype(F32) * zeta).astype(BF16), vl_ref[...])

    def chunk(c, carry):
        rs = pl.multiple_of(c * BLOCK, BLOCK)
        q = q_ref[pl.ds(rs, BLOCK), :]
        k = k_ref[pl.ds(rs, BLOCK), :]
        v = v_ref[pl.ds(rs, BLOCK), :]
        state = state_ref[...]
        scores = (_dot_nt(q, k) * decay).astype(BF16)
        o = _dot(scores, v) + _dot((q.astype(F32) * xi).astype(BF16), state.astype(BF16))
        kv = _dot_tn((k.astype(F32) * zeta).astype(BF16), v)
        state_ref[...] = cd * state + kv
        mu = jnp.mean(o, axis=-1, keepdims=True)
        d = o - mu
        var = jnp.mean(d * d, axis=-1, keepdims=True)
        yn = d * lax.rsqrt(var + GN_EPS) * gn
        o_ref[pl.ds(rs, BLOCK), :] = (sg_ref[pl.ds(rs, BLOCK), :].astype(F32) * yn).astype(o_ref.dtype)
        return carry

    lax.fori_loop(0, n_chunks, chunk, 0)


def _ret(rq, rk, rv, sg, rkl, rvl, gn, tables, batch, seq):
    decay, zeta, xi, cd = tables
    head_qk = pl.BlockSpec((None, seq, RET_DK), lambda b, h: (b, 0, h))
    head_v = pl.BlockSpec((None, seq, RET_DV), lambda b, h: (b, 0, h))
    table = pl.BlockSpec((None, BLOCK, BLOCK), lambda b, h: (h, 0, 0))
    return pl.pallas_call(
        _ret_body,
        grid=(batch, RET_HEADS),
        in_specs=[head_qk, head_qk, head_v, head_v,
                  pl.BlockSpec((BLOCK, RET_DK), lambda b, h: (0, h)),
                  pl.BlockSpec((BLOCK, RET_DV), lambda b, h: (0, h)),
                  pl.BlockSpec((1, RET_DV), lambda b, h: (0, h)),
                  table, table, table,
                  pl.BlockSpec((None, 1, RET_DV), lambda b, h: (h, 0, 0))],
        out_specs=head_v,
        out_shape=jax.ShapeDtypeStruct((batch, seq, D_MODEL), BF16),
        scratch_shapes=[pltpu.VMEM((RET_DK, RET_DV), F32)],
        compiler_params=_params(("arbitrary", "arbitrary")),
        name="ret",
    )(rq, rk, rv, sg, rkl, rvl, gn, decay, zeta, xi, cd)


def _out_body(ya_ref, yb_ref, ga_ref, gb_ref, h_ref, wa_ref, wb_ref, wo_ref, g_ref, win_ref,
              wout_ref, o_ref):
    mixed = (ga_ref[...].astype(F32) * _dot(ya_ref[...], wa_ref[...])
             + gb_ref[...].astype(F32) * _dot(yb_ref[...], wb_ref[...]))
    h = h_ref[...] + _dot(mixed.astype(BF16), wo_ref[...])
    xn = _rms_rows(h, g_ref[...]).astype(BF16)
    o_ref[...] = h + 0.5 * _swiglu(xn, win_ref, wout_ref)


def _out(ya, yb, ga, gb, h, consts, tm):
    rows = h.shape[0]
    row_spec = pl.BlockSpec((tm, D_MODEL), lambda i: (i, 0))
    return pl.pallas_call(
        _out_body,
        grid=(rows // tm,),
        in_specs=[row_spec] * 5 + [_const_spec(a.shape) for a in consts],
        out_specs=row_spec,
        out_shape=jax.ShapeDtypeStruct((rows, D_MODEL), F32),
        compiler_params=_params(("arbitrary",)),
        name="out",
    )(ya, yb, ga, gb, h, *consts)


def _ffn_weights(w_in, w_out):
    win = w_in.astype(BF16).reshape(D_MODEL, 2, N_FF_CHUNKS, FF_CHUNK).transpose(1, 2, 0, 3)
    wout = w_out.astype(BF16).reshape(N_FF_CHUNKS, FF_CHUNK, D_MODEL)
    return win, wout


def _position_tables(seq):
    half = RET_DK // 2
    pos = jnp.arange(BLOCK + seq, dtype=F32) - N_EMPTY
    inv = ROPE_BASE ** (-jnp.arange(half, dtype=F32) / half)
    ang = pos[:, None] * inv[None, :]
    cos = jnp.concatenate([jnp.cos(ang), jnp.cos(ang)], axis=1)
    sin = jnp.concatenate([-jnp.sin(ang), jnp.sin(ang)], axis=1)
    return cos, sin


def _retention_tables():
    log_gamma = jnp.log1p(-jnp.exp2(-5.0 - jnp.arange(RET_HEADS, dtype=F32)))
    n = jnp.arange(BLOCK, dtype=F32)
    diff = n[:, None] - n[None, :]
    lg = log_gamma[:, None, None]
    decay = jnp.where(diff >= 0, jnp.exp(lg * jnp.maximum(diff, 0.0)), 0.0)
    zeta = jnp.exp(log_gamma[:, None] * (BLOCK - 1 - n)[None, :])
    xi = jnp.exp(log_gamma[:, None] * (n + 1.0)[None, :])
    cd = jnp.exp(log_gamma * BLOCK)
    lanes = (RET_HEADS, BLOCK, RET_DK)
    return (decay, jnp.broadcast_to(zeta[:, :, None], lanes), jnp.broadcast_to(xi[:, :, None], lanes),
            jnp.broadcast_to(cd[:, None, None], (RET_HEADS, 1, RET_DV)))


def kernel(x, meta_tokens, norm_ffn1, w_ffn1_in, w_ffn1_out, norm_mix, w_in, b_forget, b_gate,
           fox_q_norm, fox_k_norm, w_o_fox, ret_gn, w_o_ret, w_out, norm_ffn2, w_ffn2_in,
           w_ffn2_out):
    batch, seq, d = x.shape
    assert d == D_MODEL and seq % ATT_BLOCK == 0 and seq % ROW_TILE == 0
    assert norm_ffn1.shape[0] == 1, "one layer"
    tiles_per_seq = seq // ROW_TILE

    win1, wout1 = _ffn_weights(w_ffn1_in[0], w_ffn1_out[0])
    win2, wout2 = _ffn_weights(w_ffn2_in[0], w_ffn2_out[0])
    wi = w_in[0].astype(BF16)
    o_ff = 3 * D_MODEL
    o_r = o_ff + FOX_HEADS
    o_g = o_r + 2 * RET_QK + 2 * D_MODEL
    wf = wi[:, :o_ff].reshape(D_MODEL, 3, D_MODEL).transpose(1, 0, 2)
    wff = jnp.pad(wi[:, o_ff:o_r], ((0, 0), (0, V7X_LANES - FOX_HEADS)))
    wr = wi[:, o_r:o_g]
    wg = wi[:, o_g:]
    grp = np.arange(V7X_MXU_DIM) // FOX_HD
    gmat = jnp.asarray((grp[:, None] == grp[None, :]) / FOX_HD, dtype=BF16)
    gq = jnp.tile(fox_q_norm[0], FOX_HEADS)[None, :]
    gk = jnp.tile(fox_k_norm[0], FOX_HEADS)[None, :]
    bf = jnp.pad(b_forget[0], (0, V7X_LANES - FOX_HEADS))[None, :]
    bg = b_gate[0][None, :]
    proj_consts = (norm_mix, wf, wff, wr, wg, gmat, gq, gk, bf, bg)
    cos, sin = _position_tables(seq)
    idx = np.arange(ROW_TILE)
    tri_incl = jnp.asarray(idx[None, :] <= idx[:, None], dtype=BF16)
    idx = np.arange(BLOCK)
    tri_after = jnp.asarray(idx[None, :] > idx[:, None], dtype=BF16)

    lead = jnp.concatenate([jnp.zeros((N_EMPTY, d), x.dtype), meta_tokens.astype(x.dtype)], axis=0)
    xr = x.reshape(batch * seq, d)

    h1 = _ffn(xr, norm_ffn1, win1, wout1, ROW_TILE)
    h1l = _ffn(lead, norm_ffn1, win1, wout1, BLOCK)

    q, k, v, c, rq, rk, rv, sg, ga, gb = _proj(
        h1, proj_consts, cos[BLOCK:], sin[BLOCK:], tri_incl, ROW_TILE, False, tiles_per_seq)
    _, kl, vl, cl, _, rkl, rvl, _, _, _ = _proj(
        h1l, proj_consts, cos[:BLOCK], sin[:BLOCK], tri_after, BLOCK, True, 1)

    def b3(a):
        return a.reshape(batch, seq, a.shape[-1])

    ck = b3(c).transpose(0, 2, 1).reshape(batch, FOX_PAIRS, 2, seq)
    clt = cl.T.reshape(FOX_PAIRS, 2, BLOCK)
    bound = FOX_HD ** 0.5 * jnp.max(jnp.abs(fox_q_norm[0])) * jnp.max(jnp.abs(fox_k_norm[0]))
    stab = jnp.stack([(bound <= SAFE_LOGIT_BOUND).astype(F32), bound * LOG2E])
    ya = _fox(stab, b3(q), b3(c), b3(k), b3(v), ck, kl, vl, clt, batch, seq)
    yb = _ret(b3(rq), b3(rk), b3(rv), b3(sg), rkl, rvl, ret_gn, _retention_tables(), batch, seq)

    out_consts = (w_o_fox[0].astype(BF16), w_o_ret[0].astype(BF16), w_out[0].astype(BF16),
                  norm_ffn2, win2, wout2)
    out = _out(ya.reshape(batch * seq, d), yb.reshape(batch * seq, d), ga, gb, h1, out_consts,
               ROW_TILE)
    return out.reshape(batch, seq, d)
```

```python
import functools
import math

import jax
import jax.numpy as jnp
import numpy as np
from jax import lax
from jax.experimental import pallas as pl
from jax.experimental.pallas import tpu as pltpu

F32 = jnp.float32
BF16 = jnp.bfloat16

D_MODEL = 1024
N_META = 16
BLOCK = 128
N_EMPTY = BLOCK - N_META
FOX_HD = 64
FOX_HEADS = D_MODEL // FOX_HD
FOX_PAIRS = FOX_HEADS // 2
RET_HEADS = 4
RET_DK = D_MODEL // (2 * RET_HEADS)
RET_DV = 2 * RET_DK
RET_QK = RET_HEADS * RET_DK
D_FF = ((8 * D_MODEL // 3 + 127) // 128) * 128
EPS = 1e-6
GN_EPS = 1e-5
ROPE_BASE = 10000.0
NEG = -1e30
LOG2E = math.log2(math.e)
SAFE_LOGIT_BOUND = 30.0

V7X_LANES = 128
V7X_MXU_DIM = 256
V7X_VMEM_BYTES = 64 * 2**20
VMEM_LIMIT = V7X_VMEM_BYTES - 8 * 2**20

ROW_TILE = 512
FF_CHUNK = V7X_MXU_DIM
N_FF_CHUNKS = D_FF // FF_CHUNK
RET_UNROLL = 16
ATT_BLOCK = 1024


def _dot(a, b):
    return jnp.dot(a, b, preferred_element_type=F32)


def _dot_nt(a, b):
    return lax.dot_general(a, b, (((1,), (1,)), ((), ())), preferred_element_type=F32)


def _dot_tn(a, b):
    return lax.dot_general(a, b, (((0,), (0,)), ((), ())), preferred_element_type=F32)


def _const_spec(shape):
    zeros = (0,) * len(shape)
    return pl.BlockSpec(shape, lambda *_: zeros, pipeline_mode=pl.Buffered(1))


def _params(semantics):
    return pltpu.CompilerParams(dimension_semantics=semantics, vmem_limit_bytes=VMEM_LIMIT)


def _rms_rows(x, g):
    ms = jnp.mean(x * x, axis=-1, keepdims=True)
    return x * lax.rsqrt(ms + EPS) * g


def _swiglu(xn, win_ref, wout_ref):
    acc = None
    for c in range(N_FF_CHUNKS):
        a = _dot(xn, win_ref[:, c * FF_CHUNK:(c + 1) * FF_CHUNK])
        b = _dot(xn, win_ref[:, D_FF + c * FF_CHUNK:D_FF + (c + 1) * FF_CHUNK])
        hm = (a * jax.nn.sigmoid(a) * b).astype(BF16)
        d = _dot(hm, wout_ref[c])
        acc = d if acc is None else acc + d
    return acc


def _ffn_body(x_ref, g_ref, win_ref, wout_ref, o_ref):
    x = x_ref[...]
    xn = _rms_rows(x, g_ref[...]).astype(BF16)
    o_ref[...] = x + 0.5 * _swiglu(xn, win_ref, wout_ref)


def _ffn(x, g, win, wout, tm):
    rows = x.shape[0]
    row_spec = pl.BlockSpec((tm, D_MODEL), lambda i: (i, 0))
    return pl.pallas_call(
        _ffn_body,
        grid=(rows // tm,),
        in_specs=[row_spec, _const_spec(g.shape), _const_spec(win.shape), _const_spec(wout.shape)],
        out_specs=row_spec,
        out_shape=jax.ShapeDtypeStruct((rows, D_MODEL), F32),
        compiler_params=_params(("arbitrary",)),
        name="ffn",
    )(x, g, win, wout)


def _group_mean_sq(x, gmat_ref):
    sq = (x * x).astype(BF16)
    w = V7X_MXU_DIM
    parts = [_dot(sq[:, i * w:(i + 1) * w], gmat_ref[...]) for i in range(D_MODEL // w)]
    return jnp.concatenate(parts, axis=1)


def _rotary_tile(x, cos, sin_signed):
    return x * cos + pltpu.roll(x, RET_DK // 2, axis=1) * sin_signed


def _proj_body(h_ref, gmix_ref, wf_ref, wff_ref, wr_ref, wg_ref, gmat_ref, gq_ref, gk_ref,
               bf_ref, bg_ref, cos_ref, sin_ref, tri_ref,
               q_ref, k_ref, v_ref, c_ref, rq_ref, rk_ref, rv_ref, sg_ref, ga_ref, gb_ref,
               carry_ref, *, lead, tiles_per_seq):
    tm = h_ref.shape[0]
    u = _rms_rows(h_ref[...], gmix_ref[...]).astype(BF16)
    if lead:
        row = lax.broadcasted_iota(jnp.int32, (tm, 1), 0)
        valid = row >= N_EMPTY
        vmask = valid.astype(F32)

    fq = _dot(u, wf_ref[:, :D_MODEL])
    qn = fq * lax.rsqrt(_group_mean_sq(fq, gmat_ref) + EPS) * gq_ref[...]
    q_ref[...] = (qn * (FOX_HD ** -0.5 * LOG2E)).astype(BF16)
    fk = _dot(u, wf_ref[:, D_MODEL:2 * D_MODEL])
    kn = fk * lax.rsqrt(_group_mean_sq(fk, gmat_ref) + EPS) * gk_ref[...]
    k_ref[...] = kn.astype(BF16)
    fv = _dot(u, wf_ref[:, 2 * D_MODEL:]).astype(BF16)
    ones = jnp.ones((tm, V7X_LANES), BF16)
    for j in range(FOX_PAIRS):
        v_ref[:, 2 * j * V7X_LANES:(2 * j + 1) * V7X_LANES] = fv[:, j * V7X_LANES:(j + 1) * V7X_LANES]
        v_ref[:, (2 * j + 1) * V7X_LANES:(2 * j + 2) * V7X_LANES] = ones

    z = _dot(u, wff_ref[...]) + bf_ref[...]
    logf = -(jnp.maximum(-z, 0.0) + jnp.log1p(jnp.exp(-jnp.abs(z))))
    if lead:
        logf = jnp.where(valid, logf, 0.0)
    p1 = logf.astype(BF16)
    r1 = logf - p1.astype(F32)
    p2 = r1.astype(BF16)
    p3 = (r1 - p2.astype(F32)).astype(BF16)
    tri = tri_ref[...]
    cum = _dot(tri, p1) + _dot(tri, p2) + _dot(tri, p3)
    if lead:
        c = -cum
    else:
        @pl.when(pl.program_id(0) % tiles_per_seq == 0)
        def _():
            carry_ref[...] = jnp.zeros_like(carry_ref)
        c = cum + carry_ref[...]
        carry_ref[...] = c[tm - 1:tm, :]
    c_ref[...] = (c * LOG2E)[:, :FOX_HEADS]

    cos = cos_ref[...]
    sin = sin_ref[...]
    rq = _dot(u, wr_ref[:, 0:RET_QK])
    rk = _dot(u, wr_ref[:, RET_QK:2 * RET_QK])
    for hh in range(RET_HEADS):
        sl = slice(hh * RET_DK, (hh + 1) * RET_DK)
        rq_ref[:, sl] = _rotary_tile(rq[:, sl], cos, sin).astype(BF16)
        kt = _rotary_tile(rk[:, sl], cos, sin) * (RET_DK ** -0.5)
        if lead:
            kt = kt * vmask
        rk_ref[:, sl] = kt.astype(BF16)
    rv = _dot(u, wr_ref[:, 2 * RET_QK:2 * RET_QK + D_MODEL])
    if lead:
        rv = rv * vmask
    rv_ref[...] = rv.astype(BF16)
    rg = _dot(u, wr_ref[:, 2 * RET_QK + D_MODEL:])
    sg_ref[...] = (rg * jax.nn.sigmoid(rg)).astype(BF16)

    bg = bg_ref[...]
    ga_ref[...] = jax.nn.sigmoid(_dot(u, wg_ref[:, :D_MODEL]) + bg[:, :D_MODEL]).astype(BF16)
    gb_ref[...] = jax.nn.sigmoid(_dot(u, wg_ref[:, D_MODEL:]) + bg[:, D_MODEL:]).astype(BF16)


def _proj(h, consts, cos, sin, tri, tm, lead, tiles_per_seq):
    rows = h.shape[0]
    grid = rows // tm

    def rows_spec(width):
        return pl.BlockSpec((tm, width), lambda i: (i, 0))

    pos_spec = pl.BlockSpec((tm, RET_DK), lambda i: (i % tiles_per_seq, 0))
    out_widths = [D_MODEL, D_MODEL, 2 * D_MODEL, FOX_HEADS, RET_QK, RET_QK, D_MODEL, D_MODEL,
                  D_MODEL, D_MODEL]
    out_dtypes = [BF16, BF16, BF16, F32, BF16, BF16, BF16, BF16, BF16, BF16]
    return pl.pallas_call(
        functools.partial(_proj_body, lead=lead, tiles_per_seq=tiles_per_seq),
        grid=(grid,),
        in_specs=[rows_spec(D_MODEL)] + [_const_spec(a.shape) for a in consts]
        + [pos_spec, pos_spec, _const_spec(tri.shape)],
        out_specs=[rows_spec(w) for w in out_widths],
        out_shape=[jax.ShapeDtypeStruct((rows, w), dt) for w, dt in zip(out_widths, out_dtypes)],
        scratch_shapes=[pltpu.VMEM((1, V7X_LANES), F32)],
        compiler_params=_params(("arbitrary",)),
        name="proj_lead" if lead else "proj",
    )(h, *consts, cos, sin, tri)


def _fox_body(stab_ref, q_ref, cq_ref, k_ref, v_ref, ck_ref, kl_ref, vl_ref, cl_ref, o_ref,
              m_ref, acc_ref, e_ref):
    tq = q_ref.shape[0]
    tk = tq // 2
    n_full = 2 * pl.program_id(2)
    lane = lax.broadcasted_iota(jnp.int32, (1, 2 * V7X_LANES), 1)
    first2 = (lane % V7X_LANES) < FOX_HD
    first = first2[:, :V7X_LANES]
    q = q_ref[...]
    zero = jnp.zeros_like(q)
    q_heads = (jnp.where(first, q, zero), jnp.where(first, zero, q))
    lead_ok = lax.broadcasted_iota(jnp.int32, (tq, BLOCK), 1) >= N_EMPTY
    row = lax.broadcasted_iota(jnp.int32, (tq, tk), 0)
    col = lax.broadcasted_iota(jnp.int32, (tq, tk), 1)
    causal = col <= row
    causal_half = causal[:tk]
    acc_ref[...] = jnp.zeros_like(acc_ref)

    def keys(j):
        ks = pl.multiple_of(j * tk, tk)
        return k_ref[pl.ds(ks, tk), :], ck_ref[:, pl.ds(ks, tk)]

    def values(j):
        return v_ref[pl.ds(pl.multiple_of(j * tk, tk), tk), :]

    use_bound = stab_ref[0] > 0.5

    @pl.when(use_bound)
    def _():
        head = lax.broadcasted_iota(jnp.int32, (1, FOX_HEADS), 1) - 2 * pl.program_id(1)
        cq_blk = cq_ref[...]
        cq = [jnp.sum(jnp.where(head == hh, cq_blk, 0.0), axis=1, keepdims=True) - stab_ref[1]
              for hh in range(2)]

        def exponents(k, ck, r0=0):
            return [_dot_nt(q_heads[hh][r0:], k) + cq[hh][r0:] - ck[hh:hh + 1, :] for hh in range(2)]

        def accumulate(e, v, allowed, r0=0):
            if allowed is not None:
                e = [jnp.where(allowed, x, NEG) for x in e]
            pvs = [_dot(jnp.exp2(x).astype(BF16), v) for x in e]
            acc_ref[r0:, :] += jnp.where(first2, pvs[0], pvs[1])

        e_lead = exponents(kl_ref[...], cl_ref[...])
        e0 = exponents(*keys(0))
        e_ref[0] = e0[0]
        e_ref[1] = e0[1]
        accumulate(e_lead, vl_ref[...], lead_ok)

        def step(j, carry):
            e_cur = [e_ref[0], e_ref[1]]
            e_next = exponents(*keys(j + 1))
            accumulate(e_cur, values(j), None)
            e_ref[0] = e_next[0]
            e_ref[1] = e_next[1]
            return carry

        lax.fori_loop(0, n_full, step, 0)
        e_last = exponents(*keys(n_full + 1), r0=tk)
        accumulate([e_ref[0], e_ref[1]], values(n_full), causal)
        accumulate(e_last, values(n_full + 1), causal_half, r0=tk)

    @pl.when(jnp.logical_not(use_bound))
    def _():
        m_ref[...] = jnp.full_like(m_ref, NEG)

        def block(k, v, ck, allowed):
            pvs, alphas = [], []
            for hh in range(2):
                s = _dot_nt(q_heads[hh], k) - ck[hh:hh + 1, :]
                if allowed is not None:
                    s = jnp.where(allowed, s, NEG)
                m_old = m_ref[hh]
                m_new = jnp.maximum(m_old, jnp.max(s, axis=1, keepdims=True))
                m_ref[hh] = m_new
                pvs.append(_dot(jnp.exp2(s - m_new).astype(BF16), v))
                alphas.append(jnp.exp2(m_old - m_new))
            acc_ref[...] = (acc_ref[...] * jnp.where(first2, alphas[0], alphas[1])
                            + jnp.where(first2, pvs[0], pvs[1]))

        block(kl_ref[...], vl_ref[...], cl_ref[...], lead_ok)

        def full_block(j, carry):
            block(keys(j)[0], values(j), keys(j)[1], None)
            return carry

        lax.fori_loop(0, n_full, full_block, 0)
        block(keys(n_full)[0], values(n_full), keys(n_full)[1], causal)
        block(keys(n_full + 1)[0], values(n_full + 1), keys(n_full + 1)[1], col + tk <= row)

    acc = acc_ref[...]
    o_ref[...] = (acc[:, :V7X_LANES] / acc[:, V7X_LANES:]).astype(o_ref.dtype)


def _fox(stab, q, cq, k, v, ck, kl, vl, cl, batch, seq):
    tq = ATT_BLOCK
    w = V7X_LANES
    return pl.pallas_call(
        _fox_body,
        grid=(batch, FOX_PAIRS, seq // tq),
        in_specs=[
            pl.BlockSpec(memory_space=pltpu.SMEM),
            pl.BlockSpec((None, tq, w), lambda b, j, i: (b, i, j)),
            pl.BlockSpec((None, tq, FOX_HEADS), lambda b, j, i: (b, i, 0)),
            pl.BlockSpec((None, seq, w), lambda b, j, i: (b, 0, j)),
            pl.BlockSpec((None, seq, 2 * w), lambda b, j, i: (b, 0, j)),
            pl.BlockSpec((None, None, 2, seq), lambda b, j, i: (b, j, 0, 0)),
            pl.BlockSpec((BLOCK, w), lambda b, j, i: (0, j)),
            pl.BlockSpec((BLOCK, 2 * w), lambda b, j, i: (0, j)),
            pl.BlockSpec((None, 2, BLOCK), lambda b, j, i: (j, 0, 0)),
        ],
        out_specs=pl.BlockSpec((None, tq, w), lambda b, j, i: (b, i, j)),
        out_shape=jax.ShapeDtypeStruct((batch, seq, D_MODEL), BF16),
        scratch_shapes=[pltpu.VMEM((2, tq, 1), F32), pltpu.VMEM((tq, 2 * w), F32),
                        pltpu.VMEM((2, tq, tq // 2), F32)],
        compiler_params=_params(("arbitrary", "arbitrary", "arbitrary")),
        name="fox",
    )(stab, q, cq, k, v, ck, kl, vl, cl)


def _ret_body(q_ref, k_ref, v_ref, sg_ref, kl_ref, vl_ref, gn_ref, decay_ref, zeta_ref, xi_ref,
              cd_ref, o_ref, state_ref):
    n_chunks = q_ref.shape[0] // BLOCK
    decay = decay_ref[...]
    zeta = zeta_ref[...]
    xi = xi_ref[...]
    cd = cd_ref[...]
    gn = gn_ref[...]

    state_ref[...] = _dot_tn((kl_ref[...].astype(F32) * zeta).astype(BF16), vl_ref[...])

    def chunk(c, carry):
        rs = pl.multiple_of(c * BLOCK, BLOCK)
        q = q_ref[pl.ds(rs, BLOCK), :]
        k = k_ref[pl.ds(rs, BLOCK), :]
        v = v_ref[pl.ds(rs, BLOCK), :]
        state = state_ref[...]
        scores = (_dot_nt(q, k) * decay).astype(BF16)
        o = _dot(scores, v) + _dot((q.astype(F32) * xi).astype(BF16), state.astype(BF16))
        kv = _dot_tn((k.astype(F32) * zeta).astype(BF16), v)
        state_ref[...] = cd * state + kv
        mu = jnp.mean(o, axis=-1, keepdims=True)
        d = o - mu
        var = jnp.mean(d * d, axis=-1, keepdims=True)
        yn = d * lax.rsqrt(var + GN_EPS) * gn
        o_ref[pl.ds(rs, BLOCK), :] = (sg_ref[pl.ds(rs, BLOCK), :].astype(F32) * yn).astype(o_ref.dtype)
        return carry

    lax.fori_loop(0, n_chunks, chunk, 0, unroll=RET_UNROLL)


def _ret(rq, rk, rv, sg, rkl, rvl, gn, tables, batch, seq):
    decay, zeta, xi, cd = tables
    head_qk = pl.BlockSpec((None, seq, RET_DK), lambda b, h: (b, 0, h))
    head_v = pl.BlockSpec((None, seq, RET_DV), lambda b, h: (b, 0, h))
    table = pl.BlockSpec((None, BLOCK, BLOCK), lambda b, h: (h, 0, 0))
    return pl.pallas_call(
        _ret_body,
        grid=(batch, RET_HEADS),
        in_specs=[head_qk, head_qk, head_v, head_v,
                  pl.BlockSpec((BLOCK, RET_DK), lambda b, h: (0, h)),
                  pl.BlockSpec((BLOCK, RET_DV), lambda b, h: (0, h)),
                  pl.BlockSpec((1, RET_DV), lambda b, h: (0, h)),
                  table, table, table,
                  pl.BlockSpec((None, 1, RET_DV), lambda b, h: (h, 0, 0))],
        out_specs=head_v,
        out_shape=jax.ShapeDtypeStruct((batch, seq, D_MODEL), BF16),
        scratch_shapes=[pltpu.VMEM((RET_DK, RET_DV), F32)],
        compiler_params=_params(("arbitrary", "arbitrary")),
        name="ret",
    )(rq, rk, rv, sg, rkl, rvl, gn, decay, zeta, xi, cd)


def _out_body(ya_ref, yb_ref, ga_ref, gb_ref, h_ref, wa_ref, wb_ref, wo_ref, g_ref, win_ref,
              wout_ref, o_ref):
    mixed = (ga_ref[...].astype(F32) * _dot(ya_ref[...], wa_ref[...])
             + gb_ref[...].astype(F32) * _dot(yb_ref[...], wb_ref[...]))
    h = h_ref[...] + _dot(mixed.astype(BF16), wo_ref[...])
    xn = _rms_rows(h, g_ref[...]).astype(BF16)
    o_ref[...] = h + 0.5 * _swiglu(xn, win_ref, wout_ref)


def _out(ya, yb, ga, gb, h, consts, tm):
    rows = h.shape[0]
    row_spec = pl.BlockSpec((tm, D_MODEL), lambda i: (i, 0))
    return pl.pallas_call(
        _out_body,
        grid=(rows // tm,),
        in_specs=[row_spec] * 5 + [_const_spec(a.shape) for a in consts],
        out_specs=row_spec,
        out_shape=jax.ShapeDtypeStruct((rows, D_MODEL), F32),
        compiler_params=_params(("arbitrary",)),
        name="out",
    )(ya, yb, ga, gb, h, *consts)


def _ffn_weights(w_in, w_out):
    return w_in.astype(BF16), w_out.astype(BF16).reshape(N_FF_CHUNKS, FF_CHUNK, D_MODEL)


def _position_tables(seq):
    half = RET_DK // 2
    pos = np.arange(BLOCK + seq, dtype=np.float64) - N_EMPTY
    inv = ROPE_BASE ** (-np.arange(half, dtype=np.float64) / half)
    ang = pos[:, None] * inv[None, :]
    cos = np.concatenate([np.cos(ang), np.cos(ang)], axis=1)
    sin = np.concatenate([-np.sin(ang), np.sin(ang)], axis=1)
    return jnp.asarray(cos, dtype=F32), jnp.asarray(sin, dtype=F32)


def _retention_tables():
    log_gamma = np.log1p(-np.exp2(-5.0 - np.arange(RET_HEADS, dtype=np.float64)))
    n = np.arange(BLOCK, dtype=np.float64)
    diff = n[:, None] - n[None, :]
    lg = log_gamma[:, None, None]
    decay = np.where(diff >= 0, np.exp(lg * np.maximum(diff, 0.0)), 0.0)
    zeta = np.exp(log_gamma[:, None] * (BLOCK - 1 - n)[None, :])
    xi = np.exp(log_gamma[:, None] * (n + 1.0)[None, :])
    cd = np.exp(log_gamma * BLOCK)
    lanes = (RET_HEADS, BLOCK, RET_DK)
    tables = (decay, np.broadcast_to(zeta[:, :, None], lanes), np.broadcast_to(xi[:, :, None], lanes),
              np.broadcast_to(cd[:, None, None], (RET_HEADS, 1, RET_DV)))
    return tuple(jnp.asarray(t, dtype=F32) for t in tables)


def kernel(x, meta_tokens, norm_ffn1, w_ffn1_in, w_ffn1_out, norm_mix, w_in, b_forget, b_gate,
           fox_q_norm, fox_k_norm, w_o_fox, ret_gn, w_o_ret, w_out, norm_ffn2, w_ffn2_in,
           w_ffn2_out):
    batch, seq, d = x.shape
    assert d == D_MODEL and seq % ATT_BLOCK == 0 and seq % ROW_TILE == 0
    assert norm_ffn1.shape[0] == 1, "one layer"
    tiles_per_seq = seq // ROW_TILE

    win1, wout1 = _ffn_weights(w_ffn1_in[0], w_ffn1_out[0])
    win2, wout2 = _ffn_weights(w_ffn2_in[0], w_ffn2_out[0])
    wi = w_in[0].astype(BF16)
    o_ff = 3 * D_MODEL
    o_r = o_ff + FOX_HEADS
    o_g = o_r + 2 * RET_QK + 2 * D_MODEL
    wf = wi[:, :o_ff]
    wff = jnp.pad(wi[:, o_ff:o_r], ((0, 0), (0, V7X_LANES - FOX_HEADS)))
    wr = wi[:, o_r:o_g]
    wg = wi[:, o_g:]
    grp = np.arange(V7X_MXU_DIM) // FOX_HD
    gmat = jnp.asarray((grp[:, None] == grp[None, :]) / FOX_HD, dtype=BF16)
    gq = jnp.tile(fox_q_norm[0], FOX_HEADS)[None, :]
    gk = jnp.tile(fox_k_norm[0], FOX_HEADS)[None, :]
    bf = jnp.pad(b_forget[0], (0, V7X_LANES - FOX_HEADS))[None, :]
    bg = b_gate[0][None, :]
    proj_consts = (norm_mix, wf, wff, wr, wg, gmat, gq, gk, bf, bg)
    cos, sin = _position_tables(seq)
    idx = np.arange(ROW_TILE)
    tri_incl = jnp.asarray(idx[None, :] <= idx[:, None], dtype=BF16)
    idx = np.arange(BLOCK)
    tri_after = jnp.asarray(idx[None, :] > idx[:, None], dtype=BF16)

    lead = jnp.concatenate([jnp.zeros((N_EMPTY, d), x.dtype), meta_tokens.astype(x.dtype)], axis=0)
    xr = x.reshape(batch * seq, d)

    h1 = _ffn(xr, norm_ffn1, win1, wout1, ROW_TILE)
    h1l = _ffn(lead, norm_ffn1, win1, wout1, BLOCK)

    q, k, v, c, rq, rk, rv, sg, ga, gb = _proj(
        h1, proj_consts, cos[BLOCK:], sin[BLOCK:], tri_incl, ROW_TILE, False, tiles_per_seq)
    _, kl, vl, cl, _, rkl, rvl, _, _, _ = _proj(
        h1l, proj_consts, cos[:BLOCK], sin[:BLOCK], tri_after, BLOCK, True, 1)

    def b3(a):
        return a.reshape(batch, seq, a.shape[-1])

    ck = b3(c).transpose(0, 2, 1).reshape(batch, FOX_PAIRS, 2, seq)
    clt = cl.T.reshape(FOX_PAIRS, 2, BLOCK)
    bound = FOX_HD ** 0.5 * jnp.max(jnp.abs(fox_q_norm[0])) * jnp.max(jnp.abs(fox_k_norm[0]))
    stab = jnp.stack([(bound <= SAFE_LOGIT_BOUND).astype(F32), bound * LOG2E])
    ya = _fox(stab, b3(q), b3(c), b3(k), b3(v), ck, kl, vl, clt, batch, seq)
    yb = _ret(b3(rq), b3(rk), b3(rv), b3(sg), rkl, rvl, ret_gn, _retention_tables(), batch, seq)

    out_consts = (w_o_fox[0].astype(BF16), w_o_ret[0].astype(BF16), w_out[0].astype(BF16),
                  norm_ffn2, win2, wout2)
    out = _out(ya.reshape(batch * seq, d), yb.reshape(batch * seq, d), ga, gb, h1, out_consts,
               ROW_TILE)
    return out.reshape(batch, seq, d)
```

```python
import functools
import math

import jax
import jax.numpy as jnp
import numpy as np
from jax import lax
from jax.experimental import pallas as pl
from jax.experimental.pallas import tpu as pltpu

F32 = jnp.float32
BF16 = jnp.bfloat16

D_MODEL = 1024
N_META = 16
BLOCK = 128
N_EMPTY = BLOCK - N_META
FOX_HD = 64
FOX_HEADS = D_MODEL // FOX_HD
FOX_PAIRS = FOX_HEADS // 2
RET_HEADS = 4
RET_DK = D_MODEL // (2 * RET_HEADS)
RET_DV = 2 * RET_DK
RET_QK = RET_HEADS * RET_DK
D_FF = ((8 * D_MODEL // 3 + 127) // 128) * 128
EPS = 1e-6
GN_EPS = 1e-5
ROPE_BASE = 10000.0
NEG = -1e30
LOG2E = math.log2(math.e)
SAFE_LOGIT_BOUND = 30.0

V7X_LANES = 128
V7X_MXU_DIM = 256
V7X_VMEM_BYTES = 64 * 2**20
VMEM_LIMIT = V7X_VMEM_BYTES - 8 * 2**20

ROW_TILE = 512
FF_CHUNK = V7X_MXU_DIM
N_FF_CHUNKS = D_FF // FF_CHUNK
RET_UNROLL = 16
ATT_BLOCK = 1024


def _dot(a, b):
    return jnp.dot(a, b, preferred_element_type=F32)


def _dot_nt(a, b):
    return lax.dot_general(a, b, (((1,), (1,)), ((), ())), preferred_element_type=F32)


def _dot_tn(a, b):
    return lax.dot_general(a, b, (((0,), (0,)), ((), ())), preferred_element_type=F32)


def _const_spec(shape):
    zeros = (0,) * len(shape)
    return pl.BlockSpec(shape, lambda *_: zeros, pipeline_mode=pl.Buffered(1))


def _params(semantics):
    return pltpu.CompilerParams(dimension_semantics=semantics, vmem_limit_bytes=VMEM_LIMIT)


def _rms_rows(x, g):
    ms = jnp.mean(x * x, axis=-1, keepdims=True)
    return x * lax.rsqrt(ms + EPS) * g


def _swiglu(xn, win_ref, wout_ref):
    acc = None
    for c in range(N_FF_CHUNKS):
        a = _dot(xn, win_ref[:, c * FF_CHUNK:(c + 1) * FF_CHUNK])
        b = _dot(xn, win_ref[:, D_FF + c * FF_CHUNK:D_FF + (c + 1) * FF_CHUNK])
        hm = (a * jax.nn.sigmoid(a) * b).astype(BF16)
        d = _dot(hm, wout_ref[c])
        acc = d if acc is None else acc + d
    return acc


def _ffn_body(x_ref, g_ref, win_ref, wout_ref, o_ref):
    x = x_ref[...]
    xn = _rms_rows(x, g_ref[...]).astype(BF16)
    o_ref[...] = x + 0.5 * _swiglu(xn, win_ref, wout_ref)


def _ffn(x, g, win, wout, tm):
    rows = x.shape[0]
    row_spec = pl.BlockSpec((tm, D_MODEL), lambda i: (i, 0))
    return pl.pallas_call(
        _ffn_body,
        grid=(rows // tm,),
        in_specs=[row_spec, _const_spec(g.shape), _const_spec(win.shape), _const_spec(wout.shape)],
        out_specs=row_spec,
        out_shape=jax.ShapeDtypeStruct((rows, D_MODEL), F32),
        compiler_params=_params(("arbitrary",)),
        name="ffn",
    )(x, g, win, wout)


def _group_mean_sq(x, gmat_ref):
    sq = (x * x).astype(BF16)
    w = V7X_MXU_DIM
    parts = [_dot(sq[:, i * w:(i + 1) * w], gmat_ref[...]) for i in range(D_MODEL // w)]
    return jnp.concatenate(parts, axis=1)


def _rotary_tile(x, cos, sin_signed):
    return x * cos + pltpu.roll(x, RET_DK // 2, axis=1) * sin_signed


def _proj_body(h_ref, gmix_ref, wf_ref, wff_ref, wr_ref, wg_ref, gmat_ref, gq_ref, gk_ref,
               bf_ref, bg_ref, cos_ref, sin_ref, tri_ref,
               q_ref, k_ref, v_ref, c_ref, rq_ref, rk_ref, rv_ref, sg_ref, ga_ref, gb_ref,
               carry_ref, *, lead, tiles_per_seq):
    tm = h_ref.shape[0]
    u = _rms_rows(h_ref[...], gmix_ref[...]).astype(BF16)
    if lead:
        row = lax.broadcasted_iota(jnp.int32, (tm, 1), 0)
        valid = row >= N_EMPTY
        vmask = valid.astype(F32)

    fq = _dot(u, wf_ref[:, :D_MODEL])
    qn = fq * lax.rsqrt(_group_mean_sq(fq, gmat_ref) + EPS) * gq_ref[...]
    q_ref[...] = (qn * (FOX_HD ** -0.5 * LOG2E)).astype(BF16)
    fk = _dot(u, wf_ref[:, D_MODEL:2 * D_MODEL])
    kn = fk * lax.rsqrt(_group_mean_sq(fk, gmat_ref) + EPS) * gk_ref[...]
    k_ref[...] = kn.astype(BF16)
    fv = _dot(u, wf_ref[:, 2 * D_MODEL:]).astype(BF16)
    ones = jnp.ones((tm, V7X_LANES), BF16)
    for j in range(FOX_PAIRS):
        v_ref[:, 2 * j * V7X_LANES:(2 * j + 1) * V7X_LANES] = fv[:, j * V7X_LANES:(j + 1) * V7X_LANES]
        v_ref[:, (2 * j + 1) * V7X_LANES:(2 * j + 2) * V7X_LANES] = ones

    z = _dot(u, wff_ref[...]) + bf_ref[...]
    logf = -(jnp.maximum(-z, 0.0) + jnp.log1p(jnp.exp(-jnp.abs(z))))
    if lead:
        logf = jnp.where(valid, logf, 0.0)
    lane = lax.broadcasted_iota(jnp.int32, (1, V7X_LANES), 1)
    logf = jnp.where(lane < FOX_HEADS, logf, 0.0)
    p1 = logf.astype(BF16).astype(F32)
    r1 = logf - p1
    p2 = r1.astype(BF16).astype(F32)
    p3 = (r1 - p2).astype(BF16).astype(F32)
    packed = p1 + pltpu.roll(p2, FOX_HEADS, axis=1) + pltpu.roll(p3, 2 * FOX_HEADS, axis=1)
    r = _dot(tri_ref[...], packed.astype(BF16))
    cum = r + (pltpu.roll(r, V7X_LANES - FOX_HEADS, axis=1)
               + pltpu.roll(r, V7X_LANES - 2 * FOX_HEADS, axis=1))
    cum = jnp.where(lane < FOX_HEADS, cum, 0.0)
    if lead:
        c = -cum
    else:
        @pl.when(pl.program_id(0) % tiles_per_seq == 0)
        def _():
            carry_ref[...] = jnp.zeros_like(carry_ref)
        c = cum + carry_ref[...]
        carry_ref[...] = c[tm - 1:tm, :]
    c_ref[...] = (c * LOG2E)[:, :FOX_HEADS]

    cos = cos_ref[...]
    sin = sin_ref[...]
    rq = _dot(u, wr_ref[:, 0:RET_QK])
    rk = _dot(u, wr_ref[:, RET_QK:2 * RET_QK])
    for hh in range(RET_HEADS):
        sl = slice(hh * RET_DK, (hh + 1) * RET_DK)
        rq_ref[:, sl] = _rotary_tile(rq[:, sl], cos, sin).astype(BF16)
        kt = _rotary_tile(rk[:, sl], cos, sin) * (RET_DK ** -0.5)
        if lead:
            kt = kt * vmask
        rk_ref[:, sl] = kt.astype(BF16)
    rv = _dot(u, wr_ref[:, 2 * RET_QK:2 * RET_QK + D_MODEL])
    if lead:
        rv = rv * vmask
    rv_ref[...] = rv.astype(BF16)
    rg = _dot(u, wr_ref[:, 2 * RET_QK + D_MODEL:])
    sg_ref[...] = (rg * jax.nn.sigmoid(rg)).astype(BF16)

    bg = bg_ref[...]
    ga_ref[...] = jax.nn.sigmoid(_dot(u, wg_ref[:, :D_MODEL]) + bg[:, :D_MODEL]).astype(BF16)
    gb_ref[...] = jax.nn.sigmoid(_dot(u, wg_ref[:, D_MODEL:]) + bg[:, D_MODEL:]).astype(BF16)


def _proj(h, consts, cos, sin, tri, tm, lead, tiles_per_seq):
    rows = h.shape[0]
    grid = rows // tm

    def rows_spec(width):
        return pl.BlockSpec((tm, width), lambda i: (i, 0))

    pos_spec = pl.BlockSpec((tm, RET_DK), lambda i: (i % tiles_per_seq, 0))
    out_widths = [D_MODEL, D_MODEL, 2 * D_MODEL, FOX_HEADS, RET_QK, RET_QK, D_MODEL, D_MODEL,
                  D_MODEL, D_MODEL]
    out_dtypes = [BF16, BF16, BF16, F32, BF16, BF16, BF16, BF16, BF16, BF16]
    return pl.pallas_call(
        functools.partial(_proj_body, lead=lead, tiles_per_seq=tiles_per_seq),
        grid=(grid,),
        in_specs=[rows_spec(D_MODEL)] + [_const_spec(a.shape) for a in consts]
        + [pos_spec, pos_spec, _const_spec(tri.shape)],
        out_specs=[rows_spec(w) for w in out_widths],
        out_shape=[jax.ShapeDtypeStruct((rows, w), dt) for w, dt in zip(out_widths, out_dtypes)],
        scratch_shapes=[pltpu.VMEM((1, V7X_LANES), F32)],
        compiler_params=_params(("arbitrary",)),
        name="proj_lead" if lead else "proj",
    )(h, *consts, cos, sin, tri)


def _fox_body(stab_ref, q_ref, cq_ref, k_ref, v_ref, ck_ref, kl_ref, vl_ref, cl_ref, o_ref,
              m_ref, acc_ref, e_ref):
    tq = q_ref.shape[0]
    tk = tq // 2
    n_full = 2 * pl.program_id(2)
    lane = lax.broadcasted_iota(jnp.int32, (1, 2 * V7X_LANES), 1)
    first2 = (lane % V7X_LANES) < FOX_HD
    first = first2[:, :V7X_LANES]
    q = q_ref[...]
    zero = jnp.zeros_like(q)
    q_heads = (jnp.where(first, q, zero), jnp.where(first, zero, q))
    lead_ok = lax.broadcasted_iota(jnp.int32, (tq, BLOCK), 1) >= N_EMPTY
    row = lax.broadcasted_iota(jnp.int32, (tq, tk), 0)
    col = lax.broadcasted_iota(jnp.int32, (tq, tk), 1)
    causal = col <= row
    causal_half = causal[:tk]
    acc_ref[...] = jnp.zeros_like(acc_ref)

    def keys(j):
        ks = pl.multiple_of(j * tk, tk)
        return k_ref[pl.ds(ks, tk), :], ck_ref[:, pl.ds(ks, tk)]

    def values(j):
        return v_ref[pl.ds(pl.multiple_of(j * tk, tk), tk), :]

    use_bound = stab_ref[0] > 0.5

    @pl.when(use_bound)
    def _():
        head = lax.broadcasted_iota(jnp.int32, (1, FOX_HEADS), 1) - 2 * pl.program_id(1)
        cq_blk = cq_ref[...]
        cq = [jnp.sum(jnp.where(head == hh, cq_blk, 0.0), axis=1, keepdims=True) - stab_ref[1]
              for hh in range(2)]

        def exponents(k, ck, r0=0):
            return [_dot_nt(q_heads[hh][r0:], k) + cq[hh][r0:] - ck[hh:hh + 1, :] for hh in range(2)]

        def accumulate(e, v, allowed, r0=0):
            if allowed is not None:
                e = [jnp.where(allowed, x, NEG) for x in e]
            pvs = [_dot(jnp.exp2(x).astype(BF16), v) for x in e]
            acc_ref[r0:, :] += jnp.where(first2, pvs[0], pvs[1])

        e_lead = exponents(kl_ref[...], cl_ref[...])
        e0 = exponents(*keys(0))
        e_ref[0] = e0[0]
        e_ref[1] = e0[1]
        accumulate(e_lead, vl_ref[...], lead_ok)

        def step(j, carry):
            e_cur = [e_ref[0], e_ref[1]]
            e_next = exponents(*keys(j + 1))
            accumulate(e_cur, values(j), None)
            e_ref[0] = e_next[0]
            e_ref[1] = e_next[1]
            return carry

        def step_pair(m, carry):
            return step(2 * m + 1, step(2 * m, carry))

        lax.fori_loop(0, pl.program_id(2), step_pair, 0)
        e_last = exponents(*keys(n_full + 1), r0=tk)
        accumulate([e_ref[0], e_ref[1]], values(n_full), causal)
        accumulate(e_last, values(n_full + 1), causal_half, r0=tk)

    @pl.when(jnp.logical_not(use_bound))
    def _():
        m_ref[...] = jnp.full_like(m_ref, NEG)

        def block(k, v, ck, allowed):
            pvs, alphas = [], []
            for hh in range(2):
                s = _dot_nt(q_heads[hh], k) - ck[hh:hh + 1, :]
                if allowed is not None:
                    s = jnp.where(allowed, s, NEG)
                m_old = m_ref[hh]
                m_new = jnp.maximum(m_old, jnp.max(s, axis=1, keepdims=True))
                m_ref[hh] = m_new
                pvs.append(_dot(jnp.exp2(s - m_new).astype(BF16), v))
                alphas.append(jnp.exp2(m_old - m_new))
            acc_ref[...] = (acc_ref[...] * jnp.where(first2, alphas[0], alphas[1])
                            + jnp.where(first2, pvs[0], pvs[1]))

        block(kl_ref[...], vl_ref[...], cl_ref[...], lead_ok)

        def full_block(j, carry):
            block(keys(j)[0], values(j), keys(j)[1], None)
            return carry

        lax.fori_loop(0, n_full, full_block, 0)
        block(keys(n_full)[0], values(n_full), keys(n_full)[1], causal)
        block(keys(n_full + 1)[0], values(n_full + 1), keys(n_full + 1)[1], col + tk <= row)

    acc = acc_ref[...]
    o_ref[...] = (acc[:, :V7X_LANES] / acc[:, V7X_LANES:]).astype(o_ref.dtype)


def _fox(stab, q, cq, k, v, ck, kl, vl, cl, batch, seq):
    tq = ATT_BLOCK
    w = V7X_LANES
    return pl.pallas_call(
        _fox_body,
        grid=(batch, FOX_PAIRS, seq // tq),
        in_specs=[
            pl.BlockSpec(memory_space=pltpu.SMEM),
            pl.BlockSpec((None, tq, w), lambda b, j, i: (b, i, j)),
            pl.BlockSpec((None, tq, FOX_HEADS), lambda b, j, i: (b, i, 0)),
            pl.BlockSpec((None, seq, w), lambda b, j, i: (b, 0, j)),
            pl.BlockSpec((None, seq, 2 * w), lambda b, j, i: (b, 0, j)),
            pl.BlockSpec((None, None, 2, seq), lambda b, j, i: (b, j, 0, 0)),
            pl.BlockSpec((BLOCK, w), lambda b, j, i: (0, j)),
            pl.BlockSpec((BLOCK, 2 * w), lambda b, j, i: (0, j)),
            pl.BlockSpec((None, 2, BLOCK), lambda b, j, i: (j, 0, 0)),
        ],
        out_specs=pl.BlockSpec((None, tq, w), lambda b, j, i: (b, i, j)),
        out_shape=jax.ShapeDtypeStruct((batch, seq, D_MODEL), BF16),
        scratch_shapes=[pltpu.VMEM((2, tq, 1), F32), pltpu.VMEM((tq, 2 * w), F32),
                        pltpu.VMEM((2, tq, tq // 2), F32)],
        compiler_params=_params(("arbitrary", "arbitrary", "arbitrary")),
        name="fox",
    )(stab, q, cq, k, v, ck, kl, vl, cl)


def _ret_body(q_ref, k_ref, v_ref, sg_ref, kl_ref, vl_ref, gn_ref, decay_ref, zeta_ref, xi_ref,
              cd_ref, o_ref, state_ref):
    n_chunks = q_ref.shape[0] // BLOCK
    decay = decay_ref[...]
    zeta = zeta_ref[...]
    xi = xi_ref[...]
    cd = cd_ref[...]
    gn = gn_ref[...]

    state_ref[...] = _dot_tn((kl_ref[...].astype(F32) * zeta).astype(BF16), vl_ref[...])

    def chunk(c, carry):
        rs = pl.multiple_of(c * BLOCK, BLOCK)
        q = q_ref[pl.ds(rs, BLOCK), :]
        k = k_ref[pl.ds(rs, BLOCK), :]
        v = v_ref[pl.ds(rs, BLOCK), :]
        state = state_ref[...]
        scores = (_dot_nt(q, k) * decay).astype(BF16)
        o = _dot(scores, v) + _dot((q.astype(F32) * xi).astype(BF16), state.astype(BF16))
        kv = _dot_tn((k.astype(F32) * zeta).astype(BF16), v)
        state_ref[...] = cd * state + kv
        mu = jnp.mean(o, axis=-1, keepdims=True)
        d = o - mu
        var = jnp.mean(d * d, axis=-1, keepdims=True)
        yn = d * lax.rsqrt(var + GN_EPS) * gn
        o_ref[pl.ds(rs, BLOCK), :] = (sg_ref[pl.ds(rs, BLOCK), :].astype(F32) * yn).astype(o_ref.dtype)
        return carry

    lax.fori_loop(0, n_chunks, chunk, 0, unroll=RET_UNROLL)


def _ret(rq, rk, rv, sg, rkl, rvl, gn, tables, batch, seq):
    decay, zeta, xi, cd = tables
    head_qk = pl.BlockSpec((None, seq, RET_DK), lambda b, h: (b, 0, h))
    head_v = pl.BlockSpec((None, seq, RET_DV), lambda b, h: (b, 0, h))
    table = pl.BlockSpec((None, BLOCK, BLOCK), lambda b, h: (h, 0, 0))
    return pl.pallas_call(
        _ret_body,
        grid=(batch, RET_HEADS),
        in_specs=[head_qk, head_qk, head_v, head_v,
                  pl.BlockSpec((BLOCK, RET_DK), lambda b, h: (0, h)),
                  pl.BlockSpec((BLOCK, RET_DV), lambda b, h: (0, h)),
                  pl.BlockSpec((1, RET_DV), lambda b, h: (0, h)),
                  table, table, table,
                  pl.BlockSpec((None, 1, RET_DV), lambda b, h: (h, 0, 0))],
        out_specs=head_v,
        out_shape=jax.ShapeDtypeStruct((batch, seq, D_MODEL), BF16),
        scratch_shapes=[pltpu.VMEM((RET_DK, RET_DV), F32)],
        compiler_params=_params(("arbitrary", "arbitrary")),
        name="ret",
    )(rq, rk, rv, sg, rkl, rvl, gn, decay, zeta, xi, cd)


def _out_body(ya_ref, yb_ref, ga_ref, gb_ref, h_ref, wa_ref, wb_ref, wo_ref, g_ref, win_ref,
              wout_ref, o_ref):
    mixed = (ga_ref[...].astype(F32) * _dot(ya_ref[...], wa_ref[...])
             + gb_ref[...].astype(F32) * _dot(yb_ref[...], wb_ref[...]))
    h = h_ref[...] + _dot(mixed.astype(BF16), wo_ref[...])
    xn = _rms_rows(h, g_ref[...]).astype(BF16)
    o_ref[...] = h + 0.5 * _swiglu(xn, win_ref, wout_ref)


def _out(ya, yb, ga, gb, h, consts, tm):
    rows = h.shape[0]
    row_spec = pl.BlockSpec((tm, D_MODEL), lambda i: (i, 0))
    return pl.pallas_call(
        _out_body,
        grid=(rows // tm,),
        in_specs=[row_spec] * 5 + [_const_spec(a.shape) for a in consts],
        out_specs=row_spec,
        out_shape=jax.ShapeDtypeStruct((rows, D_MODEL), F32),
        compiler_params=_params(("arbitrary",)),
        name="out",
    )(ya, yb, ga, gb, h, *consts)


def _ffn_weights(w_in, w_out):
    return w_in.astype(BF16), w_out.astype(BF16).reshape(N_FF_CHUNKS, FF_CHUNK, D_MODEL)


def _position_tables(seq):
    half = RET_DK // 2
    pos = np.arange(BLOCK + seq, dtype=np.float64) - N_EMPTY
    inv = ROPE_BASE ** (-np.arange(half, dtype=np.float64) / half)
    ang = pos[:, None] * inv[None, :]
    cos = np.concatenate([np.cos(ang), np.cos(ang)], axis=1)
    sin = np.concatenate([-np.sin(ang), np.sin(ang)], axis=1)
    return jnp.asarray(cos, dtype=F32), jnp.asarray(sin, dtype=F32)


def _retention_tables():
    log_gamma = np.log1p(-np.exp2(-5.0 - np.arange(RET_HEADS, dtype=np.float64)))
    n = np.arange(BLOCK, dtype=np.float64)
    diff = n[:, None] - n[None, :]
    lg = log_gamma[:, None, None]
    decay = np.where(diff >= 0, np.exp(lg * np.maximum(diff, 0.0)), 0.0)
    zeta = np.exp(log_gamma[:, None] * (BLOCK - 1 - n)[None, :])
    xi = np.exp(log_gamma[:, None] * (n + 1.0)[None, :])
    cd = np.exp(log_gamma * BLOCK)
    lanes = (RET_HEADS, BLOCK, RET_DK)
    tables = (decay, np.broadcast_to(zeta[:, :, None], lanes), np.broadcast_to(xi[:, :, None], lanes),
              np.broadcast_to(cd[:, None, None], (RET_HEADS, 1, RET_DV)))
    return tuple(jnp.asarray(t, dtype=F32) for t in tables)


def kernel(x, meta_tokens, norm_ffn1, w_ffn1_in, w_ffn1_out, norm_mix, w_in, b_forget, b_gate,
           fox_q_norm, fox_k_norm, w_o_fox, ret_gn, w_o_ret, w_out, norm_ffn2, w_ffn2_in,
           w_ffn2_out):
    batch, seq, d = x.shape
    assert d == D_MODEL and seq % ATT_BLOCK == 0 and seq % ROW_TILE == 0
    assert norm_ffn1.shape[0] == 1, "one layer"
    tiles_per_seq = seq // ROW_TILE

    win1, wout1 = _ffn_weights(w_ffn1_in[0], w_ffn1_out[0])
    win2, wout2 = _ffn_weights(w_ffn2_in[0], w_ffn2_out[0])
    wi = w_in[0]
    o_ff = 3 * D_MODEL
    o_r = o_ff + FOX_HEADS
    o_g = o_r + 2 * RET_QK + 2 * D_MODEL
    wf = wi[:, :o_ff].astype(BF16)
    wff = jnp.pad(wi[:, o_ff:o_r].astype(BF16), ((0, 0), (0, V7X_LANES - FOX_HEADS)))
    wr = wi[:, o_r:o_g].astype(BF16)
    wg = wi[:, o_g:].astype(BF16)
    grp = np.arange(V7X_MXU_DIM) // FOX_HD
    gmat = jnp.asarray((grp[:, None] == grp[None, :]) / FOX_HD, dtype=BF16)
    gq = jnp.tile(fox_q_norm[0], FOX_HEADS)[None, :]
    gk = jnp.tile(fox_k_norm[0], FOX_HEADS)[None, :]
    bf = jnp.pad(b_forget[0], (0, V7X_LANES - FOX_HEADS))[None, :]
    bg = b_gate[0][None, :]
    proj_consts = (norm_mix, wf, wff, wr, wg, gmat, gq, gk, bf, bg)
    cos, sin = _position_tables(seq)
    idx = np.arange(ROW_TILE)
    tri_incl = jnp.asarray(idx[None, :] <= idx[:, None], dtype=BF16)
    idx = np.arange(BLOCK)
    tri_after = jnp.asarray(idx[None, :] > idx[:, None], dtype=BF16)

    lead = jnp.concatenate([jnp.zeros((N_EMPTY, d), x.dtype), meta_tokens.astype(x.dtype)], axis=0)
    xr = x.reshape(batch * seq, d)

    h1 = _ffn(xr, norm_ffn1, win1, wout1, ROW_TILE)
    h1l = _ffn(lead, norm_ffn1, win1, wout1, BLOCK)

    q, k, v, c, rq, rk, rv, sg, ga, gb = _proj(
        h1, proj_consts, cos[BLOCK:], sin[BLOCK:], tri_incl, ROW_TILE, False, tiles_per_seq)
    _, kl, vl, cl, _, rkl, rvl, _, _, _ = _proj(
        h1l, proj_consts, cos[:BLOCK], sin[:BLOCK], tri_after, BLOCK, True, 1)

    def b3(a):
        return a.reshape(batch, seq, a.shape[-1])

    ck = b3(c).transpose(0, 2, 1).reshape(batch, FOX_PAIRS, 2, seq)
    clt = cl.T.reshape(FOX_PAIRS, 2, BLOCK)
    bound = FOX_HD ** 0.5 * jnp.max(jnp.abs(fox_q_norm[0])) * jnp.max(jnp.abs(fox_k_norm[0]))
    stab = jnp.stack([(bound <= SAFE_LOGIT_BOUND).astype(F32), bound * LOG2E])
    ya = _fox(stab, b3(q), b3(c), b3(k), b3(v), ck, kl, vl, clt, batch, seq)
    yb = _ret(b3(rq), b3(rk), b3(rv), b3(sg), rkl, rvl, ret_gn, _retention_tables(), batch, seq)

    out_consts = (w_o_fox[0].astype(BF16), w_o_ret[0].astype(BF16), w_out[0].astype(BF16),
                  norm_ffn2, win2, wout2)
    out = _out(ya.reshape(batch * seq, d), yb.reshape(batch * seq, d), ga, gb, h1, out_consts,
               ROW_TILE)
    return out.reshape(batch, seq, d)
```

```python
import functools
import math

import jax
import jax.numpy as jnp
import numpy as np
from jax import lax
from jax.experimental import pallas as pl
from jax.experimental.pallas import tpu as pltpu

F32 = jnp.float32
BF16 = jnp.bfloat16

D_MODEL = 1024
N_META = 16
BLOCK = 128
N_EMPTY = BLOCK - N_META
FOX_HD = 64
FOX_HEADS = D_MODEL // FOX_HD
FOX_PAIRS = FOX_HEADS // 2
RET_HEADS = 4
RET_DK = D_MODEL // (2 * RET_HEADS)
RET_DV = 2 * RET_DK
RET_QK = RET_HEADS * RET_DK
D_FF = ((8 * D_MODEL // 3 + 127) // 128) * 128
EPS = 1e-6
GN_EPS = 1e-5
ROPE_BASE = 10000.0
NEG = -1e30
LOG2E = math.log2(math.e)
SAFE_LOGIT_BOUND = 30.0

V7X_LANES = 128
V7X_MXU_DIM = 256
V7X_VMEM_BYTES = 64 * 2**20
VMEM_LIMIT = V7X_VMEM_BYTES - 8 * 2**20

ROW_TILE = 512
FF_CHUNK = V7X_MXU_DIM
N_FF_CHUNKS = D_FF // FF_CHUNK
RET_UNROLL = 16
ATT_BLOCK = 1024


def _dot(a, b):
    return jnp.dot(a, b, preferred_element_type=F32)


def _dot_nt(a, b):
    return lax.dot_general(a, b, (((1,), (1,)), ((), ())), preferred_element_type=F32)


def _dot_tn(a, b):
    return lax.dot_general(a, b, (((0,), (0,)), ((), ())), preferred_element_type=F32)


def _const_spec(shape):
    zeros = (0,) * len(shape)
    return pl.BlockSpec(shape, lambda *_: zeros, pipeline_mode=pl.Buffered(1))


def _params(semantics):
    return pltpu.CompilerParams(dimension_semantics=semantics, vmem_limit_bytes=VMEM_LIMIT)


def _rms_rows(x, g):
    ms = jnp.mean(x * x, axis=-1, keepdims=True)
    return x * lax.rsqrt(ms + EPS) * g


def _swiglu(xn, win_ref, wout_ref):
    acc = None
    for c in range(N_FF_CHUNKS):
        a = _dot(xn, win_ref[:, c * FF_CHUNK:(c + 1) * FF_CHUNK].astype(BF16))
        b = _dot(xn, win_ref[:, D_FF + c * FF_CHUNK:D_FF + (c + 1) * FF_CHUNK].astype(BF16))
        hm = (a * jax.nn.sigmoid(a) * b).astype(BF16)
        d = _dot(hm, wout_ref[c].astype(BF16))
        acc = d if acc is None else acc + d
    return acc


def _ffn_body(x_ref, g_ref, win_ref, wout_ref, o_ref):
    x = x_ref[...]
    xn = _rms_rows(x, g_ref[...]).astype(BF16)
    o_ref[...] = x + 0.5 * _swiglu(xn, win_ref, wout_ref)


def _ffn(x, g, win, wout, tm):
    rows = x.shape[0]
    row_spec = pl.BlockSpec((tm, D_MODEL), lambda i: (i, 0))
    return pl.pallas_call(
        _ffn_body,
        grid=(rows // tm,),
        in_specs=[row_spec, _const_spec(g.shape), _const_spec(win.shape), _const_spec(wout.shape)],
        out_specs=row_spec,
        out_shape=jax.ShapeDtypeStruct((rows, D_MODEL), F32),
        compiler_params=_params(("arbitrary",)),
        name="ffn",
    )(x, g, win, wout)


def _group_mean_sq(x, gmat_ref):
    sq = (x * x).astype(BF16)
    w = V7X_MXU_DIM
    parts = [_dot(sq[:, i * w:(i + 1) * w], gmat_ref[...]) for i in range(D_MODEL // w)]
    return jnp.concatenate(parts, axis=1)


def _rotary_tile(x, cos, sin_signed):
    return x * cos + pltpu.roll(x, RET_DK // 2, axis=1) * sin_signed


COL_FQ = 0
COL_FK = COL_FQ + D_MODEL
COL_FV = COL_FK + D_MODEL
COL_RQ = COL_FV + D_MODEL
COL_RK = COL_RQ + RET_QK
COL_RV = COL_RK + RET_QK
COL_RG = COL_RV + D_MODEL
COL_GA = COL_RG + D_MODEL
COL_GB = COL_GA + D_MODEL
COL_FF = COL_GB + D_MODEL
COL_END = COL_FF + V7X_LANES


def _proj_body(h_ref, gmix_ref, w_ref, gmat_ref, gq_ref, gk_ref,
               bf_ref, bg_ref, cos_ref, sin_ref, tri_ref,
               q_ref, k_ref, v_ref, c_ref, rq_ref, rk_ref, rv_ref, sg_ref, ga_ref, gb_ref,
               carry_ref, *, lead, tiles_per_seq):
    tm = h_ref.shape[0]
    u = _rms_rows(h_ref[...], gmix_ref[...]).astype(BF16)
    if lead:
        row = lax.broadcasted_iota(jnp.int32, (tm, 1), 0)
        valid = row >= N_EMPTY
        vmask = valid.astype(F32)

    fq = _dot(u, w_ref[:, COL_FQ:COL_FK])
    qn = fq * lax.rsqrt(_group_mean_sq(fq, gmat_ref) + EPS) * gq_ref[...]
    q_ref[...] = (qn * (FOX_HD ** -0.5 * LOG2E)).astype(BF16)
    fk = _dot(u, w_ref[:, COL_FK:COL_FV])
    kn = fk * lax.rsqrt(_group_mean_sq(fk, gmat_ref) + EPS) * gk_ref[...]
    k_ref[...] = kn.astype(BF16)
    fv = _dot(u, w_ref[:, COL_FV:COL_RQ]).astype(BF16)
    ones = jnp.ones((tm, V7X_LANES), BF16)
    for j in range(FOX_PAIRS):
        v_ref[:, 2 * j * V7X_LANES:(2 * j + 1) * V7X_LANES] = fv[:, j * V7X_LANES:(j + 1) * V7X_LANES]
        v_ref[:, (2 * j + 1) * V7X_LANES:(2 * j + 2) * V7X_LANES] = ones

    z = _dot(u, w_ref[:, COL_FF:]) + bf_ref[...]
    logf = -(jnp.maximum(-z, 0.0) + jnp.log1p(jnp.exp(-jnp.abs(z))))
    if lead:
        logf = jnp.where(valid, logf, 0.0)
    lane = lax.broadcasted_iota(jnp.int32, (1, V7X_LANES), 1)
    logf = jnp.where(lane < FOX_HEADS, logf, 0.0)
    p1 = logf.astype(BF16).astype(F32)
    r1 = logf - p1
    p2 = r1.astype(BF16).astype(F32)
    p3 = (r1 - p2).astype(BF16).astype(F32)
    packed = p1 + pltpu.roll(p2, FOX_HEADS, axis=1) + pltpu.roll(p3, 2 * FOX_HEADS, axis=1)
    r = _dot(tri_ref[...], packed.astype(BF16))
    cum = r + (pltpu.roll(r, V7X_LANES - FOX_HEADS, axis=1)
               + pltpu.roll(r, V7X_LANES - 2 * FOX_HEADS, axis=1))
    cum = jnp.where(lane < FOX_HEADS, cum, 0.0)
    if lead:
        c = -cum
    else:
        @pl.when(pl.program_id(0) % tiles_per_seq == 0)
        def _():
            carry_ref[...] = jnp.zeros_like(carry_ref)
        c = cum + carry_ref[...]
        carry_ref[...] = c[tm - 1:tm, :]
    c_ref[...] = (c * LOG2E)[:, :FOX_HEADS]

    cos = cos_ref[...]
    sin = sin_ref[...]
    rq = _dot(u, w_ref[:, COL_RQ:COL_RK])
    rk = _dot(u, w_ref[:, COL_RK:COL_RV])
    for hh in range(RET_HEADS):
        sl = slice(hh * RET_DK, (hh + 1) * RET_DK)
        rq_ref[:, sl] = _rotary_tile(rq[:, sl], cos, sin).astype(BF16)
        kt = _rotary_tile(rk[:, sl], cos, sin) * (RET_DK ** -0.5)
        if lead:
            kt = kt * vmask
        rk_ref[:, sl] = kt.astype(BF16)
    rv = _dot(u, w_ref[:, COL_RV:COL_RG])
    if lead:
        rv = rv * vmask
    rv_ref[...] = rv.astype(BF16)
    rg = _dot(u, w_ref[:, COL_RG:COL_GA])
    sg_ref[...] = (rg * jax.nn.sigmoid(rg)).astype(BF16)

    bg = bg_ref[...]
    ga_ref[...] = jax.nn.sigmoid(_dot(u, w_ref[:, COL_GA:COL_GB]) + bg[:, :D_MODEL]).astype(BF16)
    gb_ref[...] = jax.nn.sigmoid(_dot(u, w_ref[:, COL_GB:COL_FF]) + bg[:, D_MODEL:]).astype(BF16)


def _proj(h, consts, cos, sin, tri, tm, lead, tiles_per_seq):
    rows = h.shape[0]
    grid = rows // tm

    def rows_spec(width):
        return pl.BlockSpec((tm, width), lambda i: (i, 0))

    pos_spec = pl.BlockSpec((tm, RET_DK), lambda i: (i % tiles_per_seq, 0))
    out_widths = [D_MODEL, D_MODEL, 2 * D_MODEL, FOX_HEADS, RET_QK, RET_QK, D_MODEL, D_MODEL,
                  D_MODEL, D_MODEL]
    out_dtypes = [BF16, BF16, BF16, F32, BF16, BF16, BF16, BF16, BF16, BF16]
    return pl.pallas_call(
        functools.partial(_proj_body, lead=lead, tiles_per_seq=tiles_per_seq),
        grid=(grid,),
        in_specs=[rows_spec(D_MODEL)] + [_const_spec(a.shape) for a in consts]
        + [pos_spec, pos_spec, _const_spec(tri.shape)],
        out_specs=[rows_spec(w) for w in out_widths],
        out_shape=[jax.ShapeDtypeStruct((rows, w), dt) for w, dt in zip(out_widths, out_dtypes)],
        scratch_shapes=[pltpu.VMEM((1, V7X_LANES), F32)],
        compiler_params=_params(("arbitrary",)),
        name="proj_lead" if lead else "proj",
    )(h, *consts, cos, sin, tri)


def _fox_body(stab_ref, q_ref, cq_ref, k_ref, v_ref, ck_ref, kl_ref, vl_ref, cl_ref, o_ref,
              m_ref, acc_ref, e_ref):
    tq = m_ref.shape[1]
    tk = tq // 2
    tr = tq // 4
    lane = lax.broadcasted_iota(jnp.int32, (1, 2 * V7X_LANES), 1)
    first2 = (lane % V7X_LANES) < FOX_HD
    first = first2[:, :V7X_LANES]
    lead_ok = lax.broadcasted_iota(jnp.int32, (tq, BLOCK), 1) >= N_EMPTY
    row = lax.broadcasted_iota(jnp.int32, (tq, tk), 0)
    col = lax.broadcasted_iota(jnp.int32, (tq, tk), 1)
    causal = col <= row
    corner = causal[:tr, :tr]
    use_bound = stab_ref[0] > 0.5

    def keys(j):
        ks = pl.multiple_of(j * tk, tk)
        return k_ref[pl.ds(ks, tk), :], ck_ref[:, pl.ds(ks, tk)]

    def values(j):
        return v_ref[pl.ds(pl.multiple_of(j * tk, tk), tk), :]

    def query_block(i, carry):
        _fox_query_block(i, tq, tk, tr, first, first2, lead_ok, row, col, causal, corner, use_bound,
                         keys, values, stab_ref, q_ref, cq_ref, kl_ref, vl_ref, cl_ref, o_ref,
                         m_ref, acc_ref, e_ref)
        return carry

    lax.fori_loop(0, q_ref.shape[0] // tq, query_block, 0)


def _fox_query_block(i, tq, tk, tr, first, first2, lead_ok, row, col, causal, corner, use_bound,
                     keys, values, stab_ref, q_ref, cq_ref, kl_ref, vl_ref, cl_ref, o_ref,
                     m_ref, acc_ref, e_ref):
    rows = pl.ds(pl.multiple_of(i * tq, tq), tq)
    n_full = 2 * i
    q = q_ref[rows, :]
    zero = jnp.zeros_like(q)
    q_heads = (jnp.where(first, q, zero), jnp.where(first, zero, q))
    acc_ref[...] = jnp.zeros_like(acc_ref)

    @pl.when(use_bound)
    def _():
        head = lax.broadcasted_iota(jnp.int32, (1, FOX_HEADS), 1) - 2 * pl.program_id(1)
        cq_blk = cq_ref[rows, :]
        cq = [jnp.sum(jnp.where(head == hh, cq_blk, 0.0), axis=1, keepdims=True) - stab_ref[1]
              for hh in range(2)]

        def exponents(k, ck, r0=0, r1=tq):
            return [_dot_nt(q_heads[hh][r0:r1], k) + cq[hh][r0:r1] - ck[hh:hh + 1, :] for hh in range(2)]

        def accumulate(e, v, allowed, r0=0):
            if allowed is not None:
                e = [jnp.where(allowed, x, NEG) for x in e]
            pvs = [_dot(jnp.exp2(x).astype(BF16), v) for x in e]
            r1 = r0 + e[0].shape[0]
            acc_ref[r0:r1, :] += jnp.where(first2, pvs[0], pvs[1])

        e_lead = exponents(kl_ref[...], cl_ref[...])
        e0 = exponents(*keys(0))
        e_ref[0] = e0[0]
        e_ref[1] = e0[1]
        accumulate(e_lead, vl_ref[...], lead_ok)

        def step(j, carry):
            e_cur = [e_ref[0], e_ref[1]]
            e_next = exponents(*keys(j + 1))
            accumulate(e_cur, values(j), None)
            e_ref[0] = e_next[0]
            e_ref[1] = e_next[1]
            return carry

        def step_pair(m, carry):
            return step(2 * m + 1, step(2 * m, carry))

        lax.fori_loop(0, i, step_pair, 0)

        kb, ckb = keys(n_full + 1)
        vb = values(n_full + 1)
        e_b1 = exponents(kb[:tr], ckb[:, :tr], 2 * tr, tq)
        e_b2 = exponents(kb[tr:], ckb[:, tr:], 3 * tr, tq)
        va = values(n_full)
        e_a = [e_ref[0], e_ref[1]]
        accumulate([x[:tr, :tr] for x in e_a], va[:tr], corner)
        accumulate([x[tr:] for x in e_a], va, causal[tr:], tr)
        accumulate([x[:tr] for x in e_b1], vb[:tr], corner, 2 * tr)
        accumulate([x[tr:] for x in e_b1], vb[:tr], None, 3 * tr)
        accumulate(e_b2, vb[tr:], corner, 3 * tr)

    @pl.when(jnp.logical_not(use_bound))
    def _():
        m_ref[...] = jnp.full_like(m_ref, NEG)

        def block(k, v, ck, allowed):
            pvs, alphas = [], []
            for hh in range(2):
                s = _dot_nt(q_heads[hh], k) - ck[hh:hh + 1, :]
                if allowed is not None:
                    s = jnp.where(allowed, s, NEG)
                m_old = m_ref[hh]
                m_new = jnp.maximum(m_old, jnp.max(s, axis=1, keepdims=True))
                m_ref[hh] = m_new
                pvs.append(_dot(jnp.exp2(s - m_new).astype(BF16), v))
                alphas.append(jnp.exp2(m_old - m_new))
            acc_ref[...] = (acc_ref[...] * jnp.where(first2, alphas[0], alphas[1])
                            + jnp.where(first2, pvs[0], pvs[1]))

        block(kl_ref[...], vl_ref[...], cl_ref[...], lead_ok)

        def full_block(j, carry):
            block(keys(j)[0], values(j), keys(j)[1], None)
            return carry

        lax.fori_loop(0, n_full, full_block, 0)
        block(keys(n_full)[0], values(n_full), keys(n_full)[1], causal)
        block(keys(n_full + 1)[0], values(n_full + 1), keys(n_full + 1)[1], col + tk <= row)

    acc = acc_ref[...]
    o_ref[rows, :] = (acc[:, :V7X_LANES] / acc[:, V7X_LANES:]).astype(o_ref.dtype)


def _fox(stab, q, cq, k, v, ck, kl, vl, cl, batch, seq):
    tq = ATT_BLOCK
    w = V7X_LANES
    return pl.pallas_call(
        _fox_body,
        grid=(batch, FOX_PAIRS),
        in_specs=[
            pl.BlockSpec(memory_space=pltpu.SMEM),
            pl.BlockSpec((None, seq, w), lambda b, j: (b, 0, j)),
            pl.BlockSpec((None, seq, FOX_HEADS), lambda b, j: (b, 0, 0)),
            pl.BlockSpec((None, seq, w), lambda b, j: (b, 0, j)),
            pl.BlockSpec((None, seq, 2 * w), lambda b, j: (b, 0, j)),
            pl.BlockSpec((None, None, 2, seq), lambda b, j: (b, j, 0, 0)),
            pl.BlockSpec((BLOCK, w), lambda b, j: (0, j)),
            pl.BlockSpec((BLOCK, 2 * w), lambda b, j: (0, j)),
            pl.BlockSpec((None, 2, BLOCK), lambda b, j: (j, 0, 0)),
        ],
        out_specs=pl.BlockSpec((None, seq, w), lambda b, j: (b, 0, j)),
        out_shape=jax.ShapeDtypeStruct((batch, seq, D_MODEL), BF16),
        scratch_shapes=[pltpu.VMEM((2, tq, 1), F32), pltpu.VMEM((tq, 2 * w), F32),
                        pltpu.VMEM((2, tq, tq // 2), F32)],
        compiler_params=_params(("arbitrary", "arbitrary")),
        name="fox",
    )(stab, q, cq, k, v, ck, kl, vl, cl)


def _ret_body(q_ref, k_ref, v_ref, sg_ref, kl_ref, vl_ref, gn_ref, decay_ref, zeta_ref, xi_ref,
              cd_ref, o_ref, state_ref):
    n_chunks = q_ref.shape[0] // BLOCK
    decay = decay_ref[...]
    zeta = zeta_ref[...]
    xi = xi_ref[...]
    cd = cd_ref[...]
    gn = gn_ref[...]

    state_ref[...] = _dot_tn((kl_ref[...].astype(F32) * zeta).astype(BF16), vl_ref[...])

    def chunk(c, carry):
        rs = pl.multiple_of(c * BLOCK, BLOCK)
        q = q_ref[pl.ds(rs, BLOCK), :]
        k = k_ref[pl.ds(rs, BLOCK), :]
        v = v_ref[pl.ds(rs, BLOCK), :]
        state = state_ref[...]
        scores = (_dot_nt(q, k) * decay).astype(BF16)
        o = _dot(scores, v) + _dot((q.astype(F32) * xi).astype(BF16), state.astype(BF16))
        kv = _dot_tn((k.astype(F32) * zeta).astype(BF16), v)
        state_ref[...] = cd * state + kv
        mu = jnp.mean(o, axis=-1, keepdims=True)
        d = o - mu
        var = jnp.mean(d * d, axis=-1, keepdims=True)
        yn = d * lax.rsqrt(var + GN_EPS) * gn
        o_ref[pl.ds(rs, BLOCK), :] = (sg_ref[pl.ds(rs, BLOCK), :].astype(F32) * yn).astype(o_ref.dtype)
        return carry

    lax.fori_loop(0, n_chunks, chunk, 0, unroll=RET_UNROLL)


def _ret(rq, rk, rv, sg, rkl, rvl, gn, tables, batch, seq):
    decay, zeta, xi, cd = tables
    head_qk = pl.BlockSpec((None, seq, RET_DK), lambda b, h: (b, 0, h))
    head_v = pl.BlockSpec((None, seq, RET_DV), lambda b, h: (b, 0, h))
    table = pl.BlockSpec((None, BLOCK, BLOCK), lambda b, h: (h, 0, 0))
    return pl.pallas_call(
        _ret_body,
        grid=(batch, RET_HEADS),
        in_specs=[head_qk, head_qk, head_v, head_v,
                  pl.BlockSpec((BLOCK, RET_DK), lambda b, h: (0, h)),
                  pl.BlockSpec((BLOCK, RET_DV), lambda b, h: (0, h)),
                  pl.BlockSpec((1, RET_DV), lambda b, h: (0, h)),
                  table, table, table,
                  pl.BlockSpec((None, 1, RET_DV), lambda b, h: (h, 0, 0))],
        out_specs=head_v,
        out_shape=jax.ShapeDtypeStruct((batch, seq, D_MODEL), BF16),
        scratch_shapes=[pltpu.VMEM((RET_DK, RET_DV), F32)],
        compiler_params=_params(("arbitrary", "arbitrary")),
        name="ret",
    )(rq, rk, rv, sg, rkl, rvl, gn, decay, zeta, xi, cd)


def _out_body(ya_ref, yb_ref, ga_ref, gb_ref, h_ref, wa_ref, wb_ref, wo_ref, g_ref, win_ref,
              wout_ref, o_ref):
    mixed = (ga_ref[...].astype(F32) * _dot(ya_ref[...], wa_ref[...])
             + gb_ref[...].astype(F32) * _dot(yb_ref[...], wb_ref[...]))
    h = h_ref[...] + _dot(mixed.astype(BF16), wo_ref[...])
    xn = _rms_rows(h, g_ref[...]).astype(BF16)
    o_ref[...] = h + 0.5 * _swiglu(xn, win_ref, wout_ref)


def _out(ya, yb, ga, gb, h, consts, tm):
    rows = h.shape[0]
    row_spec = pl.BlockSpec((tm, D_MODEL), lambda i: (i, 0))
    return pl.pallas_call(
        _out_body,
        grid=(rows // tm,),
        in_specs=[row_spec] * 5 + [_const_spec(a.shape) for a in consts],
        out_specs=row_spec,
        out_shape=jax.ShapeDtypeStruct((rows, D_MODEL), F32),
        compiler_params=_params(("arbitrary",)),
        name="out",
    )(ya, yb, ga, gb, h, *consts)


def _ffn_weights(w_in, w_out, dtype):
    return w_in.astype(dtype), w_out.astype(dtype).reshape(N_FF_CHUNKS, FF_CHUNK, D_MODEL)


def _position_tables(seq):
    half = RET_DK // 2
    pos = np.arange(BLOCK + seq, dtype=np.float64) - N_EMPTY
    inv = ROPE_BASE ** (-np.arange(half, dtype=np.float64) / half)
    ang = pos[:, None] * inv[None, :]
    cos = np.concatenate([np.cos(ang), np.cos(ang)], axis=1)
    sin = np.concatenate([-np.sin(ang), np.sin(ang)], axis=1)
    return jnp.asarray(cos, dtype=F32), jnp.asarray(sin, dtype=F32)


def _retention_tables():
    log_gamma = np.log1p(-np.exp2(-5.0 - np.arange(RET_HEADS, dtype=np.float64)))
    n = np.arange(BLOCK, dtype=np.float64)
    diff = n[:, None] - n[None, :]
    lg = log_gamma[:, None, None]
    decay = np.where(diff >= 0, np.exp(lg * np.maximum(diff, 0.0)), 0.0)
    zeta = np.exp(log_gamma[:, None] * (BLOCK - 1 - n)[None, :])
    xi = np.exp(log_gamma[:, None] * (n + 1.0)[None, :])
    cd = np.exp(log_gamma * BLOCK)
    lanes = (RET_HEADS, BLOCK, RET_DK)
    tables = (decay, np.broadcast_to(zeta[:, :, None], lanes), np.broadcast_to(xi[:, :, None], lanes),
              np.broadcast_to(cd[:, None, None], (RET_HEADS, 1, RET_DV)))
    return tuple(jnp.asarray(t, dtype=F32) for t in tables)


def kernel(x, meta_tokens, norm_ffn1, w_ffn1_in, w_ffn1_out, norm_mix, w_in, b_forget, b_gate,
           fox_q_norm, fox_k_norm, w_o_fox, ret_gn, w_o_ret, w_out, norm_ffn2, w_ffn2_in,
           w_ffn2_out):
    batch, seq, d = x.shape
    assert d == D_MODEL and seq % ATT_BLOCK == 0 and seq % ROW_TILE == 0
    assert norm_ffn1.shape[0] == 1, "one layer"
    tiles_per_seq = seq // ROW_TILE

    win1, wout1 = _ffn_weights(w_ffn1_in[0], w_ffn1_out[0], F32)
    win2, wout2 = _ffn_weights(w_ffn2_in[0], w_ffn2_out[0], BF16)
    wi = w_in[0]
    o_ff = 3 * D_MODEL
    o_r = o_ff + FOX_HEADS
    o_g = o_r + 2 * RET_QK + 2 * D_MODEL
    assert o_g + 2 * D_MODEL == wi.shape[1] and COL_END == wi.shape[1] - FOX_HEADS + V7X_LANES
    w_proj = jnp.concatenate(
        [wi[:, :o_ff], wi[:, o_r:], wi[:, o_ff:o_r],
         jnp.zeros((D_MODEL, V7X_LANES - FOX_HEADS), wi.dtype)], axis=1).astype(BF16)
    grp = np.arange(V7X_MXU_DIM) // FOX_HD
    gmat = jnp.asarray((grp[:, None] == grp[None, :]) / FOX_HD, dtype=BF16)
    gq = jnp.tile(fox_q_norm[0], FOX_HEADS)[None, :]
    gk = jnp.tile(fox_k_norm[0], FOX_HEADS)[None, :]
    bf = jnp.pad(b_forget[0], (0, V7X_LANES - FOX_HEADS))[None, :]
    bg = b_gate[0][None, :]
    proj_consts = (norm_mix, w_proj, gmat, gq, gk, bf, bg)
    cos, sin = _position_tables(seq)
    idx = np.arange(ROW_TILE)
    tri_incl = jnp.asarray(idx[None, :] <= idx[:, None], dtype=BF16)
    idx = np.arange(BLOCK)
    tri_after = jnp.asarray(idx[None, :] > idx[:, None], dtype=BF16)

    lead = jnp.concatenate([jnp.zeros((N_EMPTY, d), x.dtype), meta_tokens.astype(x.dtype)], axis=0)
    xr = x.reshape(batch * seq, d)

    h1 = _ffn(xr, norm_ffn1, win1, wout1, ROW_TILE)
    h1l = _ffn(lead, norm_ffn1, win1, wout1, BLOCK)

    q, k, v, c, rq, rk, rv, sg, ga, gb = _proj(
        h1, proj_consts, cos[BLOCK:], sin[BLOCK:], tri_incl, ROW_TILE, False, tiles_per_seq)
    _, kl, vl, cl, _, rkl, rvl, _, _, _ = _proj(
        h1l, proj_consts, cos[:BLOCK], sin[:BLOCK], tri_after, BLOCK, True, 1)

    def b3(a):
        return a.reshape(batch, seq, a.shape[-1])

    ck = b3(c).transpose(0, 2, 1).reshape(batch, FOX_PAIRS, 2, seq)
    clt = cl.T.reshape(FOX_PAIRS, 2, BLOCK)
    bound = FOX_HD ** 0.5 * jnp.max(jnp.abs(fox_q_norm[0])) * jnp.max(jnp.abs(fox_k_norm[0]))
    stab = jnp.stack([(bound <= SAFE_LOGIT_BOUND).astype(F32), bound * LOG2E])
    ya = _fox(stab, b3(q), b3(c), b3(k), b3(v), ck, kl, vl, clt, batch, seq)
    yb = _ret(b3(rq), b3(rk), b3(rv), b3(sg), rkl, rvl, ret_gn, _retention_tables(), batch, seq)

    out_consts = (w_o_fox[0].astype(BF16), w_o_ret[0].astype(BF16), w_out[0].astype(BF16),
                  norm_ffn2, win2, wout2)
    out = _out(ya.reshape(batch * seq, d), yb.reshape(batch * seq, d), ga, gb, h1, out_consts,
               ROW_TILE)
    return out.reshape(batch, seq, d)
```

```python
import functools
import math

import jax
import jax.numpy as jnp
import numpy as np
from jax import lax
from jax.experimental import pallas as pl
from jax.experimental.pallas import tpu as pltpu

F32 = jnp.float32
BF16 = jnp.bfloat16

D_MODEL = 1024
N_META = 16
BLOCK = 128
N_EMPTY = BLOCK - N_META
FOX_HD = 64
FOX_HEADS = D_MODEL // FOX_HD
FOX_PAIRS = FOX_HEADS // 2
RET_HEADS = 4
RET_DK = D_MODEL // (2 * RET_HEADS)
RET_DV = 2 * RET_DK
RET_QK = RET_HEADS * RET_DK
D_FF = ((8 * D_MODEL // 3 + 127) // 128) * 128
EPS = 1e-6
GN_EPS = 1e-5
ROPE_BASE = 10000.0
NEG = -1e30
LOG2E = math.log2(math.e)
SAFE_LOGIT_BOUND = 30.0
F32_EXP2_ZERO = 151.0

V7X_LANES = 128
V7X_MXU_DIM = 256
V7X_VMEM_BYTES = 64 * 2**20
VMEM_LIMIT = V7X_VMEM_BYTES - 8 * 2**20

ROW_TILE = 512
FF_CHUNK = V7X_MXU_DIM
N_FF_CHUNKS = D_FF // FF_CHUNK
RET_UNROLL = 16
ATT_BLOCK = 1024


def _dot(a, b):
    return jnp.dot(a, b, preferred_element_type=F32)


def _dot_nt(a, b):
    return lax.dot_general(a, b, (((1,), (1,)), ((), ())), preferred_element_type=F32)


def _dot_tn(a, b):
    return lax.dot_general(a, b, (((0,), (0,)), ((), ())), preferred_element_type=F32)


def _const_spec(shape, block=None):
    zeros = (0,) * len(shape)
    return pl.BlockSpec(block or shape, lambda *_: zeros, pipeline_mode=pl.Buffered(1))


def _params(semantics):
    return pltpu.CompilerParams(dimension_semantics=semantics, vmem_limit_bytes=VMEM_LIMIT)


def _rms_rows(x, g):
    ms = jnp.mean(x * x, axis=-1, keepdims=True)
    return x * lax.rsqrt(ms + EPS) * g


def _swiglu(xn, win_ref, wout_ref):
    acc = None
    for c in range(N_FF_CHUNKS):
        a = _dot(xn, win_ref[:, c * FF_CHUNK:(c + 1) * FF_CHUNK].astype(BF16))
        b = _dot(xn, win_ref[:, D_FF + c * FF_CHUNK:D_FF + (c + 1) * FF_CHUNK].astype(BF16))
        hm = (a * jax.nn.sigmoid(a) * b).astype(BF16)
        d = _dot(hm, wout_ref[c].astype(BF16))
        acc = d if acc is None else acc + d
    return acc


def _ffn_body(x_ref, g_ref, win_ref, wout_ref, o_ref):
    x = x_ref[...]
    xn = _rms_rows(x, g_ref[...]).astype(BF16)
    o_ref[...] = x + 0.5 * _swiglu(xn, win_ref, wout_ref)


def _ffn(x, g, win, wout, tm):
    rows = x.shape[0]
    row_spec = pl.BlockSpec((tm, D_MODEL), lambda i: (i, 0))
    return pl.pallas_call(
        _ffn_body,
        grid=(rows // tm,),
        in_specs=[row_spec, _const_spec(g.shape), _const_spec(win.shape), _const_spec(wout.shape)],
        out_specs=row_spec,
        out_shape=jax.ShapeDtypeStruct((rows, D_MODEL), F32),
        compiler_params=_params(("arbitrary",)),
        name="ffn",
    )(x, g, win, wout)


def _group_mean_sq(x, gmat_ref):
    sq = (x * x).astype(BF16)
    w = V7X_MXU_DIM
    parts = [_dot(sq[:, i * w:(i + 1) * w], gmat_ref[...]) for i in range(D_MODEL // w)]
    return jnp.concatenate(parts, axis=1)


def _rotary_tile(x, cos, sin_signed):
    return x * cos + pltpu.roll(x, RET_DK // 2, axis=1) * sin_signed


COL_FQ = 0
COL_FK = COL_FQ + D_MODEL
COL_FV = COL_FK + D_MODEL
COL_RQ = COL_FV + D_MODEL
COL_RK = COL_RQ + RET_QK
COL_RV = COL_RK + RET_QK
COL_RG = COL_RV + D_MODEL
COL_GA = COL_RG + D_MODEL
COL_GB = COL_GA + D_MODEL
COL_FF = COL_GB + D_MODEL
COL_END = COL_FF + V7X_LANES


def _proj_body(h_ref, gmix_ref, wa_ref, wb_ref, wff_ref, gmat_ref, gq_ref, gk_ref,
               bf_ref, bg_ref, cos_ref, sin_ref, tri_ref,
               q_ref, k_ref, v_ref, c_ref, rq_ref, rk_ref, rv_ref, sg_ref, ga_ref, gb_ref,
               carry_ref, *, lead, tiles_per_seq):
    tm = h_ref.shape[0]

    def w(c0, c1):
        if c1 <= COL_RQ:
            return wa_ref[:, c0:c1]
        if c0 >= COL_FF:
            return wff_ref[...]
        return wb_ref[:, c0 - COL_RQ:c1 - COL_RQ]

    u = _rms_rows(h_ref[...], gmix_ref[...]).astype(BF16)
    if lead:
        row = lax.broadcasted_iota(jnp.int32, (tm, 1), 0)
        valid = row >= N_EMPTY
        vmask = valid.astype(F32)

    fq = _dot(u, w(COL_FQ, COL_FK))
    qn = fq * lax.rsqrt(_group_mean_sq(fq, gmat_ref) + EPS) * gq_ref[...]
    q_ref[...] = (qn * (FOX_HD ** -0.5 * LOG2E)).astype(BF16)
    fk = _dot(u, w(COL_FK, COL_FV))
    kn = fk * lax.rsqrt(_group_mean_sq(fk, gmat_ref) + EPS) * gk_ref[...]
    k_ref[...] = kn.astype(BF16)
    fv = _dot(u, w(COL_FV, COL_RQ)).astype(BF16)
    ones = jnp.ones((tm, V7X_LANES), BF16)
    for j in range(FOX_PAIRS):
        v_ref[:, 2 * j * V7X_LANES:(2 * j + 1) * V7X_LANES] = fv[:, j * V7X_LANES:(j + 1) * V7X_LANES]
        v_ref[:, (2 * j + 1) * V7X_LANES:(2 * j + 2) * V7X_LANES] = ones

    z = _dot(u, w(COL_FF, COL_END)) + bf_ref[...]
    logf = -(jnp.maximum(-z, 0.0) + jnp.log1p(jnp.exp(-jnp.abs(z))))
    if lead:
        logf = jnp.where(valid, logf, 0.0)
    lane = lax.broadcasted_iota(jnp.int32, (1, V7X_LANES), 1)
    logf = jnp.where(lane < FOX_HEADS, logf, 0.0)
    p1 = logf.astype(BF16).astype(F32)
    r1 = logf - p1
    p2 = r1.astype(BF16).astype(F32)
    p3 = (r1 - p2).astype(BF16).astype(F32)
    packed = p1 + pltpu.roll(p2, FOX_HEADS, axis=1) + pltpu.roll(p3, 2 * FOX_HEADS, axis=1)
    r = _dot(tri_ref[...], packed.astype(BF16))
    cum = r + (pltpu.roll(r, V7X_LANES - FOX_HEADS, axis=1)
               + pltpu.roll(r, V7X_LANES - 2 * FOX_HEADS, axis=1))
    cum = jnp.where(lane < FOX_HEADS, cum, 0.0)
    if lead:
        c = -cum
    else:
        @pl.when(pl.program_id(0) % tiles_per_seq == 0)
        def _():
            carry_ref[...] = jnp.zeros_like(carry_ref)
        c = cum + carry_ref[...]
        carry_ref[...] = c[tm - 1:tm, :]
    c_ref[...] = (c * LOG2E)[:, :FOX_HEADS]

    cos = cos_ref[...]
    sin = sin_ref[...]
    rq = _dot(u, w(COL_RQ, COL_RK))
    rk = _dot(u, w(COL_RK, COL_RV))
    for hh in range(RET_HEADS):
        sl = slice(hh * RET_DK, (hh + 1) * RET_DK)
        rq_ref[:, sl] = _rotary_tile(rq[:, sl], cos, sin).astype(BF16)
        kt = _rotary_tile(rk[:, sl], cos, sin) * (RET_DK ** -0.5)
        if lead:
            kt = kt * vmask
        rk_ref[:, sl] = kt.astype(BF16)
    rv = _dot(u, w(COL_RV, COL_RG))
    if lead:
        rv = rv * vmask
    rv_ref[...] = rv.astype(BF16)
    rg = _dot(u, w(COL_RG, COL_GA))
    sg_ref[...] = (rg * jax.nn.sigmoid(rg)).astype(BF16)

    bg = bg_ref[...]
    ga_ref[...] = jax.nn.sigmoid(_dot(u, w(COL_GA, COL_GB)) + bg[:, :D_MODEL]).astype(BF16)
    gb_ref[...] = jax.nn.sigmoid(_dot(u, w(COL_GB, COL_FF)) + bg[:, D_MODEL:]).astype(BF16)


def _proj(h, consts, cos, sin, tri, tm, lead, tiles_per_seq):
    rows = h.shape[0]
    grid = rows // tm

    def rows_spec(width):
        return pl.BlockSpec((tm, width), lambda i: (i, 0))

    pos_spec = pl.BlockSpec((tm, RET_DK), lambda i: (i % tiles_per_seq, 0))
    out_widths = [D_MODEL, D_MODEL, 2 * D_MODEL, FOX_HEADS, RET_QK, RET_QK, D_MODEL, D_MODEL,
                  D_MODEL, D_MODEL]
    out_dtypes = [BF16, BF16, BF16, F32, BF16, BF16, BF16, BF16, BF16, BF16]
    return pl.pallas_call(
        functools.partial(_proj_body, lead=lead, tiles_per_seq=tiles_per_seq),
        grid=(grid,),
        in_specs=[rows_spec(D_MODEL)] + [_const_spec(a.shape, blk) for a, blk in consts]
        + [pos_spec, pos_spec, _const_spec(tri.shape)],
        out_specs=[rows_spec(w) for w in out_widths],
        out_shape=[jax.ShapeDtypeStruct((rows, w), dt) for w, dt in zip(out_widths, out_dtypes)],
        scratch_shapes=[pltpu.VMEM((1, V7X_LANES), F32)],
        compiler_params=_params(("arbitrary",)),
        name="proj_lead" if lead else "proj",
    )(h, *[a for a, _ in consts], cos, sin, tri)


def _fox_body(stab_ref, cs_ref, ce_ref, q_ref, cq_ref, k_ref, v_ref, ck_ref, kl_ref, vl_ref, cl_ref,
              o_ref, m_ref, acc_ref, e_ref):
    tq = m_ref.shape[1]
    tk = tq // 2
    tr = tq // 4
    lane = lax.broadcasted_iota(jnp.int32, (1, 2 * V7X_LANES), 1)
    first2 = (lane % V7X_LANES) < FOX_HD
    first = first2[:, :V7X_LANES]
    lead_ok = lax.broadcasted_iota(jnp.int32, (tq, BLOCK), 1) >= N_EMPTY
    row = lax.broadcasted_iota(jnp.int32, (tq, tk), 0)
    col = lax.broadcasted_iota(jnp.int32, (tq, tk), 1)
    causal = col <= row
    corner = causal[:tr, :tr]
    use_bound = stab_ref[0] > 0.5

    def keys(j):
        ks = pl.multiple_of(j * tk, tk)
        return k_ref[pl.ds(ks, tk), :], ck_ref[:, pl.ds(ks, tk)]

    def values(j):
        return v_ref[pl.ds(pl.multiple_of(j * tk, tk), tk), :]

    nq = q_ref.shape[0] // tq
    nk = k_ref.shape[0] // tk

    def query_block(i, carry):
        _fox_query_block(i, tq, tk, tr, nq, nk, first, first2, lead_ok, row, col, causal, corner,
                         use_bound, keys, values, stab_ref, cs_ref, ce_ref, q_ref, cq_ref, kl_ref,
                         vl_ref, cl_ref, o_ref, m_ref, acc_ref, e_ref)
        return carry

    lax.fori_loop(0, nq, query_block, 0)


def _fox_query_block(i, tq, tk, tr, nq, nk, first, first2, lead_ok, row, col, causal, corner,
                     use_bound, keys, values, stab_ref, cs_ref, ce_ref, q_ref, cq_ref, kl_ref,
                     vl_ref, cl_ref, o_ref, m_ref, acc_ref, e_ref):
    rows = pl.ds(pl.multiple_of(i * tq, tq), tq)
    n_full = 2 * i
    q = q_ref[rows, :]
    zero = jnp.zeros_like(q)
    q_heads = (jnp.where(first, q, zero), jnp.where(first, zero, q))
    acc_ref[...] = jnp.zeros_like(acc_ref)

    @pl.when(use_bound)
    def _():
        head = lax.broadcasted_iota(jnp.int32, (1, FOX_HEADS), 1) - 2 * pl.program_id(1)
        cq_blk = cq_ref[rows, :]
        cq = [jnp.sum(jnp.where(head == hh, cq_blk, 0.0), axis=1, keepdims=True) - stab_ref[1]
              for hh in range(2)]

        def exponents(k, ck, r0=0, r1=tq):
            return [_dot_nt(q_heads[hh][r0:r1], k) + cq[hh][r0:r1] - ck[hh:hh + 1, :] for hh in range(2)]

        def accumulate(e, v, allowed, r0=0):
            if allowed is not None:
                e = [jnp.where(allowed, x, NEG) for x in e]
            pvs = [_dot(jnp.exp2(x).astype(BF16), v) for x in e]
            r1 = r0 + e[0].shape[0]
            acc_ref[r0:r1, :] += jnp.where(first2, pvs[0], pvs[1])

        cutoff = -stab_ref[2]
        slot = 2 * pl.program_id(1)
        cs_base = (pl.program_id(0) * nq + i) * FOX_HEADS
        ce_base = pl.program_id(0) * nk * FOX_HEADS
        c_first = [cs_ref[cs_base + slot + hh] for hh in range(2)]
        skip_lead = jnp.maximum(c_first[0], c_first[1]) < cutoff
        j0 = jnp.int32(0)
        for j in range(nk - 2):
            gap = jnp.maximum(c_first[0] - ce_ref[ce_base + j * FOX_HEADS + slot],
                              c_first[1] - ce_ref[ce_base + j * FOX_HEADS + slot + 1])
            j0 += jnp.logical_and(gap < cutoff, j < n_full).astype(jnp.int32)

        e0 = exponents(*keys(j0))
        e_ref[0] = e0[0]
        e_ref[1] = e0[1]

        @pl.when(jnp.logical_not(skip_lead))
        def _():
            accumulate(exponents(kl_ref[...], cl_ref[...]), vl_ref[...], lead_ok)

        def step(j, carry):
            e_cur = [e_ref[0], e_ref[1]]
            e_next = exponents(*keys(j + 1))
            accumulate(e_cur, values(j), None)
            e_ref[0] = e_next[0]
            e_ref[1] = e_next[1]
            return carry

        def step_pair(m, carry):
            return step(2 * m + 1, step(2 * m, carry))

        @pl.when(j0 % 2 == 1)
        def _():
            step(j0, 0)

        lax.fori_loop((j0 + 1) // 2, i, step_pair, 0)

        kb, ckb = keys(n_full + 1)
        vb = values(n_full + 1)
        e_b1 = exponents(kb[:tr], ckb[:, :tr], 2 * tr, tq)
        e_b2 = exponents(kb[tr:], ckb[:, tr:], 3 * tr, tq)
        va = values(n_full)
        e_a = [e_ref[0], e_ref[1]]
        accumulate([x[:tr, :tr] for x in e_a], va[:tr], corner)
        accumulate([x[tr:] for x in e_a], va, causal[tr:], tr)
        accumulate([x[:tr] for x in e_b1], vb[:tr], corner, 2 * tr)
        accumulate([x[tr:] for x in e_b1], vb[:tr], None, 3 * tr)
        accumulate(e_b2, vb[tr:], corner, 3 * tr)

    @pl.when(jnp.logical_not(use_bound))
    def _():
        m_ref[...] = jnp.full_like(m_ref, NEG)

        def block(k, v, ck, allowed):
            pvs, alphas = [], []
            for hh in range(2):
                s = _dot_nt(q_heads[hh], k) - ck[hh:hh + 1, :]
                if allowed is not None:
                    s = jnp.where(allowed, s, NEG)
                m_old = m_ref[hh]
                m_new = jnp.maximum(m_old, jnp.max(s, axis=1, keepdims=True))
                m_ref[hh] = m_new
                pvs.append(_dot(jnp.exp2(s - m_new).astype(BF16), v))
                alphas.append(jnp.exp2(m_old - m_new))
            acc_ref[...] = (acc_ref[...] * jnp.where(first2, alphas[0], alphas[1])
                            + jnp.where(first2, pvs[0], pvs[1]))

        block(kl_ref[...], vl_ref[...], cl_ref[...], lead_ok)

        def full_block(j, carry):
            block(keys(j)[0], values(j), keys(j)[1], None)
            return carry

        lax.fori_loop(0, n_full, full_block, 0)
        block(keys(n_full)[0], values(n_full), keys(n_full)[1], causal)
        block(keys(n_full + 1)[0], values(n_full + 1), keys(n_full + 1)[1], col + tk <= row)

    acc = acc_ref[...]
    o_ref[rows, :] = (acc[:, :V7X_LANES] / acc[:, V7X_LANES:]).astype(o_ref.dtype)


def _fox(stab, cs, ce, q, cq, k, v, ck, kl, vl, cl, batch, seq):
    tq = ATT_BLOCK
    w = V7X_LANES
    return pl.pallas_call(
        _fox_body,
        grid=(batch, FOX_PAIRS),
        in_specs=[
            pl.BlockSpec(memory_space=pltpu.SMEM),
            pl.BlockSpec(memory_space=pltpu.SMEM),
            pl.BlockSpec(memory_space=pltpu.SMEM),
            pl.BlockSpec((None, seq, w), lambda b, j: (b, 0, j)),
            pl.BlockSpec((None, seq, FOX_HEADS), lambda b, j: (b, 0, 0)),
            pl.BlockSpec((None, seq, w), lambda b, j: (b, 0, j)),
            pl.BlockSpec((None, seq, 2 * w), lambda b, j: (b, 0, j)),
            pl.BlockSpec((None, None, 2, seq), lambda b, j: (b, j, 0, 0)),
            pl.BlockSpec((BLOCK, w), lambda b, j: (0, j)),
            pl.BlockSpec((BLOCK, 2 * w), lambda b, j: (0, j)),
            pl.BlockSpec((None, 2, BLOCK), lambda b, j: (j, 0, 0)),
        ],
        out_specs=pl.BlockSpec((None, seq, w), lambda b, j: (b, 0, j)),
        out_shape=jax.ShapeDtypeStruct((batch, seq, D_MODEL), BF16),
        scratch_shapes=[pltpu.VMEM((2, tq, 1), F32), pltpu.VMEM((tq, 2 * w), F32),
                        pltpu.VMEM((2, tq, tq // 2), F32)],
        compiler_params=_params(("arbitrary", "arbitrary")),
        name="fox",
    )(stab, cs, ce, q, cq, k, v, ck, kl, vl, cl)


def _ret_body(q_ref, k_ref, v_ref, sg_ref, kl_ref, vl_ref, gn_ref, decay_ref, zeta_ref, xi_ref,
              cd_ref, o_ref, state_ref):
    n_chunks = q_ref.shape[0] // BLOCK
    decay = decay_ref[...]
    zeta = zeta_ref[...]
    xi = xi_ref[...]
    cd = cd_ref[...]
    gn = gn_ref[...]

    state_ref[...] = _dot_tn((kl_ref[...].astype(F32) * zeta).astype(BF16), vl_ref[...])

    def chunk(c, carry):
        rs = pl.multiple_of(c * BLOCK, BLOCK)
        q = q_ref[pl.ds(rs, BLOCK), :]
        k = k_ref[pl.ds(rs, BLOCK), :]
        v = v_ref[pl.ds(rs, BLOCK), :]
        state = state_ref[...]
        scores = (_dot_nt(q, k) * decay).astype(BF16)
        o = _dot(scores, v) + _dot((q.astype(F32) * xi).astype(BF16), state.astype(BF16))
        kv = _dot_tn((k.astype(F32) * zeta).astype(BF16), v)
        state_ref[...] = cd * state + kv
        mu = jnp.mean(o, axis=-1, keepdims=True)
        d = o - mu
        var = jnp.mean(d * d, axis=-1, keepdims=True)
        yn = d * lax.rsqrt(var + GN_EPS) * gn
        o_ref[pl.ds(rs, BLOCK), :] = (sg_ref[pl.ds(rs, BLOCK), :].astype(F32) * yn).astype(o_ref.dtype)
        return carry

    lax.fori_loop(0, n_chunks, chunk, 0, unroll=RET_UNROLL)


def _ret(rq, rk, rv, sg, rkl, rvl, gn, tables, batch, seq):
    decay, zeta, xi, cd = tables
    head_qk = pl.BlockSpec((None, seq, RET_DK), lambda b, h: (b, 0, h))
    head_v = pl.BlockSpec((None, seq, RET_DV), lambda b, h: (b, 0, h))
    table = pl.BlockSpec((None, BLOCK, BLOCK), lambda b, h: (h, 0, 0))
    return pl.pallas_call(
        _ret_body,
        grid=(batch, RET_HEADS),
        in_specs=[head_qk, head_qk, head_v, head_v,
                  pl.BlockSpec((BLOCK, RET_DK), lambda b, h: (0, h)),
                  pl.BlockSpec((BLOCK, RET_DV), lambda b, h: (0, h)),
                  pl.BlockSpec((1, RET_DV), lambda b, h: (0, h)),
                  table, table, table,
                  pl.BlockSpec((None, 1, RET_DV), lambda b, h: (h, 0, 0))],
        out_specs=head_v,
        out_shape=jax.ShapeDtypeStruct((batch, seq, D_MODEL), BF16),
        scratch_shapes=[pltpu.VMEM((RET_DK, RET_DV), F32)],
        compiler_params=_params(("arbitrary", "arbitrary")),
        name="ret",
    )(rq, rk, rv, sg, rkl, rvl, gn, decay, zeta, xi, cd)


def _out_body(ya_ref, yb_ref, ga_ref, gb_ref, h_ref, wa_ref, wb_ref, wo_ref, g_ref, win_ref,
              wout_ref, o_ref):
    mixed = (ga_ref[...].astype(F32) * _dot(ya_ref[...], wa_ref[...])
             + gb_ref[...].astype(F32) * _dot(yb_ref[...], wb_ref[...]))
    h = h_ref[...] + _dot(mixed.astype(BF16), wo_ref[...])
    xn = _rms_rows(h, g_ref[...]).astype(BF16)
    o_ref[...] = h + 0.5 * _swiglu(xn, win_ref, wout_ref)


def _out(ya, yb, ga, gb, h, consts, tm):
    rows = h.shape[0]
    row_spec = pl.BlockSpec((tm, D_MODEL), lambda i: (i, 0))
    return pl.pallas_call(
        _out_body,
        grid=(rows // tm,),
        in_specs=[row_spec] * 5 + [_const_spec(a.shape) for a in consts],
        out_specs=row_spec,
        out_shape=jax.ShapeDtypeStruct((rows, D_MODEL), F32),
        compiler_params=_params(("arbitrary",)),
        name="out",
    )(ya, yb, ga, gb, h, *consts)


def _ffn_weights(w_in, w_out, dtype):
    return w_in.astype(dtype), w_out.astype(dtype).reshape(N_FF_CHUNKS, FF_CHUNK, D_MODEL)


def _position_tables(seq):
    half = RET_DK // 2
    pos = np.arange(BLOCK + seq, dtype=np.float64) - N_EMPTY
    inv = ROPE_BASE ** (-np.arange(half, dtype=np.float64) / half)
    ang = pos[:, None] * inv[None, :]
    cos = np.concatenate([np.cos(ang), np.cos(ang)], axis=1)
    sin = np.concatenate([-np.sin(ang), np.sin(ang)], axis=1)
    return jnp.asarray(cos, dtype=F32), jnp.asarray(sin, dtype=F32)


def _retention_tables():
    log_gamma = np.log1p(-np.exp2(-5.0 - np.arange(RET_HEADS, dtype=np.float64)))
    n = np.arange(BLOCK, dtype=np.float64)
    diff = n[:, None] - n[None, :]
    lg = log_gamma[:, None, None]
    decay = np.where(diff >= 0, np.exp(lg * np.maximum(diff, 0.0)), 0.0)
    zeta = np.exp(log_gamma[:, None] * (BLOCK - 1 - n)[None, :])
    xi = np.exp(log_gamma[:, None] * (n + 1.0)[None, :])
    cd = np.exp(log_gamma * BLOCK)
    lanes = (RET_HEADS, BLOCK, RET_DK)
    tables = (decay, np.broadcast_to(zeta[:, :, None], lanes), np.broadcast_to(xi[:, :, None], lanes),
              np.broadcast_to(cd[:, None, None], (RET_HEADS, 1, RET_DV)))
    return tuple(jnp.asarray(t, dtype=F32) for t in tables)


def kernel(x, meta_tokens, norm_ffn1, w_ffn1_in, w_ffn1_out, norm_mix, w_in, b_forget, b_gate,
           fox_q_norm, fox_k_norm, w_o_fox, ret_gn, w_o_ret, w_out, norm_ffn2, w_ffn2_in,
           w_ffn2_out):
    batch, seq, d = x.shape
    assert d == D_MODEL and seq % ATT_BLOCK == 0 and seq % ROW_TILE == 0
    assert norm_ffn1.shape[0] == 1, "one layer"
    tiles_per_seq = seq // ROW_TILE

    win1, wout1 = _ffn_weights(w_ffn1_in[0], w_ffn1_out[0], F32)
    win2, wout2 = _ffn_weights(w_ffn2_in[0], w_ffn2_out[0], BF16)
    wi = w_in[0].astype(BF16)
    o_ff = COL_RQ
    o_r = o_ff + FOX_HEADS
    assert wi.shape[1] - o_r == COL_FF - COL_RQ
    w_rest = wi[:, o_r:]
    w_ff = jnp.pad(wi[:, o_ff:o_r], ((0, 0), (0, V7X_LANES - FOX_HEADS)))
    grp = np.arange(V7X_MXU_DIM) // FOX_HD
    gmat = jnp.asarray((grp[:, None] == grp[None, :]) / FOX_HD, dtype=BF16)
    gq = jnp.tile(fox_q_norm[0], FOX_HEADS)[None, :]
    gk = jnp.tile(fox_k_norm[0], FOX_HEADS)[None, :]
    bf = jnp.pad(b_forget[0], (0, V7X_LANES - FOX_HEADS))[None, :]
    bg = b_gate[0][None, :]
    proj_consts = ((norm_mix, None), (wi, (D_MODEL, COL_RQ)), (w_rest, None), (w_ff, None), (gmat, None),
                   (gq, None), (gk, None), (bf, None), (bg, None))
    cos, sin = _position_tables(seq)
    idx = np.arange(ROW_TILE)
    tri_incl = jnp.asarray(idx[None, :] <= idx[:, None], dtype=BF16)
    idx = np.arange(BLOCK)
    tri_after = jnp.asarray(idx[None, :] > idx[:, None], dtype=BF16)

    lead = jnp.concatenate([jnp.zeros((N_EMPTY, d), x.dtype), meta_tokens.astype(x.dtype)], axis=0)
    xr = x.reshape(batch * seq, d)

    h1 = _ffn(xr, norm_ffn1, win1, wout1, ROW_TILE)
    h1l = _ffn(lead, norm_ffn1, win1, wout1, BLOCK)

    q, k, v, c, rq, rk, rv, sg, ga, gb = _proj(
        h1, proj_consts, cos[BLOCK:], sin[BLOCK:], tri_incl, ROW_TILE, False, tiles_per_seq)
    _, kl, vl, cl, _, rkl, rvl, _, _, _ = _proj(
        h1l, proj_consts, cos[:BLOCK], sin[:BLOCK], tri_after, BLOCK, True, 1)

    def b3(a):
        return a.reshape(batch, seq, a.shape[-1])

    ck = b3(c).transpose(0, 2, 1).reshape(batch, FOX_PAIRS, 2, seq)
    clt = cl.T.reshape(FOX_PAIRS, 2, BLOCK)
    bound = FOX_HD ** 0.5 * jnp.max(jnp.abs(fox_q_norm[0])) * jnp.max(jnp.abs(fox_k_norm[0]))
    bound2 = bound * LOG2E
    stab = jnp.stack([(bound <= SAFE_LOGIT_BOUND).astype(F32), bound2, F32_EXP2_ZERO + 2.0 * bound2])
    c3 = b3(c)
    cs = c3[:, ::ATT_BLOCK, :].reshape(-1)
    ce = c3[:, ATT_BLOCK // 2 - 1::ATT_BLOCK // 2, :].reshape(-1)
    ya = _fox(stab, cs, ce, b3(q), c3, b3(k), b3(v), ck, kl, vl, clt, batch, seq)
    yb = _ret(b3(rq), b3(rk), b3(rv), b3(sg), rkl, rvl, ret_gn, _retention_tables(), batch, seq)

    out_consts = (w_o_fox[0].astype(BF16), w_o_ret[0].astype(BF16), w_out[0].astype(BF16),
                  norm_ffn2, win2, wout2)
    out = _out(ya.reshape(batch * seq, d), yb.reshape(batch * seq, d), ga, gb, h1, out_consts,
               ROW_TILE)
    return out.reshape(batch, seq, d)
```

```python
import functools
import math

import jax
import jax.numpy as jnp
import numpy as np
from jax import lax
from jax.experimental import pallas as pl
from jax.experimental.pallas import tpu as pltpu

F32 = jnp.float32
BF16 = jnp.bfloat16

D_MODEL = 1024
N_META = 16
BLOCK = 128
N_EMPTY = BLOCK - N_META
FOX_HD = 64
FOX_HEADS = D_MODEL // FOX_HD
FOX_PAIRS = FOX_HEADS // 2
RET_HEADS = 4
RET_DK = D_MODEL // (2 * RET_HEADS)
RET_DV = 2 * RET_DK
RET_QK = RET_HEADS * RET_DK
D_FF = ((8 * D_MODEL // 3 + 127) // 128) * 128
EPS = 1e-6
GN_EPS = 1e-5
ROPE_BASE = 10000.0
NEG = -1e30
LOG2E = math.log2(math.e)
SAFE_LOGIT_BOUND = 30.0
F32_EXP2_ZERO = 151.0

V7X_LANES = 128
V7X_MXU_DIM = 256
V7X_VMEM_BYTES = 64 * 2**20
VMEM_LIMIT = V7X_VMEM_BYTES - 8 * 2**20

ROW_TILE = 512
FF_CHUNK = V7X_MXU_DIM
N_FF_CHUNKS = D_FF // FF_CHUNK
RET_UNROLL = 16
ATT_BLOCK = 1024


def _dot(a, b):
    return jnp.dot(a, b, preferred_element_type=F32)


def _dot_nt(a, b):
    return lax.dot_general(a, b, (((1,), (1,)), ((), ())), preferred_element_type=F32)


def _dot_tn(a, b):
    return lax.dot_general(a, b, (((0,), (0,)), ((), ())), preferred_element_type=F32)


def _const_spec(shape, block=None):
    zeros = (0,) * len(shape)
    return pl.BlockSpec(block or shape, lambda *_: zeros, pipeline_mode=pl.Buffered(1))


def _params(semantics):
    return pltpu.CompilerParams(dimension_semantics=semantics, vmem_limit_bytes=VMEM_LIMIT)


def _rms_rows(x, g):
    ms = jnp.mean(x * x, axis=-1, keepdims=True)
    return x * lax.rsqrt(ms + EPS) * g


def _swiglu(xn, win_ref, wout_ref):
    acc = None
    for c in range(N_FF_CHUNKS):
        a = _dot(xn, win_ref[:, c * FF_CHUNK:(c + 1) * FF_CHUNK].astype(BF16))
        b = _dot(xn, win_ref[:, D_FF + c * FF_CHUNK:D_FF + (c + 1) * FF_CHUNK].astype(BF16))
        hm = (a * jax.nn.sigmoid(a) * b).astype(BF16)
        d = _dot(hm, wout_ref[c].astype(BF16))
        acc = d if acc is None else acc + d
    return acc


def _ffn_body(x_ref, g_ref, win_ref, wout_ref, o_ref):
    x = x_ref[...]
    xn = _rms_rows(x, g_ref[...]).astype(BF16)
    o_ref[...] = x + 0.5 * _swiglu(xn, win_ref, wout_ref)


def _ffn(x, g, win, wout, tm):
    rows = x.shape[0]
    row_spec = pl.BlockSpec((tm, D_MODEL), lambda i: (i, 0))
    return pl.pallas_call(
        _ffn_body,
        grid=(rows // tm,),
        in_specs=[row_spec, _const_spec(g.shape), _const_spec(win.shape), _const_spec(wout.shape)],
        out_specs=row_spec,
        out_shape=jax.ShapeDtypeStruct((rows, D_MODEL), F32),
        compiler_params=_params(("arbitrary",)),
        name="ffn",
    )(x, g, win, wout)


def _ffn_cast_body(x_ref, g_ref, win_ref, wout_ref, wa_src, wff_src, wb_src,
                   o_ref, wa_dst, wff_dst, wb_dst):
    _ffn_body(x_ref, g_ref, win_ref, wout_ref, o_ref)
    wa_dst[...] = wa_src[...].astype(BF16)
    wb_dst[...] = wb_src[...].astype(BF16)
    lane = lax.broadcasted_iota(jnp.int32, (1, V7X_LANES), 1)
    wff_dst[...] = jnp.where(lane < FOX_HEADS, wff_src[...], 0.0).astype(BF16)


def _ffn_and_cast(x, g, win, wout, w_in, w_rest, tm):
    rows = x.shape[0]
    steps = rows // tm
    slab = D_MODEL // steps
    assert slab * steps == D_MODEL and slab % 16 == 0
    row_spec = pl.BlockSpec((tm, D_MODEL), lambda i: (i, 0))

    def slab_spec(width, col_block=0):
        return pl.BlockSpec((slab, width), lambda i: (i, col_block))

    return pl.pallas_call(
        _ffn_cast_body,
        grid=(steps,),
        in_specs=[row_spec, _const_spec(g.shape), _const_spec(win.shape), _const_spec(wout.shape),
                  slab_spec(COL_RQ), slab_spec(V7X_LANES, COL_RQ // V7X_LANES), slab_spec(w_rest.shape[1])],
        out_specs=[row_spec, slab_spec(COL_RQ), slab_spec(V7X_LANES), slab_spec(w_rest.shape[1])],
        out_shape=[jax.ShapeDtypeStruct((rows, D_MODEL), F32),
                   jax.ShapeDtypeStruct((D_MODEL, COL_RQ), BF16),
                   jax.ShapeDtypeStruct((D_MODEL, V7X_LANES), BF16),
                   jax.ShapeDtypeStruct((D_MODEL, w_rest.shape[1]), BF16)],
        compiler_params=_params(("arbitrary",)),
        name="ffn_cast",
    )(x, g, win, wout, w_in, w_in, w_rest)


def _group_mean_sq(x, gmat_ref):
    sq = (x * x).astype(BF16)
    w = V7X_MXU_DIM
    parts = [_dot(sq[:, i * w:(i + 1) * w], gmat_ref[...]) for i in range(D_MODEL // w)]
    return jnp.concatenate(parts, axis=1)


def _rotary_tile(x, cos, sin_signed):
    return x * cos + pltpu.roll(x, RET_DK // 2, axis=1) * sin_signed


COL_FQ = 0
COL_FK = COL_FQ + D_MODEL
COL_FV = COL_FK + D_MODEL
COL_RQ = COL_FV + D_MODEL
COL_RK = COL_RQ + RET_QK
COL_RV = COL_RK + RET_QK
COL_RG = COL_RV + D_MODEL
COL_GA = COL_RG + D_MODEL
COL_GB = COL_GA + D_MODEL
COL_FF = COL_GB + D_MODEL
COL_END = COL_FF + V7X_LANES


def _proj_body(h_ref, gmix_ref, wa_ref, wb_ref, wff_ref, gmat_ref, gq_ref, gk_ref,
               bf_ref, bg_ref, cos_ref, sin_ref, tri_ref,
               q_ref, k_ref, v_ref, c_ref, rq_ref, rk_ref, rv_ref, sg_ref, ga_ref, gb_ref,
               carry_ref, *, lead, tiles_per_seq):
    tm = h_ref.shape[0]

    def w(c0, c1):
        if c1 <= COL_RQ:
            return wa_ref[:, c0:c1]
        if c0 >= COL_FF:
            return wff_ref[...]
        return wb_ref[:, c0 - COL_RQ:c1 - COL_RQ]

    u = _rms_rows(h_ref[...], gmix_ref[...]).astype(BF16)
    if lead:
        row = lax.broadcasted_iota(jnp.int32, (tm, 1), 0)
        valid = row >= N_EMPTY
        vmask = valid.astype(F32)

    fq = _dot(u, w(COL_FQ, COL_FK))
    qn = fq * lax.rsqrt(_group_mean_sq(fq, gmat_ref) + EPS) * gq_ref[...]
    q_ref[...] = (qn * (FOX_HD ** -0.5 * LOG2E)).astype(BF16)
    fk = _dot(u, w(COL_FK, COL_FV))
    kn = fk * lax.rsqrt(_group_mean_sq(fk, gmat_ref) + EPS) * gk_ref[...]
    k_ref[...] = kn.astype(BF16)
    fv = _dot(u, w(COL_FV, COL_RQ)).astype(BF16)
    ones = jnp.ones((tm, V7X_LANES), BF16)
    for j in range(FOX_PAIRS):
        v_ref[:, 2 * j * V7X_LANES:(2 * j + 1) * V7X_LANES] = fv[:, j * V7X_LANES:(j + 1) * V7X_LANES]
        v_ref[:, (2 * j + 1) * V7X_LANES:(2 * j + 2) * V7X_LANES] = ones

    z = _dot(u, w(COL_FF, COL_END)) + bf_ref[...]
    logf = -(jnp.maximum(-z, 0.0) + jnp.log1p(jnp.exp(-jnp.abs(z))))
    if lead:
        logf = jnp.where(valid, logf, 0.0)
    lane = lax.broadcasted_iota(jnp.int32, (1, V7X_LANES), 1)
    logf = jnp.where(lane < FOX_HEADS, logf, 0.0)
    p1 = logf.astype(BF16).astype(F32)
    r1 = logf - p1
    p2 = r1.astype(BF16).astype(F32)
    p3 = (r1 - p2).astype(BF16).astype(F32)
    packed = p1 + pltpu.roll(p2, FOX_HEADS, axis=1) + pltpu.roll(p3, 2 * FOX_HEADS, axis=1)
    r = _dot(tri_ref[...], packed.astype(BF16))
    cum = r + (pltpu.roll(r, V7X_LANES - FOX_HEADS, axis=1)
               + pltpu.roll(r, V7X_LANES - 2 * FOX_HEADS, axis=1))
    cum = jnp.where(lane < FOX_HEADS, cum, 0.0)
    if lead:
        c = -cum
    else:
        @pl.when(pl.program_id(0) % tiles_per_seq == 0)
        def _():
            carry_ref[...] = jnp.zeros_like(carry_ref)
        c = cum + carry_ref[...]
        carry_ref[...] = c[tm - 1:tm, :]
    c_ref[...] = (c * LOG2E)[:, :FOX_HEADS]

    cos = cos_ref[...]
    sin = sin_ref[...]
    rq = _dot(u, w(COL_RQ, COL_RK))
    rk = _dot(u, w(COL_RK, COL_RV))
    for hh in range(RET_HEADS):
        sl = slice(hh * RET_DK, (hh + 1) * RET_DK)
        rq_ref[:, sl] = _rotary_tile(rq[:, sl], cos, sin).astype(BF16)
        kt = _rotary_tile(rk[:, sl], cos, sin) * (RET_DK ** -0.5)
        if lead:
            kt = kt * vmask
        rk_ref[:, sl] = kt.astype(BF16)
    rv = _dot(u, w(COL_RV, COL_RG))
    if lead:
        rv = rv * vmask
    rv_ref[...] = rv.astype(BF16)
    rg = _dot(u, w(COL_RG, COL_GA))
    sg_ref[...] = (rg * jax.nn.sigmoid(rg)).astype(BF16)

    bg = bg_ref[...]
    ga_ref[...] = jax.nn.sigmoid(_dot(u, w(COL_GA, COL_GB)) + bg[:, :D_MODEL]).astype(BF16)
    gb_ref[...] = jax.nn.sigmoid(_dot(u, w(COL_GB, COL_FF)) + bg[:, D_MODEL:]).astype(BF16)


def _proj(h, consts, cos, sin, tri, tm, lead, tiles_per_seq):
    rows = h.shape[0]
    grid = rows // tm

    def rows_spec(width):
        return pl.BlockSpec((tm, width), lambda i: (i, 0))

    pos_spec = pl.BlockSpec((tm, RET_DK), lambda i: (i % tiles_per_seq, 0))
    out_widths = [D_MODEL, D_MODEL, 2 * D_MODEL, FOX_HEADS, RET_QK, RET_QK, D_MODEL, D_MODEL,
                  D_MODEL, D_MODEL]
    out_dtypes = [BF16, BF16, BF16, F32, BF16, BF16, BF16, BF16, BF16, BF16]
    return pl.pallas_call(
        functools.partial(_proj_body, lead=lead, tiles_per_seq=tiles_per_seq),
        grid=(grid,),
        in_specs=[rows_spec(D_MODEL)] + [_const_spec(a.shape, blk) for a, blk in consts]
        + [pos_spec, pos_spec, _const_spec(tri.shape)],
        out_specs=[rows_spec(w) for w in out_widths],
        out_shape=[jax.ShapeDtypeStruct((rows, w), dt) for w, dt in zip(out_widths, out_dtypes)],
        scratch_shapes=[pltpu.VMEM((1, V7X_LANES), F32)],
        compiler_params=_params(("arbitrary",)),
        name="proj_lead" if lead else "proj",
    )(h, *[a for a, _ in consts], cos, sin, tri)


def _fox_body(stab_ref, cs_ref, ce_ref, q_ref, cq_ref, k_ref, v_ref, ck_ref, kl_ref, vl_ref, cl_ref,
              *rest, n_cast, n_cast_short, short_steps):
    srcs, (o_ref, *dsts), (m_ref, acc_ref, e_ref) = (rest[:n_cast], rest[n_cast:2 * n_cast + 1],
                                                     rest[2 * n_cast + 1:])
    step = pl.program_id(0) * pl.num_programs(1) + pl.program_id(1)
    for src, dst in zip(srcs[:n_cast - n_cast_short], dsts):
        dst[...] = src[...].astype(BF16)

    @pl.when(step < short_steps)
    def _():
        for src, dst in zip(srcs[n_cast - n_cast_short:], dsts[n_cast - n_cast_short:]):
            dst[...] = src[...].astype(BF16)

    _fox_attend(stab_ref, cs_ref, ce_ref, q_ref, cq_ref, k_ref, v_ref, ck_ref, kl_ref, vl_ref, cl_ref,
                o_ref, m_ref, acc_ref, e_ref)


def _fox_attend(stab_ref, cs_ref, ce_ref, q_ref, cq_ref, k_ref, v_ref, ck_ref, kl_ref, vl_ref, cl_ref,
                o_ref, m_ref, acc_ref, e_ref):
    tq = m_ref.shape[1]
    tk = tq // 2
    tr = tq // 4
    lane = lax.broadcasted_iota(jnp.int32, (1, 2 * V7X_LANES), 1)
    first2 = (lane % V7X_LANES) < FOX_HD
    first = first2[:, :V7X_LANES]
    lead_ok = lax.broadcasted_iota(jnp.int32, (tq, BLOCK), 1) >= N_EMPTY
    row = lax.broadcasted_iota(jnp.int32, (tq, tk), 0)
    col = lax.broadcasted_iota(jnp.int32, (tq, tk), 1)
    causal = col <= row
    corner = causal[:tr, :tr]
    use_bound = stab_ref[0] > 0.5

    def keys(j):
        ks = pl.multiple_of(j * tk, tk)
        return k_ref[pl.ds(ks, tk), :], ck_ref[:, pl.ds(ks, tk)]

    def values(j):
        return v_ref[pl.ds(pl.multiple_of(j * tk, tk), tk), :]

    nq = q_ref.shape[0] // tq
    nk = k_ref.shape[0] // tk

    def query_block(i, carry):
        _fox_query_block(i, tq, tk, tr, nq, nk, first, first2, lead_ok, row, col, causal, corner,
                         use_bound, keys, values, stab_ref, cs_ref, ce_ref, q_ref, cq_ref, kl_ref,
                         vl_ref, cl_ref, o_ref, m_ref, acc_ref, e_ref)
        return carry

    lax.fori_loop(0, nq, query_block, 0)


def _fox_query_block(i, tq, tk, tr, nq, nk, first, first2, lead_ok, row, col, causal, corner,
                     use_bound, keys, values, stab_ref, cs_ref, ce_ref, q_ref, cq_ref, kl_ref,
                     vl_ref, cl_ref, o_ref, m_ref, acc_ref, e_ref):
    rows = pl.ds(pl.multiple_of(i * tq, tq), tq)
    n_full = 2 * i
    q = q_ref[rows, :]
    zero = jnp.zeros_like(q)
    q_heads = (jnp.where(first, q, zero), jnp.where(first, zero, q))
    acc_ref[...] = jnp.zeros_like(acc_ref)

    @pl.when(use_bound)
    def _():
        head = lax.broadcasted_iota(jnp.int32, (1, FOX_HEADS), 1) - 2 * pl.program_id(1)
        cq_blk = cq_ref[rows, :]
        cq = [jnp.sum(jnp.where(head == hh, cq_blk, 0.0), axis=1, keepdims=True) - stab_ref[1]
              for hh in range(2)]

        def exponents(k, ck, r0=0, r1=tq):
            return [_dot_nt(q_heads[hh][r0:r1], k) + cq[hh][r0:r1] - ck[hh:hh + 1, :] for hh in range(2)]

        def accumulate(e, v, allowed, r0=0):
            if allowed is not None:
                e = [jnp.where(allowed, x, NEG) for x in e]
            pvs = [_dot(jnp.exp2(x).astype(BF16), v) for x in e]
            r1 = r0 + e[0].shape[0]
            acc_ref[r0:r1, :] += jnp.where(first2, pvs[0], pvs[1])

        cutoff = -stab_ref[2]
        slot = 2 * pl.program_id(1)
        cs_base = (pl.program_id(0) * nq + i) * FOX_HEADS
        ce_base = pl.program_id(0) * nk * FOX_HEADS
        c_first = [cs_ref[cs_base + slot + hh] for hh in range(2)]
        skip_lead = jnp.maximum(c_first[0], c_first[1]) < cutoff
        j0 = jnp.int32(0)
        for j in range(nk - 2):
            gap = jnp.maximum(c_first[0] - ce_ref[ce_base + j * FOX_HEADS + slot],
                              c_first[1] - ce_ref[ce_base + j * FOX_HEADS + slot + 1])
            j0 += jnp.logical_and(gap < cutoff, j < n_full).astype(jnp.int32)

        e0 = exponents(*keys(j0))
        e_ref[0] = e0[0]
        e_ref[1] = e0[1]

        @pl.when(jnp.logical_not(skip_lead))
        def _():
            accumulate(exponents(kl_ref[...], cl_ref[...]), vl_ref[...], lead_ok)

        def step(j, carry):
            e_cur = [e_ref[0], e_ref[1]]
            e_next = exponents(*keys(j + 1))
            accumulate(e_cur, values(j), None)
            e_ref[0] = e_next[0]
            e_ref[1] = e_next[1]
            return carry

        def step_pair(m, carry):
            return step(2 * m + 1, step(2 * m, carry))

        @pl.when(j0 % 2 == 1)
        def _():
            step(j0, 0)

        lax.fori_loop((j0 + 1) // 2, i, step_pair, 0)

        kb, ckb = keys(n_full + 1)
        vb = values(n_full + 1)
        e_b1 = exponents(kb[:tr], ckb[:, :tr], 2 * tr, tq)
        e_b2 = exponents(kb[tr:], ckb[:, tr:], 3 * tr, tq)
        va = values(n_full)
        e_a = [e_ref[0], e_ref[1]]
        accumulate([x[:tr, :tr] for x in e_a], va[:tr], corner)
        accumulate([x[tr:] for x in e_a], va, causal[tr:], tr)
        accumulate([x[:tr] for x in e_b1], vb[:tr], corner, 2 * tr)
        accumulate([x[tr:] for x in e_b1], vb[:tr], None, 3 * tr)
        accumulate(e_b2, vb[tr:], corner, 3 * tr)

    @pl.when(jnp.logical_not(use_bound))
    def _():
        m_ref[...] = jnp.full_like(m_ref, NEG)

        def block(k, v, ck, allowed):
            pvs, alphas = [], []
            for hh in range(2):
                s = _dot_nt(q_heads[hh], k) - ck[hh:hh + 1, :]
                if allowed is not None:
                    s = jnp.where(allowed, s, NEG)
                m_old = m_ref[hh]
                m_new = jnp.maximum(m_old, jnp.max(s, axis=1, keepdims=True))
                m_ref[hh] = m_new
                pvs.append(_dot(jnp.exp2(s - m_new).astype(BF16), v))
                alphas.append(jnp.exp2(m_old - m_new))
            acc_ref[...] = (acc_ref[...] * jnp.where(first2, alphas[0], alphas[1])
                            + jnp.where(first2, pvs[0], pvs[1]))

        block(kl_ref[...], vl_ref[...], cl_ref[...], lead_ok)

        def full_block(j, carry):
            block(keys(j)[0], values(j), keys(j)[1], None)
            return carry

        lax.fori_loop(0, n_full, full_block, 0)
        block(keys(n_full)[0], values(n_full), keys(n_full)[1], causal)
        block(keys(n_full + 1)[0], values(n_full + 1), keys(n_full + 1)[1], col + tk <= row)

    acc = acc_ref[...]
    o_ref[rows, :] = (acc[:, :V7X_LANES] / acc[:, V7X_LANES:]).astype(o_ref.dtype)


def _fox(stab, cs, ce, q, cq, k, v, ck, kl, vl, cl, weights, short_weight, batch, seq):
    tq = ATT_BLOCK
    w = V7X_LANES
    steps = batch * FOX_PAIRS
    cast_specs, cast_shapes = [], []
    for a in weights:
        slab = a.shape[0] // steps
        assert slab * steps == a.shape[0] and slab % 16 == 0
        cast_specs.append(pl.BlockSpec((slab, a.shape[1]), lambda b, j: (b * FOX_PAIRS + j, 0)))
        cast_shapes.append(jax.ShapeDtypeStruct(a.shape, BF16))
    short_steps = short_weight.shape[0] // FF_CHUNK
    assert short_steps * FF_CHUNK == short_weight.shape[0] and short_steps <= steps
    cast_specs.append(pl.BlockSpec(
        (FF_CHUNK, short_weight.shape[1]),
        lambda b, j: (jnp.minimum(b * FOX_PAIRS + j, short_steps - 1), 0)))
    cast_shapes.append(jax.ShapeDtypeStruct(short_weight.shape, BF16))
    n_cast = len(cast_specs)
    return pl.pallas_call(
        functools.partial(_fox_body, n_cast=n_cast, n_cast_short=1, short_steps=short_steps),
        grid=(batch, FOX_PAIRS),
        in_specs=[
            pl.BlockSpec(memory_space=pltpu.SMEM),
            pl.BlockSpec(memory_space=pltpu.SMEM),
            pl.BlockSpec(memory_space=pltpu.SMEM),
            pl.BlockSpec((None, seq, w), lambda b, j: (b, 0, j)),
            pl.BlockSpec((None, seq, FOX_HEADS), lambda b, j: (b, 0, 0)),
            pl.BlockSpec((None, seq, w), lambda b, j: (b, 0, j)),
            pl.BlockSpec((None, seq, 2 * w), lambda b, j: (b, 0, j)),
            pl.BlockSpec((None, None, 2, seq), lambda b, j: (b, j, 0, 0)),
            pl.BlockSpec((BLOCK, w), lambda b, j: (0, j)),
            pl.BlockSpec((BLOCK, 2 * w), lambda b, j: (0, j)),
            pl.BlockSpec((None, 2, BLOCK), lambda b, j: (j, 0, 0)),
        ] + cast_specs,
        out_specs=[pl.BlockSpec((None, seq, w), lambda b, j: (b, 0, j))] + cast_specs,
        out_shape=[jax.ShapeDtypeStruct((batch, seq, D_MODEL), BF16)] + cast_shapes,
        scratch_shapes=[pltpu.VMEM((2, tq, 1), F32), pltpu.VMEM((tq, 2 * w), F32),
                        pltpu.VMEM((2, tq, tq // 2), F32)],
        compiler_params=_params(("arbitrary", "arbitrary")),
        name="fox",
    )(stab, cs, ce, q, cq, k, v, ck, kl, vl, cl, *weights, short_weight)


def _ret_body(q_ref, k_ref, v_ref, sg_ref, kl_ref, vl_ref, gn_ref, decay_ref, zeta_ref, xi_ref,
              cd_ref, o_ref, state_ref):
    n_chunks = q_ref.shape[0] // BLOCK
    decay = decay_ref[...]
    zeta = zeta_ref[...]
    xi = xi_ref[...]
    cd = cd_ref[...]
    gn = gn_ref[...]

    state_ref[...] = _dot_tn((kl_ref[...].astype(F32) * zeta).astype(BF16), vl_ref[...])

    def chunk(c, carry):
        rs = pl.multiple_of(c * BLOCK, BLOCK)
        q = q_ref[pl.ds(rs, BLOCK), :]
        k = k_ref[pl.ds(rs, BLOCK), :]
        v = v_ref[pl.ds(rs, BLOCK), :]
        state = state_ref[...]
        scores = (_dot_nt(q, k) * decay).astype(BF16)
        o = _dot(scores, v) + _dot((q.astype(F32) * xi).astype(BF16), state.astype(BF16))
        kv = _dot_tn((k.astype(F32) * zeta).astype(BF16), v)
        state_ref[...] = cd * state + kv
        mu = jnp.mean(o, axis=-1, keepdims=True)
        d = o - mu
        var = jnp.mean(d * d, axis=-1, keepdims=True)
        yn = d * lax.rsqrt(var + GN_EPS) * gn
        o_ref[pl.ds(rs, BLOCK), :] = (sg_ref[pl.ds(rs, BLOCK), :].astype(F32) * yn).astype(o_ref.dtype)
        return carry

    lax.fori_loop(0, n_chunks, chunk, 0, unroll=RET_UNROLL)


def _ret(rq, rk, rv, sg, rkl, rvl, gn, tables, batch, seq):
    decay, zeta, xi, cd = tables
    head_qk = pl.BlockSpec((None, seq, RET_DK), lambda b, h: (b, 0, h))
    head_v = pl.BlockSpec((None, seq, RET_DV), lambda b, h: (b, 0, h))
    table = pl.BlockSpec((None, BLOCK, BLOCK), lambda b, h: (h, 0, 0))
    return pl.pallas_call(
        _ret_body,
        grid=(batch, RET_HEADS),
        in_specs=[head_qk, head_qk, head_v, head_v,
                  pl.BlockSpec((BLOCK, RET_DK), lambda b, h: (0, h)),
                  pl.BlockSpec((BLOCK, RET_DV), lambda b, h: (0, h)),
                  pl.BlockSpec((1, RET_DV), lambda b, h: (0, h)),
                  table, table, table,
                  pl.BlockSpec((None, 1, RET_DV), lambda b, h: (h, 0, 0))],
        out_specs=head_v,
        out_shape=jax.ShapeDtypeStruct((batch, seq, D_MODEL), BF16),
        scratch_shapes=[pltpu.VMEM((RET_DK, RET_DV), F32)],
        compiler_params=_params(("arbitrary", "arbitrary")),
        name="ret",
    )(rq, rk, rv, sg, rkl, rvl, gn, decay, zeta, xi, cd)


def _out_body(ya_ref, yb_ref, ga_ref, gb_ref, h_ref, wa_ref, wb_ref, wo_ref, g_ref, win_ref,
              wout_ref, o_ref):
    mixed = (ga_ref[...].astype(F32) * _dot(ya_ref[...], wa_ref[...])
             + gb_ref[...].astype(F32) * _dot(yb_ref[...], wb_ref[...]))
    h = h_ref[...] + _dot(mixed.astype(BF16), wo_ref[...])
    xn = _rms_rows(h, g_ref[...]).astype(BF16)
    o_ref[...] = h + 0.5 * _swiglu(xn, win_ref, wout_ref)


def _out(ya, yb, ga, gb, h, consts, tm):
    rows = h.shape[0]
    row_spec = pl.BlockSpec((tm, D_MODEL), lambda i: (i, 0))
    return pl.pallas_call(
        _out_body,
        grid=(rows // tm,),
        in_specs=[row_spec] * 5 + [_const_spec(a.shape) for a in consts],
        out_specs=row_spec,
        out_shape=jax.ShapeDtypeStruct((rows, D_MODEL), F32),
        compiler_params=_params(("arbitrary",)),
        name="out",
    )(ya, yb, ga, gb, h, *consts)


def _ffn_weights(w_in, w_out, dtype):
    return w_in.astype(dtype), w_out.astype(dtype).reshape(N_FF_CHUNKS, FF_CHUNK, D_MODEL)


def _position_tables(seq):
    half = RET_DK // 2
    pos = np.arange(BLOCK + seq, dtype=np.float64) - N_EMPTY
    inv = ROPE_BASE ** (-np.arange(half, dtype=np.float64) / half)
    ang = pos[:, None] * inv[None, :]
    cos = np.concatenate([np.cos(ang), np.cos(ang)], axis=1)
    sin = np.concatenate([-np.sin(ang), np.sin(ang)], axis=1)
    return jnp.asarray(cos, dtype=F32), jnp.asarray(sin, dtype=F32)


def _retention_tables():
    log_gamma = np.log1p(-np.exp2(-5.0 - np.arange(RET_HEADS, dtype=np.float64)))
    n = np.arange(BLOCK, dtype=np.float64)
    diff = n[:, None] - n[None, :]
    lg = log_gamma[:, None, None]
    decay = np.where(diff >= 0, np.exp(lg * np.maximum(diff, 0.0)), 0.0)
    zeta = np.exp(log_gamma[:, None] * (BLOCK - 1 - n)[None, :])
    xi = np.exp(log_gamma[:, None] * (n + 1.0)[None, :])
    cd = np.exp(log_gamma * BLOCK)
    lanes = (RET_HEADS, BLOCK, RET_DK)
    tables = (decay, np.broadcast_to(zeta[:, :, None], lanes), np.broadcast_to(xi[:, :, None], lanes),
              np.broadcast_to(cd[:, None, None], (RET_HEADS, 1, RET_DV)))
    return tuple(jnp.asarray(t, dtype=F32) for t in tables)


def kernel(x, meta_tokens, norm_ffn1, w_ffn1_in, w_ffn1_out, norm_mix, w_in, b_forget, b_gate,
           fox_q_norm, fox_k_norm, w_o_fox, ret_gn, w_o_ret, w_out, norm_ffn2, w_ffn2_in,
           w_ffn2_out):
    batch, seq, d = x.shape
    assert d == D_MODEL and seq % ATT_BLOCK == 0 and seq % ROW_TILE == 0
    assert norm_ffn1.shape[0] == 1, "one layer"
    tiles_per_seq = seq // ROW_TILE

    win1, wout1 = _ffn_weights(w_ffn1_in[0], w_ffn1_out[0], F32)
    wi = w_in[0]
    o_r = COL_RQ + FOX_HEADS
    assert wi.shape[1] - o_r == COL_FF - COL_RQ
    w_rest = wi[:, o_r:]
    grp = np.arange(V7X_MXU_DIM) // FOX_HD
    gmat = jnp.asarray((grp[:, None] == grp[None, :]) / FOX_HD, dtype=BF16)
    gq = jnp.tile(fox_q_norm[0], FOX_HEADS)[None, :]
    gk = jnp.tile(fox_k_norm[0], FOX_HEADS)[None, :]
    bf = jnp.pad(b_forget[0], (0, V7X_LANES - FOX_HEADS))[None, :]
    bg = b_gate[0][None, :]
    cos, sin = _position_tables(seq)
    idx = np.arange(ROW_TILE)
    tri_incl = jnp.asarray(idx[None, :] <= idx[:, None], dtype=BF16)
    idx = np.arange(BLOCK)
    tri_after = jnp.asarray(idx[None, :] > idx[:, None], dtype=BF16)

    lead = jnp.concatenate([jnp.zeros((N_EMPTY, d), x.dtype), meta_tokens.astype(x.dtype)], axis=0)
    xr = x.reshape(batch * seq, d)

    h1, wa, w_ff, wb = _ffn_and_cast(xr, norm_ffn1, win1, wout1, wi, w_rest, ROW_TILE)
    h1l = _ffn(lead, norm_ffn1, win1, wout1, BLOCK)
    proj_consts = ((norm_mix, None), (wa, None), (wb, None), (w_ff, None), (gmat, None),
                   (gq, None), (gk, None), (bf, None), (bg, None))

    q, k, v, c, rq, rk, rv, sg, ga, gb = _proj(
        h1, proj_consts, cos[BLOCK:], sin[BLOCK:], tri_incl, ROW_TILE, False, tiles_per_seq)
    _, kl, vl, cl, _, rkl, rvl, _, _, _ = _proj(
        h1l, proj_consts, cos[:BLOCK], sin[:BLOCK], tri_after, BLOCK, True, 1)

    def b3(a):
        return a.reshape(batch, seq, a.shape[-1])

    ck = b3(c).transpose(0, 2, 1).reshape(batch, FOX_PAIRS, 2, seq)
    clt = cl.T.reshape(FOX_PAIRS, 2, BLOCK)
    bound = FOX_HD ** 0.5 * jnp.max(jnp.abs(fox_q_norm[0])) * jnp.max(jnp.abs(fox_k_norm[0]))
    bound2 = bound * LOG2E
    stab = jnp.stack([(bound <= SAFE_LOGIT_BOUND).astype(F32), bound2, F32_EXP2_ZERO + 2.0 * bound2])
    c3 = b3(c)
    cs = c3[:, ::ATT_BLOCK, :].reshape(-1)
    ce = c3[:, ATT_BLOCK // 2 - 1::ATT_BLOCK // 2, :].reshape(-1)
    ya, win2, wof, wor, wo, wout2 = _fox(
        stab, cs, ce, b3(q), c3, b3(k), b3(v), ck, kl, vl, clt,
        (w_ffn2_in[0], w_o_fox[0], w_o_ret[0], w_out[0]), w_ffn2_out[0], batch, seq)
    yb = _ret(b3(rq), b3(rk), b3(rv), b3(sg), rkl, rvl, ret_gn, _retention_tables(), batch, seq)

    out_consts = (wof, wor, wo, norm_ffn2, win2, wout2.reshape(N_FF_CHUNKS, FF_CHUNK, D_MODEL))
    out = _out(ya.reshape(batch * seq, d), yb.reshape(batch * seq, d), ga, gb, h1, out_consts,
               ROW_TILE)
    return out.reshape(batch, seq, d)
```

```python
import functools
import math

import jax
import jax.numpy as jnp
import numpy as np
from jax import lax
from jax.experimental import pallas as pl
from jax.experimental.pallas import tpu as pltpu

F32 = jnp.float32
BF16 = jnp.bfloat16

D_MODEL = 1024
N_META = 16
BLOCK = 128
N_EMPTY = BLOCK - N_META
FOX_HD = 64
FOX_HEADS = D_MODEL // FOX_HD
FOX_PAIRS = FOX_HEADS // 2
RET_HEADS = 4
RET_DK = D_MODEL // (2 * RET_HEADS)
RET_DV = 2 * RET_DK
RET_QK = RET_HEADS * RET_DK
D_FF = ((8 * D_MODEL // 3 + 127) // 128) * 128
EPS = 1e-6
GN_EPS = 1e-5
ROPE_BASE = 10000.0
NEG = -1e30
LOG2E = math.log2(math.e)
SAFE_LOGIT_BOUND = 30.0
F32_EXP2_ZERO = 151.0

V7X_LANES = 128
V7X_MXU_DIM = 256
V7X_VMEM_BYTES = 64 * 2**20
VMEM_LIMIT = V7X_VMEM_BYTES - 8 * 2**20

ROW_TILE = 512
FF_CHUNK = V7X_MXU_DIM
N_FF_CHUNKS = D_FF // FF_CHUNK
RET_UNROLL = 16
ATT_BLOCK = 1024


def _dot(a, b):
    return jnp.dot(a, b, preferred_element_type=F32)


def _dot_nt(a, b):
    return lax.dot_general(a, b, (((1,), (1,)), ((), ())), preferred_element_type=F32)


def _dot_tn(a, b):
    return lax.dot_general(a, b, (((0,), (0,)), ((), ())), preferred_element_type=F32)


def _const_spec(shape, block=None):
    zeros = (0,) * len(shape)
    return pl.BlockSpec(block or shape, lambda *_: zeros, pipeline_mode=pl.Buffered(1))


def _params(semantics):
    return pltpu.CompilerParams(dimension_semantics=semantics, vmem_limit_bytes=VMEM_LIMIT)


def _rms_rows(x, g):
    ms = jnp.mean(x * x, axis=-1, keepdims=True)
    return x * lax.rsqrt(ms + EPS) * g


def _swiglu(xn, win_ref, wout_ref):
    acc = None
    for c in range(N_FF_CHUNKS):
        a = _dot(xn, win_ref[:, c * FF_CHUNK:(c + 1) * FF_CHUNK].astype(BF16))
        b = _dot(xn, win_ref[:, D_FF + c * FF_CHUNK:D_FF + (c + 1) * FF_CHUNK].astype(BF16))
        hm = (a * jax.nn.sigmoid(a) * b).astype(BF16)
        d = _dot(hm, wout_ref[c].astype(BF16))
        acc = d if acc is None else acc + d
    return acc


def _ffn_body(x_ref, g_ref, win_ref, wout_ref, o_ref):
    x = x_ref[...]
    xn = _rms_rows(x, g_ref[...]).astype(BF16)
    o_ref[...] = x + 0.5 * _swiglu(xn, win_ref, wout_ref)


def _ffn(x, g, win, wout, tm):
    rows = x.shape[0]
    row_spec = pl.BlockSpec((tm, D_MODEL), lambda i: (i, 0))
    return pl.pallas_call(
        _ffn_body,
        grid=(rows // tm,),
        in_specs=[row_spec, _const_spec(g.shape), _const_spec(win.shape), _const_spec(wout.shape)],
        out_specs=row_spec,
        out_shape=jax.ShapeDtypeStruct((rows, D_MODEL), F32),
        compiler_params=_params(("arbitrary",)),
        name="ffn",
    )(x, g, win, wout)


def _ffn_cast_body(x_ref, g_ref, win_ref, wout_ref, w_src, o_ref, w_dst, *, cast_steps):
    _ffn_body(x_ref, g_ref, win_ref, wout_ref, o_ref)

    @pl.when(pl.program_id(0) < cast_steps)
    def _():
        w_dst[...] = w_src[...].astype(BF16)


def _ffn_and_cast(x, g, win, wout, w_t, tm):
    rows = x.shape[0]
    steps = rows // tm
    n = w_t.shape[0]
    cast_steps = max(s for s in range(1, steps + 1) if n % s == 0 and (n // s) % 16 == 0)
    slab = n // cast_steps
    row_spec = pl.BlockSpec((tm, D_MODEL), lambda i: (i, 0))
    slab_spec = pl.BlockSpec((slab, w_t.shape[1]), lambda i: (jnp.minimum(i, cast_steps - 1), 0))
    return pl.pallas_call(
        functools.partial(_ffn_cast_body, cast_steps=cast_steps),
        grid=(steps,),
        in_specs=[row_spec, _const_spec(g.shape), _const_spec(win.shape), _const_spec(wout.shape),
                  slab_spec],
        out_specs=[row_spec, slab_spec],
        out_shape=[jax.ShapeDtypeStruct((rows, D_MODEL), F32), jax.ShapeDtypeStruct(w_t.shape, BF16)],
        compiler_params=_params(("arbitrary",)),
        name="ffn_cast",
    )(x, g, win, wout, w_t)


def _group_mean_sq(x, gmat_ref):
    sq = (x * x).astype(BF16)
    w = V7X_MXU_DIM
    parts = [_dot(sq[:, i * w:(i + 1) * w], gmat_ref[...]) for i in range(D_MODEL // w)]
    return jnp.concatenate(parts, axis=1)


def _rotary_tile(x, cos, sin_signed):
    return x * cos + pltpu.roll(x, RET_DK // 2, axis=1) * sin_signed


COL_FQ = 0
COL_FK = COL_FQ + D_MODEL
COL_FV = COL_FK + D_MODEL
COL_FF = COL_FV + D_MODEL
COL_RQ = COL_FF + FOX_HEADS
COL_RK = COL_RQ + RET_QK
COL_RV = COL_RK + RET_QK
COL_RG = COL_RV + D_MODEL
COL_GA = COL_RG + D_MODEL
COL_GB = COL_GA + D_MODEL
COL_END = COL_GB + D_MODEL


def _proj_body(h_ref, gmix_ref, wt_ref, gmat_ref, gq_ref, gk_ref,
               bf_ref, bg_ref, cos_ref, sin_ref, tri_ref,
               q_ref, k_ref, v_ref, c_ref, rq_ref, rk_ref, rv_ref, sg_ref, ga_ref, gb_ref,
               carry_ref, *, lead, tiles_per_seq):
    tm = h_ref.shape[0]
    u = _rms_rows(h_ref[...], gmix_ref[...]).astype(BF16)

    def project(c0, c1):
        return _dot_nt(u, wt_ref[c0:c1, :])

    if lead:
        row = lax.broadcasted_iota(jnp.int32, (tm, 1), 0)
        valid = row >= N_EMPTY
        vmask = valid.astype(F32)

    fq = project(COL_FQ, COL_FK)
    qn = fq * lax.rsqrt(_group_mean_sq(fq, gmat_ref) + EPS) * gq_ref[...]
    q_ref[...] = (qn * (FOX_HD ** -0.5 * LOG2E)).astype(BF16)
    fk = project(COL_FK, COL_FV)
    kn = fk * lax.rsqrt(_group_mean_sq(fk, gmat_ref) + EPS) * gk_ref[...]
    k_ref[...] = kn.astype(BF16)
    fv = project(COL_FV, COL_FF).astype(BF16)
    ones = jnp.ones((tm, V7X_LANES), BF16)
    for j in range(FOX_PAIRS):
        v_ref[:, 2 * j * V7X_LANES:(2 * j + 1) * V7X_LANES] = fv[:, j * V7X_LANES:(j + 1) * V7X_LANES]
        v_ref[:, (2 * j + 1) * V7X_LANES:(2 * j + 2) * V7X_LANES] = ones

    z = project(COL_FF, COL_FF + V7X_LANES) + bf_ref[...]
    logf = -(jnp.maximum(-z, 0.0) + jnp.log1p(jnp.exp(-jnp.abs(z))))
    if lead:
        logf = jnp.where(valid, logf, 0.0)
    lane = lax.broadcasted_iota(jnp.int32, (1, V7X_LANES), 1)
    logf = jnp.where(lane < FOX_HEADS, logf, 0.0)
    p1 = logf.astype(BF16).astype(F32)
    r1 = logf - p1
    p2 = r1.astype(BF16).astype(F32)
    p3 = (r1 - p2).astype(BF16).astype(F32)
    packed = p1 + pltpu.roll(p2, FOX_HEADS, axis=1) + pltpu.roll(p3, 2 * FOX_HEADS, axis=1)
    r = _dot(tri_ref[...], packed.astype(BF16))
    cum = r + (pltpu.roll(r, V7X_LANES - FOX_HEADS, axis=1)
               + pltpu.roll(r, V7X_LANES - 2 * FOX_HEADS, axis=1))
    cum = jnp.where(lane < FOX_HEADS, cum, 0.0)
    if lead:
        c = -cum
    else:
        @pl.when(pl.program_id(0) % tiles_per_seq == 0)
        def _():
            carry_ref[...] = jnp.zeros_like(carry_ref)
        c = cum + carry_ref[...]
        carry_ref[...] = c[tm - 1:tm, :]
    c_ref[...] = (c * LOG2E)[:, :FOX_HEADS]

    cos = cos_ref[...]
    sin = sin_ref[...]
    rq = project(COL_RQ, COL_RK)
    rk = project(COL_RK, COL_RV)
    for hh in range(RET_HEADS):
        sl = slice(hh * RET_DK, (hh + 1) * RET_DK)
        rq_ref[:, sl] = _rotary_tile(rq[:, sl], cos, sin).astype(BF16)
        kt = _rotary_tile(rk[:, sl], cos, sin) * (RET_DK ** -0.5)
        if lead:
            kt = kt * vmask
        rk_ref[:, sl] = kt.astype(BF16)
    rv = project(COL_RV, COL_RG)
    if lead:
        rv = rv * vmask
    rv_ref[...] = rv.astype(BF16)
    rg = project(COL_RG, COL_GA)
    sg_ref[...] = (rg * jax.nn.sigmoid(rg)).astype(BF16)

    bg = bg_ref[...]
    ga_ref[...] = jax.nn.sigmoid(project(COL_GA, COL_GB) + bg[:, :D_MODEL]).astype(BF16)
    gb_ref[...] = jax.nn.sigmoid(project(COL_GB, COL_END) + bg[:, D_MODEL:]).astype(BF16)


def _proj(h, consts, cos, sin, tri, tm, lead, tiles_per_seq):
    rows = h.shape[0]
    grid = rows // tm

    def rows_spec(width):
        return pl.BlockSpec((tm, width), lambda i: (i, 0))

    pos_spec = pl.BlockSpec((tm, RET_DK), lambda i: (i % tiles_per_seq, 0))
    out_widths = [D_MODEL, D_MODEL, 2 * D_MODEL, FOX_HEADS, RET_QK, RET_QK, D_MODEL, D_MODEL,
                  D_MODEL, D_MODEL]
    out_dtypes = [BF16, BF16, BF16, F32, BF16, BF16, BF16, BF16, BF16, BF16]
    return pl.pallas_call(
        functools.partial(_proj_body, lead=lead, tiles_per_seq=tiles_per_seq),
        grid=(grid,),
        in_specs=[rows_spec(D_MODEL)] + [_const_spec(a.shape, blk) for a, blk in consts]
        + [pos_spec, pos_spec, _const_spec(tri.shape)],
        out_specs=[rows_spec(w) for w in out_widths],
        out_shape=[jax.ShapeDtypeStruct((rows, w), dt) for w, dt in zip(out_widths, out_dtypes)],
        scratch_shapes=[pltpu.VMEM((1, V7X_LANES), F32)],
        compiler_params=_params(("arbitrary",)),
        name="proj_lead" if lead else "proj",
    )(h, *[a for a, _ in consts], cos, sin, tri)


def _fox_body(stab_ref, cs_ref, ce_ref, q_ref, cq_ref, k_ref, v_ref, ck_ref, kl_ref, vl_ref, cl_ref,
              *rest, n_cast, n_cast_short, short_steps):
    srcs, (o_ref, *dsts), (m_ref, acc_ref, e_ref) = (rest[:n_cast], rest[n_cast:2 * n_cast + 1],
                                                     rest[2 * n_cast + 1:])
    step = pl.program_id(0) * pl.num_programs(1) + pl.program_id(1)
    for src, dst in zip(srcs[:n_cast - n_cast_short], dsts):
        dst[...] = src[...].astype(BF16)

    @pl.when(step < short_steps)
    def _():
        for src, dst in zip(srcs[n_cast - n_cast_short:], dsts[n_cast - n_cast_short:]):
            dst[...] = src[...].astype(BF16)

    _fox_attend(stab_ref, cs_ref, ce_ref, q_ref, cq_ref, k_ref, v_ref, ck_ref, kl_ref, vl_ref, cl_ref,
                o_ref, m_ref, acc_ref, e_ref)


def _fox_attend(stab_ref, cs_ref, ce_ref, q_ref, cq_ref, k_ref, v_ref, ck_ref, kl_ref, vl_ref, cl_ref,
                o_ref, m_ref, acc_ref, e_ref):
    tq = m_ref.shape[1]
    tk = tq // 2
    tr = tq // 4
    lane = lax.broadcasted_iota(jnp.int32, (1, 2 * V7X_LANES), 1)
    first2 = (lane % V7X_LANES) < FOX_HD
    first = first2[:, :V7X_LANES]
    lead_ok = lax.broadcasted_iota(jnp.int32, (tq, BLOCK), 1) >= N_EMPTY
    row = lax.broadcasted_iota(jnp.int32, (tq, tk), 0)
    col = lax.broadcasted_iota(jnp.int32, (tq, tk), 1)
    causal = col <= row
    corner = causal[:tr, :tr]
    use_bound = stab_ref[0] > 0.5

    def keys(j):
        ks = pl.multiple_of(j * tk, tk)
        return k_ref[pl.ds(ks, tk), :], ck_ref[:, pl.ds(ks, tk)]

    def values(j):
        return v_ref[pl.ds(pl.multiple_of(j * tk, tk), tk), :]

    nq = q_ref.shape[0] // tq
    nk = k_ref.shape[0] // tk

    def query_block(i, carry):
        _fox_query_block(i, tq, tk, tr, nq, nk, first, first2, lead_ok, row, col, causal, corner,
                         use_bound, keys, values, stab_ref, cs_ref, ce_ref, q_ref, cq_ref, kl_ref,
                         vl_ref, cl_ref, o_ref, m_ref, acc_ref, e_ref)
        return carry

    lax.fori_loop(0, nq, query_block, 0)


def _fox_query_block(i, tq, tk, tr, nq, nk, first, first2, lead_ok, row, col, causal, corner,
                     use_bound, keys, values, stab_ref, cs_ref, ce_ref, q_ref, cq_ref, kl_ref,
                     vl_ref, cl_ref, o_ref, m_ref, acc_ref, e_ref):
    rows = pl.ds(pl.multiple_of(i * tq, tq), tq)
    n_full = 2 * i
    q = q_ref[rows, :]
    zero = jnp.zeros_like(q)
    q_heads = (jnp.where(first, q, zero), jnp.where(first, zero, q))
    acc_ref[...] = jnp.zeros_like(acc_ref)

    @pl.when(use_bound)
    def _():
        head = lax.broadcasted_iota(jnp.int32, (1, FOX_HEADS), 1) - 2 * pl.program_id(1)
        cq_blk = cq_ref[rows, :]
        cq = [jnp.sum(jnp.where(head == hh, cq_blk, 0.0), axis=1, keepdims=True) - stab_ref[1]
              for hh in range(2)]

        def exponents(k, ck, r0=0, r1=tq):
            return [_dot_nt(q_heads[hh][r0:r1], k) + cq[hh][r0:r1] - ck[hh:hh + 1, :] for hh in range(2)]

        def accumulate(e, v, allowed, r0=0):
            if allowed is not None:
                e = [jnp.where(allowed, x, NEG) for x in e]
            pvs = [_dot(jnp.exp2(x).astype(BF16), v) for x in e]
            r1 = r0 + e[0].shape[0]
            acc_ref[r0:r1, :] += jnp.where(first2, pvs[0], pvs[1])

        cutoff = -stab_ref[2]
        slot = 2 * pl.program_id(1)
        cs_base = (pl.program_id(0) * nq + i) * FOX_HEADS
        ce_base = pl.program_id(0) * nk * FOX_HEADS
        c_first = [cs_ref[cs_base + slot + hh] for hh in range(2)]
        skip_lead = jnp.maximum(c_first[0], c_first[1]) < cutoff
        j0 = jnp.int32(0)
        for j in range(nk - 2):
            gap = jnp.maximum(c_first[0] - ce_ref[ce_base + j * FOX_HEADS + slot],
                              c_first[1] - ce_ref[ce_base + j * FOX_HEADS + slot + 1])
            j0 += jnp.logical_and(gap < cutoff, j < n_full).astype(jnp.int32)

        e0 = exponents(*keys(j0))
        e_ref[0] = e0[0]
        e_ref[1] = e0[1]

        @pl.when(jnp.logical_not(skip_lead))
        def _():
            accumulate(exponents(kl_ref[...], cl_ref[...]), vl_ref[...], lead_ok)

        def step(j, carry):
            e_cur = [e_ref[0], e_ref[1]]
            e_next = exponents(*keys(j + 1))
            accumulate(e_cur, values(j), None)
            e_ref[0] = e_next[0]
            e_ref[1] = e_next[1]
            return carry

        def step_pair(m, carry):
            return step(2 * m + 1, step(2 * m, carry))

        @pl.when(j0 % 2 == 1)
        def _():
            step(j0, 0)

        lax.fori_loop((j0 + 1) // 2, i, step_pair, 0)

        kb, ckb = keys(n_full + 1)
        vb = values(n_full + 1)
        e_b1 = exponents(kb[:tr], ckb[:, :tr], 2 * tr, tq)
        e_b2 = exponents(kb[tr:], ckb[:, tr:], 3 * tr, tq)
        va = values(n_full)
        e_a = [e_ref[0], e_ref[1]]
        accumulate([x[:tr, :tr] for x in e_a], va[:tr], corner)
        accumulate([x[tr:] for x in e_a], va, causal[tr:], tr)
        accumulate([x[:tr] for x in e_b1], vb[:tr], corner, 2 * tr)
        accumulate([x[tr:] for x in e_b1], vb[:tr], None, 3 * tr)
        accumulate(e_b2, vb[tr:], corner, 3 * tr)

    @pl.when(jnp.logical_not(use_bound))
    def _():
        m_ref[...] = jnp.full_like(m_ref, NEG)

        def block(k, v, ck, allowed):
            pvs, alphas = [], []
            for hh in range(2):
                s = _dot_nt(q_heads[hh], k) - ck[hh:hh + 1, :]
                if allowed is not None:
                    s = jnp.where(allowed, s, NEG)
                m_old = m_ref[hh]
                m_new = jnp.maximum(m_old, jnp.max(s, axis=1, keepdims=True))
                m_ref[hh] = m_new
                pvs.append(_dot(jnp.exp2(s - m_new).astype(BF16), v))
                alphas.append(jnp.exp2(m_old - m_new))
            acc_ref[...] = (acc_ref[...] * jnp.where(first2, alphas[0], alphas[1])
                            + jnp.where(first2, pvs[0], pvs[1]))

        block(kl_ref[...], vl_ref[...], cl_ref[...], lead_ok)

        def full_block(j, carry):
            block(keys(j)[0], values(j), keys(j)[1], None)
            return carry

        lax.fori_loop(0, n_full, full_block, 0)
        block(keys(n_full)[0], values(n_full), keys(n_full)[1], causal)
        block(keys(n_full + 1)[0], values(n_full + 1), keys(n_full + 1)[1], col + tk <= row)

    acc = acc_ref[...]
    o_ref[rows, :] = (acc[:, :V7X_LANES] / acc[:, V7X_LANES:]).astype(o_ref.dtype)


def _fox(stab, cs, ce, q, cq, k, v, ck, kl, vl, cl, weights, short_weight, batch, seq):
    tq = ATT_BLOCK
    w = V7X_LANES
    steps = batch * FOX_PAIRS
    cast_specs, cast_shapes = [], []
    for a in weights:
        slab = a.shape[0] // steps
        assert slab * steps == a.shape[0] and slab % 16 == 0
        cast_specs.append(pl.BlockSpec((slab, a.shape[1]), lambda b, j: (b * FOX_PAIRS + j, 0)))
        cast_shapes.append(jax.ShapeDtypeStruct(a.shape, BF16))
    short_steps = short_weight.shape[0] // FF_CHUNK
    assert short_steps * FF_CHUNK == short_weight.shape[0] and short_steps <= steps
    cast_specs.append(pl.BlockSpec(
        (FF_CHUNK, short_weight.shape[1]),
        lambda b, j: (jnp.minimum(b * FOX_PAIRS + j, short_steps - 1), 0)))
    cast_shapes.append(jax.ShapeDtypeStruct(short_weight.shape, BF16))
    n_cast = len(cast_specs)
    return pl.pallas_call(
        functools.partial(_fox_body, n_cast=n_cast, n_cast_short=1, short_steps=short_steps),
        grid=(batch, FOX_PAIRS),
        in_specs=[
            pl.BlockSpec(memory_space=pltpu.SMEM),
            pl.BlockSpec(memory_space=pltpu.SMEM),
            pl.BlockSpec(memory_space=pltpu.SMEM),
            pl.BlockSpec((None, seq, w), lambda b, j: (b, 0, j)),
            pl.BlockSpec((None, seq, FOX_HEADS), lambda b, j: (b, 0, 0)),
            pl.BlockSpec((None, seq, w), lambda b, j: (b, 0, j)),
            pl.BlockSpec((None, seq, 2 * w), lambda b, j: (b, 0, j)),
            pl.BlockSpec((None, None, 2, seq), lambda b, j: (b, j, 0, 0)),
            pl.BlockSpec((BLOCK, w), lambda b, j: (0, j)),
            pl.BlockSpec((BLOCK, 2 * w), lambda b, j: (0, j)),
            pl.BlockSpec((None, 2, BLOCK), lambda b, j: (j, 0, 0)),
        ] + cast_specs,
        out_specs=[pl.BlockSpec((None, seq, w), lambda b, j: (b, 0, j))] + cast_specs,
        out_shape=[jax.ShapeDtypeStruct((batch, seq, D_MODEL), BF16)] + cast_shapes,
        scratch_shapes=[pltpu.VMEM((2, tq, 1), F32), pltpu.VMEM((tq, 2 * w), F32),
                        pltpu.VMEM((2, tq, tq // 2), F32)],
        compiler_params=_params(("arbitrary", "arbitrary")),
        name="fox",
    )(stab, cs, ce, q, cq, k, v, ck, kl, vl, cl, *weights, short_weight)


def _ret_body(q_ref, k_ref, v_ref, sg_ref, kl_ref, vl_ref, gn_ref, decay_ref, zeta_ref, xi_ref,
              cd_ref, o_ref, state_ref):
    n_chunks = q_ref.shape[0] // BLOCK
    decay = decay_ref[...]
    zeta = zeta_ref[...]
    xi = xi_ref[...]
    cd = cd_ref[...]
    gn = gn_ref[...]

    state_ref[...] = _dot_tn((kl_ref[...].astype(F32) * zeta).astype(BF16), vl_ref[...])

    def chunk(c, carry):
        rs = pl.multiple_of(c * BLOCK, BLOCK)
        q = q_ref[pl.ds(rs, BLOCK), :]
        k = k_ref[pl.ds(rs, BLOCK), :]
        v = v_ref[pl.ds(rs, BLOCK), :]
        state = state_ref[...]
        scores = (_dot_nt(q, k) * decay).astype(BF16)
        o = _dot(scores, v) + _dot((q.astype(F32) * xi).astype(BF16), state.astype(BF16))
        kv = _dot_tn((k.astype(F32) * zeta).astype(BF16), v)
        state_ref[...] = cd * state + kv
        mu = jnp.mean(o, axis=-1, keepdims=True)
        d = o - mu
        var = jnp.mean(d * d, axis=-1, keepdims=True)
        yn = d * lax.rsqrt(var + GN_EPS) * gn
        o_ref[pl.ds(rs, BLOCK), :] = (sg_ref[pl.ds(rs, BLOCK), :].astype(F32) * yn).astype(o_ref.dtype)
        return carry

    lax.fori_loop(0, n_chunks, chunk, 0, unroll=RET_UNROLL)


def _ret(rq, rk, rv, sg, rkl, rvl, gn, tables, batch, seq):
    decay, zeta, xi, cd = tables
    head_qk = pl.BlockSpec((None, seq, RET_DK), lambda b, h: (b, 0, h))
    head_v = pl.BlockSpec((None, seq, RET_DV), lambda b, h: (b, 0, h))
    table = pl.BlockSpec((None, BLOCK, BLOCK), lambda b, h: (h, 0, 0))
    return pl.pallas_call(
        _ret_body,
        grid=(batch, RET_HEADS),
        in_specs=[head_qk, head_qk, head_v, head_v,
                  pl.BlockSpec((BLOCK, RET_DK), lambda b, h: (0, h)),
                  pl.BlockSpec((BLOCK, RET_DV), lambda b, h: (0, h)),
                  pl.BlockSpec((1, RET_DV), lambda b, h: (0, h)),
                  table, table, table,
                  pl.BlockSpec((None, 1, RET_DV), lambda b, h: (h, 0, 0))],
        out_specs=head_v,
        out_shape=jax.ShapeDtypeStruct((batch, seq, D_MODEL), BF16),
        scratch_shapes=[pltpu.VMEM((RET_DK, RET_DV), F32)],
        compiler_params=_params(("arbitrary", "arbitrary")),
        name="ret",
    )(rq, rk, rv, sg, rkl, rvl, gn, decay, zeta, xi, cd)


def _out_body(ya_ref, yb_ref, ga_ref, gb_ref, h_ref, wa_ref, wb_ref, wo_ref, g_ref, win_ref,
              wout_ref, o_ref):
    mixed = (ga_ref[...].astype(F32) * _dot(ya_ref[...], wa_ref[...])
             + gb_ref[...].astype(F32) * _dot(yb_ref[...], wb_ref[...]))
    h = h_ref[...] + _dot(mixed.astype(BF16), wo_ref[...])
    xn = _rms_rows(h, g_ref[...]).astype(BF16)
    o_ref[...] = h + 0.5 * _swiglu(xn, win_ref, wout_ref)


def _out(ya, yb, ga, gb, h, consts, tm):
    rows = h.shape[0]
    row_spec = pl.BlockSpec((tm, D_MODEL), lambda i: (i, 0))
    return pl.pallas_call(
        _out_body,
        grid=(rows // tm,),
        in_specs=[row_spec] * 5 + [_const_spec(a.shape) for a in consts],
        out_specs=row_spec,
        out_shape=jax.ShapeDtypeStruct((rows, D_MODEL), F32),
        compiler_params=_params(("arbitrary",)),
        name="out",
    )(ya, yb, ga, gb, h, *consts)


def _ffn_weights(w_in, w_out, dtype):
    return w_in.astype(dtype), w_out.astype(dtype).reshape(N_FF_CHUNKS, FF_CHUNK, D_MODEL)


def _position_tables(seq):
    half = RET_DK // 2
    pos = np.arange(BLOCK + seq, dtype=np.float64) - N_EMPTY
    inv = ROPE_BASE ** (-np.arange(half, dtype=np.float64) / half)
    ang = pos[:, None] * inv[None, :]
    cos = np.concatenate([np.cos(ang), np.cos(ang)], axis=1)
    sin = np.concatenate([-np.sin(ang), np.sin(ang)], axis=1)
    return jnp.asarray(cos, dtype=F32), jnp.asarray(sin, dtype=F32)


def _retention_tables():
    log_gamma = np.log1p(-np.exp2(-5.0 - np.arange(RET_HEADS, dtype=np.float64)))
    n = np.arange(BLOCK, dtype=np.float64)
    diff = n[:, None] - n[None, :]
    lg = log_gamma[:, None, None]
    decay = np.where(diff >= 0, np.exp(lg * np.maximum(diff, 0.0)), 0.0)
    zeta = np.exp(log_gamma[:, None] * (BLOCK - 1 - n)[None, :])
    xi = np.exp(log_gamma[:, None] * (n + 1.0)[None, :])
    cd = np.exp(log_gamma * BLOCK)
    lanes = (RET_HEADS, BLOCK, RET_DK)
    tables = (decay, np.broadcast_to(zeta[:, :, None], lanes), np.broadcast_to(xi[:, :, None], lanes),
              np.broadcast_to(cd[:, None, None], (RET_HEADS, 1, RET_DV)))
    return tuple(jnp.asarray(t, dtype=F32) for t in tables)


def kernel(x, meta_tokens, norm_ffn1, w_ffn1_in, w_ffn1_out, norm_mix, w_in, b_forget, b_gate,
           fox_q_norm, fox_k_norm, w_o_fox, ret_gn, w_o_ret, w_out, norm_ffn2, w_ffn2_in,
           w_ffn2_out):
    batch, seq, d = x.shape
    assert d == D_MODEL and seq % ATT_BLOCK == 0 and seq % ROW_TILE == 0
    assert norm_ffn1.shape[0] == 1, "one layer"
    tiles_per_seq = seq // ROW_TILE

    win1, wout1 = _ffn_weights(w_ffn1_in[0], w_ffn1_out[0], F32)
    grp = np.arange(V7X_MXU_DIM) // FOX_HD
    gmat = jnp.asarray((grp[:, None] == grp[None, :]) / FOX_HD, dtype=BF16)
    gq = jnp.tile(fox_q_norm[0], FOX_HEADS)[None, :]
    gk = jnp.tile(fox_k_norm[0], FOX_HEADS)[None, :]
    bf = jnp.pad(b_forget[0], (0, V7X_LANES - FOX_HEADS))[None, :]
    bg = b_gate[0][None, :]
    cos, sin = _position_tables(seq)
    idx = np.arange(ROW_TILE)
    tri_incl = jnp.asarray(idx[None, :] <= idx[:, None], dtype=BF16)
    idx = np.arange(BLOCK)
    tri_after = jnp.asarray(idx[None, :] > idx[:, None], dtype=BF16)

    lead = jnp.concatenate([jnp.zeros((N_EMPTY, d), x.dtype), meta_tokens.astype(x.dtype)], axis=0)
    xr = x.reshape(batch * seq, d)

    assert w_in.shape[2] == COL_END
    h1, wt = _ffn_and_cast(xr, norm_ffn1, win1, wout1, w_in[0].T, ROW_TILE)
    h1l = _ffn(lead, norm_ffn1, win1, wout1, BLOCK)
    proj_consts = ((norm_mix, None), (wt, None), (gmat, None), (gq, None), (gk, None), (bf, None),
                   (bg, None))

    q, k, v, c, rq, rk, rv, sg, ga, gb = _proj(
        h1, proj_consts, cos[BLOCK:], sin[BLOCK:], tri_incl, ROW_TILE, False, tiles_per_seq)
    _, kl, vl, cl, _, rkl, rvl, _, _, _ = _proj(
        h1l, proj_consts, cos[:BLOCK], sin[:BLOCK], tri_after, BLOCK, True, 1)

    def b3(a):
        return a.reshape(batch, seq, a.shape[-1])

    ck = b3(c).transpose(0, 2, 1).reshape(batch, FOX_PAIRS, 2, seq)
    clt = cl.T.reshape(FOX_PAIRS, 2, BLOCK)
    bound = FOX_HD ** 0.5 * jnp.max(jnp.abs(fox_q_norm[0])) * jnp.max(jnp.abs(fox_k_norm[0]))
    bound2 = bound * LOG2E
    stab = jnp.stack([(bound <= SAFE_LOGIT_BOUND).astype(F32), bound2, F32_EXP2_ZERO + 2.0 * bound2])
    c3 = b3(c)
    cs = c3[:, ::ATT_BLOCK, :].reshape(-1)
    ce = c3[:, ATT_BLOCK // 2 - 1::ATT_BLOCK // 2, :].reshape(-1)
    ya, win2, wof, wor, wo, wout2 = _fox(
        stab, cs, ce, b3(q), c3, b3(k), b3(v), ck, kl, vl, clt,
        (w_ffn2_in[0], w_o_fox[0], w_o_ret[0], w_out[0]), w_ffn2_out[0], batch, seq)
    yb = _ret(b3(rq), b3(rk), b3(rv), b3(sg), rkl, rvl, ret_gn, _retention_tables(), batch, seq)

    out_consts = (wof, wor, wo, norm_ffn2, win2, wout2.reshape(N_FF_CHUNKS, FF_CHUNK, D_MODEL))
    out = _out(ya.reshape(batch * seq, d), yb.reshape(batch * seq, d), ga, gb, h1, out_consts,
               ROW_TILE)
    return out.reshape(batch, seq, d)
```

```python
import functools
import math

import jax
import jax.numpy as jnp
import numpy as np
from jax import lax
from jax.experimental import pallas as pl
from jax.experimental.pallas import tpu as pltpu

F32 = jnp.float32
BF16 = jnp.bfloat16

D_MODEL = 1024
N_META = 16
BLOCK = 128
N_EMPTY = BLOCK - N_META
FOX_HD = 64
FOX_HEADS = D_MODEL // FOX_HD
FOX_PAIRS = FOX_HEADS // 2
RET_HEADS = 4
RET_DK = D_MODEL // (2 * RET_HEADS)
RET_DV = 2 * RET_DK
RET_QK = RET_HEADS * RET_DK
D_FF = ((8 * D_MODEL // 3 + 127) // 128) * 128
EPS = 1e-6
GN_EPS = 1e-5
ROPE_BASE = 10000.0
NEG = -1e30
LOG2E = math.log2(math.e)
SAFE_LOGIT_BOUND = 30.0
F32_EXP2_ZERO = 151.0

V7X_LANES = 128
V7X_MXU_DIM = 256
V7X_VMEM_BYTES = 64 * 2**20
VMEM_LIMIT = V7X_VMEM_BYTES - 8 * 2**20

ROW_TILE = 512
FF_CHUNK = V7X_MXU_DIM
N_FF_CHUNKS = D_FF // FF_CHUNK
RET_UNROLL = 16
ATT_BLOCK = 1024


def _dot(a, b):
    return jnp.dot(a, b, preferred_element_type=F32)


def _dot_nt(a, b):
    return lax.dot_general(a, b, (((1,), (1,)), ((), ())), preferred_element_type=F32)


def _dot_tn(a, b):
    return lax.dot_general(a, b, (((0,), (0,)), ((), ())), preferred_element_type=F32)


def _const_spec(shape, block=None):
    zeros = (0,) * len(shape)
    return pl.BlockSpec(block or shape, lambda *_: zeros, pipeline_mode=pl.Buffered(1))


def _params(semantics):
    return pltpu.CompilerParams(dimension_semantics=semantics, vmem_limit_bytes=VMEM_LIMIT)


def _rms_rows(x, g):
    ms = jnp.mean(x * x, axis=-1, keepdims=True)
    return x * lax.rsqrt(ms + EPS) * g


def _swiglu(xn, win_ref, wout_ref):
    acc = None
    for c in range(N_FF_CHUNKS):
        a = _dot(xn, win_ref[:, c * FF_CHUNK:(c + 1) * FF_CHUNK].astype(BF16))
        b = _dot(xn, win_ref[:, D_FF + c * FF_CHUNK:D_FF + (c + 1) * FF_CHUNK].astype(BF16))
        hm = (a * jax.nn.sigmoid(a) * b).astype(BF16)
        d = _dot(hm, wout_ref[c].astype(BF16))
        acc = d if acc is None else acc + d
    return acc


def _ffn_body(x_ref, g_ref, win_ref, wout_ref, o_ref):
    x = x_ref[...]
    xn = _rms_rows(x, g_ref[...]).astype(BF16)
    o_ref[...] = x + 0.5 * _swiglu(xn, win_ref, wout_ref)


def _ffn(x, g, win, wout, tm):
    rows = x.shape[0]
    row_spec = pl.BlockSpec((tm, D_MODEL), lambda i: (i, 0))
    return pl.pallas_call(
        _ffn_body,
        grid=(rows // tm,),
        in_specs=[row_spec, _const_spec(g.shape), _const_spec(win.shape), _const_spec(wout.shape)],
        out_specs=row_spec,
        out_shape=jax.ShapeDtypeStruct((rows, D_MODEL), F32),
        compiler_params=_params(("arbitrary",)),
        name="ffn",
    )(x, g, win, wout)


def _ffn_cast_body(x_ref, g_ref, win_ref, wout_ref, w_src, o_ref, w_dst, *, cast_steps):
    _ffn_body(x_ref, g_ref, win_ref, wout_ref, o_ref)

    @pl.when(pl.program_id(0) < cast_steps)
    def _():
        w_dst[...] = w_src[...].astype(BF16)


def _ffn_and_cast(x, g, win, wout, w_t, tm):
    rows = x.shape[0]
    steps = rows // tm
    n = w_t.shape[0]
    cast_steps = max(s for s in range(1, steps + 1) if n % s == 0 and (n // s) % 16 == 0)
    slab = n // cast_steps
    row_spec = pl.BlockSpec((tm, D_MODEL), lambda i: (i, 0))
    slab_spec = pl.BlockSpec((slab, w_t.shape[1]), lambda i: (jnp.minimum(i, cast_steps - 1), 0))
    return pl.pallas_call(
        functools.partial(_ffn_cast_body, cast_steps=cast_steps),
        grid=(steps,),
        in_specs=[row_spec, _const_spec(g.shape), _const_spec(win.shape), _const_spec(wout.shape),
                  slab_spec],
        out_specs=[row_spec, slab_spec],
        out_shape=[jax.ShapeDtypeStruct((rows, D_MODEL), F32), jax.ShapeDtypeStruct(w_t.shape, BF16)],
        compiler_params=_params(("arbitrary",)),
        name="ffn_cast",
    )(x, g, win, wout, w_t)


def _group_mean_sq(x, gmat_ref):
    sq = (x * x).astype(BF16)
    w = V7X_MXU_DIM
    parts = [_dot(sq[:, i * w:(i + 1) * w], gmat_ref[...]) for i in range(D_MODEL // w)]
    return jnp.concatenate(parts, axis=1)


def _rotary_tile(x, cos, sin_signed):
    return x * cos + pltpu.roll(x, RET_DK // 2, axis=1) * sin_signed


COL_FQ = 0
COL_FK = COL_FQ + D_MODEL
COL_FV = COL_FK + D_MODEL
COL_FF = COL_FV + D_MODEL
COL_RQ = COL_FF + FOX_HEADS
COL_RK = COL_RQ + RET_QK
COL_RV = COL_RK + RET_QK
COL_RG = COL_RV + D_MODEL
COL_GA = COL_RG + D_MODEL
COL_GB = COL_GA + D_MODEL
COL_END = COL_GB + D_MODEL


def _proj_body(h_ref, gmix_ref, wt_ref, gmat_ref, gq_ref, gk_ref,
               bf_ref, bg_ref, qs_ref, ks_ref, cos_ref, sin_ref, tri_ref,
               q_ref, k_ref, v_ref, c_ref, rq_ref, rk_ref, rv_ref, sg_ref, ga_ref, gb_ref,
               carry_ref, *, lead, tiles_per_seq):
    tm = h_ref.shape[0]
    u = _rms_rows(h_ref[...], gmix_ref[...]).astype(BF16)

    def project(c0, c1):
        return _dot_nt(u, wt_ref[c0:c1, :])

    if lead:
        row = lax.broadcasted_iota(jnp.int32, (tm, 1), 0)
        valid = row >= N_EMPTY
        vmask = valid.astype(F32)

    fq = project(COL_FQ, COL_FK)
    qn = fq * lax.rsqrt(_group_mean_sq(fq, gmat_ref) + EPS) * gq_ref[...]
    q_ref[...] = (qn * (FOX_HD ** -0.5 * LOG2E)).astype(BF16)
    fk = project(COL_FK, COL_FV)
    kn = fk * lax.rsqrt(_group_mean_sq(fk, gmat_ref) + EPS) * gk_ref[...]
    k_ref[...] = kn.astype(BF16)
    fv = project(COL_FV, COL_FF).astype(BF16)
    ones = jnp.ones((tm, V7X_LANES), BF16)
    for j in range(FOX_PAIRS):
        v_ref[:, 2 * j * V7X_LANES:(2 * j + 1) * V7X_LANES] = fv[:, j * V7X_LANES:(j + 1) * V7X_LANES]
        v_ref[:, (2 * j + 1) * V7X_LANES:(2 * j + 2) * V7X_LANES] = ones

    z = project(COL_FF, COL_FF + V7X_LANES) + bf_ref[...]
    logf = -(jnp.maximum(-z, 0.0) + jnp.log1p(jnp.exp(-jnp.abs(z))))
    if lead:
        logf = jnp.where(valid, logf, 0.0)
    lane = lax.broadcasted_iota(jnp.int32, (1, V7X_LANES), 1)
    logf = jnp.where(lane < FOX_HEADS, logf, 0.0)
    p1 = logf.astype(BF16).astype(F32)
    r1 = logf - p1
    p2 = r1.astype(BF16).astype(F32)
    p3 = (r1 - p2).astype(BF16).astype(F32)
    packed = p1 + pltpu.roll(p2, FOX_HEADS, axis=1) + pltpu.roll(p3, 2 * FOX_HEADS, axis=1)
    r = _dot(tri_ref[...], packed.astype(BF16))
    cum = r + (pltpu.roll(r, V7X_LANES - FOX_HEADS, axis=1)
               + pltpu.roll(r, V7X_LANES - 2 * FOX_HEADS, axis=1))
    cum = jnp.where(lane < FOX_HEADS, cum, 0.0)
    if lead:
        c = -cum
    else:
        @pl.when(pl.program_id(0) % tiles_per_seq == 0)
        def _():
            carry_ref[...] = jnp.zeros_like(carry_ref)
        c = cum + carry_ref[...]
        carry_ref[...] = c[tm - 1:tm, :]
    c_ref[...] = (c * LOG2E)[:, :FOX_HEADS]

    cos = cos_ref[...]
    sin = sin_ref[...]
    rq = project(COL_RQ, COL_RK)
    rk = project(COL_RK, COL_RV)
    for hh in range(RET_HEADS):
        sl = slice(hh * RET_DK, (hh + 1) * RET_DK)
        rq_ref[:, sl] = (_rotary_tile(rq[:, sl], cos, sin) * qs_ref[hh]).astype(BF16)
        kt = _rotary_tile(rk[:, sl], cos, sin) * ks_ref[hh]
        if lead:
            kt = kt * vmask
        rk_ref[:, sl] = kt.astype(BF16)
    rv = project(COL_RV, COL_RG)
    if lead:
        rv = rv * vmask
    rv_ref[...] = rv.astype(BF16)
    rg = project(COL_RG, COL_GA)
    sg_ref[...] = (rg * jax.nn.sigmoid(rg)).astype(BF16)

    bg = bg_ref[...]
    ga_ref[...] = jax.nn.sigmoid(project(COL_GA, COL_GB) + bg[:, :D_MODEL]).astype(BF16)
    gb_ref[...] = jax.nn.sigmoid(project(COL_GB, COL_END) + bg[:, D_MODEL:]).astype(BF16)


def _proj(h, consts, cos, sin, tri, tm, lead, tiles_per_seq):
    rows = h.shape[0]
    grid = rows // tm

    def rows_spec(width):
        return pl.BlockSpec((tm, width), lambda i: (i, 0))

    pos_spec = pl.BlockSpec((tm, RET_DK), lambda i: (i % tiles_per_seq, 0))
    out_widths = [D_MODEL, D_MODEL, 2 * D_MODEL, FOX_HEADS, RET_QK, RET_QK, D_MODEL, D_MODEL,
                  D_MODEL, D_MODEL]
    out_dtypes = [BF16, BF16, BF16, F32, BF16, BF16, BF16, BF16, BF16, BF16]
    return pl.pallas_call(
        functools.partial(_proj_body, lead=lead, tiles_per_seq=tiles_per_seq),
        grid=(grid,),
        in_specs=[rows_spec(D_MODEL)] + [_const_spec(a.shape, blk) for a, blk in consts]
        + [pos_spec, pos_spec, _const_spec(tri.shape)],
        out_specs=[rows_spec(w) for w in out_widths],
        out_shape=[jax.ShapeDtypeStruct((rows, w), dt) for w, dt in zip(out_widths, out_dtypes)],
        scratch_shapes=[pltpu.VMEM((1, V7X_LANES), F32)],
        compiler_params=_params(("arbitrary",)),
        name="proj_lead" if lead else "proj",
    )(h, *[a for a, _ in consts], cos, sin, tri)


def _fox_body(stab_ref, cs_ref, ce_ref, q_ref, cq_ref, k_ref, v_ref, ck_ref, kl_ref, vl_ref, cl_ref,
              *rest, n_cast, n_cast_short, short_steps):
    srcs, (o_ref, *dsts), (m_ref, acc_ref, e_ref) = (rest[:n_cast], rest[n_cast:2 * n_cast + 1],
                                                     rest[2 * n_cast + 1:])
    step = pl.program_id(0) * pl.num_programs(1) + pl.program_id(1)
    for src, dst in zip(srcs[:n_cast - n_cast_short], dsts):
        dst[...] = src[...].astype(BF16)

    @pl.when(step < short_steps)
    def _():
        for src, dst in zip(srcs[n_cast - n_cast_short:], dsts[n_cast - n_cast_short:]):
            dst[...] = src[...].astype(BF16)

    _fox_attend(stab_ref, cs_ref, ce_ref, q_ref, cq_ref, k_ref, v_ref, ck_ref, kl_ref, vl_ref, cl_ref,
                o_ref, m_ref, acc_ref, e_ref)


def _fox_attend(stab_ref, cs_ref, ce_ref, q_ref, cq_ref, k_ref, v_ref, ck_ref, kl_ref, vl_ref, cl_ref,
                o_ref, m_ref, acc_ref, e_ref):
    tq = m_ref.shape[1]
    tk = tq // 2
    tr = tq // 4
    lane = lax.broadcasted_iota(jnp.int32, (1, 2 * V7X_LANES), 1)
    first2 = (lane % V7X_LANES) < FOX_HD
    first = first2[:, :V7X_LANES]
    lead_ok = lax.broadcasted_iota(jnp.int32, (tq, BLOCK), 1) >= N_EMPTY
    row = lax.broadcasted_iota(jnp.int32, (tq, tk), 0)
    col = lax.broadcasted_iota(jnp.int32, (tq, tk), 1)
    causal = col <= row
    corner = causal[:tr, :tr]
    use_bound = stab_ref[0] > 0.5

    def keys(j):
        ks = pl.multiple_of(j * tk, tk)
        return k_ref[pl.ds(ks, tk), :], ck_ref[:, pl.ds(ks, tk)]

    def values(j):
        return v_ref[pl.ds(pl.multiple_of(j * tk, tk), tk), :]

    nq = q_ref.shape[0] // tq
    nk = k_ref.shape[0] // tk

    def query_block(i, carry):
        _fox_query_block(i, tq, tk, tr, nq, nk, first, first2, lead_ok, row, col, causal, corner,
                         use_bound, keys, values, stab_ref, cs_ref, ce_ref, q_ref, cq_ref, kl_ref,
                         vl_ref, cl_ref, o_ref, m_ref, acc_ref, e_ref)
        return carry

    lax.fori_loop(0, nq, query_block, 0)


def _fox_query_block(i, tq, tk, tr, nq, nk, first, first2, lead_ok, row, col, causal, corner,
                     use_bound, keys, values, stab_ref, cs_ref, ce_ref, q_ref, cq_ref, kl_ref,
                     vl_ref, cl_ref, o_ref, m_ref, acc_ref, e_ref):
    rows = pl.ds(pl.multiple_of(i * tq, tq), tq)
    n_full = 2 * i
    q = q_ref[rows, :]
    zero = jnp.zeros_like(q)
    q_heads = (jnp.where(first, q, zero), jnp.where(first, zero, q))
    acc_ref[...] = jnp.zeros_like(acc_ref)

    @pl.when(use_bound)
    def _():
        head = lax.broadcasted_iota(jnp.int32, (1, FOX_HEADS), 1) - 2 * pl.program_id(1)
        cq_blk = cq_ref[rows, :]
        cq = [jnp.sum(jnp.where(head == hh, cq_blk, 0.0), axis=1, keepdims=True) - stab_ref[1]
              for hh in range(2)]

        def exponents(k, ck, r0=0, r1=tq):
            return [_dot_nt(q_heads[hh][r0:r1], k) + cq[hh][r0:r1] - ck[hh:hh + 1, :] for hh in range(2)]

        def accumulate(e, v, allowed, r0=0):
            if allowed is not None:
                e = [jnp.where(allowed, x, NEG) for x in e]
            pvs = [_dot(jnp.exp2(x).astype(BF16), v) for x in e]
            r1 = r0 + e[0].shape[0]
            acc_ref[r0:r1, :] += jnp.where(first2, pvs[0], pvs[1])

        cutoff = -stab_ref[2]
        slot = 2 * pl.program_id(1)
        cs_base = (pl.program_id(0) * nq + i) * FOX_HEADS
        ce_base = pl.program_id(0) * nk * FOX_HEADS
        c_first = [cs_ref[cs_base + slot + hh] for hh in range(2)]
        skip_lead = jnp.maximum(c_first[0], c_first[1]) < cutoff
        j0 = jnp.int32(0)
        for j in range(nk - 2):
            gap = jnp.maximum(c_first[0] - ce_ref[ce_base + j * FOX_HEADS + slot],
                              c_first[1] - ce_ref[ce_base + j * FOX_HEADS + slot + 1])
            j0 += jnp.logical_and(gap < cutoff, j < n_full).astype(jnp.int32)

        e0 = exponents(*keys(j0))
        e_ref[0] = e0[0]
        e_ref[1] = e0[1]

        @pl.when(jnp.logical_not(skip_lead))
        def _():
            accumulate(exponents(kl_ref[...], cl_ref[...]), vl_ref[...], lead_ok)

        def step(j, carry):
            e_cur = [e_ref[0], e_ref[1]]
            e_next = exponents(*keys(j + 1))
            accumulate(e_cur, values(j), None)
            e_ref[0] = e_next[0]
            e_ref[1] = e_next[1]
            return carry

        def step_pair(m, carry):
            return step(2 * m + 1, step(2 * m, carry))

        @pl.when(j0 % 2 == 1)
        def _():
            step(j0, 0)

        lax.fori_loop((j0 + 1) // 2, i, step_pair, 0)

        kb, ckb = keys(n_full + 1)
        vb = values(n_full + 1)
        e_b1 = exponents(kb[:tr], ckb[:, :tr], 2 * tr, tq)
        e_b2 = exponents(kb[tr:], ckb[:, tr:], 3 * tr, tq)
        va = values(n_full)
        e_a = [e_ref[0], e_ref[1]]
        strips = [
            [([x[:tr, :tr] for x in e_a], va[:tr], corner)],
            [([x[tr:2 * tr] for x in e_a], va, causal[tr:2 * tr])],
            [([x[2 * tr:3 * tr] for x in e_a], va, None), ([x[:tr] for x in e_b1], vb[:tr], corner)],
            [([x[3 * tr:] for x in e_a], va, None), ([x[tr:] for x in e_b1], vb[:tr], None),
             (e_b2, vb[tr:], corner)],
        ]
        for s, parts in enumerate(strips):
            pv = [None, None]
            for e, v, allowed in parts:
                if allowed is not None:
                    e = [jnp.where(allowed, x, NEG) for x in e]
                for hh in range(2):
                    d = _dot(jnp.exp2(e[hh]).astype(BF16), v)
                    pv[hh] = d if pv[hh] is None else pv[hh] + d
            acc_ref[s * tr:(s + 1) * tr, :] += jnp.where(first2, pv[0], pv[1])

    @pl.when(jnp.logical_not(use_bound))
    def _():
        m_ref[...] = jnp.full_like(m_ref, NEG)

        def block(k, v, ck, allowed):
            pvs, alphas = [], []
            for hh in range(2):
                s = _dot_nt(q_heads[hh], k) - ck[hh:hh + 1, :]
                if allowed is not None:
                    s = jnp.where(allowed, s, NEG)
                m_old = m_ref[hh]
                m_new = jnp.maximum(m_old, jnp.max(s, axis=1, keepdims=True))
                m_ref[hh] = m_new
                pvs.append(_dot(jnp.exp2(s - m_new).astype(BF16), v))
                alphas.append(jnp.exp2(m_old - m_new))
            acc_ref[...] = (acc_ref[...] * jnp.where(first2, alphas[0], alphas[1])
                            + jnp.where(first2, pvs[0], pvs[1]))

        block(kl_ref[...], vl_ref[...], cl_ref[...], lead_ok)

        def full_block(j, carry):
            block(keys(j)[0], values(j), keys(j)[1], None)
            return carry

        lax.fori_loop(0, n_full, full_block, 0)
        block(keys(n_full)[0], values(n_full), keys(n_full)[1], causal)
        block(keys(n_full + 1)[0], values(n_full + 1), keys(n_full + 1)[1], col + tk <= row)

    acc = acc_ref[...]
    o_ref[rows, :] = (acc[:, :V7X_LANES] / acc[:, V7X_LANES:]).astype(o_ref.dtype)


def _fox(stab, cs, ce, q, cq, k, v, ck, kl, vl, cl, weights, short_weight, batch, seq):
    tq = ATT_BLOCK
    w = V7X_LANES
    steps = batch * FOX_PAIRS
    cast_specs, cast_shapes = [], []
    for a in weights:
        slab = a.shape[0] // steps
        assert slab * steps == a.shape[0] and slab % 16 == 0
        cast_specs.append(pl.BlockSpec((slab, a.shape[1]), lambda b, j: (b * FOX_PAIRS + j, 0)))
        cast_shapes.append(jax.ShapeDtypeStruct(a.shape, BF16))
    short_steps = short_weight.shape[0] // FF_CHUNK
    assert short_steps * FF_CHUNK == short_weight.shape[0] and short_steps <= steps
    cast_specs.append(pl.BlockSpec(
        (FF_CHUNK, short_weight.shape[1]),
        lambda b, j: (jnp.minimum(b * FOX_PAIRS + j, short_steps - 1), 0)))
    cast_shapes.append(jax.ShapeDtypeStruct(short_weight.shape, BF16))
    n_cast = len(cast_specs)
    return pl.pallas_call(
        functools.partial(_fox_body, n_cast=n_cast, n_cast_short=1, short_steps=short_steps),
        grid=(batch, FOX_PAIRS),
        in_specs=[
            pl.BlockSpec(memory_space=pltpu.SMEM),
            pl.BlockSpec(memory_space=pltpu.SMEM),
            pl.BlockSpec(memory_space=pltpu.SMEM),
            pl.BlockSpec((None, seq, w), lambda b, j: (b, 0, j)),
            pl.BlockSpec((None, seq, FOX_HEADS), lambda b, j: (b, 0, 0)),
            pl.BlockSpec((None, seq, w), lambda b, j: (b, 0, j)),
            pl.BlockSpec((None, seq, 2 * w), lambda b, j: (b, 0, j)),
            pl.BlockSpec((None, None, 2, seq), lambda b, j: (b, j, 0, 0)),
            pl.BlockSpec((BLOCK, w), lambda b, j: (0, j)),
            pl.BlockSpec((BLOCK, 2 * w), lambda b, j: (0, j)),
            pl.BlockSpec((None, 2, BLOCK), lambda b, j: (j, 0, 0)),
        ] + cast_specs,
        out_specs=[pl.BlockSpec((None, seq, w), lambda b, j: (b, 0, j))] + cast_specs,
        out_shape=[jax.ShapeDtypeStruct((batch, seq, D_MODEL), BF16)] + cast_shapes,
        scratch_shapes=[pltpu.VMEM((2, tq, 1), F32), pltpu.VMEM((tq, 2 * w), F32),
                        pltpu.VMEM((2, tq, tq // 2), F32)],
        compiler_params=_params(("arbitrary", "arbitrary")),
        name="fox",
    )(stab, cs, ce, q, cq, k, v, ck, kl, vl, cl, *weights, short_weight)


def _ret_body(q_ref, k_ref, v_ref, sg_ref, kl_ref, vl_ref, gn_ref, cd_ref, o_ref, kv_ref, state_ref):
    n_chunks = q_ref.shape[0] // BLOCK
    cd = cd_ref[...]
    gn = gn_ref[...]
    causal = (lax.broadcasted_iota(jnp.int32, (BLOCK, BLOCK), 1)
              <= lax.broadcasted_iota(jnp.int32, (BLOCK, BLOCK), 0))

    def rows(c):
        return pl.ds(pl.multiple_of(c * BLOCK, BLOCK), BLOCK)

    def summarise(c, carry):
        kv_ref[c] = _dot_tn(k_ref[rows(c), :], v_ref[rows(c), :])
        return carry

    lax.fori_loop(0, n_chunks, summarise, 0, unroll=RET_UNROLL)

    def advance(c, state):
        state_ref[c] = state.astype(BF16)
        return cd * (state + kv_ref[c])

    lead_state = cd * _dot_tn(kl_ref[...], vl_ref[...])
    lax.fori_loop(0, n_chunks, advance, lead_state, unroll=RET_UNROLL)

    def emit(c, carry):
        q = q_ref[rows(c), :]
        v = v_ref[rows(c), :]
        scores = jnp.where(causal, _dot_nt(q, k_ref[rows(c), :]), 0.0).astype(BF16)
        o = _dot(jnp.concatenate([scores, q], axis=1), jnp.concatenate([v, state_ref[c]], axis=0))
        mu = jnp.mean(o, axis=-1, keepdims=True)
        d = o - mu
        var = jnp.mean(d * d, axis=-1, keepdims=True)
        yn = d * lax.rsqrt(var + GN_EPS) * gn
        o_ref[rows(c), :] = (sg_ref[rows(c), :].astype(F32) * yn).astype(o_ref.dtype)
        return carry

    lax.fori_loop(0, n_chunks, emit, 0, unroll=RET_UNROLL)


def _ret(rq, rk, rv, sg, rkl, rvl, gn, cd, batch, seq):
    head_qk = pl.BlockSpec((None, seq, RET_DK), lambda b, h: (b, 0, h))
    head_v = pl.BlockSpec((None, seq, RET_DV), lambda b, h: (b, 0, h))
    return pl.pallas_call(
        _ret_body,
        grid=(batch, RET_HEADS),
        in_specs=[head_qk, head_qk, head_v, head_v,
                  pl.BlockSpec((BLOCK, RET_DK), lambda b, h: (0, h)),
                  pl.BlockSpec((BLOCK, RET_DV), lambda b, h: (0, h)),
                  pl.BlockSpec((1, RET_DV), lambda b, h: (0, h)),
                  pl.BlockSpec((None, 1, RET_DV), lambda b, h: (h, 0, 0))],
        out_specs=head_v,
        out_shape=jax.ShapeDtypeStruct((batch, seq, D_MODEL), BF16),
        scratch_shapes=[pltpu.VMEM((seq // BLOCK, RET_DK, RET_DV), F32),
                        pltpu.VMEM((seq // BLOCK, RET_DK, RET_DV), BF16)],
        compiler_params=_params(("arbitrary", "arbitrary")),
        name="ret",
    )(rq, rk, rv, sg, rkl, rvl, gn, cd)


def _out_body(ya_ref, yb_ref, ga_ref, gb_ref, h_ref, wa_ref, wb_ref, wo_ref, g_ref, win_ref,
              wout_ref, o_ref):
    mixed = (ga_ref[...].astype(F32) * _dot(ya_ref[...], wa_ref[...])
             + gb_ref[...].astype(F32) * _dot(yb_ref[...], wb_ref[...]))
    h = h_ref[...] + _dot(mixed.astype(BF16), wo_ref[...])
    xn = _rms_rows(h, g_ref[...]).astype(BF16)
    o_ref[...] = h + 0.5 * _swiglu(xn, win_ref, wout_ref)


def _out(ya, yb, ga, gb, h, consts, tm):
    rows = h.shape[0]
    row_spec = pl.BlockSpec((tm, D_MODEL), lambda i: (i, 0))
    return pl.pallas_call(
        _out_body,
        grid=(rows // tm,),
        in_specs=[row_spec] * 5 + [_const_spec(a.shape) for a in consts],
        out_specs=row_spec,
        out_shape=jax.ShapeDtypeStruct((rows, D_MODEL), F32),
        compiler_params=_params(("arbitrary",)),
        name="out",
    )(ya, yb, ga, gb, h, *consts)


def _ffn_weights(w_in, w_out, dtype):
    return w_in.astype(dtype), w_out.astype(dtype).reshape(N_FF_CHUNKS, FF_CHUNK, D_MODEL)


def _position_tables(seq):
    half = RET_DK // 2
    pos = np.arange(BLOCK + seq, dtype=np.float64) - N_EMPTY
    inv = ROPE_BASE ** (-np.arange(half, dtype=np.float64) / half)
    ang = pos[:, None] * inv[None, :]
    cos = np.concatenate([np.cos(ang), np.cos(ang)], axis=1)
    sin = np.concatenate([-np.sin(ang), np.sin(ang)], axis=1)
    return jnp.asarray(cos, dtype=F32), jnp.asarray(sin, dtype=F32)


def _retention_tables(rows):
    log_gamma = np.log1p(-np.exp2(-5.0 - np.arange(RET_HEADS, dtype=np.float64)))
    n = (np.arange(rows) % BLOCK + 1.0)[None, :, None]
    lg = log_gamma[:, None, None]
    shape = (RET_HEADS, rows, RET_DK)
    q_scale = np.broadcast_to(np.exp(lg * n), shape)
    k_scale = np.broadcast_to(np.exp(-lg * n) * RET_DK ** -0.5, shape)
    cd = np.broadcast_to(np.exp(log_gamma * BLOCK)[:, None, None], (RET_HEADS, 1, RET_DV))
    return tuple(jnp.asarray(t, dtype=F32) for t in (q_scale, k_scale, cd))


def kernel(x, meta_tokens, norm_ffn1, w_ffn1_in, w_ffn1_out, norm_mix, w_in, b_forget, b_gate,
           fox_q_norm, fox_k_norm, w_o_fox, ret_gn, w_o_ret, w_out, norm_ffn2, w_ffn2_in,
           w_ffn2_out):
    batch, seq, d = x.shape
    assert d == D_MODEL and seq % ATT_BLOCK == 0 and seq % ROW_TILE == 0
    assert norm_ffn1.shape[0] == 1, "one layer"
    tiles_per_seq = seq // ROW_TILE

    win1, wout1 = _ffn_weights(w_ffn1_in[0], w_ffn1_out[0], F32)
    grp = np.arange(V7X_MXU_DIM) // FOX_HD
    gmat = jnp.asarray((grp[:, None] == grp[None, :]) / FOX_HD, dtype=BF16)
    gq = jnp.tile(fox_q_norm[0], FOX_HEADS)[None, :]
    gk = jnp.tile(fox_k_norm[0], FOX_HEADS)[None, :]
    bf = jnp.pad(b_forget[0], (0, V7X_LANES - FOX_HEADS))[None, :]
    bg = b_gate[0][None, :]
    cos, sin = _position_tables(seq)
    idx = np.arange(ROW_TILE)
    tri_incl = jnp.asarray(idx[None, :] <= idx[:, None], dtype=BF16)
    idx = np.arange(BLOCK)
    tri_after = jnp.asarray(idx[None, :] > idx[:, None], dtype=BF16)

    lead = jnp.concatenate([jnp.zeros((N_EMPTY, d), x.dtype), meta_tokens.astype(x.dtype)], axis=0)
    xr = x.reshape(batch * seq, d)

    assert w_in.shape[2] == COL_END
    h1, wt = _ffn_and_cast(xr, norm_ffn1, win1, wout1, w_in[0].T, ROW_TILE)
    h1l = _ffn(lead, norm_ffn1, win1, wout1, BLOCK)
    proj_consts = ((norm_mix, None), (wt, None), (gmat, None), (gq, None), (gk, None), (bf, None),
                   (bg, None))
    q_scale, k_scale, chunk_decay = _retention_tables(ROW_TILE)

    q, k, v, c, rq, rk, rv, sg, ga, gb = _proj(
        h1, proj_consts + ((q_scale, None), (k_scale, None)), cos[BLOCK:], sin[BLOCK:], tri_incl,
        ROW_TILE, False, tiles_per_seq)
    _, kl, vl, cl, _, rkl, rvl, _, _, _ = _proj(
        h1l, proj_consts + ((q_scale[:, :BLOCK], None), (k_scale[:, :BLOCK], None)), cos[:BLOCK],
        sin[:BLOCK], tri_after, BLOCK, True, 1)

    def b3(a):
        return a.reshape(batch, seq, a.shape[-1])

    ck = b3(c).transpose(0, 2, 1).reshape(batch, FOX_PAIRS, 2, seq)
    clt = cl.T.reshape(FOX_PAIRS, 2, BLOCK)
    bound = FOX_HD ** 0.5 * jnp.max(jnp.abs(fox_q_norm[0])) * jnp.max(jnp.abs(fox_k_norm[0]))
    bound2 = bound * LOG2E
    stab = jnp.stack([(bound <= SAFE_LOGIT_BOUND).astype(F32), bound2, F32_EXP2_ZERO + 2.0 * bound2])
    c3 = b3(c)
    cs = c3[:, ::ATT_BLOCK, :].reshape(-1)
    ce = c3[:, ATT_BLOCK // 2 - 1::ATT_BLOCK // 2, :].reshape(-1)
    ya, win2, wof, wor, wo, wout2 = _fox(
        stab, cs, ce, b3(q), c3, b3(k), b3(v), ck, kl, vl, clt,
        (w_ffn2_in[0], w_o_fox[0], w_o_ret[0], w_out[0]), w_ffn2_out[0], batch, seq)
    yb = _ret(b3(rq), b3(rk), b3(rv), b3(sg), rkl, rvl, ret_gn, chunk_decay, batch, seq)

    out_consts = (wof, wor, wo, norm_ffn2, win2, wout2.reshape(N_FF_CHUNKS, FF_CHUNK, D_MODEL))
    out = _out(ya.reshape(batch * seq, d), yb.reshape(batch * seq, d), ga, gb, h1, out_consts,
               ROW_TILE)
    return out.reshape(batch, seq, d)
```

```python
import functools
import math

import jax
import jax.numpy as jnp
import numpy as np
from jax import lax
from jax.experimental import pallas as pl
from jax.experimental.pallas import tpu as pltpu

F32 = jnp.float32
BF16 = jnp.bfloat16

D_MODEL = 1024
N_META = 16
BLOCK = 128
N_EMPTY = BLOCK - N_META
FOX_HD = 64
FOX_HEADS = D_MODEL // FOX_HD
FOX_PAIRS = FOX_HEADS // 2
RET_HEADS = 4
RET_DK = D_MODEL // (2 * RET_HEADS)
RET_DV = 2 * RET_DK
RET_QK = RET_HEADS * RET_DK
D_FF = ((8 * D_MODEL // 3 + 127) // 128) * 128
EPS = 1e-6
GN_EPS = 1e-5
ROPE_BASE = 10000.0
NEG = -1e30
LOG2E = math.log2(math.e)
SAFE_LOGIT_BOUND = 30.0
F32_EXP2_ZERO = 151.0

V7X_LANES = 128
V7X_MXU_DIM = 256
V7X_VMEM_BYTES = 64 * 2**20
VMEM_LIMIT = V7X_VMEM_BYTES - 8 * 2**20

ROW_TILE = 512
FF_CHUNK = V7X_MXU_DIM
N_FF_CHUNKS = D_FF // FF_CHUNK
RET_UNROLL = 16
ATT_BLOCK = 1024


def _dot(a, b):
    return jnp.dot(a, b, preferred_element_type=F32)


def _dot_nt(a, b):
    return lax.dot_general(a, b, (((1,), (1,)), ((), ())), preferred_element_type=F32)


def _dot_tn(a, b):
    return lax.dot_general(a, b, (((0,), (0,)), ((), ())), preferred_element_type=F32)


def _const_spec(shape, block=None):
    zeros = (0,) * len(shape)
    return pl.BlockSpec(block or shape, lambda *_: zeros, pipeline_mode=pl.Buffered(1))


def _params(semantics):
    return pltpu.CompilerParams(dimension_semantics=semantics, vmem_limit_bytes=VMEM_LIMIT)


def _rms_rows(x, g):
    ms = jnp.mean(x * x, axis=-1, keepdims=True)
    return x * lax.rsqrt(ms + EPS) * g


def _swiglu(xn, win_ref, wout_ref):
    acc = None
    for c in range(N_FF_CHUNKS):
        a = _dot(xn, win_ref[:, c * FF_CHUNK:(c + 1) * FF_CHUNK].astype(BF16))
        b = _dot(xn, win_ref[:, D_FF + c * FF_CHUNK:D_FF + (c + 1) * FF_CHUNK].astype(BF16))
        hm = (a * jax.nn.sigmoid(a) * b).astype(BF16)
        d = _dot(hm, wout_ref[c].astype(BF16))
        acc = d if acc is None else acc + d
    return acc


def _ffn_body(x_ref, g_ref, win_ref, wout_ref, o_ref):
    x = x_ref[...]
    xn = _rms_rows(x, g_ref[...]).astype(BF16)
    o_ref[...] = x + 0.5 * _swiglu(xn, win_ref, wout_ref)


def _ffn_cast_body(x_ref, lead_ref, g_ref, win_ref, wout_ref, w_src, o_ref, ol_ref, w_dst, *,
                   n_tiles, cast_steps):
    i = pl.program_id(0)

    @pl.when(i < n_tiles)
    def _():
        _ffn_body(x_ref, g_ref, win_ref, wout_ref, o_ref)

    @pl.when(i == n_tiles)
    def _():
        _ffn_body(lead_ref, g_ref, win_ref, wout_ref, ol_ref)

    @pl.when(i < cast_steps)
    def _():
        w_dst[...] = w_src[...].astype(BF16)


def _ffn_and_cast(x, lead, g, win, wout, w_t, tm):
    rows = x.shape[0]
    n_tiles = rows // tm
    n = w_t.shape[0]
    cast_steps = max(s for s in range(1, n_tiles + 1) if n % s == 0 and (n // s) % 16 == 0)
    slab = n // cast_steps
    row_spec = pl.BlockSpec((tm, D_MODEL), lambda i: (jnp.minimum(i, n_tiles - 1), 0))
    lead_spec = pl.BlockSpec(lead.shape, lambda i: (0, 0))
    slab_spec = pl.BlockSpec((slab, w_t.shape[1]), lambda i: (jnp.minimum(i, cast_steps - 1), 0))
    return pl.pallas_call(
        functools.partial(_ffn_cast_body, n_tiles=n_tiles, cast_steps=cast_steps),
        grid=(n_tiles + 1,),
        in_specs=[row_spec, lead_spec, _const_spec(g.shape), _const_spec(win.shape),
                  _const_spec(wout.shape), slab_spec],
        out_specs=[row_spec, lead_spec, slab_spec],
        out_shape=[jax.ShapeDtypeStruct((rows, D_MODEL), F32), jax.ShapeDtypeStruct(lead.shape, F32),
                   jax.ShapeDtypeStruct(w_t.shape, BF16)],
        compiler_params=_params(("arbitrary",)),
        name="ffn_cast",
    )(x, lead, g, win, wout, w_t)


def _group_mean_sq(x, gmat_ref):
    sq = (x * x).astype(BF16)
    w = V7X_MXU_DIM
    parts = [_dot(sq[:, i * w:(i + 1) * w], gmat_ref[...]) for i in range(D_MODEL // w)]
    return jnp.concatenate(parts, axis=1)


def _rotary_tile(x, cos, sin_signed):
    return x * cos + pltpu.roll(x, RET_DK // 2, axis=1) * sin_signed


COL_FQ = 0
COL_FK = COL_FQ + D_MODEL
COL_FV = COL_FK + D_MODEL
COL_FF = COL_FV + D_MODEL
COL_RQ = COL_FF + FOX_HEADS
COL_RK = COL_RQ + RET_QK
COL_RV = COL_RK + RET_QK
COL_RG = COL_RV + D_MODEL
COL_GA = COL_RG + D_MODEL
COL_GB = COL_GA + D_MODEL
COL_END = COL_GB + D_MODEL


def _proj_rows(h, cos, sin, tri, q_scale, k_scale, gmix_ref, wt_ref, gmat_ref, gq_ref, gk_ref, bf_ref,
               bg_ref, carry_ref, outs, *, lead):
    q_ref, k_ref, v_ref, c_ref, rq_ref, rk_ref, rv_ref, sg_ref, ga_ref, gb_ref = outs
    tm = h.shape[0]
    u = _rms_rows(h, gmix_ref[...]).astype(BF16)

    def project(c0, c1):
        return _dot_nt(u, wt_ref[c0:c1, :])

    if lead:
        row = lax.broadcasted_iota(jnp.int32, (tm, 1), 0)
        valid = row >= N_EMPTY
        vmask = valid.astype(F32)


    z = project(COL_FF, COL_FF + V7X_LANES) + bf_ref[...]
    logf = -(jnp.maximum(-z, 0.0) + jnp.log1p(jnp.exp(-jnp.abs(z))))
    if lead:
        logf = jnp.where(valid, logf, 0.0)
    lane = lax.broadcasted_iota(jnp.int32, (1, V7X_LANES), 1)
    logf = jnp.where(lane < FOX_HEADS, logf, 0.0)
    p1 = logf.astype(BF16).astype(F32)
    r1 = logf - p1
    p2 = r1.astype(BF16).astype(F32)
    p3 = (r1 - p2).astype(BF16).astype(F32)
    packed = (p1 + pltpu.roll(p2, FOX_HEADS, axis=1) + pltpu.roll(p3, 2 * FOX_HEADS, axis=1)).astype(BF16)

    fq = project(COL_FQ, COL_FK) if q_ref is not None else None
    fk = project(COL_FK, COL_FV)
    rq = project(COL_RQ, COL_RK) if rq_ref is not None else None
    rk = project(COL_RK, COL_RV)

    if q_ref is not None:
        qn = fq * lax.rsqrt(_group_mean_sq(fq, gmat_ref) + EPS) * gq_ref[...]
        q_ref[...] = (qn * (FOX_HD ** -0.5 * LOG2E)).astype(BF16)
    kn = fk * lax.rsqrt(_group_mean_sq(fk, gmat_ref) + EPS) * gk_ref[...]
    k_ref[...] = kn.astype(BF16)

    r = _dot(tri, packed)
    cum = r + (pltpu.roll(r, V7X_LANES - FOX_HEADS, axis=1)
               + pltpu.roll(r, V7X_LANES - 2 * FOX_HEADS, axis=1))
    cum = jnp.where(lane < FOX_HEADS, cum, 0.0)
    if lead:
        c = -cum
    else:
        c = cum + carry_ref[...]
        carry_ref[...] = c[tm - 1:tm, :]
    c_ref[...] = (c * LOG2E)[:, :FOX_HEADS]

    for hh in range(RET_HEADS):
        sl = slice(hh * RET_DK, (hh + 1) * RET_DK)
        if rq_ref is not None:
            rq_ref[:, sl] = (_rotary_tile(rq[:, sl], cos, sin) * q_scale(hh)).astype(BF16)
        kt = _rotary_tile(rk[:, sl], cos, sin) * k_scale(hh)
        if lead:
            kt = kt * vmask
        rk_ref[:, sl] = kt.astype(BF16)
    rv = project(COL_RV, COL_RG)
    if lead:
        rv = rv * vmask
    rv_ref[...] = rv.astype(BF16)
    if sg_ref is not None:
        rg = project(COL_RG, COL_GA)
        sg_ref[...] = (rg * jax.nn.sigmoid(rg)).astype(BF16)

    if ga_ref is not None:
        bg = bg_ref[...]
        ga_ref[...] = jax.nn.sigmoid(project(COL_GA, COL_GB) + bg[:, :D_MODEL]).astype(BF16)
        gb_ref[...] = jax.nn.sigmoid(project(COL_GB, COL_END) + bg[:, D_MODEL:]).astype(BF16)

    fv = project(COL_FV, COL_FF).astype(BF16)
    ones = jnp.ones((tm, V7X_LANES), BF16)
    for j in range(FOX_PAIRS):
        v_ref[:, 2 * j * V7X_LANES:(2 * j + 1) * V7X_LANES] = fv[:, j * V7X_LANES:(j + 1) * V7X_LANES]
        v_ref[:, (2 * j + 1) * V7X_LANES:(2 * j + 2) * V7X_LANES] = ones


def _proj_body(h_ref, hl_ref, gmix_ref, wt_ref, gmat_ref, gq_ref, gk_ref, bf_ref, bg_ref, qs_ref, ks_ref,
               cos_ref, sin_ref, cosl_ref, sinl_ref, tri_ref, tril_ref,
               q_ref, k_ref, v_ref, c_ref, rq_ref, rk_ref, rv_ref, sg_ref, ga_ref, gb_ref,
               kl_ref, vl_ref, cl_ref, rkl_ref, rvl_ref, carry_ref, *, n_tiles, tiles_per_seq):
    i = pl.program_id(0)
    consts = (gmix_ref, wt_ref, gmat_ref, gq_ref, gk_ref, bf_ref, bg_ref, carry_ref)

    @pl.when(i % tiles_per_seq == 0)
    def _():
        carry_ref[...] = jnp.zeros_like(carry_ref)

    @pl.when(i < n_tiles)
    def _():
        outs = (q_ref, k_ref, v_ref, c_ref, rq_ref, rk_ref, rv_ref, sg_ref, ga_ref, gb_ref)
        _proj_rows(h_ref[...], cos_ref[...], sin_ref[...], tri_ref[...], lambda hh: qs_ref[hh],
                   lambda hh: ks_ref[hh], *consts, outs, lead=False)

    @pl.when(i == n_tiles)
    def _():
        outs = (None, kl_ref, vl_ref, cl_ref, None, rkl_ref, rvl_ref, None, None, None)
        _proj_rows(hl_ref[...], cosl_ref[...], sinl_ref[...], tril_ref[...], lambda hh: qs_ref[hh, :BLOCK],
                   lambda hh: ks_ref[hh, :BLOCK], *consts, outs, lead=True)


def _proj(h, h_lead, consts, cos, sin, tri, tri_lead, tm, tiles_per_seq):
    rows = h.shape[0]
    n_tiles = rows // tm
    assert tm % BLOCK == 0

    def rows_spec(width):
        return pl.BlockSpec((tm, width), lambda i: (jnp.minimum(i, n_tiles - 1), 0))

    def lead_spec(width):
        return pl.BlockSpec((BLOCK, width), lambda i: (0, 0))

    pos_spec = pl.BlockSpec((tm, RET_DK), lambda i: (i % tiles_per_seq, 0))
    out_widths = [D_MODEL, D_MODEL, 2 * D_MODEL, FOX_HEADS, RET_QK, RET_QK, D_MODEL, D_MODEL,
                  D_MODEL, D_MODEL]
    out_dtypes = [BF16, BF16, BF16, F32, BF16, BF16, BF16, BF16, BF16, BF16]
    lead_outs = [1, 2, 3, 5, 6]
    return pl.pallas_call(
        functools.partial(_proj_body, n_tiles=n_tiles, tiles_per_seq=tiles_per_seq),
        grid=(n_tiles + 1,),
        in_specs=[rows_spec(D_MODEL), lead_spec(D_MODEL)] + [_const_spec(a.shape) for a in consts]
        + [pos_spec, pos_spec, lead_spec(RET_DK), lead_spec(RET_DK), _const_spec(tri.shape),
           _const_spec(tri_lead.shape)],
        out_specs=[rows_spec(w) for w in out_widths] + [lead_spec(out_widths[j]) for j in lead_outs],
        out_shape=[jax.ShapeDtypeStruct((rows, w), dt) for w, dt in zip(out_widths, out_dtypes)]
        + [jax.ShapeDtypeStruct((BLOCK, out_widths[j]), out_dtypes[j]) for j in lead_outs],
        scratch_shapes=[pltpu.VMEM((1, V7X_LANES), F32)],
        compiler_params=_params(("arbitrary",)),
        name="proj",
    )(h, h_lead, *consts, cos[BLOCK:], sin[BLOCK:], cos, sin, tri, tri_lead)


def _fox_body(stab_ref, cs_ref, ce_ref, q_ref, cq_ref, k_ref, v_ref, ck_ref, kl_ref, vl_ref, cl_ref,
              *rest, n_cast, n_cast_short, short_steps):
    srcs, (o_ref, *dsts), (m_ref, acc_ref, e_ref) = (rest[:n_cast], rest[n_cast:2 * n_cast + 1],
                                                     rest[2 * n_cast + 1:])
    step = pl.program_id(0) * pl.num_programs(1) + pl.program_id(1)
    for src, dst in zip(srcs[:n_cast - n_cast_short], dsts):
        dst[...] = src[...].astype(BF16)

    @pl.when(step < short_steps)
    def _():
        for src, dst in zip(srcs[n_cast - n_cast_short:], dsts[n_cast - n_cast_short:]):
            dst[...] = src[...].astype(BF16)

    _fox_attend(stab_ref, cs_ref, ce_ref, q_ref, cq_ref, k_ref, v_ref, ck_ref, kl_ref, vl_ref, cl_ref,
                o_ref, m_ref, acc_ref, e_ref)


def _fox_attend(stab_ref, cs_ref, ce_ref, q_ref, cq_ref, k_ref, v_ref, ck_ref, kl_ref, vl_ref, cl_ref,
                o_ref, m_ref, acc_ref, e_ref):
    tq = m_ref.shape[1]
    tk = tq // 2
    tr = tq // 4
    lane = lax.broadcasted_iota(jnp.int32, (1, 2 * V7X_LANES), 1)
    first2 = (lane % V7X_LANES) < FOX_HD
    first = first2[:, :V7X_LANES]
    lead_ok = lax.broadcasted_iota(jnp.int32, (tq, BLOCK), 1) >= N_EMPTY
    row = lax.broadcasted_iota(jnp.int32, (tq, tk), 0)
    col = lax.broadcasted_iota(jnp.int32, (tq, tk), 1)
    causal = col <= row
    corner = causal[:tr, :tr]
    use_bound = stab_ref[0] > 0.5

    def keys(j):
        ks = pl.multiple_of(j * tk, tk)
        return k_ref[pl.ds(ks, tk), :], ck_ref[:, pl.ds(ks, tk)]

    def values(j):
        return v_ref[pl.ds(pl.multiple_of(j * tk, tk), tk), :]

    nq = q_ref.shape[0] // tq
    nk = k_ref.shape[0] // tk

    def query_block(i, carry):
        _fox_query_block(i, tq, tk, tr, nq, nk, first, first2, lead_ok, row, col, causal, corner,
                         use_bound, keys, values, stab_ref, cs_ref, ce_ref, q_ref, cq_ref, kl_ref,
                         vl_ref, cl_ref, o_ref, m_ref, acc_ref, e_ref)
        return carry

    lax.fori_loop(0, nq, query_block, 0)


def _fox_query_block(i, tq, tk, tr, nq, nk, first, first2, lead_ok, row, col, causal, corner,
                     use_bound, keys, values, stab_ref, cs_ref, ce_ref, q_ref, cq_ref, kl_ref,
                     vl_ref, cl_ref, o_ref, m_ref, acc_ref, e_ref):
    rows = pl.ds(pl.multiple_of(i * tq, tq), tq)
    n_full = 2 * i
    q = q_ref[rows, :]
    zero = jnp.zeros_like(q)
    q_heads = (jnp.where(first, q, zero), jnp.where(first, zero, q))
    acc_ref[...] = jnp.zeros_like(acc_ref)

    @pl.when(use_bound)
    def _():
        head = lax.broadcasted_iota(jnp.int32, (1, FOX_HEADS), 1) - 2 * pl.program_id(1)
        cq_blk = cq_ref[rows, :]
        cq = [jnp.sum(jnp.where(head == hh, cq_blk, 0.0), axis=1, keepdims=True) - stab_ref[1]
              for hh in range(2)]

        def exponents(k, ck, r0=0, r1=tq):
            return [_dot_nt(q_heads[hh][r0:r1], k) + cq[hh][r0:r1] - ck[hh:hh + 1, :] for hh in range(2)]

        def accumulate(e, v, allowed, r0=0):
            if allowed is not None:
                e = [jnp.where(allowed, x, NEG) for x in e]
            pvs = [_dot(jnp.exp2(x).astype(BF16), v) for x in e]
            r1 = r0 + e[0].shape[0]
            acc_ref[r0:r1, :] += jnp.where(first2, pvs[0], pvs[1])

        cutoff = -stab_ref[2]
        slot = 2 * pl.program_id(1)
        cs_base = (pl.program_id(0) * nq + i) * FOX_HEADS
        ce_base = pl.program_id(0) * nk * FOX_HEADS
        c_first = [cs_ref[cs_base + slot + hh] for hh in range(2)]
        skip_lead = jnp.maximum(c_first[0], c_first[1]) < cutoff
        j0 = jnp.int32(0)
        for j in range(nk - 2):
            gap = jnp.maximum(c_first[0] - ce_ref[ce_base + j * FOX_HEADS + slot],
                              c_first[1] - ce_ref[ce_base + j * FOX_HEADS + slot + 1])
            j0 += jnp.logical_and(gap < cutoff, j < n_full).astype(jnp.int32)

        e0 = exponents(*keys(j0))
        e_ref[0] = e0[0]
        e_ref[1] = e0[1]

        @pl.when(jnp.logical_not(skip_lead))
        def _():
            accumulate(exponents(kl_ref[...], cl_ref[...]), vl_ref[...], lead_ok)

        def step(j, carry):
            e_cur = [e_ref[0], e_ref[1]]
            e_next = exponents(*keys(j + 1))
            accumulate(e_cur, values(j), None)
            e_ref[0] = e_next[0]
            e_ref[1] = e_next[1]
            return carry

        def step_pair(m, carry):
            return step(2 * m + 1, step(2 * m, carry))

        @pl.when(j0 % 2 == 1)
        def _():
            step(j0, 0)

        lax.fori_loop((j0 + 1) // 2, i, step_pair, 0)

        kb, ckb = keys(n_full + 1)
        vb = values(n_full + 1)
        e_b1 = exponents(kb[:tr], ckb[:, :tr], 2 * tr, tq)
        e_b2 = exponents(kb[tr:], ckb[:, tr:], 3 * tr, tq)
        va = values(n_full)
        e_a = [e_ref[0], e_ref[1]]
        strips = [
            [([x[:tr, :tr] for x in e_a], va[:tr], corner)],
            [([x[tr:2 * tr] for x in e_a], va, causal[tr:2 * tr])],
            [([x[2 * tr:3 * tr] for x in e_a], va, None), ([x[:tr] for x in e_b1], vb[:tr], corner)],
            [([x[3 * tr:] for x in e_a], va, None), ([x[tr:] for x in e_b1], vb[:tr], None),
             (e_b2, vb[tr:], corner)],
        ]
        for s, parts in enumerate(strips):
            pv = [None, None]
            for e, v, allowed in parts:
                if allowed is not None:
                    e = [jnp.where(allowed, x, NEG) for x in e]
                for hh in range(2):
                    d = _dot(jnp.exp2(e[hh]).astype(BF16), v)
                    pv[hh] = d if pv[hh] is None else pv[hh] + d
            acc_ref[s * tr:(s + 1) * tr, :] += jnp.where(first2, pv[0], pv[1])

    @pl.when(jnp.logical_not(use_bound))
    def _():
        m_ref[...] = jnp.full_like(m_ref, NEG)

        def block(k, v, ck, allowed):
            pvs, alphas = [], []
            for hh in range(2):
                s = _dot_nt(q_heads[hh], k) - ck[hh:hh + 1, :]
                if allowed is not None:
                    s = jnp.where(allowed, s, NEG)
                m_old = m_ref[hh]
                m_new = jnp.maximum(m_old, jnp.max(s, axis=1, keepdims=True))
                m_ref[hh] = m_new
                pvs.append(_dot(jnp.exp2(s - m_new).astype(BF16), v))
                alphas.append(jnp.exp2(m_old - m_new))
            acc_ref[...] = (acc_ref[...] * jnp.where(first2, alphas[0], alphas[1])
                            + jnp.where(first2, pvs[0], pvs[1]))

        block(kl_ref[...], vl_ref[...], cl_ref[...], lead_ok)

        def full_block(j, carry):
            block(keys(j)[0], values(j), keys(j)[1], None)
            return carry

        lax.fori_loop(0, n_full, full_block, 0)
        block(keys(n_full)[0], values(n_full), keys(n_full)[1], causal)
        block(keys(n_full + 1)[0], values(n_full + 1), keys(n_full + 1)[1], col + tk <= row)

    acc = acc_ref[...]
    o_ref[rows, :] = (acc[:, :V7X_LANES] / acc[:, V7X_LANES:]).astype(o_ref.dtype)


def _fox(stab, cs, ce, q, cq, k, v, ck, kl, vl, cl, weights, short_weight, batch, seq):
    tq = ATT_BLOCK
    w = V7X_LANES
    steps = batch * FOX_PAIRS
    cast_specs, cast_shapes = [], []
    for a in weights:
        slab = a.shape[0] // steps
        assert slab * steps == a.shape[0] and slab % 16 == 0
        cast_specs.append(pl.BlockSpec((slab, a.shape[1]), lambda b, j: (b * FOX_PAIRS + j, 0)))
        cast_shapes.append(jax.ShapeDtypeStruct(a.shape, BF16))
    short_steps = short_weight.shape[0] // FF_CHUNK
    assert short_steps * FF_CHUNK == short_weight.shape[0] and short_steps <= steps
    cast_specs.append(pl.BlockSpec(
        (FF_CHUNK, short_weight.shape[1]),
        lambda b, j: (jnp.minimum(b * FOX_PAIRS + j, short_steps - 1), 0)))
    cast_shapes.append(jax.ShapeDtypeStruct(short_weight.shape, BF16))
    n_cast = len(cast_specs)
    return pl.pallas_call(
        functools.partial(_fox_body, n_cast=n_cast, n_cast_short=1, short_steps=short_steps),
        grid=(batch, FOX_PAIRS),
        in_specs=[
            pl.BlockSpec(memory_space=pltpu.SMEM),
            pl.BlockSpec(memory_space=pltpu.SMEM),
            pl.BlockSpec(memory_space=pltpu.SMEM),
            pl.BlockSpec((None, seq, w), lambda b, j: (b, 0, j)),
            pl.BlockSpec((None, seq, FOX_HEADS), lambda b, j: (b, 0, 0)),
            pl.BlockSpec((None, seq, w), lambda b, j: (b, 0, j)),
            pl.BlockSpec((None, seq, 2 * w), lambda b, j: (b, 0, j)),
            pl.BlockSpec((None, None, 2, seq), lambda b, j: (b, j, 0, 0)),
            pl.BlockSpec((BLOCK, w), lambda b, j: (0, j)),
            pl.BlockSpec((BLOCK, 2 * w), lambda b, j: (0, j)),
            pl.BlockSpec((None, 2, BLOCK), lambda b, j: (j, 0, 0)),
        ] + cast_specs,
        out_specs=[pl.BlockSpec((None, seq, w), lambda b, j: (b, 0, j))] + cast_specs,
        out_shape=[jax.ShapeDtypeStruct((batch, seq, D_MODEL), BF16)] + cast_shapes,
        scratch_shapes=[pltpu.VMEM((2, tq, 1), F32), pltpu.VMEM((tq, 2 * w), F32),
                        pltpu.VMEM((2, tq, tq // 2), F32)],
        compiler_params=_params(("arbitrary", "arbitrary")),
        name="fox",
    )(stab, cs, ce, q, cq, k, v, ck, kl, vl, cl, *weights, short_weight)


def _ret_body(q_ref, k_ref, v_ref, sg_ref, kl_ref, vl_ref, gn_ref, cd_ref, o_ref, kv_ref, state_ref):
    n_chunks = q_ref.shape[0] // BLOCK
    cd = cd_ref[...]
    gn = gn_ref[...]
    causal = (lax.broadcasted_iota(jnp.int32, (BLOCK, BLOCK), 1)
              <= lax.broadcasted_iota(jnp.int32, (BLOCK, BLOCK), 0))

    def rows(c):
        return pl.ds(pl.multiple_of(c * BLOCK, BLOCK), BLOCK)

    def summarise(c, carry):
        kv_ref[c] = _dot_tn(k_ref[rows(c), :], v_ref[rows(c), :])
        return carry

    lax.fori_loop(0, n_chunks, summarise, 0, unroll=RET_UNROLL)

    def advance(c, state):
        state_ref[c] = state.astype(BF16)
        return cd * (state + kv_ref[c])

    lead_state = cd * _dot_tn(kl_ref[...], vl_ref[...])
    lax.fori_loop(0, n_chunks, advance, lead_state, unroll=RET_UNROLL)

    def emit(c, carry):
        q = q_ref[rows(c), :]
        v = v_ref[rows(c), :]
        scores = jnp.where(causal, _dot_nt(q, k_ref[rows(c), :]), 0.0).astype(BF16)
        o = _dot(jnp.concatenate([scores, q], axis=1), jnp.concatenate([v, state_ref[c]], axis=0))
        mu = jnp.mean(o, axis=-1, keepdims=True)
        d = o - mu
        var = jnp.mean(d * d, axis=-1, keepdims=True)
        yn = d * lax.rsqrt(var + GN_EPS) * gn
        o_ref[rows(c), :] = (sg_ref[rows(c), :].astype(F32) * yn).astype(o_ref.dtype)
        return carry

    lax.fori_loop(0, n_chunks, emit, 0, unroll=RET_UNROLL)


def _ret(rq, rk, rv, sg, rkl, rvl, gn, cd, batch, seq):
    head_qk = pl.BlockSpec((None, seq, RET_DK), lambda b, h: (b, 0, h))
    head_v = pl.BlockSpec((None, seq, RET_DV), lambda b, h: (b, 0, h))
    return pl.pallas_call(
        _ret_body,
        grid=(batch, RET_HEADS),
        in_specs=[head_qk, head_qk, head_v, head_v,
                  pl.BlockSpec((BLOCK, RET_DK), lambda b, h: (0, h)),
                  pl.BlockSpec((BLOCK, RET_DV), lambda b, h: (0, h)),
                  pl.BlockSpec((1, RET_DV), lambda b, h: (0, h)),
                  pl.BlockSpec((None, 1, RET_DV), lambda b, h: (h, 0, 0))],
        out_specs=head_v,
        out_shape=jax.ShapeDtypeStruct((batch, seq, D_MODEL), BF16),
        scratch_shapes=[pltpu.VMEM((seq // BLOCK, RET_DK, RET_DV), F32),
                        pltpu.VMEM((seq // BLOCK, RET_DK, RET_DV), BF16)],
        compiler_params=_params(("arbitrary", "arbitrary")),
        name="ret",
    )(rq, rk, rv, sg, rkl, rvl, gn, cd)


def _out_body(ya_ref, yb_ref, ga_ref, gb_ref, h_ref, wa_ref, wb_ref, wo_ref, g_ref, win_ref,
              wout_ref, o_ref):
    mixed = (ga_ref[...].astype(F32) * _dot(ya_ref[...], wa_ref[...])
             + gb_ref[...].astype(F32) * _dot(yb_ref[...], wb_ref[...]))
    h = h_ref[...] + _dot(mixed.astype(BF16), wo_ref[...])
    xn = _rms_rows(h, g_ref[...]).astype(BF16)
    o_ref[...] = h + 0.5 * _swiglu(xn, win_ref, wout_ref)


def _out(ya, yb, ga, gb, h, consts, tm):
    rows = h.shape[0]
    row_spec = pl.BlockSpec((tm, D_MODEL), lambda i: (i, 0))
    return pl.pallas_call(
        _out_body,
        grid=(rows // tm,),
        in_specs=[row_spec] * 5 + [_const_spec(a.shape) for a in consts],
        out_specs=row_spec,
        out_shape=jax.ShapeDtypeStruct((rows, D_MODEL), F32),
        compiler_params=_params(("arbitrary",)),
        name="out",
    )(ya, yb, ga, gb, h, *consts)


def _ffn_weights(w_in, w_out, dtype):
    return w_in.astype(dtype), w_out.astype(dtype).reshape(N_FF_CHUNKS, FF_CHUNK, D_MODEL)


def _position_tables(seq):
    half = RET_DK // 2
    pos = np.arange(BLOCK + seq, dtype=np.float64) - N_EMPTY
    inv = ROPE_BASE ** (-np.arange(half, dtype=np.float64) / half)
    ang = pos[:, None] * inv[None, :]
    cos = np.concatenate([np.cos(ang), np.cos(ang)], axis=1)
    sin = np.concatenate([-np.sin(ang), np.sin(ang)], axis=1)
    return jnp.asarray(cos, dtype=F32), jnp.asarray(sin, dtype=F32)


def _retention_tables(rows):
    log_gamma = np.log1p(-np.exp2(-5.0 - np.arange(RET_HEADS, dtype=np.float64)))
    n = (np.arange(rows) % BLOCK + 1.0)[None, :, None]
    lg = log_gamma[:, None, None]
    shape = (RET_HEADS, rows, RET_DK)
    q_scale = np.broadcast_to(np.exp(lg * n), shape)
    k_scale = np.broadcast_to(np.exp(-lg * n) * RET_DK ** -0.5, shape)
    cd = np.broadcast_to(np.exp(log_gamma * BLOCK)[:, None, None], (RET_HEADS, 1, RET_DV))
    return tuple(jnp.asarray(t, dtype=F32) for t in (q_scale, k_scale, cd))


def kernel(x, meta_tokens, norm_ffn1, w_ffn1_in, w_ffn1_out, norm_mix, w_in, b_forget, b_gate,
           fox_q_norm, fox_k_norm, w_o_fox, ret_gn, w_o_ret, w_out, norm_ffn2, w_ffn2_in,
           w_ffn2_out):
    batch, seq, d = x.shape
    assert d == D_MODEL and seq % ATT_BLOCK == 0 and seq % ROW_TILE == 0
    assert norm_ffn1.shape[0] == 1, "one layer"
    tiles_per_seq = seq // ROW_TILE

    win1, wout1 = _ffn_weights(w_ffn1_in[0], w_ffn1_out[0], F32)
    grp = np.arange(V7X_MXU_DIM) // FOX_HD
    gmat = jnp.asarray((grp[:, None] == grp[None, :]) / FOX_HD, dtype=BF16)
    gq = jnp.tile(fox_q_norm[0], FOX_HEADS)[None, :]
    gk = jnp.tile(fox_k_norm[0], FOX_HEADS)[None, :]
    bf = jnp.pad(b_forget[0], (0, V7X_LANES - FOX_HEADS))[None, :]
    bg = b_gate[0][None, :]
    cos, sin = _position_tables(seq)
    idx = np.arange(ROW_TILE)
    tri_incl = jnp.asarray(idx[None, :] <= idx[:, None], dtype=BF16)
    idx = np.arange(BLOCK)
    tri_after = jnp.asarray(idx[None, :] > idx[:, None], dtype=BF16)

    lead = jnp.concatenate([jnp.zeros((N_EMPTY, d), x.dtype), meta_tokens.astype(x.dtype)], axis=0)
    xr = x.reshape(batch * seq, d)

    assert w_in.shape[2] == COL_END
    h1, h1l, wt = _ffn_and_cast(xr, lead, norm_ffn1, win1, wout1, w_in[0].T, ROW_TILE)
    q_scale, k_scale, chunk_decay = _retention_tables(ROW_TILE)
    proj_consts = (norm_mix, wt, gmat, gq, gk, bf, bg, q_scale, k_scale)
    q, k, v, c, rq, rk, rv, sg, ga, gb, kl, vl, cl, rkl, rvl = _proj(
        h1, h1l, proj_consts, cos, sin, tri_incl, tri_after, ROW_TILE, tiles_per_seq)

    def b3(a):
        return a.reshape(batch, seq, a.shape[-1])

    ck = b3(c).transpose(0, 2, 1).reshape(batch, FOX_PAIRS, 2, seq)
    clt = cl.T.reshape(FOX_PAIRS, 2, BLOCK)
    bound = FOX_HD ** 0.5 * jnp.max(jnp.abs(fox_q_norm[0])) * jnp.max(jnp.abs(fox_k_norm[0]))
    bound2 = bound * LOG2E
    stab = jnp.stack([(bound <= SAFE_LOGIT_BOUND).astype(F32), bound2, F32_EXP2_ZERO + 2.0 * bound2])
    c3 = b3(c)
    cs = c3[:, ::ATT_BLOCK, :].reshape(-1)
    ce = c3[:, ATT_BLOCK // 2 - 1::ATT_BLOCK // 2, :].reshape(-1)
    ya, win2, wof, wor, wo, wout2 = _fox(
        stab, cs, ce, b3(q), c3, b3(k), b3(v), ck, kl, vl, clt,
        (w_ffn2_in[0], w_o_fox[0], w_o_ret[0], w_out[0]), w_ffn2_out[0], batch, seq)
    yb = _ret(b3(rq), b3(rk), b3(rv), b3(sg), rkl, rvl, ret_gn, chunk_decay, batch, seq)

    out_consts = (wof, wor, wo, norm_ffn2, win2, wout2.reshape(N_FF_CHUNKS, FF_CHUNK, D_MODEL))
    out = _out(ya.reshape(batch * seq, d), yb.reshape(batch * seq, d), ga, gb, h1, out_consts,
               ROW_TILE)
    return out.reshape(batch, seq, d)
```

```python
import functools
import math

import jax
import jax.numpy as jnp
import numpy as np
from jax import lax
from jax.experimental import pallas as pl
from jax.experimental.pallas import tpu as pltpu

F32 = jnp.float32
BF16 = jnp.bfloat16

D_MODEL = 1024
N_META = 16
BLOCK = 128
N_EMPTY = BLOCK - N_META
FOX_HD = 64
FOX_HEADS = D_MODEL // FOX_HD
FOX_PAIRS = FOX_HEADS // 2
RET_HEADS = 4
RET_DK = D_MODEL // (2 * RET_HEADS)
RET_DV = 2 * RET_DK
RET_QK = RET_HEADS * RET_DK
D_FF = ((8 * D_MODEL // 3 + 127) // 128) * 128
EPS = 1e-6
GN_EPS = 1e-5
ROPE_BASE = 10000.0
NEG = -1e30
LOG2E = math.log2(math.e)
SAFE_LOGIT_BOUND = 30.0
F32_EXP2_ZERO = 151.0

V7X_LANES = 128
BF16_SUBLANES = 16
V7X_MXU_DIM = 256
V7X_VMEM_BYTES = 64 * 2**20
VMEM_COMPILER_RESERVE = 8 * 2**20
VMEM_LIMIT = V7X_VMEM_BYTES - VMEM_COMPILER_RESERVE

ROW_TILE = 512
FF_CHUNK = V7X_MXU_DIM
N_FF_CHUNKS = D_FF // FF_CHUNK
RET_UNROLL = 16
ATT_BLOCK = 1024


def _dot(a, b):
    return jnp.dot(a, b, preferred_element_type=F32)


def _dot_nt(a, b):
    return lax.dot_general(a, b, (((1,), (1,)), ((), ())), preferred_element_type=F32)


def _dot_tn(a, b):
    return lax.dot_general(a, b, (((0,), (0,)), ((), ())), preferred_element_type=F32)


def _const_spec(shape, block=None):
    zeros = (0,) * len(shape)
    return pl.BlockSpec(block or shape, lambda *_: zeros, pipeline_mode=pl.Buffered(1))


def _params(semantics):
    return pltpu.CompilerParams(dimension_semantics=semantics, vmem_limit_bytes=VMEM_LIMIT)


def _rms_rows(x, g):
    ms = jnp.mean(x * x, axis=-1, keepdims=True)
    return x * lax.rsqrt(ms + EPS) * g


def _swiglu(xn, win_ref, wout_ref):
    acc = None
    for c in range(N_FF_CHUNKS):
        a = _dot(xn, win_ref[:, c * FF_CHUNK:(c + 1) * FF_CHUNK].astype(BF16))
        b = _dot(xn, win_ref[:, D_FF + c * FF_CHUNK:D_FF + (c + 1) * FF_CHUNK].astype(BF16))
        hm = (a * jax.nn.sigmoid(a) * b).astype(BF16)
        d = _dot(hm, wout_ref[c].astype(BF16))
        acc = d if acc is None else acc + d
    return acc


def _ffn_body(x_ref, g_ref, win_ref, wout_ref, o_ref):
    x = x_ref[...]
    xn = _rms_rows(x, g_ref[...]).astype(BF16)
    o_ref[...] = x + 0.5 * _swiglu(xn, win_ref, wout_ref)


def _ffn_cast_body(x_ref, lead_ref, g_ref, win_ref, wout_ref, w_src, o_ref, ol_ref, w_dst, *, n_tiles):
    i = pl.program_id(0)

    @pl.when(i < n_tiles)
    def _():
        _ffn_body(x_ref, g_ref, win_ref, wout_ref, o_ref)
        w_dst[...] = w_src[...].astype(BF16)

    @pl.when(i == n_tiles)
    def _():
        _ffn_body(lead_ref, g_ref, win_ref, wout_ref, ol_ref)


def _ffn_and_cast(x, lead, g, win, wout, w_t, tm):
    rows = x.shape[0]
    n_tiles = rows // tm
    n = w_t.shape[0]
    slab = min(s for s in range(BF16_SUBLANES, n + 1, BF16_SUBLANES) if n % s == 0 and n // s <= n_tiles)
    cast_steps = n // slab
    row_spec = pl.BlockSpec((tm, D_MODEL), lambda i: (jnp.minimum(i, n_tiles - 1), 0))
    lead_spec = pl.BlockSpec(lead.shape, lambda i: (0, 0))
    slab_spec = pl.BlockSpec((slab, w_t.shape[1]), lambda i: (jnp.minimum(i, cast_steps - 1), 0))
    return pl.pallas_call(
        functools.partial(_ffn_cast_body, n_tiles=n_tiles),
        grid=(n_tiles + 1,),
        in_specs=[row_spec, lead_spec, _const_spec(g.shape), _const_spec(win.shape),
                  _const_spec(wout.shape), slab_spec],
        out_specs=[row_spec, lead_spec, slab_spec],
        out_shape=[jax.ShapeDtypeStruct((rows, D_MODEL), F32), jax.ShapeDtypeStruct(lead.shape, F32),
                   jax.ShapeDtypeStruct(w_t.shape, BF16)],
        compiler_params=_params(("arbitrary",)),
        name="ffn_cast",
    )(x, lead, g, win, wout, w_t)


def _group_mean_sq(x, gmat_ref):
    sq = (x * x).astype(BF16)
    w = V7X_MXU_DIM
    parts = [_dot(sq[:, i * w:(i + 1) * w], gmat_ref[...]) for i in range(D_MODEL // w)]
    return jnp.concatenate(parts, axis=1)


def _rotary_tile(x, cos, sin_signed):
    return x * cos + pltpu.roll(x, RET_DK // 2, axis=1) * sin_signed


COL_FQ = 0
COL_FK = COL_FQ + D_MODEL
COL_FV = COL_FK + D_MODEL
COL_FF = COL_FV + D_MODEL
COL_RQ = COL_FF + FOX_HEADS
COL_RK = COL_RQ + RET_QK
COL_RV = COL_RK + RET_QK
COL_RG = COL_RV + D_MODEL
COL_GA = COL_RG + D_MODEL
COL_GB = COL_GA + D_MODEL
COL_END = COL_GB + D_MODEL


def _proj_rows(h, cos, sin, tri, q_scale, k_scale, gmix_ref, wt_ref, gmat_ref, gq_ref, gk_ref, bf_ref,
               bg_ref, carry_ref, outs, *, lead):
    q_ref, k_ref, v_ref, c_ref, rq_ref, rk_ref, rv_ref, sg_ref, ga_ref, gb_ref = outs
    tm = h.shape[0]
    u = _rms_rows(h, gmix_ref[...]).astype(BF16)

    def project(c0, c1):
        return _dot_nt(u, wt_ref[c0:c1, :])

    if lead:
        row = lax.broadcasted_iota(jnp.int32, (tm, 1), 0)
        valid = row >= N_EMPTY
        vmask = valid.astype(F32)


    z = project(COL_FF, COL_FF + V7X_LANES) + bf_ref[...]
    logf = -(jnp.maximum(-z, 0.0) + jnp.log1p(jnp.exp(-jnp.abs(z))))
    if lead:
        logf = jnp.where(valid, logf, 0.0)
    lane = lax.broadcasted_iota(jnp.int32, (1, V7X_LANES), 1)
    logf = jnp.where(lane < FOX_HEADS, logf, 0.0)
    p1 = logf.astype(BF16).astype(F32)
    r1 = logf - p1
    p2 = r1.astype(BF16).astype(F32)
    p3 = (r1 - p2).astype(BF16).astype(F32)
    packed = (p1 + pltpu.roll(p2, FOX_HEADS, axis=1) + pltpu.roll(p3, 2 * FOX_HEADS, axis=1)).astype(BF16)

    fq = project(COL_FQ, COL_FK) if q_ref is not None else None
    fk = project(COL_FK, COL_FV)
    rq = project(COL_RQ, COL_RK) if rq_ref is not None else None
    rk = project(COL_RK, COL_RV)

    if q_ref is not None:
        qn = fq * lax.rsqrt(_group_mean_sq(fq, gmat_ref) + EPS) * gq_ref[...]
        q_ref[...] = (qn * (FOX_HD ** -0.5 * LOG2E)).astype(BF16)
    kn = fk * lax.rsqrt(_group_mean_sq(fk, gmat_ref) + EPS) * gk_ref[...]
    k_ref[...] = kn.astype(BF16)

    r = _dot(tri, packed)
    cum = r + (pltpu.roll(r, V7X_LANES - FOX_HEADS, axis=1)
               + pltpu.roll(r, V7X_LANES - 2 * FOX_HEADS, axis=1))
    cum = jnp.where(lane < FOX_HEADS, cum, 0.0)
    if lead:
        c = -cum
    else:
        c = cum + carry_ref[...]
        carry_ref[...] = c[tm - 1:tm, :]
    c_ref[...] = (c * LOG2E)[:, :FOX_HEADS]

    for hh in range(RET_HEADS):
        sl = slice(hh * RET_DK, (hh + 1) * RET_DK)
        if rq_ref is not None:
            rq_ref[:, sl] = (_rotary_tile(rq[:, sl], cos, sin) * q_scale(hh)).astype(BF16)
        kt = _rotary_tile(rk[:, sl], cos, sin) * k_scale(hh)
        if lead:
            kt = kt * vmask
        rk_ref[:, sl] = kt.astype(BF16)
    rv = project(COL_RV, COL_RG)
    if lead:
        rv = rv * vmask
    rv_ref[...] = rv.astype(BF16)
    if sg_ref is not None:
        rg = project(COL_RG, COL_GA)
        sg_ref[...] = (rg * jax.nn.sigmoid(rg)).astype(BF16)

    if ga_ref is not None:
        bg = bg_ref[...]
        ga_ref[...] = jax.nn.sigmoid(project(COL_GA, COL_GB) + bg[:, :D_MODEL]).astype(BF16)
        gb_ref[...] = jax.nn.sigmoid(project(COL_GB, COL_END) + bg[:, D_MODEL:]).astype(BF16)

    fv = project(COL_FV, COL_FF).astype(BF16)
    ones = jnp.ones((tm, V7X_LANES), BF16)
    for j in range(FOX_PAIRS):
        v_ref[:, 2 * j * V7X_LANES:(2 * j + 1) * V7X_LANES] = fv[:, j * V7X_LANES:(j + 1) * V7X_LANES]
        v_ref[:, (2 * j + 1) * V7X_LANES:(2 * j + 2) * V7X_LANES] = ones


def _proj_body(h_ref, hl_ref, gmix_ref, wt_ref, gmat_ref, gq_ref, gk_ref, bf_ref, bg_ref, qs_ref, ks_ref,
               cos_ref, sin_ref, cosl_ref, sinl_ref, tri_ref, tril_ref,
               q_ref, k_ref, v_ref, c_ref, rq_ref, rk_ref, rv_ref, sg_ref, ga_ref, gb_ref,
               kl_ref, vl_ref, cl_ref, rkl_ref, rvl_ref, carry_ref, *, n_tiles, tiles_per_seq):
    i = pl.program_id(0)
    consts = (gmix_ref, wt_ref, gmat_ref, gq_ref, gk_ref, bf_ref, bg_ref, carry_ref)

    @pl.when(i % tiles_per_seq == 0)
    def _():
        carry_ref[...] = jnp.zeros_like(carry_ref)

    @pl.when(i < n_tiles)
    def _():
        outs = (q_ref, k_ref, v_ref, c_ref, rq_ref, rk_ref, rv_ref, sg_ref, ga_ref, gb_ref)
        _proj_rows(h_ref[...], cos_ref[...], sin_ref[...], tri_ref[...], lambda hh: qs_ref[hh],
                   lambda hh: ks_ref[hh], *consts, outs, lead=False)

    @pl.when(i == n_tiles)
    def _():
        outs = (None, kl_ref, vl_ref, cl_ref, None, rkl_ref, rvl_ref, None, None, None)
        _proj_rows(hl_ref[...], cosl_ref[...], sinl_ref[...], tril_ref[...], lambda hh: qs_ref[hh, :BLOCK],
                   lambda hh: ks_ref[hh, :BLOCK], *consts, outs, lead=True)


def _proj(h, h_lead, consts, cos, sin, tri, tri_lead, tm, tiles_per_seq):
    rows = h.shape[0]
    n_tiles = rows // tm
    assert tm % BLOCK == 0

    def rows_spec(width):
        return pl.BlockSpec((tm, width), lambda i: (jnp.minimum(i, n_tiles - 1), 0))

    def lead_spec(width):
        return pl.BlockSpec((BLOCK, width), lambda i: (0, 0))

    pos_spec = pl.BlockSpec((tm, RET_DK), lambda i: (i % tiles_per_seq, 0))
    out_widths = [D_MODEL, D_MODEL, 2 * D_MODEL, FOX_HEADS, RET_QK, RET_QK, D_MODEL, D_MODEL,
                  D_MODEL, D_MODEL]
    out_dtypes = [BF16, BF16, BF16, F32, BF16, BF16, BF16, BF16, BF16, BF16]
    lead_outs = [1, 2, 3, 5, 6]
    return pl.pallas_call(
        functools.partial(_proj_body, n_tiles=n_tiles, tiles_per_seq=tiles_per_seq),
        grid=(n_tiles + 1,),
        in_specs=[rows_spec(D_MODEL), lead_spec(D_MODEL)] + [_const_spec(a.shape) for a in consts]
        + [pos_spec, pos_spec, lead_spec(RET_DK), lead_spec(RET_DK), _const_spec(tri.shape),
           _const_spec(tri_lead.shape)],
        out_specs=[rows_spec(w) for w in out_widths] + [lead_spec(out_widths[j]) for j in lead_outs],
        out_shape=[jax.ShapeDtypeStruct((rows, w), dt) for w, dt in zip(out_widths, out_dtypes)]
        + [jax.ShapeDtypeStruct((BLOCK, out_widths[j]), out_dtypes[j]) for j in lead_outs],
        scratch_shapes=[pltpu.VMEM((1, V7X_LANES), F32)],
        compiler_params=_params(("arbitrary",)),
        name="proj",
    )(h, h_lead, *consts, cos[BLOCK:], sin[BLOCK:], cos, sin, tri, tri_lead)


def _fox_body(stab_ref, cs_ref, ce_ref, q_ref, cq_ref, k_ref, v_ref, ck_ref, kl_ref, vl_ref, cl_ref,
              *rest, n_cast, n_cast_short, short_steps):
    srcs, (o_ref, *dsts), (m_ref, acc_ref, e_ref) = (rest[:n_cast], rest[n_cast:2 * n_cast + 1],
                                                     rest[2 * n_cast + 1:])
    step = pl.program_id(0) * pl.num_programs(1) + pl.program_id(1)
    for src, dst in zip(srcs[:n_cast - n_cast_short], dsts):
        dst[...] = src[...].astype(BF16)

    @pl.when(step < short_steps)
    def _():
        for src, dst in zip(srcs[n_cast - n_cast_short:], dsts[n_cast - n_cast_short:]):
            dst[...] = src[...].astype(BF16)

    _fox_attend(stab_ref, cs_ref, ce_ref, q_ref, cq_ref, k_ref, v_ref, ck_ref, kl_ref, vl_ref, cl_ref,
                o_ref, m_ref, acc_ref, e_ref)


def _fox_attend(stab_ref, cs_ref, ce_ref, q_ref, cq_ref, k_ref, v_ref, ck_ref, kl_ref, vl_ref, cl_ref,
                o_ref, m_ref, acc_ref, e_ref):
    tq = m_ref.shape[1]
    tk = tq // 2
    tr = tq // 4
    lane = lax.broadcasted_iota(jnp.int32, (1, 2 * V7X_LANES), 1)
    first2 = (lane % V7X_LANES) < FOX_HD
    first = first2[:, :V7X_LANES]
    lead_ok = lax.broadcasted_iota(jnp.int32, (tq, BLOCK), 1) >= N_EMPTY
    row = lax.broadcasted_iota(jnp.int32, (tq, tk), 0)
    col = lax.broadcasted_iota(jnp.int32, (tq, tk), 1)
    causal = col <= row
    corner = causal[:tr, :tr]
    use_bound = stab_ref[0] > 0.5

    def keys(j):
        ks = pl.multiple_of(j * tk, tk)
        return k_ref[pl.ds(ks, tk), :], ck_ref[:, pl.ds(ks, tk)]

    def values(j):
        return v_ref[pl.ds(pl.multiple_of(j * tk, tk), tk), :]

    nq = q_ref.shape[0] // tq
    nk = k_ref.shape[0] // tk

    def query_block(i, carry):
        _fox_query_block(i, tq, tk, tr, nq, nk, first, first2, lead_ok, row, col, causal, corner,
                         use_bound, keys, values, stab_ref, cs_ref, ce_ref, q_ref, cq_ref, kl_ref,
                         vl_ref, cl_ref, o_ref, m_ref, acc_ref, e_ref)
        return carry

    lax.fori_loop(0, nq, query_block, 0)


def _fox_query_block(i, tq, tk, tr, nq, nk, first, first2, lead_ok, row, col, causal, corner,
                     use_bound, keys, values, stab_ref, cs_ref, ce_ref, q_ref, cq_ref, kl_ref,
                     vl_ref, cl_ref, o_ref, m_ref, acc_ref, e_ref):
    rows = pl.ds(pl.multiple_of(i * tq, tq), tq)
    n_full = 2 * i
    q = q_ref[rows, :]
    zero = jnp.zeros_like(q)
    q_heads = (jnp.where(first, q, zero), jnp.where(first, zero, q))
    acc_ref[...] = jnp.zeros_like(acc_ref)

    @pl.when(use_bound)
    def _():
        head = lax.broadcasted_iota(jnp.int32, (1, FOX_HEADS), 1) - 2 * pl.program_id(1)
        cq_blk = cq_ref[rows, :]
        cq = [jnp.sum(jnp.where(head == hh, cq_blk, 0.0), axis=1, keepdims=True) - stab_ref[1]
              for hh in range(2)]

        def exponents(k, ck, r0=0, r1=tq):
            return [_dot_nt(q_heads[hh][r0:r1], k) + cq[hh][r0:r1] - ck[hh:hh + 1, :] for hh in range(2)]

        def accumulate(e, v, allowed, r0=0):
            if allowed is not None:
                e = [jnp.where(allowed, x, NEG) for x in e]
            pvs = [_dot(jnp.exp2(x).astype(BF16), v) for x in e]
            r1 = r0 + e[0].shape[0]
            acc_ref[r0:r1, :] += jnp.where(first2, pvs[0], pvs[1])

        cutoff = -stab_ref[2]
        slot = 2 * pl.program_id(1)
        cs_base = (pl.program_id(0) * nq + i) * FOX_HEADS
        ce_base = pl.program_id(0) * nk * FOX_HEADS
        c_first = [cs_ref[cs_base + slot + hh] for hh in range(2)]
        skip_lead = jnp.maximum(c_first[0], c_first[1]) < cutoff
        j0 = jnp.int32(0)
        for j in range(nk - 2):
            gap = jnp.maximum(c_first[0] - ce_ref[ce_base + j * FOX_HEADS + slot],
                              c_first[1] - ce_ref[ce_base + j * FOX_HEADS + slot + 1])
            j0 += jnp.logical_and(gap < cutoff, j < n_full).astype(jnp.int32)

        e0 = exponents(*keys(j0))
        e_ref[0] = e0[0]
        e_ref[1] = e0[1]

        @pl.when(jnp.logical_not(skip_lead))
        def _():
            accumulate(exponents(kl_ref[...], cl_ref[...]), vl_ref[...], lead_ok)

        def step(j, carry):
            e_cur = [e_ref[0], e_ref[1]]
            e_next = exponents(*keys(j + 1))
            accumulate(e_cur, values(j), None)
            e_ref[0] = e_next[0]
            e_ref[1] = e_next[1]
            return carry

        def step_pair(m, carry):
            return step(2 * m + 1, step(2 * m, carry))

        @pl.when(j0 % 2 == 1)
        def _():
            step(j0, 0)

        lax.fori_loop((j0 + 1) // 2, i, step_pair, 0)

        kb, ckb = keys(n_full + 1)
        vb = values(n_full + 1)
        e_b1 = exponents(kb[:tr], ckb[:, :tr], 2 * tr, tq)
        e_b2 = exponents(kb[tr:], ckb[:, tr:], 3 * tr, tq)
        va = values(n_full)
        e_a = [e_ref[0], e_ref[1]]
        strips = [
            [([x[:tr, :tr] for x in e_a], va[:tr], corner)],
            [([x[tr:2 * tr] for x in e_a], va, causal[tr:2 * tr])],
            [([x[2 * tr:3 * tr] for x in e_a], va, None), ([x[:tr] for x in e_b1], vb[:tr], corner)],
            [([x[3 * tr:] for x in e_a], va, None), ([x[tr:] for x in e_b1], vb[:tr], None),
             (e_b2, vb[tr:], corner)],
        ]
        for s, parts in enumerate(strips):
            pv = [None, None]
            for e, v, allowed in parts:
                if allowed is not None:
                    e = [jnp.where(allowed, x, NEG) for x in e]
                for hh in range(2):
                    d = _dot(jnp.exp2(e[hh]).astype(BF16), v)
                    pv[hh] = d if pv[hh] is None else pv[hh] + d
            acc_ref[s * tr:(s + 1) * tr, :] += jnp.where(first2, pv[0], pv[1])

    @pl.when(jnp.logical_not(use_bound))
    def _():
        m_ref[...] = jnp.full_like(m_ref, NEG)

        def block(k, v, ck, allowed):
            pvs, alphas = [], []
            for hh in range(2):
                s = _dot_nt(q_heads[hh], k) - ck[hh:hh + 1, :]
                if allowed is not None:
                    s = jnp.where(allowed, s, NEG)
                m_old = m_ref[hh]
                m_new = jnp.maximum(m_old, jnp.max(s, axis=1, keepdims=True))
                m_ref[hh] = m_new
                pvs.append(_dot(jnp.exp2(s - m_new).astype(BF16), v))
                alphas.append(jnp.exp2(m_old - m_new))
            acc_ref[...] = (acc_ref[...] * jnp.where(first2, alphas[0], alphas[1])
                            + jnp.where(first2, pvs[0], pvs[1]))

        block(kl_ref[...], vl_ref[...], cl_ref[...], lead_ok)

        def full_block(j, carry):
            block(keys(j)[0], values(j), keys(j)[1], None)
            return carry

        lax.fori_loop(0, n_full, full_block, 0)
        block(keys(n_full)[0], values(n_full), keys(n_full)[1], causal)
        block(keys(n_full + 1)[0], values(n_full + 1), keys(n_full + 1)[1], col + tk <= row)

    acc = acc_ref[...]
    o_ref[rows, :] = (acc[:, :V7X_LANES] / acc[:, V7X_LANES:]).astype(o_ref.dtype)


def _fox(stab, cs, ce, q, cq, k, v, ck, kl, vl, cl, weights, short_weight, batch, seq):
    tq = ATT_BLOCK
    w = V7X_LANES
    steps = batch * FOX_PAIRS
    cast_specs, cast_shapes = [], []
    for a in weights:
        slab = a.shape[0] // steps
        assert slab * steps == a.shape[0] and slab % BF16_SUBLANES == 0
        cast_specs.append(pl.BlockSpec((slab, a.shape[1]), lambda b, j: (b * FOX_PAIRS + j, 0)))
        cast_shapes.append(jax.ShapeDtypeStruct(a.shape, BF16))
    short_steps = short_weight.shape[0] // FF_CHUNK
    assert short_steps * FF_CHUNK == short_weight.shape[0] and short_steps <= steps
    cast_specs.append(pl.BlockSpec(
        (FF_CHUNK, short_weight.shape[1]),
        lambda b, j: (jnp.minimum(b * FOX_PAIRS + j, short_steps - 1), 0)))
    cast_shapes.append(jax.ShapeDtypeStruct(short_weight.shape, BF16))
    n_cast = len(cast_specs)
    return pl.pallas_call(
        functools.partial(_fox_body, n_cast=n_cast, n_cast_short=1, short_steps=short_steps),
        grid=(batch, FOX_PAIRS),
        in_specs=[
            pl.BlockSpec(memory_space=pltpu.SMEM),
            pl.BlockSpec(memory_space=pltpu.SMEM),
            pl.BlockSpec(memory_space=pltpu.SMEM),
            pl.BlockSpec((None, seq, w), lambda b, j: (b, 0, j)),
            pl.BlockSpec((None, seq, FOX_HEADS), lambda b, j: (b, 0, 0)),
            pl.BlockSpec((None, seq, w), lambda b, j: (b, 0, j)),
            pl.BlockSpec((None, seq, 2 * w), lambda b, j: (b, 0, j)),
            pl.BlockSpec((None, None, 2, seq), lambda b, j: (b, j, 0, 0)),
            pl.BlockSpec((BLOCK, w), lambda b, j: (0, j)),
            pl.BlockSpec((BLOCK, 2 * w), lambda b, j: (0, j)),
            pl.BlockSpec((None, 2, BLOCK), lambda b, j: (j, 0, 0)),
        ] + cast_specs,
        out_specs=[pl.BlockSpec((None, seq, w), lambda b, j: (b, 0, j))] + cast_specs,
        out_shape=[jax.ShapeDtypeStruct((batch, seq, D_MODEL), BF16)] + cast_shapes,
        scratch_shapes=[pltpu.VMEM((2, tq, 1), F32), pltpu.VMEM((tq, 2 * w), F32),
                        pltpu.VMEM((2, tq, tq // 2), F32)],
        compiler_params=_params(("arbitrary", "arbitrary")),
        name="fox",
    )(stab, cs, ce, q, cq, k, v, ck, kl, vl, cl, *weights, short_weight)


def _ret_body(q_ref, k_ref, v_ref, sg_ref, kl_ref, vl_ref, gn_ref, cd_ref, o_ref, kv_ref, state_ref):
    n_chunks = q_ref.shape[0] // BLOCK
    cd = cd_ref[...]
    gn = gn_ref[...]
    causal = (lax.broadcasted_iota(jnp.int32, (BLOCK, BLOCK), 1)
              <= lax.broadcasted_iota(jnp.int32, (BLOCK, BLOCK), 0))

    def rows(c):
        return pl.ds(pl.multiple_of(c * BLOCK, BLOCK), BLOCK)

    def summarise(c, carry):
        kv_ref[c] = _dot_tn(k_ref[rows(c), :], v_ref[rows(c), :])
        return carry

    lax.fori_loop(0, n_chunks, summarise, 0, unroll=RET_UNROLL)

    def advance(c, state):
        state_ref[c] = state.astype(BF16)
        return cd * (state + kv_ref[c])

    lead_state = cd * _dot_tn(kl_ref[...], vl_ref[...])
    lax.fori_loop(0, n_chunks, advance, lead_state, unroll=RET_UNROLL)

    def emit(c, carry):
        q = q_ref[rows(c), :]
        v = v_ref[rows(c), :]
        scores = jnp.where(causal, _dot_nt(q, k_ref[rows(c), :]), 0.0).astype(BF16)
        o = _dot(jnp.concatenate([scores, q], axis=1), jnp.concatenate([v, state_ref[c]], axis=0))
        mu = jnp.mean(o, axis=-1, keepdims=True)
        d = o - mu
        var = jnp.mean(d * d, axis=-1, keepdims=True)
        yn = d * lax.rsqrt(var + GN_EPS) * gn
        o_ref[rows(c), :] = (sg_ref[rows(c), :].astype(F32) * yn).astype(o_ref.dtype)
        return carry

    lax.fori_loop(0, n_chunks, emit, 0, unroll=RET_UNROLL)


def _ret(rq, rk, rv, sg, rkl, rvl, gn, cd, batch, seq):
    head_qk = pl.BlockSpec((None, seq, RET_DK), lambda b, h: (b, 0, h))
    head_v = pl.BlockSpec((None, seq, RET_DV), lambda b, h: (b, 0, h))
    return pl.pallas_call(
        _ret_body,
        grid=(batch, RET_HEADS),
        in_specs=[head_qk, head_qk, head_v, head_v,
                  pl.BlockSpec((BLOCK, RET_DK), lambda b, h: (0, h)),
                  pl.BlockSpec((BLOCK, RET_DV), lambda b, h: (0, h)),
                  pl.BlockSpec((1, RET_DV), lambda b, h: (0, h)),
                  pl.BlockSpec((None, 1, RET_DV), lambda b, h: (h, 0, 0))],
        out_specs=head_v,
        out_shape=jax.ShapeDtypeStruct((batch, seq, D_MODEL), BF16),
        scratch_shapes=[pltpu.VMEM((seq // BLOCK, RET_DK, RET_DV), F32),
                        pltpu.VMEM((seq // BLOCK, RET_DK, RET_DV), BF16)],
        compiler_params=_params(("arbitrary", "arbitrary")),
        name="ret",
    )(rq, rk, rv, sg, rkl, rvl, gn, cd)


def _out_body(ya_ref, yb_ref, ga_ref, gb_ref, h_ref, wa_ref, wb_ref, wo_ref, g_ref, win_ref,
              wout_ref, o_ref):
    mixed = (ga_ref[...].astype(F32) * _dot(ya_ref[...], wa_ref[...])
             + gb_ref[...].astype(F32) * _dot(yb_ref[...], wb_ref[...]))
    h = h_ref[...] + _dot(mixed.astype(BF16), wo_ref[...])
    xn = _rms_rows(h, g_ref[...]).astype(BF16)
    o_ref[...] = h + 0.5 * _swiglu(xn, win_ref, wout_ref)


def _out(ya, yb, ga, gb, h, consts, tm):
    rows = h.shape[0]
    row_spec = pl.BlockSpec((tm, D_MODEL), lambda i: (i, 0))
    return pl.pallas_call(
        _out_body,
        grid=(rows // tm,),
        in_specs=[row_spec] * 5 + [_const_spec(a.shape) for a in consts],
        out_specs=row_spec,
        out_shape=jax.ShapeDtypeStruct((rows, D_MODEL), F32),
        compiler_params=_params(("arbitrary",)),
        name="out",
    )(ya, yb, ga, gb, h, *consts)


def _ffn_weights(w_in, w_out, dtype):
    return w_in.astype(dtype), w_out.astype(dtype).reshape(N_FF_CHUNKS, FF_CHUNK, D_MODEL)


def _position_tables(seq):
    half = RET_DK // 2
    pos = np.arange(BLOCK + seq, dtype=np.float64) - N_EMPTY
    inv = ROPE_BASE ** (-np.arange(half, dtype=np.float64) / half)
    ang = pos[:, None] * inv[None, :]
    cos = np.concatenate([np.cos(ang), np.cos(ang)], axis=1)
    sin = np.concatenate([-np.sin(ang), np.sin(ang)], axis=1)
    return jnp.asarray(cos, dtype=F32), jnp.asarray(sin, dtype=F32)


def _retention_tables(rows):
    log_gamma = np.log1p(-np.exp2(-5.0 - np.arange(RET_HEADS, dtype=np.float64)))
    n = (np.arange(rows) % BLOCK + 1.0)[None, :, None]
    lg = log_gamma[:, None, None]
    shape = (RET_HEADS, rows, RET_DK)
    q_scale = np.broadcast_to(np.exp(lg * n), shape)
    k_scale = np.broadcast_to(np.exp(-lg * n) * RET_DK ** -0.5, shape)
    cd = np.broadcast_to(np.exp(log_gamma * BLOCK)[:, None, None], (RET_HEADS, 1, RET_DV))
    return tuple(jnp.asarray(t, dtype=F32) for t in (q_scale, k_scale, cd))


def kernel(x, meta_tokens, norm_ffn1, w_ffn1_in, w_ffn1_out, norm_mix, w_in, b_forget, b_gate,
           fox_q_norm, fox_k_norm, w_o_fox, ret_gn, w_o_ret, w_out, norm_ffn2, w_ffn2_in,
           w_ffn2_out):
    batch, seq, d = x.shape
    assert d == D_MODEL and seq % ATT_BLOCK == 0 and seq % ROW_TILE == 0
    assert norm_ffn1.shape[0] == 1, "one layer"
    tiles_per_seq = seq // ROW_TILE

    win1, wout1 = _ffn_weights(w_ffn1_in[0], w_ffn1_out[0], F32)
    grp = np.arange(V7X_MXU_DIM) // FOX_HD
    gmat = jnp.asarray((grp[:, None] == grp[None, :]) / FOX_HD, dtype=BF16)
    gq = jnp.tile(fox_q_norm[0], FOX_HEADS)[None, :]
    gk = jnp.tile(fox_k_norm[0], FOX_HEADS)[None, :]
    bf = jnp.pad(b_forget[0], (0, V7X_LANES - FOX_HEADS))[None, :]
    bg = b_gate[0][None, :]
    cos, sin = _position_tables(seq)
    idx = np.arange(ROW_TILE)
    tri_incl = jnp.asarray(idx[None, :] <= idx[:, None], dtype=BF16)
    idx = np.arange(BLOCK)
    tri_after = jnp.asarray(idx[None, :] > idx[:, None], dtype=BF16)

    lead = jnp.concatenate([jnp.zeros((N_EMPTY, d), x.dtype), meta_tokens.astype(x.dtype)], axis=0)
    xr = x.reshape(batch * seq, d)

    assert w_in.shape[2] == COL_END
    h1, h1l, wt = _ffn_and_cast(xr, lead, norm_ffn1, win1, wout1, w_in[0].T, ROW_TILE)
    q_scale, k_scale, chunk_decay = _retention_tables(ROW_TILE)
    proj_consts = (norm_mix, wt, gmat, gq, gk, bf, bg, q_scale, k_scale)
    q, k, v, c, rq, rk, rv, sg, ga, gb, kl, vl, cl, rkl, rvl = _proj(
        h1, h1l, proj_consts, cos, sin, tri_incl, tri_after, ROW_TILE, tiles_per_seq)

    def b3(a):
        return a.reshape(batch, seq, a.shape[-1])

    ck = b3(c).transpose(0, 2, 1).reshape(batch, FOX_PAIRS, 2, seq)
    clt = cl.T.reshape(FOX_PAIRS, 2, BLOCK)
    bound = FOX_HD ** 0.5 * jnp.max(jnp.abs(fox_q_norm[0])) * jnp.max(jnp.abs(fox_k_norm[0]))
    bound2 = bound * LOG2E
    stab = jnp.stack([(bound <= SAFE_LOGIT_BOUND).astype(F32), bound2, F32_EXP2_ZERO + 2.0 * bound2])
    c3 = b3(c)
    cs = c3[:, ::ATT_BLOCK, :].reshape(-1)
    ce = c3[:, ATT_BLOCK // 2 - 1::ATT_BLOCK // 2, :].reshape(-1)
    ya, win2, wof, wor, wo, wout2 = _fox(
        stab, cs, ce, b3(q), c3, b3(k), b3(v), ck, kl, vl, clt,
        (w_ffn2_in[0], w_o_fox[0], w_o_ret[0], w_out[0]), w_ffn2_out[0], batch, seq)
    yb = _ret(b3(rq), b3(rk), b3(rv), b3(sg), rkl, rvl, ret_gn, chunk_decay, batch, seq)

    out_consts = (wof, wor, wo, norm_ffn2, win2, wout2.reshape(N_FF_CHUNKS, FF_CHUNK, D_MODEL))
    out = _out(ya.reshape(batch * seq, d), yb.reshape(batch * seq, d), ga, gb, h1, out_consts,
               ROW_TILE)
    return out.reshape(batch, seq, d)
```

```python
import functools
import math

import jax
import jax.numpy as jnp
import numpy as np
from jax import lax
from jax.experimental import pallas as pl
from jax.experimental.pallas import tpu as pltpu

F32 = jnp.float32
BF16 = jnp.bfloat16

D_MODEL = 1024
N_META = 16
BLOCK = 128
N_EMPTY = BLOCK - N_META
FOX_HD = 64
FOX_HEADS = D_MODEL // FOX_HD
FOX_PAIRS = FOX_HEADS // 2
RET_HEADS = 4
RET_DK = D_MODEL // (2 * RET_HEADS)
RET_DV = 2 * RET_DK
RET_QK = RET_HEADS * RET_DK
D_FF = ((8 * D_MODEL // 3 + 127) // 128) * 128
EPS = 1e-6
GN_EPS = 1e-5
ROPE_BASE = 10000.0
NEG = -1e30
LOG2E = math.log2(math.e)
SAFE_LOGIT_BOUND = 30.0
F32_EXP2_ZERO = 151.0

V7X_LANES = 128
BF16_SUBLANES = 16
V7X_MXU_DIM = 256
V7X_VMEM_BYTES = 64 * 2**20
VMEM_COMPILER_RESERVE = 8 * 2**20
VMEM_LIMIT = V7X_VMEM_BYTES - VMEM_COMPILER_RESERVE

ROW_TILE = 512
FF_CHUNK = V7X_MXU_DIM
N_FF_CHUNKS = D_FF // FF_CHUNK
RET_UNROLL = 16
ATT_BLOCK = 1024


def _dot(a, b):
    return jnp.dot(a, b, preferred_element_type=F32)


def _dot_nt(a, b):
    return lax.dot_general(a, b, (((1,), (1,)), ((), ())), preferred_element_type=F32)


def _dot_tn(a, b):
    return lax.dot_general(a, b, (((0,), (0,)), ((), ())), preferred_element_type=F32)


def _const_spec(shape, block=None):
    zeros = (0,) * len(shape)
    return pl.BlockSpec(block or shape, lambda *_: zeros, pipeline_mode=pl.Buffered(1))


def _params(semantics):
    return pltpu.CompilerParams(dimension_semantics=semantics, vmem_limit_bytes=VMEM_LIMIT)


def _rms_rows(x, g):
    ms = jnp.mean(x * x, axis=-1, keepdims=True)
    return x * lax.rsqrt(ms + EPS) * g


def _swiglu(xn, win_ref, wout_ref):
    acc = None
    for c in range(N_FF_CHUNKS):
        a = _dot(xn, win_ref[:, c * FF_CHUNK:(c + 1) * FF_CHUNK].astype(BF16))
        b = _dot(xn, win_ref[:, D_FF + c * FF_CHUNK:D_FF + (c + 1) * FF_CHUNK].astype(BF16))
        hm = (a * jax.nn.sigmoid(a) * b).astype(BF16)
        d = _dot(hm, wout_ref[c].astype(BF16))
        acc = d if acc is None else acc + d
    return acc


def _ffn_body(x_ref, g_ref, win_ref, wout_ref, o_ref):
    x = x_ref[...]
    xn = _rms_rows(x, g_ref[...]).astype(BF16)
    o_ref[...] = x + 0.5 * _swiglu(xn, win_ref, wout_ref)


def _ffn_cast_body(x_ref, lead_ref, g_ref, win_ref, wout_ref, w_src, o_ref, ol_ref, w_dst, *, n_tiles):
    i = pl.program_id(0)

    @pl.when(i < n_tiles)
    def _():
        _ffn_body(x_ref, g_ref, win_ref, wout_ref, o_ref)
        w_dst[...] = w_src[...].astype(BF16)

    @pl.when(i == n_tiles)
    def _():
        _ffn_body(lead_ref, g_ref, win_ref, wout_ref, ol_ref)


def _ffn_and_cast(x, lead, g, win, wout, w_t, tm):
    rows = x.shape[0]
    n_tiles = rows // tm
    n = w_t.shape[0]
    slab = min(s for s in range(BF16_SUBLANES, n + 1, BF16_SUBLANES) if n % s == 0 and n // s <= n_tiles)
    cast_steps = n // slab
    row_spec = pl.BlockSpec((tm, D_MODEL), lambda i: (jnp.minimum(i, n_tiles - 1), 0))
    lead_spec = pl.BlockSpec(lead.shape, lambda i: (0, 0))
    slab_spec = pl.BlockSpec((slab, w_t.shape[1]), lambda i: (jnp.minimum(i, cast_steps - 1), 0))
    return pl.pallas_call(
        functools.partial(_ffn_cast_body, n_tiles=n_tiles),
        grid=(n_tiles + 1,),
        in_specs=[row_spec, lead_spec, _const_spec(g.shape), _const_spec(win.shape),
                  _const_spec(wout.shape), slab_spec],
        out_specs=[row_spec, lead_spec, slab_spec],
        out_shape=[jax.ShapeDtypeStruct((rows, D_MODEL), F32), jax.ShapeDtypeStruct(lead.shape, F32),
                   jax.ShapeDtypeStruct(w_t.shape, BF16)],
        compiler_params=_params(("arbitrary",)),
        name="ffn_cast",
    )(x, lead, g, win, wout, w_t)


def _group_mean_sq(x, gmat_ref):
    sq = (x * x).astype(BF16)
    w = V7X_MXU_DIM
    parts = [_dot(sq[:, i * w:(i + 1) * w], gmat_ref[...]) for i in range(D_MODEL // w)]
    return jnp.concatenate(parts, axis=1)


def _rotary_tile(x, cos, sin_signed):
    return x * cos + pltpu.roll(x, RET_DK // 2, axis=1) * sin_signed


COL_FQ = 0
COL_FK = COL_FQ + D_MODEL
COL_FV = COL_FK + D_MODEL
COL_FF = COL_FV + D_MODEL
COL_RQ = COL_FF + FOX_HEADS
COL_RK = COL_RQ + RET_QK
COL_RV = COL_RK + RET_QK
COL_RG = COL_RV + D_MODEL
COL_GA = COL_RG + D_MODEL
COL_GB = COL_GA + D_MODEL
COL_END = COL_GB + D_MODEL


def _proj_rows(h, cos, sin, tri, q_scale, k_scale, gmix_ref, wt_ref, gmat_ref, gq_ref, gk_ref, bf_ref,
               bg_ref, carry_ref, outs, *, lead):
    q_ref, k_ref, v_ref, c_ref, rq_ref, rk_ref, rv_ref, sg_ref, ga_ref, gb_ref = outs
    tm = h.shape[0]
    u = _rms_rows(h, gmix_ref[...]).astype(BF16)

    def project(c0, c1):
        return _dot_nt(u, wt_ref[c0:c1, :])

    if lead:
        row = lax.broadcasted_iota(jnp.int32, (tm, 1), 0)
        valid = row >= N_EMPTY
        vmask = valid.astype(F32)


    z = project(COL_FF, COL_FF + V7X_LANES) + bf_ref[...]
    logf = -(jnp.maximum(-z, 0.0) + jnp.log1p(jnp.exp(-jnp.abs(z))))
    if lead:
        logf = jnp.where(valid, logf, 0.0)
    lane = lax.broadcasted_iota(jnp.int32, (1, V7X_LANES), 1)
    logf = jnp.where(lane < FOX_HEADS, logf, 0.0)
    p1 = logf.astype(BF16).astype(F32)
    r1 = logf - p1
    p2 = r1.astype(BF16).astype(F32)
    p3 = (r1 - p2).astype(BF16).astype(F32)
    packed = (p1 + pltpu.roll(p2, FOX_HEADS, axis=1) + pltpu.roll(p3, 2 * FOX_HEADS, axis=1)).astype(BF16)

    fq = project(COL_FQ, COL_FK) if q_ref is not None else None
    fk = project(COL_FK, COL_FV)
    rq = project(COL_RQ, COL_RK) if rq_ref is not None else None
    rk = project(COL_RK, COL_RV)

    if q_ref is not None:
        qn = fq * lax.rsqrt(_group_mean_sq(fq, gmat_ref) + EPS) * gq_ref[...]
        q_ref[...] = (qn * (FOX_HD ** -0.5 * LOG2E)).astype(BF16)
    kn = fk * lax.rsqrt(_group_mean_sq(fk, gmat_ref) + EPS) * gk_ref[...]
    k_ref[...] = kn.astype(BF16)

    r = _dot(tri, packed)
    cum = r + (pltpu.roll(r, V7X_LANES - FOX_HEADS, axis=1)
               + pltpu.roll(r, V7X_LANES - 2 * FOX_HEADS, axis=1))
    cum = jnp.where(lane < FOX_HEADS, cum, 0.0)
    if lead:
        c = -cum
    else:
        c = cum + carry_ref[...]
        carry_ref[...] = c[tm - 1:tm, :]
    c_ref[...] = c * LOG2E

    for hh in range(RET_HEADS):
        sl = slice(hh * RET_DK, (hh + 1) * RET_DK)
        if rq_ref is not None:
            rq_ref[:, sl] = (_rotary_tile(rq[:, sl], cos, sin) * q_scale(hh)).astype(BF16)
        kt = _rotary_tile(rk[:, sl], cos, sin) * k_scale(hh)
        if lead:
            kt = kt * vmask
        rk_ref[:, sl] = kt.astype(BF16)
    rv = project(COL_RV, COL_RG)
    if lead:
        rv = rv * vmask
    rv_ref[...] = rv.astype(BF16)
    if sg_ref is not None:
        rg = project(COL_RG, COL_GA)
        sg_ref[...] = (rg * jax.nn.sigmoid(rg)).astype(BF16)

    if ga_ref is not None:
        bg = bg_ref[...]
        ga_ref[...] = jax.nn.sigmoid(project(COL_GA, COL_GB) + bg[:, :D_MODEL]).astype(BF16)
        gb_ref[...] = jax.nn.sigmoid(project(COL_GB, COL_END) + bg[:, D_MODEL:]).astype(BF16)

    fv = project(COL_FV, COL_FF).astype(BF16)
    ones = jnp.ones((tm, V7X_LANES), BF16)
    for j in range(FOX_PAIRS):
        v_ref[:, 2 * j * V7X_LANES:(2 * j + 1) * V7X_LANES] = fv[:, j * V7X_LANES:(j + 1) * V7X_LANES]
        v_ref[:, (2 * j + 1) * V7X_LANES:(2 * j + 2) * V7X_LANES] = ones


def _proj_body(h_ref, hl_ref, gmix_ref, wt_ref, gmat_ref, gq_ref, gk_ref, bf_ref, bg_ref, qs_ref, ks_ref,
               cos_ref, sin_ref, cosl_ref, sinl_ref, tri_ref, tril_ref,
               q_ref, k_ref, v_ref, c_ref, rq_ref, rk_ref, rv_ref, sg_ref, ga_ref, gb_ref,
               kl_ref, vl_ref, cl_ref, rkl_ref, rvl_ref, carry_ref, *, n_tiles, tiles_per_seq):
    i = pl.program_id(0)
    consts = (gmix_ref, wt_ref, gmat_ref, gq_ref, gk_ref, bf_ref, bg_ref, carry_ref)

    @pl.when(i % tiles_per_seq == 0)
    def _():
        carry_ref[...] = jnp.zeros_like(carry_ref)

    @pl.when(i < n_tiles)
    def _():
        outs = (q_ref, k_ref, v_ref, c_ref, rq_ref, rk_ref, rv_ref, sg_ref, ga_ref, gb_ref)
        _proj_rows(h_ref[...], cos_ref[...], sin_ref[...], tri_ref[...], lambda hh: qs_ref[hh],
                   lambda hh: ks_ref[hh], *consts, outs, lead=False)

    @pl.when(i == n_tiles)
    def _():
        outs = (None, kl_ref, vl_ref, cl_ref, None, rkl_ref, rvl_ref, None, None, None)
        _proj_rows(hl_ref[...], cosl_ref[...], sinl_ref[...], tril_ref[...], lambda hh: qs_ref[hh, :BLOCK],
                   lambda hh: ks_ref[hh, :BLOCK], *consts, outs, lead=True)


def _proj(h, h_lead, consts, cos, sin, tri, tri_lead, tm, tiles_per_seq):
    rows = h.shape[0]
    n_tiles = rows // tm
    assert tm % BLOCK == 0

    def rows_spec(width):
        return pl.BlockSpec((tm, width), lambda i: (jnp.minimum(i, n_tiles - 1), 0))

    def lead_spec(width):
        return pl.BlockSpec((BLOCK, width), lambda i: (0, 0))

    pos_spec = pl.BlockSpec((tm, RET_DK), lambda i: (i % tiles_per_seq, 0))
    out_widths = [D_MODEL, D_MODEL, 2 * D_MODEL, V7X_LANES, RET_QK, RET_QK, D_MODEL, D_MODEL,
                  D_MODEL, D_MODEL]
    out_dtypes = [BF16, BF16, BF16, F32, BF16, BF16, BF16, BF16, BF16, BF16]
    lead_outs = [1, 2, 3, 5, 6]
    return pl.pallas_call(
        functools.partial(_proj_body, n_tiles=n_tiles, tiles_per_seq=tiles_per_seq),
        grid=(n_tiles + 1,),
        in_specs=[rows_spec(D_MODEL), lead_spec(D_MODEL)] + [_const_spec(a.shape) for a in consts]
        + [pos_spec, pos_spec, lead_spec(RET_DK), lead_spec(RET_DK), _const_spec(tri.shape),
           _const_spec(tri_lead.shape)],
        out_specs=[rows_spec(w) for w in out_widths] + [lead_spec(out_widths[j]) for j in lead_outs],
        out_shape=[jax.ShapeDtypeStruct((rows, w), dt) for w, dt in zip(out_widths, out_dtypes)]
        + [jax.ShapeDtypeStruct((BLOCK, out_widths[j]), out_dtypes[j]) for j in lead_outs],
        scratch_shapes=[pltpu.VMEM((1, V7X_LANES), F32)],
        compiler_params=_params(("arbitrary",)),
        name="proj",
    )(h, h_lead, *consts, cos[BLOCK:], sin[BLOCK:], cos, sin, tri, tri_lead)


def _fox_body(stab_ref, cs_ref, ce_ref, q_ref, cq_ref, k_ref, v_ref, ck_ref, kl_ref, vl_ref, cl_ref,
              *rest, n_cast, n_cast_short, short_steps):
    srcs, (o_ref, *dsts), (m_ref, acc_ref, e_ref) = (rest[:n_cast], rest[n_cast:2 * n_cast + 1],
                                                     rest[2 * n_cast + 1:])
    step = pl.program_id(0) * pl.num_programs(1) + pl.program_id(1)
    for src, dst in zip(srcs[:n_cast - n_cast_short], dsts):
        dst[...] = src[...].astype(BF16)

    @pl.when(step < short_steps)
    def _():
        for src, dst in zip(srcs[n_cast - n_cast_short:], dsts[n_cast - n_cast_short:]):
            dst[...] = src[...].astype(BF16)

    _fox_attend(stab_ref, cs_ref, ce_ref, q_ref, cq_ref, k_ref, v_ref, ck_ref, kl_ref, vl_ref, cl_ref,
                o_ref, m_ref, acc_ref, e_ref)


def _fox_attend(stab_ref, cs_ref, ce_ref, q_ref, cq_ref, k_ref, v_ref, ck_ref, kl_ref, vl_ref, cl_ref,
                o_ref, m_ref, acc_ref, e_ref):
    tq = m_ref.shape[1]
    tk = tq // 2
    tr = tq // 4
    lane = lax.broadcasted_iota(jnp.int32, (1, 2 * V7X_LANES), 1)
    first2 = (lane % V7X_LANES) < FOX_HD
    first = first2[:, :V7X_LANES]
    lead_ok = lax.broadcasted_iota(jnp.int32, (tq, BLOCK), 1) >= N_EMPTY
    row = lax.broadcasted_iota(jnp.int32, (tq, tk), 0)
    col = lax.broadcasted_iota(jnp.int32, (tq, tk), 1)
    causal = col <= row
    corner = causal[:tr, :tr]
    use_bound = stab_ref[0] > 0.5

    def keys(j):
        ks = pl.multiple_of(j * tk, tk)
        return k_ref[pl.ds(ks, tk), :], ck_ref[:, pl.ds(ks, tk)]

    def values(j):
        return v_ref[pl.ds(pl.multiple_of(j * tk, tk), tk), :]

    nq = q_ref.shape[0] // tq
    nk = k_ref.shape[0] // tk

    def query_block(i, carry):
        _fox_query_block(i, tq, tk, tr, nq, nk, first, first2, lead_ok, row, col, causal, corner,
                         use_bound, keys, values, stab_ref, cs_ref, ce_ref, q_ref, cq_ref, kl_ref,
                         vl_ref, cl_ref, o_ref, m_ref, acc_ref, e_ref)
        return carry

    lax.fori_loop(0, nq, query_block, 0)


def _fox_query_block(i, tq, tk, tr, nq, nk, first, first2, lead_ok, row, col, causal, corner,
                     use_bound, keys, values, stab_ref, cs_ref, ce_ref, q_ref, cq_ref, kl_ref,
                     vl_ref, cl_ref, o_ref, m_ref, acc_ref, e_ref):
    rows = pl.ds(pl.multiple_of(i * tq, tq), tq)
    n_full = 2 * i
    q = q_ref[rows, :]
    zero = jnp.zeros_like(q)
    q_heads = (jnp.where(first, q, zero), jnp.where(first, zero, q))
    acc_ref[...] = jnp.zeros_like(acc_ref)

    @pl.when(use_bound)
    def _():
        head = lax.broadcasted_iota(jnp.int32, (1, V7X_LANES), 1) - 2 * pl.program_id(1)
        cq_blk = cq_ref[rows, :]
        cq = [jnp.sum(jnp.where(head == hh, cq_blk, 0.0), axis=1, keepdims=True) - stab_ref[1]
              for hh in range(2)]

        def exponents(k, ck, r0=0, r1=tq):
            return [_dot_nt(q_heads[hh][r0:r1], k) + cq[hh][r0:r1] - ck[hh:hh + 1, :] for hh in range(2)]

        def accumulate(e, v, allowed, r0=0):
            if allowed is not None:
                e = [jnp.where(allowed, x, NEG) for x in e]
            pvs = [_dot(jnp.exp2(x).astype(BF16), v) for x in e]
            r1 = r0 + e[0].shape[0]
            acc_ref[r0:r1, :] += jnp.where(first2, pvs[0], pvs[1])

        cutoff = -stab_ref[2]
        slot = 2 * pl.program_id(1)
        cs_base = (pl.program_id(0) * nq + i) * FOX_HEADS
        ce_base = pl.program_id(0) * nk * FOX_HEADS
        c_first = [cs_ref[cs_base + slot + hh] for hh in range(2)]
        skip_lead = jnp.maximum(c_first[0], c_first[1]) < cutoff
        j0 = jnp.int32(0)
        for j in range(nk - 2):
            gap = jnp.maximum(c_first[0] - ce_ref[ce_base + j * FOX_HEADS + slot],
                              c_first[1] - ce_ref[ce_base + j * FOX_HEADS + slot + 1])
            j0 += jnp.logical_and(gap < cutoff, j < n_full).astype(jnp.int32)

        e0 = exponents(*keys(j0))
        e_ref[0] = e0[0]
        e_ref[1] = e0[1]

        @pl.when(jnp.logical_not(skip_lead))
        def _():
            accumulate(exponents(kl_ref[...], cl_ref[...]), vl_ref[...], lead_ok)

        def step(j, carry):
            e_cur = [e_ref[0], e_ref[1]]
            e_next = exponents(*keys(j + 1))
            accumulate(e_cur, values(j), None)
            e_ref[0] = e_next[0]
            e_ref[1] = e_next[1]
            return carry

        def step_pair(m, carry):
            return step(2 * m + 1, step(2 * m, carry))

        @pl.when(j0 % 2 == 1)
        def _():
            step(j0, 0)

        lax.fori_loop((j0 + 1) // 2, i, step_pair, 0)

        kb, ckb = keys(n_full + 1)
        vb = values(n_full + 1)
        e_b1 = exponents(kb[:tr], ckb[:, :tr], 2 * tr, tq)
        e_b2 = exponents(kb[tr:], ckb[:, tr:], 3 * tr, tq)
        va = values(n_full)
        e_a = [e_ref[0], e_ref[1]]
        strips = [
            [([x[:tr, :tr] for x in e_a], va[:tr], corner)],
            [([x[tr:2 * tr] for x in e_a], va, causal[tr:2 * tr])],
            [([x[2 * tr:3 * tr] for x in e_a], va, None), ([x[:tr] for x in e_b1], vb[:tr], corner)],
            [([x[3 * tr:] for x in e_a], va, None), ([x[tr:] for x in e_b1], vb[:tr], None),
             (e_b2, vb[tr:], corner)],
        ]
        for s, parts in enumerate(strips):
            pv = [None, None]
            for e, v, allowed in parts:
                if allowed is not None:
                    e = [jnp.where(allowed, x, NEG) for x in e]
                for hh in range(2):
                    d = _dot(jnp.exp2(e[hh]).astype(BF16), v)
                    pv[hh] = d if pv[hh] is None else pv[hh] + d
            acc_ref[s * tr:(s + 1) * tr, :] += jnp.where(first2, pv[0], pv[1])

    @pl.when(jnp.logical_not(use_bound))
    def _():
        m_ref[...] = jnp.full_like(m_ref, NEG)

        def block(k, v, ck, allowed):
            pvs, alphas = [], []
            for hh in range(2):
                s = _dot_nt(q_heads[hh], k) - ck[hh:hh + 1, :]
                if allowed is not None:
                    s = jnp.where(allowed, s, NEG)
                m_old = m_ref[hh]
                m_new = jnp.maximum(m_old, jnp.max(s, axis=1, keepdims=True))
                m_ref[hh] = m_new
                pvs.append(_dot(jnp.exp2(s - m_new).astype(BF16), v))
                alphas.append(jnp.exp2(m_old - m_new))
            acc_ref[...] = (acc_ref[...] * jnp.where(first2, alphas[0], alphas[1])
                            + jnp.where(first2, pvs[0], pvs[1]))

        block(kl_ref[...], vl_ref[...], cl_ref[...], lead_ok)

        def full_block(j, carry):
            block(keys(j)[0], values(j), keys(j)[1], None)
            return carry

        lax.fori_loop(0, n_full, full_block, 0)
        block(keys(n_full)[0], values(n_full), keys(n_full)[1], causal)
        block(keys(n_full + 1)[0], values(n_full + 1), keys(n_full + 1)[1], col + tk <= row)

    acc = acc_ref[...]
    o_ref[rows, :] = (acc[:, :V7X_LANES] / acc[:, V7X_LANES:]).astype(o_ref.dtype)


def _fox(stab, cs, ce, q, cq, k, v, ck, kl, vl, cl, weights, short_weight, batch, seq):
    tq = ATT_BLOCK
    w = V7X_LANES
    steps = batch * FOX_PAIRS
    cast_specs, cast_shapes = [], []
    for a in weights:
        slab = a.shape[0] // steps
        assert slab * steps == a.shape[0] and slab % BF16_SUBLANES == 0
        cast_specs.append(pl.BlockSpec((slab, a.shape[1]), lambda b, j: (b * FOX_PAIRS + j, 0)))
        cast_shapes.append(jax.ShapeDtypeStruct(a.shape, BF16))
    short_steps = short_weight.shape[0] // FF_CHUNK
    assert short_steps * FF_CHUNK == short_weight.shape[0] and short_steps <= steps
    cast_specs.append(pl.BlockSpec(
        (FF_CHUNK, short_weight.shape[1]),
        lambda b, j: (jnp.minimum(b * FOX_PAIRS + j, short_steps - 1), 0)))
    cast_shapes.append(jax.ShapeDtypeStruct(short_weight.shape, BF16))
    n_cast = len(cast_specs)
    return pl.pallas_call(
        functools.partial(_fox_body, n_cast=n_cast, n_cast_short=1, short_steps=short_steps),
        grid=(batch, FOX_PAIRS),
        in_specs=[
            pl.BlockSpec(memory_space=pltpu.SMEM),
            pl.BlockSpec(memory_space=pltpu.SMEM),
            pl.BlockSpec(memory_space=pltpu.SMEM),
            pl.BlockSpec((None, seq, w), lambda b, j: (b, 0, j)),
            pl.BlockSpec((None, seq, w), lambda b, j: (b, 0, 0)),
            pl.BlockSpec((None, seq, w), lambda b, j: (b, 0, j)),
            pl.BlockSpec((None, seq, 2 * w), lambda b, j: (b, 0, j)),
            pl.BlockSpec((None, None, 2, seq), lambda b, j: (b, j, 0, 0)),
            pl.BlockSpec((BLOCK, w), lambda b, j: (0, j)),
            pl.BlockSpec((BLOCK, 2 * w), lambda b, j: (0, j)),
            pl.BlockSpec((None, 2, BLOCK), lambda b, j: (j, 0, 0)),
        ] + cast_specs,
        out_specs=[pl.BlockSpec((None, seq, w), lambda b, j: (b, 0, j))] + cast_specs,
        out_shape=[jax.ShapeDtypeStruct((batch, seq, D_MODEL), BF16)] + cast_shapes,
        scratch_shapes=[pltpu.VMEM((2, tq, 1), F32), pltpu.VMEM((tq, 2 * w), F32),
                        pltpu.VMEM((2, tq, tq // 2), F32)],
        compiler_params=_params(("arbitrary", "arbitrary")),
        name="fox",
    )(stab, cs, ce, q, cq, k, v, ck, kl, vl, cl, *weights, short_weight)


def _ret_body(q_ref, k_ref, v_ref, sg_ref, kl_ref, vl_ref, gn_ref, cd_ref, o_ref, kv_ref, state_ref):
    n_chunks = q_ref.shape[0] // BLOCK
    cd = cd_ref[...]
    gn = gn_ref[...]
    causal = (lax.broadcasted_iota(jnp.int32, (BLOCK, BLOCK), 1)
              <= lax.broadcasted_iota(jnp.int32, (BLOCK, BLOCK), 0))

    def rows(c):
        return pl.ds(pl.multiple_of(c * BLOCK, BLOCK), BLOCK)

    def summarise(c, carry):
        kv_ref[c] = _dot_tn(k_ref[rows(c), :], v_ref[rows(c), :])
        return carry

    lax.fori_loop(0, n_chunks, summarise, 0, unroll=RET_UNROLL)

    def advance(c, state):
        state_ref[c] = state.astype(BF16)
        return cd * (state + kv_ref[c])

    lead_state = cd * _dot_tn(kl_ref[...], vl_ref[...])
    lax.fori_loop(0, n_chunks, advance, lead_state, unroll=RET_UNROLL)

    def emit(c, carry):
        q = q_ref[rows(c), :]
        v = v_ref[rows(c), :]
        scores = jnp.where(causal, _dot_nt(q, k_ref[rows(c), :]), 0.0).astype(BF16)
        o = _dot(jnp.concatenate([scores, q], axis=1), jnp.concatenate([v, state_ref[c]], axis=0))
        mu = jnp.mean(o, axis=-1, keepdims=True)
        d = o - mu
        var = jnp.mean(d * d, axis=-1, keepdims=True)
        yn = d * lax.rsqrt(var + GN_EPS) * gn
        o_ref[rows(c), :] = (sg_ref[rows(c), :].astype(F32) * yn).astype(o_ref.dtype)
        return carry

    lax.fori_loop(0, n_chunks, emit, 0, unroll=RET_UNROLL)


def _ret(rq, rk, rv, sg, rkl, rvl, gn, cd, batch, seq):
    head_qk = pl.BlockSpec((None, seq, RET_DK), lambda b, h: (b, 0, h))
    head_v = pl.BlockSpec((None, seq, RET_DV), lambda b, h: (b, 0, h))
    return pl.pallas_call(
        _ret_body,
        grid=(batch, RET_HEADS),
        in_specs=[head_qk, head_qk, head_v, head_v,
                  pl.BlockSpec((BLOCK, RET_DK), lambda b, h: (0, h)),
                  pl.BlockSpec((BLOCK, RET_DV), lambda b, h: (0, h)),
                  pl.BlockSpec((1, RET_DV), lambda b, h: (0, h)),
                  pl.BlockSpec((None, 1, RET_DV), lambda b, h: (h, 0, 0))],
        out_specs=head_v,
        out_shape=jax.ShapeDtypeStruct((batch, seq, D_MODEL), BF16),
        scratch_shapes=[pltpu.VMEM((seq // BLOCK, RET_DK, RET_DV), F32),
                        pltpu.VMEM((seq // BLOCK, RET_DK, RET_DV), BF16)],
        compiler_params=_params(("arbitrary", "arbitrary")),
        name="ret",
    )(rq, rk, rv, sg, rkl, rvl, gn, cd)


def _out_body(ya_ref, yb_ref, ga_ref, gb_ref, h_ref, wa_ref, wb_ref, wo_ref, g_ref, win_ref,
              wout_ref, o_ref):
    mixed = (ga_ref[...].astype(F32) * _dot(ya_ref[...], wa_ref[...])
             + gb_ref[...].astype(F32) * _dot(yb_ref[...], wb_ref[...]))
    h = h_ref[...] + _dot(mixed.astype(BF16), wo_ref[...])
    xn = _rms_rows(h, g_ref[...]).astype(BF16)
    o_ref[...] = h + 0.5 * _swiglu(xn, win_ref, wout_ref)


def _out(ya, yb, ga, gb, h, consts, tm):
    rows = h.shape[0]
    row_spec = pl.BlockSpec((tm, D_MODEL), lambda i: (i, 0))
    return pl.pallas_call(
        _out_body,
        grid=(rows // tm,),
        in_specs=[row_spec] * 5 + [_const_spec(a.shape) for a in consts],
        out_specs=row_spec,
        out_shape=jax.ShapeDtypeStruct((rows, D_MODEL), F32),
        compiler_params=_params(("arbitrary",)),
        name="out",
    )(ya, yb, ga, gb, h, *consts)


def _ffn_weights(w_in, w_out, dtype):
    return w_in.astype(dtype), w_out.astype(dtype).reshape(N_FF_CHUNKS, FF_CHUNK, D_MODEL)


def _position_tables(seq):
    half = RET_DK // 2
    pos = np.arange(BLOCK + seq, dtype=np.float64) - N_EMPTY
    inv = ROPE_BASE ** (-np.arange(half, dtype=np.float64) / half)
    ang = pos[:, None] * inv[None, :]
    cos = np.concatenate([np.cos(ang), np.cos(ang)], axis=1)
    sin = np.concatenate([-np.sin(ang), np.sin(ang)], axis=1)
    return jnp.asarray(cos, dtype=F32), jnp.asarray(sin, dtype=F32)


def _retention_tables(rows):
    log_gamma = np.log1p(-np.exp2(-5.0 - np.arange(RET_HEADS, dtype=np.float64)))
    n = (np.arange(rows) % BLOCK + 1.0)[None, :, None]
    lg = log_gamma[:, None, None]
    shape = (RET_HEADS, rows, RET_DK)
    q_scale = np.broadcast_to(np.exp(lg * n), shape)
    k_scale = np.broadcast_to(np.exp(-lg * n) * RET_DK ** -0.5, shape)
    cd = np.broadcast_to(np.exp(log_gamma * BLOCK)[:, None, None], (RET_HEADS, 1, RET_DV))
    return tuple(jnp.asarray(t, dtype=F32) for t in (q_scale, k_scale, cd))


def kernel(x, meta_tokens, norm_ffn1, w_ffn1_in, w_ffn1_out, norm_mix, w_in, b_forget, b_gate,
           fox_q_norm, fox_k_norm, w_o_fox, ret_gn, w_o_ret, w_out, norm_ffn2, w_ffn2_in,
           w_ffn2_out):
    batch, seq, d = x.shape
    assert d == D_MODEL and seq % ATT_BLOCK == 0 and seq % ROW_TILE == 0
    assert norm_ffn1.shape[0] == 1, "one layer"
    tiles_per_seq = seq // ROW_TILE

    win1, wout1 = _ffn_weights(w_ffn1_in[0], w_ffn1_out[0], F32)
    grp = np.arange(V7X_MXU_DIM) // FOX_HD
    gmat = jnp.asarray((grp[:, None] == grp[None, :]) / FOX_HD, dtype=BF16)
    gq = jnp.tile(fox_q_norm[0], FOX_HEADS)[None, :]
    gk = jnp.tile(fox_k_norm[0], FOX_HEADS)[None, :]
    bf = jnp.pad(b_forget[0], (0, V7X_LANES - FOX_HEADS))[None, :]
    bg = b_gate[0][None, :]
    cos, sin = _position_tables(seq)
    idx = np.arange(ROW_TILE)
    tri_incl = jnp.asarray(idx[None, :] <= idx[:, None], dtype=BF16)
    idx = np.arange(BLOCK)
    tri_after = jnp.asarray(idx[None, :] > idx[:, None], dtype=BF16)

    lead = jnp.concatenate([jnp.zeros((N_EMPTY, d), x.dtype), meta_tokens.astype(x.dtype)], axis=0)
    xr = x.reshape(batch * seq, d)

    assert w_in.shape[2] == COL_END
    h1, h1l, wt = _ffn_and_cast(xr, lead, norm_ffn1, win1, wout1, w_in[0].T, ROW_TILE)
    q_scale, k_scale, chunk_decay = _retention_tables(ROW_TILE)
    proj_consts = (norm_mix, wt, gmat, gq, gk, bf, bg, q_scale, k_scale)
    q, k, v, c, rq, rk, rv, sg, ga, gb, kl, vl, cl, rkl, rvl = _proj(
        h1, h1l, proj_consts, cos, sin, tri_incl, tri_after, ROW_TILE, tiles_per_seq)

    def b3(a):
        return a.reshape(batch, seq, a.shape[-1])

    c3 = b3(c)
    ch = c3[:, :, :FOX_HEADS]
    ck = ch.transpose(0, 2, 1).reshape(batch, FOX_PAIRS, 2, seq)
    clt = cl[:, :FOX_HEADS].T.reshape(FOX_PAIRS, 2, BLOCK)
    bound = FOX_HD ** 0.5 * jnp.max(jnp.abs(fox_q_norm[0])) * jnp.max(jnp.abs(fox_k_norm[0]))
    bound2 = bound * LOG2E
    stab = jnp.stack([(bound <= SAFE_LOGIT_BOUND).astype(F32), bound2, F32_EXP2_ZERO + 2.0 * bound2])
    cs = ch[:, ::ATT_BLOCK, :].reshape(-1)
    ce = ch[:, ATT_BLOCK // 2 - 1::ATT_BLOCK // 2, :].reshape(-1)
    ya, win2, wof, wor, wo, wout2 = _fox(
        stab, cs, ce, b3(q), c3, b3(k), b3(v), ck, kl, vl, clt,
        (w_ffn2_in[0], w_o_fox[0], w_o_ret[0], w_out[0]), w_ffn2_out[0], batch, seq)
    yb = _ret(b3(rq), b3(rk), b3(rv), b3(sg), rkl, rvl, ret_gn, chunk_decay, batch, seq)

    out_consts = (wof, wor, wo, norm_ffn2, win2, wout2.reshape(N_FF_CHUNKS, FF_CHUNK, D_MODEL))
    out = _out(ya.reshape(batch * seq, d), yb.reshape(batch * seq, d), ga, gb, h1, out_consts,
               ROW_TILE)
    return out.reshape(batch, seq, d)
```

```python
import functools
import math

import jax
import jax.numpy as jnp
import numpy as np
from jax import lax
from jax.experimental import pallas as pl
from jax.experimental.pallas import tpu as pltpu

F32 = jnp.float32
BF16 = jnp.bfloat16

D_MODEL = 1024
N_META = 16
BLOCK = 128
N_EMPTY = BLOCK - N_META
FOX_HD = 64
FOX_HEADS = D_MODEL // FOX_HD
FOX_PAIRS = FOX_HEADS // 2
RET_HEADS = 4
RET_DK = D_MODEL // (2 * RET_HEADS)
RET_DV = 2 * RET_DK
RET_QK = RET_HEADS * RET_DK
D_FF = ((8 * D_MODEL // 3 + 127) // 128) * 128
EPS = 1e-6
GN_EPS = 1e-5
ROPE_BASE = 10000.0
NEG = -1e30
LOG2E = math.log2(math.e)
SAFE_LOGIT_BOUND = 30.0
F32_EXP2_ZERO = 151.0

V7X_LANES = 128
BF16_SUBLANES = 16
V7X_MXU_DIM = 256
V7X_VMEM_BYTES = 64 * 2**20
VMEM_COMPILER_RESERVE = 8 * 2**20
VMEM_LIMIT = V7X_VMEM_BYTES - VMEM_COMPILER_RESERVE

ROW_TILE = 512
FF_CHUNK = V7X_MXU_DIM
N_FF_CHUNKS = D_FF // FF_CHUNK
RET_UNROLL = 16
ATT_BLOCK = 1024


def _dot(a, b):
    return jnp.dot(a, b, preferred_element_type=F32)


def _dot_nt(a, b):
    return lax.dot_general(a, b, (((1,), (1,)), ((), ())), preferred_element_type=F32)


def _dot_tn(a, b):
    return lax.dot_general(a, b, (((0,), (0,)), ((), ())), preferred_element_type=F32)


def _const_spec(shape, block=None):
    zeros = (0,) * len(shape)
    return pl.BlockSpec(block or shape, lambda *_: zeros, pipeline_mode=pl.Buffered(1))


def _params(semantics):
    return pltpu.CompilerParams(dimension_semantics=semantics, vmem_limit_bytes=VMEM_LIMIT)


def _rms_rows(x, g):
    ms = jnp.mean(x * x, axis=-1, keepdims=True)
    return x * lax.rsqrt(ms + EPS) * g


def _swiglu(xn, win_ref, wout_ref):
    acc = None
    for c in range(N_FF_CHUNKS):
        a = _dot(xn, win_ref[:, c * FF_CHUNK:(c + 1) * FF_CHUNK].astype(BF16))
        b = _dot(xn, win_ref[:, D_FF + c * FF_CHUNK:D_FF + (c + 1) * FF_CHUNK].astype(BF16))
        hm = (a * jax.nn.sigmoid(a) * b).astype(BF16)
        d = _dot(hm, wout_ref[c].astype(BF16))
        acc = d if acc is None else acc + d
    return acc


def _ffn_body(x_ref, g_ref, win_ref, wout_ref, o_ref):
    x = x_ref[...]
    xn = _rms_rows(x, g_ref[...]).astype(BF16)
    o_ref[...] = x + 0.5 * _swiglu(xn, win_ref, wout_ref)


def _ffn_cast_body(x_ref, lead_ref, g_ref, win_ref, wout_ref, w_src, o_ref, ol_ref, w_dst, *, n_tiles):
    i = pl.program_id(0)

    @pl.when(i < n_tiles)
    def _():
        _ffn_body(x_ref, g_ref, win_ref, wout_ref, o_ref)
        w_dst[...] = w_src[...].astype(BF16)

    @pl.when(i == n_tiles)
    def _():
        _ffn_body(lead_ref, g_ref, win_ref, wout_ref, ol_ref)


def _ffn_and_cast(x, lead, g, win, wout, w_t, tm):
    rows = x.shape[0]
    n_tiles = rows // tm
    n = w_t.shape[0]
    slab = min(s for s in range(BF16_SUBLANES, n + 1, BF16_SUBLANES) if n % s == 0 and n // s <= n_tiles)
    cast_steps = n // slab
    row_spec = pl.BlockSpec((tm, D_MODEL), lambda i: (jnp.minimum(i, n_tiles - 1), 0))
    lead_spec = pl.BlockSpec(lead.shape, lambda i: (0, 0))
    slab_spec = pl.BlockSpec((slab, w_t.shape[1]), lambda i: (jnp.minimum(i, cast_steps - 1), 0))
    return pl.pallas_call(
        functools.partial(_ffn_cast_body, n_tiles=n_tiles),
        grid=(n_tiles + 1,),
        in_specs=[row_spec, lead_spec, _const_spec(g.shape), _const_spec(win.shape),
                  _const_spec(wout.shape), slab_spec],
        out_specs=[row_spec, lead_spec, slab_spec],
        out_shape=[jax.ShapeDtypeStruct((rows, D_MODEL), F32), jax.ShapeDtypeStruct(lead.shape, F32),
                   jax.ShapeDtypeStruct(w_t.shape, BF16)],
        compiler_params=_params(("arbitrary",)),
        name="ffn_cast",
    )(x, lead, g, win, wout, w_t)


def _group_mean_sq(x, gmat_ref):
    sq = (x * x).astype(BF16)
    w = V7X_MXU_DIM
    parts = [_dot(sq[:, i * w:(i + 1) * w], gmat_ref[...]) for i in range(D_MODEL // w)]
    return jnp.concatenate(parts, axis=1)


def _rotary_tile(x, cos, sin_signed):
    return x * cos + pltpu.roll(x, RET_DK // 2, axis=1) * sin_signed


COL_FQ = 0
COL_FK = COL_FQ + D_MODEL
COL_FV = COL_FK + D_MODEL
COL_FF = COL_FV + D_MODEL
COL_RQ = COL_FF + FOX_HEADS
COL_RK = COL_RQ + RET_QK
COL_RV = COL_RK + RET_QK
COL_RG = COL_RV + D_MODEL
COL_GA = COL_RG + D_MODEL
COL_GB = COL_GA + D_MODEL
COL_END = COL_GB + D_MODEL

OFF_Q = 0
OFF_K = OFF_Q + D_MODEL
OFF_V = OFF_K + D_MODEL
OFF_RQ = OFF_V + 2 * D_MODEL
OFF_RK = OFF_RQ + RET_QK
OFF_RV = OFF_RK + RET_QK
OFF_SG = OFF_RV + D_MODEL
OFF_GA = OFF_SG + D_MODEL
OFF_GB = OFF_GA + D_MODEL
OFF_END = OFF_GB + D_MODEL
LOFF_K = 0
LOFF_V = LOFF_K + D_MODEL
LOFF_RK = LOFF_V + 2 * D_MODEL
LOFF_RV = LOFF_RK + RET_QK
LOFF_END = LOFF_RV + D_MODEL


def _proj_rows(h, cos, sin, tri, q_scale, k_scale, gmix_ref, wt_ref, gmat_ref, gq_ref, gk_ref, bf_ref,
               bg_ref, carry_ref, outs, *, lead):
    q_ref, k_ref, v_ref, c_ref, rq_ref, rk_ref, rv_ref, sg_ref, ga_ref, gb_ref = outs
    tm = h.shape[0]
    u = _rms_rows(h, gmix_ref[...]).astype(BF16)

    def project(c0, c1):
        return _dot_nt(u, wt_ref[c0:c1, :])

    if lead:
        row = lax.broadcasted_iota(jnp.int32, (tm, 1), 0)
        valid = row >= N_EMPTY
        vmask = valid.astype(F32)


    z = project(COL_FF, COL_FF + V7X_LANES) + bf_ref[...]
    logf = -(jnp.maximum(-z, 0.0) + jnp.log1p(jnp.exp(-jnp.abs(z))))
    if lead:
        logf = jnp.where(valid, logf, 0.0)
    lane = lax.broadcasted_iota(jnp.int32, (1, V7X_LANES), 1)
    logf = jnp.where(lane < FOX_HEADS, logf, 0.0)
    p1 = logf.astype(BF16).astype(F32)
    r1 = logf - p1
    p2 = r1.astype(BF16).astype(F32)
    p3 = (r1 - p2).astype(BF16).astype(F32)
    packed = (p1 + pltpu.roll(p2, FOX_HEADS, axis=1) + pltpu.roll(p3, 2 * FOX_HEADS, axis=1)).astype(BF16)

    fq = project(COL_FQ, COL_FK) if q_ref is not None else None
    fk = project(COL_FK, COL_FV)
    rq = project(COL_RQ, COL_RK) if rq_ref is not None else None
    rk = project(COL_RK, COL_RV)

    if q_ref is not None:
        qn = fq * lax.rsqrt(_group_mean_sq(fq, gmat_ref) + EPS) * gq_ref[...]
        q_ref[...] = (qn * (FOX_HD ** -0.5 * LOG2E)).astype(BF16)
    kn = fk * lax.rsqrt(_group_mean_sq(fk, gmat_ref) + EPS) * gk_ref[...]
    k_ref[...] = kn.astype(BF16)

    r = _dot(tri, packed)
    cum = r + (pltpu.roll(r, V7X_LANES - FOX_HEADS, axis=1)
               + pltpu.roll(r, V7X_LANES - 2 * FOX_HEADS, axis=1))
    cum = jnp.where(lane < FOX_HEADS, cum, 0.0)
    if lead:
        c = -cum
    else:
        c = cum + carry_ref[...]
        carry_ref[...] = c[tm - 1:tm, :]
    c_ref[...] = (c * LOG2E)[:, :FOX_HEADS]

    for hh in range(RET_HEADS):
        sl = slice(hh * RET_DK, (hh + 1) * RET_DK)
        if rq_ref is not None:
            rq_ref[:, sl] = (_rotary_tile(rq[:, sl], cos, sin) * q_scale(hh)).astype(BF16)
        kt = _rotary_tile(rk[:, sl], cos, sin) * k_scale(hh)
        if lead:
            kt = kt * vmask
        rk_ref[:, sl] = kt.astype(BF16)
    rv = project(COL_RV, COL_RG)
    if lead:
        rv = rv * vmask
    rv_ref[...] = rv.astype(BF16)
    if sg_ref is not None:
        rg = project(COL_RG, COL_GA)
        sg_ref[...] = (rg * jax.nn.sigmoid(rg)).astype(BF16)

    if ga_ref is not None:
        bg = bg_ref[...]
        ga_ref[...] = jax.nn.sigmoid(project(COL_GA, COL_GB) + bg[:, :D_MODEL]).astype(BF16)
        gb_ref[...] = jax.nn.sigmoid(project(COL_GB, COL_END) + bg[:, D_MODEL:]).astype(BF16)

    fv = project(COL_FV, COL_FF).astype(BF16)
    ones = jnp.ones((tm, V7X_LANES), BF16)
    for j in range(FOX_PAIRS):
        v_ref[:, 2 * j * V7X_LANES:(2 * j + 1) * V7X_LANES] = fv[:, j * V7X_LANES:(j + 1) * V7X_LANES]
        v_ref[:, (2 * j + 1) * V7X_LANES:(2 * j + 2) * V7X_LANES] = ones


def _proj_body(h_ref, hl_ref, gmix_ref, wt_ref, gmat_ref, gq_ref, gk_ref, bf_ref, bg_ref, qs_ref, ks_ref,
               cos_ref, sin_ref, cosl_ref, sinl_ref, tri_ref, tril_ref,
               p_ref, c_ref, pl_ref, cl_ref, carry_ref, *, n_tiles, tiles_per_seq):
    i = pl.program_id(0)
    consts = (gmix_ref, wt_ref, gmat_ref, gq_ref, gk_ref, bf_ref, bg_ref, carry_ref)

    @pl.when(i % tiles_per_seq == 0)
    def _():
        carry_ref[...] = jnp.zeros_like(carry_ref)

    @pl.when(i < n_tiles)
    def _():
        bounds = (OFF_Q, OFF_K, OFF_V, OFF_RQ, OFF_RK, OFF_RV, OFF_SG, OFF_GA, OFF_GB, OFF_END)
        q, k, v, rq, rk, rv, sg, ga, gb = [p_ref.at[:, a:b] for a, b in zip(bounds[:-1], bounds[1:])]
        _proj_rows(h_ref[...], cos_ref[...], sin_ref[...], tri_ref[...], lambda hh: qs_ref[hh],
                   lambda hh: ks_ref[hh], *consts, (q, k, v, c_ref, rq, rk, rv, sg, ga, gb), lead=False)

    @pl.when(i == n_tiles)
    def _():
        bounds = (LOFF_K, LOFF_V, LOFF_RK, LOFF_RV, LOFF_END)
        k, v, rk, rv = [pl_ref.at[:, a:b] for a, b in zip(bounds[:-1], bounds[1:])]
        outs = (None, k, v, cl_ref, None, rk, rv, None, None, None)
        _proj_rows(hl_ref[...], cosl_ref[...], sinl_ref[...], tril_ref[...], lambda hh: qs_ref[hh, :BLOCK],
                   lambda hh: ks_ref[hh, :BLOCK], *consts, outs, lead=True)


def _proj(h, h_lead, consts, cos, sin, tri, tri_lead, tm, tiles_per_seq):
    rows = h.shape[0]
    n_tiles = rows // tm
    assert tm % BLOCK == 0

    def rows_spec(width):
        return pl.BlockSpec((tm, width), lambda i: (jnp.minimum(i, n_tiles - 1), 0))

    def lead_spec(width):
        return pl.BlockSpec((BLOCK, width), lambda i: (0, 0))

    pos_spec = pl.BlockSpec((tm, RET_DK), lambda i: (i % tiles_per_seq, 0))
    return pl.pallas_call(
        functools.partial(_proj_body, n_tiles=n_tiles, tiles_per_seq=tiles_per_seq),
        grid=(n_tiles + 1,),
        in_specs=[rows_spec(D_MODEL), lead_spec(D_MODEL)] + [_const_spec(a.shape) for a in consts]
        + [pos_spec, pos_spec, lead_spec(RET_DK), lead_spec(RET_DK), _const_spec(tri.shape),
           _const_spec(tri_lead.shape)],
        out_specs=[rows_spec(OFF_END), rows_spec(FOX_HEADS), lead_spec(LOFF_END), lead_spec(FOX_HEADS)],
        out_shape=[jax.ShapeDtypeStruct((rows, OFF_END), BF16), jax.ShapeDtypeStruct((rows, FOX_HEADS), F32),
                   jax.ShapeDtypeStruct((BLOCK, LOFF_END), BF16),
                   jax.ShapeDtypeStruct((BLOCK, FOX_HEADS), F32)],
        scratch_shapes=[pltpu.VMEM((1, V7X_LANES), F32)],
        compiler_params=_params(("arbitrary",)),
        name="proj",
    )(h, h_lead, *consts, cos[BLOCK:], sin[BLOCK:], cos, sin, tri, tri_lead)


def _fox_body(stab_ref, cs_ref, ce_ref, q_ref, cq_ref, k_ref, v_ref, ck_ref, kl_ref, vl_ref, cl_ref,
              *rest, n_cast, n_cast_short, short_steps):
    srcs, (o_ref, *dsts), (m_ref, acc_ref, e_ref) = (rest[:n_cast], rest[n_cast:2 * n_cast + 1],
                                                     rest[2 * n_cast + 1:])
    step = pl.program_id(0) * pl.num_programs(1) + pl.program_id(1)
    for src, dst in zip(srcs[:n_cast - n_cast_short], dsts):
        dst[...] = src[...].astype(BF16)

    @pl.when(step < short_steps)
    def _():
        for src, dst in zip(srcs[n_cast - n_cast_short:], dsts[n_cast - n_cast_short:]):
            dst[...] = src[...].astype(BF16)

    _fox_attend(stab_ref, cs_ref, ce_ref, q_ref, cq_ref, k_ref, v_ref, ck_ref, kl_ref, vl_ref, cl_ref,
                o_ref, m_ref, acc_ref, e_ref)


def _fox_attend(stab_ref, cs_ref, ce_ref, q_ref, cq_ref, k_ref, v_ref, ck_ref, kl_ref, vl_ref, cl_ref,
                o_ref, m_ref, acc_ref, e_ref):
    tq = m_ref.shape[1]
    tk = tq // 2
    tr = tq // 4
    lane = lax.broadcasted_iota(jnp.int32, (1, 2 * V7X_LANES), 1)
    first2 = (lane % V7X_LANES) < FOX_HD
    first = first2[:, :V7X_LANES]
    lead_ok = lax.broadcasted_iota(jnp.int32, (tq, BLOCK), 1) >= N_EMPTY
    row = lax.broadcasted_iota(jnp.int32, (tq, tk), 0)
    col = lax.broadcasted_iota(jnp.int32, (tq, tk), 1)
    causal = col <= row
    corner = causal[:tr, :tr]
    use_bound = stab_ref[0] > 0.5

    def keys(j):
        ks = pl.multiple_of(j * tk, tk)
        return k_ref[pl.ds(ks, tk), :], ck_ref[:, pl.ds(ks, tk)]

    def values(j):
        return v_ref[pl.ds(pl.multiple_of(j * tk, tk), tk), :]

    nq = q_ref.shape[0] // tq
    nk = k_ref.shape[0] // tk

    def query_block(i, carry):
        _fox_query_block(i, tq, tk, tr, nq, nk, first, first2, lead_ok, row, col, causal, corner,
                         use_bound, keys, values, stab_ref, cs_ref, ce_ref, q_ref, cq_ref, kl_ref,
                         vl_ref, cl_ref, o_ref, m_ref, acc_ref, e_ref)
        return carry

    lax.fori_loop(0, nq, query_block, 0)


def _fox_query_block(i, tq, tk, tr, nq, nk, first, first2, lead_ok, row, col, causal, corner,
                     use_bound, keys, values, stab_ref, cs_ref, ce_ref, q_ref, cq_ref, kl_ref,
                     vl_ref, cl_ref, o_ref, m_ref, acc_ref, e_ref):
    rows = pl.ds(pl.multiple_of(i * tq, tq), tq)
    n_full = 2 * i
    q = q_ref[rows, :]
    zero = jnp.zeros_like(q)
    q_heads = (jnp.where(first, q, zero), jnp.where(first, zero, q))
    acc_ref[...] = jnp.zeros_like(acc_ref)

    @pl.when(use_bound)
    def _():
        head = lax.broadcasted_iota(jnp.int32, (1, FOX_HEADS), 1) - 2 * pl.program_id(1)
        cq_blk = cq_ref[rows, :]
        cq = [jnp.sum(jnp.where(head == hh, cq_blk, 0.0), axis=1, keepdims=True) - stab_ref[1]
              for hh in range(2)]

        def exponents(k, ck, r0=0, r1=tq):
            return [_dot_nt(q_heads[hh][r0:r1], k) + cq[hh][r0:r1] - ck[hh:hh + 1, :] for hh in range(2)]

        def accumulate(e, v, allowed, r0=0):
            if allowed is not None:
                e = [jnp.where(allowed, x, NEG) for x in e]
            pvs = [_dot(jnp.exp2(x).astype(BF16), v) for x in e]
            r1 = r0 + e[0].shape[0]
            acc_ref[r0:r1, :] += jnp.where(first2, pvs[0], pvs[1])

        cutoff = -stab_ref[2]
        slot = 2 * pl.program_id(1)
        cs_base = (pl.program_id(0) * nq + i) * FOX_HEADS
        ce_base = pl.program_id(0) * nk * FOX_HEADS
        c_first = [cs_ref[cs_base + slot + hh] for hh in range(2)]
        skip_lead = jnp.maximum(c_first[0], c_first[1]) < cutoff
        j0 = jnp.int32(0)
        for j in range(nk - 2):
            gap = jnp.maximum(c_first[0] - ce_ref[ce_base + j * FOX_HEADS + slot],
                              c_first[1] - ce_ref[ce_base + j * FOX_HEADS + slot + 1])
            j0 += jnp.logical_and(gap < cutoff, j < n_full).astype(jnp.int32)

        e0 = exponents(*keys(j0))
        e_ref[0] = e0[0]
        e_ref[1] = e0[1]

        @pl.when(jnp.logical_not(skip_lead))
        def _():
            accumulate(exponents(kl_ref[...], cl_ref[...]), vl_ref[...], lead_ok)

        def step(j, carry):
            e_cur = [e_ref[0], e_ref[1]]
            e_next = exponents(*keys(j + 1))
            accumulate(e_cur, values(j), None)
            e_ref[0] = e_next[0]
            e_ref[1] = e_next[1]
            return carry

        def step_pair(m, carry):
            return step(2 * m + 1, step(2 * m, carry))

        @pl.when(j0 % 2 == 1)
        def _():
            step(j0, 0)

        lax.fori_loop((j0 + 1) // 2, i, step_pair, 0)

        kb, ckb = keys(n_full + 1)
        vb = values(n_full + 1)
        e_b1 = exponents(kb[:tr], ckb[:, :tr], 2 * tr, tq)
        e_b2 = exponents(kb[tr:], ckb[:, tr:], 3 * tr, tq)
        va = values(n_full)
        e_a = [e_ref[0], e_ref[1]]
        strips = [
            [([x[:tr, :tr] for x in e_a], va[:tr], corner)],
            [([x[tr:2 * tr] for x in e_a], va, causal[tr:2 * tr])],
            [([x[2 * tr:3 * tr] for x in e_a], va, None), ([x[:tr] for x in e_b1], vb[:tr], corner)],
            [([x[3 * tr:] for x in e_a], va, None), ([x[tr:] for x in e_b1], vb[:tr], None),
             (e_b2, vb[tr:], corner)],
        ]
        for s, parts in enumerate(strips):
            pv = [None, None]
            for e, v, allowed in parts:
                if allowed is not None:
                    e = [jnp.where(allowed, x, NEG) for x in e]
                for hh in range(2):
                    d = _dot(jnp.exp2(e[hh]).astype(BF16), v)
                    pv[hh] = d if pv[hh] is None else pv[hh] + d
            acc_ref[s * tr:(s + 1) * tr, :] += jnp.where(first2, pv[0], pv[1])

    @pl.when(jnp.logical_not(use_bound))
    def _():
        m_ref[...] = jnp.full_like(m_ref, NEG)

        def block(k, v, ck, allowed):
            pvs, alphas = [], []
            for hh in range(2):
                s = _dot_nt(q_heads[hh], k) - ck[hh:hh + 1, :]
                if allowed is not None:
                    s = jnp.where(allowed, s, NEG)
                m_old = m_ref[hh]
                m_new = jnp.maximum(m_old, jnp.max(s, axis=1, keepdims=True))
                m_ref[hh] = m_new
                pvs.append(_dot(jnp.exp2(s - m_new).astype(BF16), v))
                alphas.append(jnp.exp2(m_old - m_new))
            acc_ref[...] = (acc_ref[...] * jnp.where(first2, alphas[0], alphas[1])
                            + jnp.where(first2, pvs[0], pvs[1]))

        block(kl_ref[...], vl_ref[...], cl_ref[...], lead_ok)

        def full_block(j, carry):
            block(keys(j)[0], values(j), keys(j)[1], None)
            return carry

        lax.fori_loop(0, n_full, full_block, 0)
        block(keys(n_full)[0], values(n_full), keys(n_full)[1], causal)
        block(keys(n_full + 1)[0], values(n_full + 1), keys(n_full + 1)[1], col + tk <= row)

    acc = acc_ref[...]
    o_ref[rows, :] = (acc[:, :V7X_LANES] / acc[:, V7X_LANES:]).astype(o_ref.dtype)


def _fox(stab, cs, ce, q, cq, k, v, ck, kl, vl, cl, weights, short_weight, batch, seq):
    tq = ATT_BLOCK
    w = V7X_LANES
    steps = batch * FOX_PAIRS
    cast_specs, cast_shapes = [], []
    for a in weights:
        slab = a.shape[0] // steps
        assert slab * steps == a.shape[0] and slab % BF16_SUBLANES == 0
        cast_specs.append(pl.BlockSpec((slab, a.shape[1]), lambda b, j: (b * FOX_PAIRS + j, 0)))
        cast_shapes.append(jax.ShapeDtypeStruct(a.shape, BF16))
    short_steps = short_weight.shape[0] // FF_CHUNK
    assert short_steps * FF_CHUNK == short_weight.shape[0] and short_steps <= steps
    cast_specs.append(pl.BlockSpec(
        (FF_CHUNK, short_weight.shape[1]),
        lambda b, j: (jnp.minimum(b * FOX_PAIRS + j, short_steps - 1), 0)))
    cast_shapes.append(jax.ShapeDtypeStruct(short_weight.shape, BF16))
    n_cast = len(cast_specs)
    return pl.pallas_call(
        functools.partial(_fox_body, n_cast=n_cast, n_cast_short=1, short_steps=short_steps),
        grid=(batch, FOX_PAIRS),
        in_specs=[
            pl.BlockSpec(memory_space=pltpu.SMEM),
            pl.BlockSpec(memory_space=pltpu.SMEM),
            pl.BlockSpec(memory_space=pltpu.SMEM),
            pl.BlockSpec((None, seq, w), lambda b, j: (b, 0, OFF_Q // w + j)),
            pl.BlockSpec((None, seq, FOX_HEADS), lambda b, j: (b, 0, 0)),
            pl.BlockSpec((None, seq, w), lambda b, j: (b, 0, OFF_K // w + j)),
            pl.BlockSpec((None, seq, 2 * w), lambda b, j: (b, 0, OFF_V // (2 * w) + j)),
            pl.BlockSpec((None, None, 2, seq), lambda b, j: (b, j, 0, 0)),
            pl.BlockSpec((BLOCK, w), lambda b, j: (0, LOFF_K // w + j)),
            pl.BlockSpec((BLOCK, 2 * w), lambda b, j: (0, LOFF_V // (2 * w) + j)),
            pl.BlockSpec((None, 2, BLOCK), lambda b, j: (j, 0, 0)),
        ] + cast_specs,
        out_specs=[pl.BlockSpec((None, seq, w), lambda b, j: (b, 0, j))] + cast_specs,
        out_shape=[jax.ShapeDtypeStruct((batch, seq, D_MODEL), BF16)] + cast_shapes,
        scratch_shapes=[pltpu.VMEM((2, tq, 1), F32), pltpu.VMEM((tq, 2 * w), F32),
                        pltpu.VMEM((2, tq, tq // 2), F32)],
        compiler_params=_params(("arbitrary", "arbitrary")),
        name="fox",
    )(stab, cs, ce, q, cq, k, v, ck, kl, vl, cl, *weights, short_weight)


def _ret_body(q_ref, k_ref, v_ref, sg_ref, kl_ref, vl_ref, gn_ref, cd_ref, o_ref, kv_ref, state_ref):
    n_chunks = q_ref.shape[0] // BLOCK
    cd = cd_ref[...]
    gn = gn_ref[...]
    causal = (lax.broadcasted_iota(jnp.int32, (BLOCK, BLOCK), 1)
              <= lax.broadcasted_iota(jnp.int32, (BLOCK, BLOCK), 0))

    def rows(c):
        return pl.ds(pl.multiple_of(c * BLOCK, BLOCK), BLOCK)

    def summarise(c, carry):
        kv_ref[c] = _dot_tn(k_ref[rows(c), :], v_ref[rows(c), :])
        return carry

    lax.fori_loop(0, n_chunks, summarise, 0, unroll=RET_UNROLL)

    def advance(c, state):
        state_ref[c] = state.astype(BF16)
        return cd * (state + kv_ref[c])

    lead_state = cd * _dot_tn(kl_ref[...], vl_ref[...])
    lax.fori_loop(0, n_chunks, advance, lead_state, unroll=RET_UNROLL)

    def emit(c, carry):
        q = q_ref[rows(c), :]
        v = v_ref[rows(c), :]
        scores = jnp.where(causal, _dot_nt(q, k_ref[rows(c), :]), 0.0).astype(BF16)
        o = _dot(jnp.concatenate([scores, q], axis=1), jnp.concatenate([v, state_ref[c]], axis=0))
        mu = jnp.mean(o, axis=-1, keepdims=True)
        d = o - mu
        var = jnp.mean(d * d, axis=-1, keepdims=True)
        yn = d * lax.rsqrt(var + GN_EPS) * gn
        o_ref[rows(c), :] = (sg_ref[rows(c), :].astype(F32) * yn).astype(o_ref.dtype)
        return carry

    lax.fori_loop(0, n_chunks, emit, 0, unroll=RET_UNROLL)


def _ret(rq, rk, rv, sg, rkl, rvl, gn, cd, batch, seq):
    def head_qk(off):
        return pl.BlockSpec((None, seq, RET_DK), lambda b, h: (b, 0, off // RET_DK + h))

    def head_v(off):
        return pl.BlockSpec((None, seq, RET_DV), lambda b, h: (b, 0, off // RET_DV + h))

    return pl.pallas_call(
        _ret_body,
        grid=(batch, RET_HEADS),
        in_specs=[head_qk(OFF_RQ), head_qk(OFF_RK), head_v(OFF_RV), head_v(OFF_SG),
                  pl.BlockSpec((BLOCK, RET_DK), lambda b, h: (0, LOFF_RK // RET_DK + h)),
                  pl.BlockSpec((BLOCK, RET_DV), lambda b, h: (0, LOFF_RV // RET_DV + h)),
                  pl.BlockSpec((1, RET_DV), lambda b, h: (0, h)),
                  pl.BlockSpec((None, 1, RET_DV), lambda b, h: (h, 0, 0))],
        out_specs=head_v(0),
        out_shape=jax.ShapeDtypeStruct((batch, seq, D_MODEL), BF16),
        scratch_shapes=[pltpu.VMEM((seq // BLOCK, RET_DK, RET_DV), F32),
                        pltpu.VMEM((seq // BLOCK, RET_DK, RET_DV), BF16)],
        compiler_params=_params(("arbitrary", "arbitrary")),
        name="ret",
    )(rq, rk, rv, sg, rkl, rvl, gn, cd)


def _out_body(ya_ref, yb_ref, ga_ref, gb_ref, h_ref, wa_ref, wb_ref, wo_ref, g_ref, win_ref,
              wout_ref, o_ref):
    mixed = (ga_ref[...].astype(F32) * _dot(ya_ref[...], wa_ref[...])
             + gb_ref[...].astype(F32) * _dot(yb_ref[...], wb_ref[...]))
    h = h_ref[...] + _dot(mixed.astype(BF16), wo_ref[...])
    xn = _rms_rows(h, g_ref[...]).astype(BF16)
    o_ref[...] = h + 0.5 * _swiglu(xn, win_ref, wout_ref)


def _out(ya, yb, ga, gb, h, consts, tm):
    rows = h.shape[0]
    row_spec = pl.BlockSpec((tm, D_MODEL), lambda i: (i, 0))

    def gate_spec(off):
        return pl.BlockSpec((tm, D_MODEL), lambda i: (i, off // D_MODEL))

    return pl.pallas_call(
        _out_body,
        grid=(rows // tm,),
        in_specs=[row_spec, row_spec, gate_spec(OFF_GA), gate_spec(OFF_GB), row_spec]
        + [_const_spec(a.shape) for a in consts],
        out_specs=row_spec,
        out_shape=jax.ShapeDtypeStruct((rows, D_MODEL), F32),
        compiler_params=_params(("arbitrary",)),
        name="out",
    )(ya, yb, ga, gb, h, *consts)


def _ffn_weights(w_in, w_out, dtype):
    return w_in.astype(dtype), w_out.astype(dtype).reshape(N_FF_CHUNKS, FF_CHUNK, D_MODEL)


def _position_tables(seq):
    half = RET_DK // 2
    pos = np.arange(BLOCK + seq, dtype=np.float64) - N_EMPTY
    inv = ROPE_BASE ** (-np.arange(half, dtype=np.float64) / half)
    ang = pos[:, None] * inv[None, :]
    cos = np.concatenate([np.cos(ang), np.cos(ang)], axis=1)
    sin = np.concatenate([-np.sin(ang), np.sin(ang)], axis=1)
    return jnp.asarray(cos, dtype=F32), jnp.asarray(sin, dtype=F32)


def _retention_tables(rows):
    log_gamma = np.log1p(-np.exp2(-5.0 - np.arange(RET_HEADS, dtype=np.float64)))
    n = (np.arange(rows) % BLOCK + 1.0)[None, :, None]
    lg = log_gamma[:, None, None]
    shape = (RET_HEADS, rows, RET_DK)
    q_scale = np.broadcast_to(np.exp(lg * n), shape)
    k_scale = np.broadcast_to(np.exp(-lg * n) * RET_DK ** -0.5, shape)
    cd = np.broadcast_to(np.exp(log_gamma * BLOCK)[:, None, None], (RET_HEADS, 1, RET_DV))
    return tuple(jnp.asarray(t, dtype=F32) for t in (q_scale, k_scale, cd))


def kernel(x, meta_tokens, norm_ffn1, w_ffn1_in, w_ffn1_out, norm_mix, w_in, b_forget, b_gate,
           fox_q_norm, fox_k_norm, w_o_fox, ret_gn, w_o_ret, w_out, norm_ffn2, w_ffn2_in,
           w_ffn2_out):
    batch, seq, d = x.shape
    assert d == D_MODEL and seq % ATT_BLOCK == 0 and seq % ROW_TILE == 0
    assert norm_ffn1.shape[0] == 1, "one layer"
    tiles_per_seq = seq // ROW_TILE

    win1, wout1 = _ffn_weights(w_ffn1_in[0], w_ffn1_out[0], F32)
    grp = np.arange(V7X_MXU_DIM) // FOX_HD
    gmat = jnp.asarray((grp[:, None] == grp[None, :]) / FOX_HD, dtype=BF16)
    gq = jnp.tile(fox_q_norm[0], FOX_HEADS)[None, :]
    gk = jnp.tile(fox_k_norm[0], FOX_HEADS)[None, :]
    bf = jnp.pad(b_forget[0], (0, V7X_LANES - FOX_HEADS))[None, :]
    bg = b_gate[0][None, :]
    cos, sin = _position_tables(seq)
    idx = np.arange(ROW_TILE)
    tri_incl = jnp.asarray(idx[None, :] <= idx[:, None], dtype=BF16)
    idx = np.arange(BLOCK)
    tri_after = jnp.asarray(idx[None, :] > idx[:, None], dtype=BF16)

    lead = jnp.concatenate([jnp.zeros((N_EMPTY, d), x.dtype), meta_tokens.astype(x.dtype)], axis=0)
    xr = x.reshape(batch * seq, d)

    assert w_in.shape[2] == COL_END
    h1, h1l, wt = _ffn_and_cast(xr, lead, norm_ffn1, win1, wout1, w_in[0].T, ROW_TILE)
    q_scale, k_scale, chunk_decay = _retention_tables(ROW_TILE)
    proj_consts = (norm_mix, wt, gmat, gq, gk, bf, bg, q_scale, k_scale)
    p, c, p_lead, cl = _proj(h1, h1l, proj_consts, cos, sin, tri_incl, tri_after, ROW_TILE, tiles_per_seq)

    def b3(a):
        return a.reshape(batch, seq, a.shape[-1])

    p3 = b3(p)

    ck = b3(c).transpose(0, 2, 1).reshape(batch, FOX_PAIRS, 2, seq)
    clt = cl.T.reshape(FOX_PAIRS, 2, BLOCK)
    bound = FOX_HD ** 0.5 * jnp.max(jnp.abs(fox_q_norm[0])) * jnp.max(jnp.abs(fox_k_norm[0]))
    bound2 = bound * LOG2E
    stab = jnp.stack([(bound <= SAFE_LOGIT_BOUND).astype(F32), bound2, F32_EXP2_ZERO + 2.0 * bound2])
    c3 = b3(c)
    cs = c3[:, ::ATT_BLOCK, :].reshape(-1)
    ce = c3[:, ATT_BLOCK // 2 - 1::ATT_BLOCK // 2, :].reshape(-1)
    ya, win2, wof, wor, wo, wout2 = _fox(
        stab, cs, ce, p3, c3, p3, p3, ck, p_lead, p_lead, clt,
        (w_ffn2_in[0], w_o_fox[0], w_o_ret[0], w_out[0]), w_ffn2_out[0], batch, seq)
    yb = _ret(p3, p3, p3, p3, p_lead, p_lead, ret_gn, chunk_decay, batch, seq)

    out_consts = (wof, wor, wo, norm_ffn2, win2, wout2.reshape(N_FF_CHUNKS, FF_CHUNK, D_MODEL))
    out = _out(ya.reshape(batch * seq, d), yb.reshape(batch * seq, d), p, p, h1, out_consts, ROW_TILE)
    return out.reshape(batch, seq, d)
```

```python
import functools
import math

import jax
import jax.numpy as jnp
import numpy as np
from jax import lax
from jax.experimental import pallas as pl
from jax.experimental.pallas import tpu as pltpu

F32 = jnp.float32
BF16 = jnp.bfloat16

D_MODEL = 1024
N_META = 16
BLOCK = 128
N_EMPTY = BLOCK - N_META
FOX_HD = 64
FOX_HEADS = D_MODEL // FOX_HD
FOX_PAIRS = FOX_HEADS // 2
RET_HEADS = 4
RET_DK = D_MODEL // (2 * RET_HEADS)
RET_DV = 2 * RET_DK
RET_QK = RET_HEADS * RET_DK
D_FF = ((8 * D_MODEL // 3 + 127) // 128) * 128
EPS = 1e-6
GN_EPS = 1e-5
ROPE_BASE = 10000.0
NEG = -1e30
LOG2E = math.log2(math.e)
SAFE_LOGIT_BOUND = 30.0
F32_EXP2_ZERO = 151.0

V7X_LANES = 128
BF16_SUBLANES = 16
V7X_MXU_DIM = 256
V7X_VMEM_BYTES = 64 * 2**20
VMEM_COMPILER_RESERVE = 8 * 2**20
VMEM_LIMIT = V7X_VMEM_BYTES - VMEM_COMPILER_RESERVE

ROW_TILE = 512
FF_CHUNK = V7X_MXU_DIM
N_FF_CHUNKS = D_FF // FF_CHUNK
RET_UNROLL = 32
ATT_BLOCK = 1024


def _dot(a, b):
    return jnp.dot(a, b, preferred_element_type=F32)


def _dot_nt(a, b):
    return lax.dot_general(a, b, (((1,), (1,)), ((), ())), preferred_element_type=F32)


def _dot_tn(a, b):
    return lax.dot_general(a, b, (((0,), (0,)), ((), ())), preferred_element_type=F32)


def _const_spec(shape, block=None):
    zeros = (0,) * len(shape)
    return pl.BlockSpec(block or shape, lambda *_: zeros, pipeline_mode=pl.Buffered(1))


def _params(semantics):
    return pltpu.CompilerParams(dimension_semantics=semantics, vmem_limit_bytes=VMEM_LIMIT)


def _rms_rows(x, g):
    ms = jnp.mean(x * x, axis=-1, keepdims=True)
    return x * lax.rsqrt(ms + EPS) * g


def _swiglu(xn, win_ref, wout_ref):
    acc = None
    for c in range(N_FF_CHUNKS):
        a = _dot(xn, win_ref[:, c * FF_CHUNK:(c + 1) * FF_CHUNK].astype(BF16))
        b = _dot(xn, win_ref[:, D_FF + c * FF_CHUNK:D_FF + (c + 1) * FF_CHUNK].astype(BF16))
        hm = (a * jax.nn.sigmoid(a) * b).astype(BF16)
        d = _dot(hm, wout_ref[c].astype(BF16))
        acc = d if acc is None else acc + d
    return acc


def _ffn_body(x_ref, g_ref, win_ref, wout_ref, o_ref):
    x = x_ref[...]
    xn = _rms_rows(x, g_ref[...]).astype(BF16)
    o_ref[...] = x + 0.5 * _swiglu(xn, win_ref, wout_ref)


def _ffn_cast_body(x_ref, lead_ref, g_ref, win_ref, wout_ref, w_src, o_ref, ol_ref, w_dst, *, n_tiles):
    i = pl.program_id(0)

    @pl.when(i < n_tiles)
    def _():
        _ffn_body(x_ref, g_ref, win_ref, wout_ref, o_ref)
        w_dst[...] = w_src[...].astype(BF16)

    @pl.when(i == n_tiles)
    def _():
        _ffn_body(lead_ref, g_ref, win_ref, wout_ref, ol_ref)


def _ffn_and_cast(x, lead, g, win, wout, w_t, tm):
    rows = x.shape[0]
    n_tiles = rows // tm
    n = w_t.shape[0]
    slab = min(s for s in range(BF16_SUBLANES, n + 1, BF16_SUBLANES) if n % s == 0 and n // s <= n_tiles)
    cast_steps = n // slab
    row_spec = pl.BlockSpec((tm, D_MODEL), lambda i: (jnp.minimum(i, n_tiles - 1), 0))
    lead_spec = pl.BlockSpec(lead.shape, lambda i: (0, 0))
    slab_spec = pl.BlockSpec((slab, w_t.shape[1]), lambda i: (jnp.minimum(i, cast_steps - 1), 0))
    return pl.pallas_call(
        functools.partial(_ffn_cast_body, n_tiles=n_tiles),
        grid=(n_tiles + 1,),
        in_specs=[row_spec, lead_spec, _const_spec(g.shape), _const_spec(win.shape),
                  _const_spec(wout.shape), slab_spec],
        out_specs=[row_spec, lead_spec, slab_spec],
        out_shape=[jax.ShapeDtypeStruct((rows, D_MODEL), F32), jax.ShapeDtypeStruct(lead.shape, F32),
                   jax.ShapeDtypeStruct(w_t.shape, BF16)],
        compiler_params=_params(("arbitrary",)),
        name="ffn_cast",
    )(x, lead, g, win, wout, w_t)


def _group_mean_sq(x, gmat_ref):
    sq = (x * x).astype(BF16)
    w = V7X_MXU_DIM
    parts = [_dot(sq[:, i * w:(i + 1) * w], gmat_ref[...]) for i in range(D_MODEL // w)]
    return jnp.concatenate(parts, axis=1)


def _rotary_tile(x, cos, sin_signed):
    return x * cos + pltpu.roll(x, RET_DK // 2, axis=1) * sin_signed


COL_FQ = 0
COL_FK = COL_FQ + D_MODEL
COL_FV = COL_FK + D_MODEL
COL_FF = COL_FV + D_MODEL
COL_RQ = COL_FF + FOX_HEADS
COL_RK = COL_RQ + RET_QK
COL_RV = COL_RK + RET_QK
COL_RG = COL_RV + D_MODEL
COL_GA = COL_RG + D_MODEL
COL_GB = COL_GA + D_MODEL
COL_END = COL_GB + D_MODEL


def _proj_rows(h, cos, sin, tri, q_scale, k_scale, gmix_ref, wt_ref, gmat_ref, gq_ref, gk_ref, bf_ref,
               bg_ref, carry_ref, outs, *, lead):
    q_ref, k_ref, v_ref, c_ref, rq_ref, rk_ref, rv_ref, sg_ref, ga_ref, gb_ref = outs
    tm = h.shape[0]
    u = _rms_rows(h, gmix_ref[...]).astype(BF16)

    def project(c0, c1):
        return _dot_nt(u, wt_ref[c0:c1, :])

    if lead:
        row = lax.broadcasted_iota(jnp.int32, (tm, 1), 0)
        valid = row >= N_EMPTY
        vmask = valid.astype(F32)


    z = project(COL_FF, COL_FF + V7X_LANES) + bf_ref[...]
    logf = -(jnp.maximum(-z, 0.0) + jnp.log1p(jnp.exp(-jnp.abs(z))))
    if lead:
        logf = jnp.where(valid, logf, 0.0)
    lane = lax.broadcasted_iota(jnp.int32, (1, V7X_LANES), 1)
    logf = jnp.where(lane < FOX_HEADS, logf, 0.0)
    p1 = logf.astype(BF16).astype(F32)
    r1 = logf - p1
    p2 = r1.astype(BF16).astype(F32)
    p3 = (r1 - p2).astype(BF16).astype(F32)
    packed = (p1 + pltpu.roll(p2, FOX_HEADS, axis=1) + pltpu.roll(p3, 2 * FOX_HEADS, axis=1)).astype(BF16)

    fq = project(COL_FQ, COL_FK) if q_ref is not None else None
    fk = project(COL_FK, COL_FV)
    rq = project(COL_RQ, COL_RK) if rq_ref is not None else None
    rk = project(COL_RK, COL_RV)

    if q_ref is not None:
        qn = fq * lax.rsqrt(_group_mean_sq(fq, gmat_ref) + EPS) * gq_ref[...]
        q_ref[...] = (qn * (FOX_HD ** -0.5 * LOG2E)).astype(BF16)
    kn = fk * lax.rsqrt(_group_mean_sq(fk, gmat_ref) + EPS) * gk_ref[...]
    k_ref[...] = kn.astype(BF16)

    r = _dot(tri, packed)
    cum = r + (pltpu.roll(r, V7X_LANES - FOX_HEADS, axis=1)
               + pltpu.roll(r, V7X_LANES - 2 * FOX_HEADS, axis=1))
    cum = jnp.where(lane < FOX_HEADS, cum, 0.0)
    if lead:
        c = -cum
    else:
        c = cum + carry_ref[...]
        carry_ref[...] = c[tm - 1:tm, :]
    c_ref[...] = (c * LOG2E)[:, :FOX_HEADS]

    for hh in range(RET_HEADS):
        sl = slice(hh * RET_DK, (hh + 1) * RET_DK)
        if rq_ref is not None:
            rq_ref[:, sl] = (_rotary_tile(rq[:, sl], cos, sin) * q_scale(hh)).astype(BF16)
        kt = _rotary_tile(rk[:, sl], cos, sin) * k_scale(hh)
        if lead:
            kt = kt * vmask
        rk_ref[:, sl] = kt.astype(BF16)
    rv = project(COL_RV, COL_RG)
    if lead:
        rv = rv * vmask
    rv_ref[...] = rv.astype(BF16)
    if sg_ref is not None:
        rg = project(COL_RG, COL_GA)
        sg_ref[...] = (rg * jax.nn.sigmoid(rg)).astype(BF16)

    if ga_ref is not None:
        bg = bg_ref[...]
        ga_ref[...] = jax.nn.sigmoid(project(COL_GA, COL_GB) + bg[:, :D_MODEL]).astype(BF16)
        gb_ref[...] = jax.nn.sigmoid(project(COL_GB, COL_END) + bg[:, D_MODEL:]).astype(BF16)

    fv = project(COL_FV, COL_FF).astype(BF16)
    ones = jnp.ones((tm, V7X_LANES), BF16)
    for j in range(FOX_PAIRS):
        v_ref[:, 2 * j * V7X_LANES:(2 * j + 1) * V7X_LANES] = fv[:, j * V7X_LANES:(j + 1) * V7X_LANES]
        v_ref[:, (2 * j + 1) * V7X_LANES:(2 * j + 2) * V7X_LANES] = ones


def _proj_body(h_ref, hl_ref, gmix_ref, wt_ref, gmat_ref, gq_ref, gk_ref, bf_ref, bg_ref, qs_ref, ks_ref,
               cos_ref, sin_ref, cosl_ref, sinl_ref, tri_ref, tril_ref,
               q_ref, k_ref, v_ref, c_ref, rq_ref, rk_ref, rv_ref, sg_ref, ga_ref, gb_ref,
               kl_ref, vl_ref, cl_ref, rkl_ref, rvl_ref, carry_ref, *, n_tiles, tiles_per_seq):
    i = pl.program_id(0)
    consts = (gmix_ref, wt_ref, gmat_ref, gq_ref, gk_ref, bf_ref, bg_ref, carry_ref)

    @pl.when(i % tiles_per_seq == 0)
    def _():
        carry_ref[...] = jnp.zeros_like(carry_ref)

    @pl.when(i < n_tiles)
    def _():
        outs = (q_ref, k_ref, v_ref, c_ref, rq_ref, rk_ref, rv_ref, sg_ref, ga_ref, gb_ref)
        _proj_rows(h_ref[...], cos_ref[...], sin_ref[...], tri_ref[...], lambda hh: qs_ref[hh],
                   lambda hh: ks_ref[hh], *consts, outs, lead=False)

    @pl.when(i == n_tiles)
    def _():
        outs = (None, kl_ref, vl_ref, cl_ref, None, rkl_ref, rvl_ref, None, None, None)
        _proj_rows(hl_ref[...], cosl_ref[...], sinl_ref[...], tril_ref[...], lambda hh: qs_ref[hh, :BLOCK],
                   lambda hh: ks_ref[hh, :BLOCK], *consts, outs, lead=True)


def _proj(h, h_lead, consts, cos, sin, tri, tri_lead, tm, tiles_per_seq):
    rows = h.shape[0]
    n_tiles = rows // tm
    assert tm % BLOCK == 0

    def rows_spec(width):
        return pl.BlockSpec((tm, width), lambda i: (jnp.minimum(i, n_tiles - 1), 0))

    def lead_spec(width):
        return pl.BlockSpec((BLOCK, width), lambda i: (0, 0))

    pos_spec = pl.BlockSpec((tm, RET_DK), lambda i: (i % tiles_per_seq, 0))
    out_widths = [D_MODEL, D_MODEL, 2 * D_MODEL, FOX_HEADS, RET_QK, RET_QK, D_MODEL, D_MODEL,
                  D_MODEL, D_MODEL]
    out_dtypes = [BF16, BF16, BF16, F32, BF16, BF16, BF16, BF16, BF16, BF16]
    lead_outs = [1, 2, 3, 5, 6]
    return pl.pallas_call(
        functools.partial(_proj_body, n_tiles=n_tiles, tiles_per_seq=tiles_per_seq),
        grid=(n_tiles + 1,),
        in_specs=[rows_spec(D_MODEL), lead_spec(D_MODEL)] + [_const_spec(a.shape) for a in consts]
        + [pos_spec, pos_spec, lead_spec(RET_DK), lead_spec(RET_DK), _const_spec(tri.shape),
           _const_spec(tri_lead.shape)],
        out_specs=[rows_spec(w) for w in out_widths] + [lead_spec(out_widths[j]) for j in lead_outs],
        out_shape=[jax.ShapeDtypeStruct((rows, w), dt) for w, dt in zip(out_widths, out_dtypes)]
        + [jax.ShapeDtypeStruct((BLOCK, out_widths[j]), out_dtypes[j]) for j in lead_outs],
        scratch_shapes=[pltpu.VMEM((1, V7X_LANES), F32)],
        compiler_params=_params(("arbitrary",)),
        name="proj",
    )(h, h_lead, *consts, cos[BLOCK:], sin[BLOCK:], cos, sin, tri, tri_lead)


def _fox_body(stab_ref, cs_ref, ce_ref, q_ref, cq_ref, k_ref, v_ref, ck_ref, kl_ref, vl_ref, cl_ref,
              *rest, n_cast, n_cast_short, short_steps):
    srcs, (o_ref, *dsts), (m_ref, acc_ref, e_ref) = (rest[:n_cast], rest[n_cast:2 * n_cast + 1],
                                                     rest[2 * n_cast + 1:])
    step = pl.program_id(0) * pl.num_programs(1) + pl.program_id(1)
    for src, dst in zip(srcs[:n_cast - n_cast_short], dsts):
        dst[...] = src[...].astype(BF16)

    @pl.when(step < short_steps)
    def _():
        for src, dst in zip(srcs[n_cast - n_cast_short:], dsts[n_cast - n_cast_short:]):
            dst[...] = src[...].astype(BF16)

    _fox_attend(stab_ref, cs_ref, ce_ref, q_ref, cq_ref, k_ref, v_ref, ck_ref, kl_ref, vl_ref, cl_ref,
                o_ref, m_ref, acc_ref, e_ref)


def _fox_attend(stab_ref, cs_ref, ce_ref, q_ref, cq_ref, k_ref, v_ref, ck_ref, kl_ref, vl_ref, cl_ref,
                o_ref, m_ref, acc_ref, e_ref):
    tq = m_ref.shape[1]
    tk = tq // 2
    tr = tq // 4
    lane = lax.broadcasted_iota(jnp.int32, (1, 2 * V7X_LANES), 1)
    first2 = (lane % V7X_LANES) < FOX_HD
    first = first2[:, :V7X_LANES]
    lead_ok = lax.broadcasted_iota(jnp.int32, (tq, BLOCK), 1) >= N_EMPTY
    row = lax.broadcasted_iota(jnp.int32, (tq, tk), 0)
    col = lax.broadcasted_iota(jnp.int32, (tq, tk), 1)
    causal = col <= row
    corner = causal[:tr, :tr]
    use_bound = stab_ref[0] > 0.5

    def keys(j):
        ks = pl.multiple_of(j * tk, tk)
        return k_ref[pl.ds(ks, tk), :], ck_ref[:, pl.ds(ks, tk)]

    def values(j):
        return v_ref[pl.ds(pl.multiple_of(j * tk, tk), tk), :]

    nq = q_ref.shape[0] // tq
    nk = k_ref.shape[0] // tk

    def query_block(i, carry):
        _fox_query_block(i, tq, tk, tr, nq, nk, first, first2, lead_ok, row, col, causal, corner,
                         use_bound, keys, values, stab_ref, cs_ref, ce_ref, q_ref, cq_ref, kl_ref,
                         vl_ref, cl_ref, o_ref, m_ref, acc_ref, e_ref)
        return carry

    lax.fori_loop(0, nq, query_block, 0)


def _fox_query_block(i, tq, tk, tr, nq, nk, first, first2, lead_ok, row, col, causal, corner,
                     use_bound, keys, values, stab_ref, cs_ref, ce_ref, q_ref, cq_ref, kl_ref,
                     vl_ref, cl_ref, o_ref, m_ref, acc_ref, e_ref):
    rows = pl.ds(pl.multiple_of(i * tq, tq), tq)
    n_full = 2 * i
    q = q_ref[rows, :]
    zero = jnp.zeros_like(q)
    q_heads = (jnp.where(first, q, zero), jnp.where(first, zero, q))
    acc_ref[...] = jnp.zeros_like(acc_ref)

    @pl.when(use_bound)
    def _():
        head = lax.broadcasted_iota(jnp.int32, (1, FOX_HEADS), 1) - 2 * pl.program_id(1)
        cq_blk = cq_ref[rows, :]
        cq = [jnp.sum(jnp.where(head == hh, cq_blk, 0.0), axis=1, keepdims=True) - stab_ref[1]
              for hh in range(2)]

        def exponents(k, ck, r0=0, r1=tq):
            return [_dot_nt(q_heads[hh][r0:r1], k) + cq[hh][r0:r1] - ck[hh:hh + 1, :] for hh in range(2)]

        def accumulate(e, v, allowed, r0=0):
            if allowed is not None:
                e = [jnp.where(allowed, x, NEG) for x in e]
            pvs = [_dot(jnp.exp2(x).astype(BF16), v) for x in e]
            r1 = r0 + e[0].shape[0]
            acc_ref[r0:r1, :] += jnp.where(first2, pvs[0], pvs[1])

        cutoff = -stab_ref[2]
        slot = 2 * pl.program_id(1)
        cs_base = (pl.program_id(0) * nq + i) * FOX_HEADS
        ce_base = pl.program_id(0) * nk * FOX_HEADS
        c_first = [cs_ref[cs_base + slot + hh] for hh in range(2)]
        skip_lead = jnp.maximum(c_first[0], c_first[1]) < cutoff
        j0 = jnp.int32(0)
        for j in range(nk - 2):
            gap = jnp.maximum(c_first[0] - ce_ref[ce_base + j * FOX_HEADS + slot],
                              c_first[1] - ce_ref[ce_base + j * FOX_HEADS + slot + 1])
            j0 += jnp.logical_and(gap < cutoff, j < n_full).astype(jnp.int32)

        e0 = exponents(*keys(j0))
        e_ref[0] = e0[0]
        e_ref[1] = e0[1]

        @pl.when(jnp.logical_not(skip_lead))
        def _():
            accumulate(exponents(kl_ref[...], cl_ref[...]), vl_ref[...], lead_ok)

        def step(j, carry):
            e_cur = [e_ref[0], e_ref[1]]
            e_next = exponents(*keys(j + 1))
            accumulate(e_cur, values(j), None)
            e_ref[0] = e_next[0]
            e_ref[1] = e_next[1]
            return carry

        def step_pair(m, carry):
            return step(2 * m + 1, step(2 * m, carry))

        @pl.when(j0 % 2 == 1)
        def _():
            step(j0, 0)

        lax.fori_loop((j0 + 1) // 2, i, step_pair, 0)

        kb, ckb = keys(n_full + 1)
        vb = values(n_full + 1)
        e_b1 = exponents(kb[:tr], ckb[:, :tr], 2 * tr, tq)
        e_b2 = exponents(kb[tr:], ckb[:, tr:], 3 * tr, tq)
        va = values(n_full)
        e_a = [e_ref[0], e_ref[1]]
        strips = [
            [([x[:tr, :tr] for x in e_a], va[:tr], corner)],
            [([x[tr:2 * tr] for x in e_a], va, causal[tr:2 * tr])],
            [([x[2 * tr:3 * tr] for x in e_a], va, None), ([x[:tr] for x in e_b1], vb[:tr], corner)],
            [([x[3 * tr:] for x in e_a], va, None), ([x[tr:] for x in e_b1], vb[:tr], None),
             (e_b2, vb[tr:], corner)],
        ]
        for s, parts in enumerate(strips):
            pv = [None, None]
            for e, v, allowed in parts:
                if allowed is not None:
                    e = [jnp.where(allowed, x, NEG) for x in e]
                for hh in range(2):
                    d = _dot(jnp.exp2(e[hh]).astype(BF16), v)
                    pv[hh] = d if pv[hh] is None else pv[hh] + d
            acc_ref[s * tr:(s + 1) * tr, :] += jnp.where(first2, pv[0], pv[1])

    @pl.when(jnp.logical_not(use_bound))
    def _():
        m_ref[...] = jnp.full_like(m_ref, NEG)

        def block(k, v, ck, allowed):
            pvs, alphas = [], []
            for hh in range(2):
                s = _dot_nt(q_heads[hh], k) - ck[hh:hh + 1, :]
                if allowed is not None:
                    s = jnp.where(allowed, s, NEG)
                m_old = m_ref[hh]
                m_new = jnp.maximum(m_old, jnp.max(s, axis=1, keepdims=True))
                m_ref[hh] = m_new
                pvs.append(_dot(jnp.exp2(s - m_new).astype(BF16), v))
                alphas.append(jnp.exp2(m_old - m_new))
            acc_ref[...] = (acc_ref[...] * jnp.where(first2, alphas[0], alphas[1])
                            + jnp.where(first2, pvs[0], pvs[1]))

        block(kl_ref[...], vl_ref[...], cl_ref[...], lead_ok)

        def full_block(j, carry):
            block(keys(j)[0], values(j), keys(j)[1], None)
            return carry

        lax.fori_loop(0, n_full, full_block, 0)
        block(keys(n_full)[0], values(n_full), keys(n_full)[1], causal)
        block(keys(n_full + 1)[0], values(n_full + 1), keys(n_full + 1)[1], col + tk <= row)

    acc = acc_ref[...]
    o_ref[rows, :] = (acc[:, :V7X_LANES] / acc[:, V7X_LANES:]).astype(o_ref.dtype)


def _fox(stab, cs, ce, q, cq, k, v, ck, kl, vl, cl, weights, short_weight, batch, seq):
    tq = ATT_BLOCK
    w = V7X_LANES
    steps = batch * FOX_PAIRS
    cast_specs, cast_shapes = [], []
    for a in weights:
        slab = a.shape[0] // steps
        assert slab * steps == a.shape[0] and slab % BF16_SUBLANES == 0
        cast_specs.append(pl.BlockSpec((slab, a.shape[1]), lambda b, j: (b * FOX_PAIRS + j, 0)))
        cast_shapes.append(jax.ShapeDtypeStruct(a.shape, BF16))
    short_steps = short_weight.shape[0] // FF_CHUNK
    assert short_steps * FF_CHUNK == short_weight.shape[0] and short_steps <= steps
    cast_specs.append(pl.BlockSpec(
        (FF_CHUNK, short_weight.shape[1]),
        lambda b, j: (jnp.minimum(b * FOX_PAIRS + j, short_steps - 1), 0)))
    cast_shapes.append(jax.ShapeDtypeStruct(short_weight.shape, BF16))
    n_cast = len(cast_specs)
    return pl.pallas_call(
        functools.partial(_fox_body, n_cast=n_cast, n_cast_short=1, short_steps=short_steps),
        grid=(batch, FOX_PAIRS),
        in_specs=[
            pl.BlockSpec(memory_space=pltpu.SMEM),
            pl.BlockSpec(memory_space=pltpu.SMEM),
            pl.BlockSpec(memory_space=pltpu.SMEM),
            pl.BlockSpec((None, seq, w), lambda b, j: (b, 0, j)),
            pl.BlockSpec((None, seq, FOX_HEADS), lambda b, j: (b, 0, 0)),
            pl.BlockSpec((None, seq, w), lambda b, j: (b, 0, j)),
            pl.BlockSpec((None, seq, 2 * w), lambda b, j: (b, 0, j)),
            pl.BlockSpec((None, None, 2, seq), lambda b, j: (b, j, 0, 0)),
            pl.BlockSpec((BLOCK, w), lambda b, j: (0, j)),
            pl.BlockSpec((BLOCK, 2 * w), lambda b, j: (0, j)),
            pl.BlockSpec((None, 2, BLOCK), lambda b, j: (j, 0, 0)),
        ] + cast_specs,
        out_specs=[pl.BlockSpec((None, seq, w), lambda b, j: (b, 0, j))] + cast_specs,
        out_shape=[jax.ShapeDtypeStruct((batch, seq, D_MODEL), BF16)] + cast_shapes,
        scratch_shapes=[pltpu.VMEM((2, tq, 1), F32), pltpu.VMEM((tq, 2 * w), F32),
                        pltpu.VMEM((2, tq, tq // 2), F32)],
        compiler_params=_params(("arbitrary", "arbitrary")),
        name="fox",
    )(stab, cs, ce, q, cq, k, v, ck, kl, vl, cl, *weights, short_weight)


def _ret_body(q_ref, k_ref, v_ref, sg_ref, kl_ref, vl_ref, gn_ref, cd_ref, o_ref, kv_ref, state_ref):
    n_chunks = q_ref.shape[0] // BLOCK
    cd = cd_ref[...]
    gn = gn_ref[...]
    causal = (lax.broadcasted_iota(jnp.int32, (BLOCK, BLOCK), 1)
              <= lax.broadcasted_iota(jnp.int32, (BLOCK, BLOCK), 0))

    def rows(c):
        return pl.ds(pl.multiple_of(c * BLOCK, BLOCK), BLOCK)

    def summarise(c, carry):
        kv_ref[c] = _dot_tn(k_ref[rows(c), :], v_ref[rows(c), :])
        return carry

    lax.fori_loop(0, n_chunks, summarise, 0, unroll=RET_UNROLL)

    def advance(c, state):
        state_ref[c] = state.astype(BF16)
        return cd * (state + kv_ref[c])

    lead_state = cd * _dot_tn(kl_ref[...], vl_ref[...])
    lax.fori_loop(0, n_chunks, advance, lead_state, unroll=RET_UNROLL)

    def emit(c, carry):
        q = q_ref[rows(c), :]
        v = v_ref[rows(c), :]
        scores = jnp.where(causal, _dot_nt(q, k_ref[rows(c), :]), 0.0).astype(BF16)
        o = _dot(jnp.concatenate([scores, q], axis=1), jnp.concatenate([v, state_ref[c]], axis=0))
        mu = jnp.mean(o, axis=-1, keepdims=True)
        d = o - mu
        var = jnp.mean(d * d, axis=-1, keepdims=True)
        yn = d * lax.rsqrt(var + GN_EPS) * gn
        o_ref[rows(c), :] = (sg_ref[rows(c), :].astype(F32) * yn).astype(o_ref.dtype)
        return carry

    lax.fori_loop(0, n_chunks, emit, 0, unroll=RET_UNROLL)


def _ret(rq, rk, rv, sg, rkl, rvl, gn, cd, batch, seq):
    head_qk = pl.BlockSpec((None, seq, RET_DK), lambda b, h: (b, 0, h))
    head_v = pl.BlockSpec((None, seq, RET_DV), lambda b, h: (b, 0, h))
    return pl.pallas_call(
        _ret_body,
        grid=(batch, RET_HEADS),
        in_specs=[head_qk, head_qk, head_v, head_v,
                  pl.BlockSpec((BLOCK, RET_DK), lambda b, h: (0, h)),
                  pl.BlockSpec((BLOCK, RET_DV), lambda b, h: (0, h)),
                  pl.BlockSpec((1, RET_DV), lambda b, h: (0, h)),
                  pl.BlockSpec((None, 1, RET_DV), lambda b, h: (h, 0, 0))],
        out_specs=head_v,
        out_shape=jax.ShapeDtypeStruct((batch, seq, D_MODEL), BF16),
        scratch_shapes=[pltpu.VMEM((seq // BLOCK, RET_DK, RET_DV), F32),
                        pltpu.VMEM((seq // BLOCK, RET_DK, RET_DV), BF16)],
        compiler_params=_params(("arbitrary", "arbitrary")),
        name="ret",
    )(rq, rk, rv, sg, rkl, rvl, gn, cd)


def _out_body(ya_ref, yb_ref, ga_ref, gb_ref, h_ref, wa_ref, wb_ref, wo_ref, g_ref, win_ref,
              wout_ref, o_ref):
    mixed = (ga_ref[...].astype(F32) * _dot(ya_ref[...], wa_ref[...])
             + gb_ref[...].astype(F32) * _dot(yb_ref[...], wb_ref[...]))
    h = h_ref[...] + _dot(mixed.astype(BF16), wo_ref[...])
    xn = _rms_rows(h, g_ref[...]).astype(BF16)
    o_ref[...] = h + 0.5 * _swiglu(xn, win_ref, wout_ref)


def _out(ya, yb, ga, gb, h, consts, tm):
    rows = h.shape[0]
    row_spec = pl.BlockSpec((tm, D_MODEL), lambda i: (i, 0))
    return pl.pallas_call(
        _out_body,
        grid=(rows // tm,),
        in_specs=[row_spec] * 5 + [_const_spec(a.shape) for a in consts],
        out_specs=row_spec,
        out_shape=jax.ShapeDtypeStruct((rows, D_MODEL), F32),
        compiler_params=_params(("arbitrary",)),
        name="out",
    )(ya, yb, ga, gb, h, *consts)


def _ffn_weights(w_in, w_out, dtype):
    return w_in.astype(dtype), w_out.astype(dtype).reshape(N_FF_CHUNKS, FF_CHUNK, D_MODEL)


def _position_tables(seq):
    half = RET_DK // 2
    pos = np.arange(BLOCK + seq, dtype=np.float64) - N_EMPTY
    inv = ROPE_BASE ** (-np.arange(half, dtype=np.float64) / half)
    ang = pos[:, None] * inv[None, :]
    cos = np.concatenate([np.cos(ang), np.cos(ang)], axis=1)
    sin = np.concatenate([-np.sin(ang), np.sin(ang)], axis=1)
    return jnp.asarray(cos, dtype=F32), jnp.asarray(sin, dtype=F32)


def _retention_tables(rows):
    log_gamma = np.log1p(-np.exp2(-5.0 - np.arange(RET_HEADS, dtype=np.float64)))
    n = (np.arange(rows) % BLOCK + 1.0)[None, :, None]
    lg = log_gamma[:, None, None]
    shape = (RET_HEADS, rows, RET_DK)
    q_scale = np.broadcast_to(np.exp(lg * n), shape)
    k_scale = np.broadcast_to(np.exp(-lg * n) * RET_DK ** -0.5, shape)
    cd = np.broadcast_to(np.exp(log_gamma * BLOCK)[:, None, None], (RET_HEADS, 1, RET_DV))
    return tuple(jnp.asarray(t, dtype=F32) for t in (q_scale, k_scale, cd))


def kernel(x, meta_tokens, norm_ffn1, w_ffn1_in, w_ffn1_out, norm_mix, w_in, b_forget, b_gate,
           fox_q_norm, fox_k_norm, w_o_fox, ret_gn, w_o_ret, w_out, norm_ffn2, w_ffn2_in,
           w_ffn2_out):
    batch, seq, d = x.shape
    assert d == D_MODEL and seq % ATT_BLOCK == 0 and seq % ROW_TILE == 0
    assert norm_ffn1.shape[0] == 1, "one layer"
    tiles_per_seq = seq // ROW_TILE

    win1, wout1 = _ffn_weights(w_ffn1_in[0], w_ffn1_out[0], F32)
    grp = np.arange(V7X_MXU_DIM) // FOX_HD
    gmat = jnp.asarray((grp[:, None] == grp[None, :]) / FOX_HD, dtype=BF16)
    gq = jnp.tile(fox_q_norm[0], FOX_HEADS)[None, :]
    gk = jnp.tile(fox_k_norm[0], FOX_HEADS)[None, :]
    bf = jnp.pad(b_forget[0], (0, V7X_LANES - FOX_HEADS))[None, :]
    bg = b_gate[0][None, :]
    cos, sin = _position_tables(seq)
    idx = np.arange(ROW_TILE)
    tri_incl = jnp.asarray(idx[None, :] <= idx[:, None], dtype=BF16)
    idx = np.arange(BLOCK)
    tri_after = jnp.asarray(idx[None, :] > idx[:, None], dtype=BF16)

    lead = jnp.concatenate([jnp.zeros((N_EMPTY, d), x.dtype), meta_tokens.astype(x.dtype)], axis=0)
    xr = x.reshape(batch * seq, d)

    assert w_in.shape[2] == COL_END
    h1, h1l, wt = _ffn_and_cast(xr, lead, norm_ffn1, win1, wout1, w_in[0].T, ROW_TILE)
    q_scale, k_scale, chunk_decay = _retention_tables(ROW_TILE)
    proj_consts = (norm_mix, wt, gmat, gq, gk, bf, bg, q_scale, k_scale)
    q, k, v, c, rq, rk, rv, sg, ga, gb, kl, vl, cl, rkl, rvl = _proj(
        h1, h1l, proj_consts, cos, sin, tri_incl, tri_after, ROW_TILE, tiles_per_seq)

    def b3(a):
        return a.reshape(batch, seq, a.shape[-1])

    ck = b3(c).transpose(0, 2, 1).reshape(batch, FOX_PAIRS, 2, seq)
    clt = cl.T.reshape(FOX_PAIRS, 2, BLOCK)
    bound = FOX_HD ** 0.5 * jnp.max(jnp.abs(fox_q_norm[0])) * jnp.max(jnp.abs(fox_k_norm[0]))
    bound2 = bound * LOG2E
    stab = jnp.stack([(bound <= SAFE_LOGIT_BOUND).astype(F32), bound2, F32_EXP2_ZERO + 2.0 * bound2])
    c3 = b3(c)
    cs = c3[:, ::ATT_BLOCK, :].reshape(-1)
    ce = c3[:, ATT_BLOCK // 2 - 1::ATT_BLOCK // 2, :].reshape(-1)
    ya, win2, wof, wor, wo, wout2 = _fox(
        stab, cs, ce, b3(q), c3, b3(k), b3(v), ck, kl, vl, clt,
        (w_ffn2_in[0], w_o_fox[0], w_o_ret[0], w_out[0]), w_ffn2_out[0], batch, seq)
    yb = _ret(b3(rq), b3(rk), b3(rv), b3(sg), rkl, rvl, ret_gn, chunk_decay, batch, seq)

    out_consts = (wof, wor, wo, norm_ffn2, win2, wout2.reshape(N_FF_CHUNKS, FF_CHUNK, D_MODEL))
    out = _out(ya.reshape(batch * seq, d), yb.reshape(batch * seq, d), ga, gb, h1, out_consts,
               ROW_TILE)
    return out.reshape(batch, seq, d)
```

```python
import functools
import math

import jax
import jax.numpy as jnp
import numpy as np
from jax import lax
from jax.experimental import pallas as pl
from jax.experimental.pallas import tpu as pltpu

F32 = jnp.float32
BF16 = jnp.bfloat16

D_MODEL = 1024
N_META = 16
BLOCK = 128
N_EMPTY = BLOCK - N_META
FOX_HD = 64
FOX_HEADS = D_MODEL // FOX_HD
FOX_PAIRS = FOX_HEADS // 2
RET_HEADS = 4
RET_DK = D_MODEL // (2 * RET_HEADS)
RET_DV = 2 * RET_DK
RET_QK = RET_HEADS * RET_DK
D_FF = ((8 * D_MODEL // 3 + 127) // 128) * 128
EPS = 1e-6
GN_EPS = 1e-5
ROPE_BASE = 10000.0
NEG = -1e30
LOG2E = math.log2(math.e)
SAFE_LOGIT_BOUND = 30.0
F32_EXP2_ZERO = 151.0

V7X_LANES = 128
BF16_SUBLANES = 16
V7X_MXU_DIM = 256
V7X_VMEM_BYTES = 64 * 2**20
VMEM_COMPILER_RESERVE = 8 * 2**20
VMEM_LIMIT = V7X_VMEM_BYTES - VMEM_COMPILER_RESERVE

ROW_TILE = 512
FF_CHUNK = V7X_MXU_DIM
N_FF_CHUNKS = D_FF // FF_CHUNK
RET_UNROLL = 32
ATT_BLOCK = 1024


def _dot(a, b):
    return jnp.dot(a, b, preferred_element_type=F32)


def _dot_nt(a, b):
    return lax.dot_general(a, b, (((1,), (1,)), ((), ())), preferred_element_type=F32)


def _dot_tn(a, b):
    return lax.dot_general(a, b, (((0,), (0,)), ((), ())), preferred_element_type=F32)


def _const_spec(shape, block=None):
    zeros = (0,) * len(shape)
    return pl.BlockSpec(block or shape, lambda *_: zeros, pipeline_mode=pl.Buffered(1))


def _params(semantics):
    return pltpu.CompilerParams(dimension_semantics=semantics, vmem_limit_bytes=VMEM_LIMIT)


def _rms_rows(x, g):
    ms = jnp.mean(x * x, axis=-1, keepdims=True)
    return x * lax.rsqrt(ms + EPS) * g


def _swiglu(xn, win_ref, wout_ref):
    acc = None
    for c in range(N_FF_CHUNKS):
        a = _dot(xn, win_ref[:, c * FF_CHUNK:(c + 1) * FF_CHUNK].astype(BF16))
        b = _dot(xn, win_ref[:, D_FF + c * FF_CHUNK:D_FF + (c + 1) * FF_CHUNK].astype(BF16))
        hm = (a * jax.nn.sigmoid(a) * b).astype(BF16)
        d = _dot(hm, wout_ref[c].astype(BF16))
        acc = d if acc is None else acc + d
    return acc


def _ffn_body(x_ref, g_ref, win_ref, wout_ref, o_ref):
    x = x_ref[...]
    xn = _rms_rows(x, g_ref[...]).astype(BF16)
    o_ref[...] = x + 0.5 * _swiglu(xn, win_ref, wout_ref)


def _ffn_cast_body(x_ref, lead_ref, g_ref, win_ref, wout_ref, w_src, o_ref, ol_ref, w_dst, *, n_tiles):
    i = pl.program_id(0)

    @pl.when(i < n_tiles)
    def _():
        _ffn_body(x_ref, g_ref, win_ref, wout_ref, o_ref)
        w_dst[...] = w_src[...].astype(BF16)

    @pl.when(i == n_tiles)
    def _():
        _ffn_body(lead_ref, g_ref, win_ref, wout_ref, ol_ref)


def _ffn_and_cast(x, lead, g, win, wout, w_t, tm):
    rows = x.shape[0]
    n_tiles = rows // tm
    n = w_t.shape[0]
    slab = min(s for s in range(BF16_SUBLANES, n + 1, BF16_SUBLANES) if n % s == 0 and n // s <= n_tiles)
    cast_steps = n // slab
    row_spec = pl.BlockSpec((tm, D_MODEL), lambda i: (jnp.minimum(i, n_tiles - 1), 0))
    lead_spec = pl.BlockSpec(lead.shape, lambda i: (0, 0))
    slab_spec = pl.BlockSpec((slab, w_t.shape[1]), lambda i: (jnp.minimum(i, cast_steps - 1), 0))
    return pl.pallas_call(
        functools.partial(_ffn_cast_body, n_tiles=n_tiles),
        grid=(n_tiles + 1,),
        in_specs=[row_spec, lead_spec, _const_spec(g.shape), _const_spec(win.shape),
                  _const_spec(wout.shape), slab_spec],
        out_specs=[row_spec, lead_spec, slab_spec],
        out_shape=[jax.ShapeDtypeStruct((rows, D_MODEL), F32), jax.ShapeDtypeStruct(lead.shape, F32),
                   jax.ShapeDtypeStruct(w_t.shape, BF16)],
        compiler_params=_params(("arbitrary",)),
        name="ffn_cast",
    )(x, lead, g, win, wout, w_t)


def _group_mean_sq(x, gmat_ref):
    sq = (x * x).astype(BF16)
    w = V7X_MXU_DIM
    parts = [_dot(sq[:, i * w:(i + 1) * w], gmat_ref[...]) for i in range(D_MODEL // w)]
    return jnp.concatenate(parts, axis=1)


def _rotary_tile(x, cos, sin_signed):
    return x * cos + pltpu.roll(x, RET_DK // 2, axis=1) * sin_signed


COL_FQ = 0
COL_FK = COL_FQ + D_MODEL
COL_FV = COL_FK + D_MODEL
COL_FF = COL_FV + D_MODEL
COL_RQ = COL_FF + FOX_HEADS
COL_RK = COL_RQ + RET_QK
COL_RV = COL_RK + RET_QK
COL_RG = COL_RV + D_MODEL
COL_GA = COL_RG + D_MODEL
COL_GB = COL_GA + D_MODEL
COL_END = COL_GB + D_MODEL


def _proj_rows(h, cos, sin, tri, q_scale, k_scale, gmix_ref, wt_ref, gmat_ref, gq_ref, gk_ref, bf_ref,
               bg_ref, carry_ref, outs, *, lead):
    q_ref, k_ref, v_ref, c_ref, rq_ref, rk_ref, rv_ref, sg_ref, ga_ref, gb_ref = outs
    tm = h.shape[0]
    u = _rms_rows(h, gmix_ref[...]).astype(BF16)

    def project(c0, c1):
        return _dot_nt(u, wt_ref[c0:c1, :])

    if lead:
        row = lax.broadcasted_iota(jnp.int32, (tm, 1), 0)
        valid = row >= N_EMPTY
        vmask = valid.astype(F32)


    z = project(COL_FF, COL_FF + V7X_LANES) + bf_ref[...]
    logf = -(jnp.maximum(-z, 0.0) + jnp.log1p(jnp.exp(-jnp.abs(z))))
    if lead:
        logf = jnp.where(valid, logf, 0.0)
    lane = lax.broadcasted_iota(jnp.int32, (1, V7X_LANES), 1)
    logf = jnp.where(lane < FOX_HEADS, logf, 0.0)
    p1 = logf.astype(BF16).astype(F32)
    r1 = logf - p1
    p2 = r1.astype(BF16).astype(F32)
    p3 = (r1 - p2).astype(BF16).astype(F32)
    packed = (p1 + pltpu.roll(p2, FOX_HEADS, axis=1) + pltpu.roll(p3, 2 * FOX_HEADS, axis=1)).astype(BF16)

    fq = project(COL_FQ, COL_FK) if q_ref is not None else None
    fk = project(COL_FK, COL_FV)
    rq = project(COL_RQ, COL_RK) if rq_ref is not None else None
    rk = project(COL_RK, COL_RV)

    if q_ref is not None:
        qn = fq * lax.rsqrt(_group_mean_sq(fq, gmat_ref) + EPS) * gq_ref[...]
        q_ref[...] = (qn * (FOX_HD ** -0.5 * LOG2E)).astype(BF16)
    kn = fk * lax.rsqrt(_group_mean_sq(fk, gmat_ref) + EPS) * gk_ref[...]
    k_ref[...] = kn.astype(BF16)

    r = _dot(tri, packed)
    cum = r + (pltpu.roll(r, V7X_LANES - FOX_HEADS, axis=1)
               + pltpu.roll(r, V7X_LANES - 2 * FOX_HEADS, axis=1))
    cum = jnp.where(lane < FOX_HEADS, cum, 0.0)
    if lead:
        c = -cum
    else:
        c = cum + carry_ref[...]
        carry_ref[...] = c[tm - 1:tm, :]
    c_ref[...] = (c * LOG2E)[:, :FOX_HEADS]

    for hh in range(RET_HEADS):
        sl = slice(hh * RET_DK, (hh + 1) * RET_DK)
        if rq_ref is not None:
            rq_ref[:, sl] = (_rotary_tile(rq[:, sl], cos, sin) * q_scale(hh)).astype(BF16)
        kt = _rotary_tile(rk[:, sl], cos, sin) * k_scale(hh)
        if lead:
            kt = kt * vmask
        rk_ref[:, sl] = kt.astype(BF16)
    rv = project(COL_RV, COL_RG)
    if lead:
        rv = rv * vmask
    rv_ref[...] = rv.astype(BF16)
    if sg_ref is not None:
        rg = project(COL_RG, COL_GA)
        sg_ref[...] = (rg * jax.nn.sigmoid(rg)).astype(BF16)

    if ga_ref is not None:
        bg = bg_ref[...]
        ga_ref[...] = jax.nn.sigmoid(project(COL_GA, COL_GB) + bg[:, :D_MODEL]).astype(BF16)
        gb_ref[...] = jax.nn.sigmoid(project(COL_GB, COL_END) + bg[:, D_MODEL:]).astype(BF16)

    fv = project(COL_FV, COL_FF).astype(BF16)
    ones = jnp.ones((tm, V7X_LANES), BF16)
    for j in range(FOX_PAIRS):
        v_ref[:, 2 * j * V7X_LANES:(2 * j + 1) * V7X_LANES] = fv[:, j * V7X_LANES:(j + 1) * V7X_LANES]
        v_ref[:, (2 * j + 1) * V7X_LANES:(2 * j + 2) * V7X_LANES] = ones


def _proj_body(h_ref, hl_ref, gmix_ref, wt_ref, gmat_ref, gq_ref, gk_ref, bf_ref, bg_ref, qs_ref, ks_ref,
               cos_ref, sin_ref, cosl_ref, sinl_ref, tri_ref, tril_ref,
               q_ref, k_ref, v_ref, c_ref, rq_ref, rk_ref, rv_ref, sg_ref, ga_ref, gb_ref,
               kl_ref, vl_ref, cl_ref, rkl_ref, rvl_ref, carry_ref, *, n_tiles, tiles_per_seq):
    i = pl.program_id(0)
    consts = (gmix_ref, wt_ref, gmat_ref, gq_ref, gk_ref, bf_ref, bg_ref, carry_ref)

    @pl.when(i % tiles_per_seq == 0)
    def _():
        carry_ref[...] = jnp.zeros_like(carry_ref)

    @pl.when(i < n_tiles)
    def _():
        outs = (q_ref, k_ref, v_ref, c_ref, rq_ref, rk_ref, rv_ref, sg_ref, ga_ref, gb_ref)
        _proj_rows(h_ref[...], cos_ref[...], sin_ref[...], tri_ref[...], lambda hh: qs_ref[hh],
                   lambda hh: ks_ref[hh], *consts, outs, lead=False)

    @pl.when(i == n_tiles)
    def _():
        outs = (None, kl_ref, vl_ref, cl_ref, None, rkl_ref, rvl_ref, None, None, None)
        _proj_rows(hl_ref[...], cosl_ref[...], sinl_ref[...], tril_ref[...], lambda hh: qs_ref[hh, :BLOCK],
                   lambda hh: ks_ref[hh, :BLOCK], *consts, outs, lead=True)


def _proj(h, h_lead, consts, cos, sin, tri, tri_lead, tm, tiles_per_seq):
    rows = h.shape[0]
    n_tiles = rows // tm
    assert tm % BLOCK == 0

    def rows_spec(width):
        return pl.BlockSpec((tm, width), lambda i: (jnp.minimum(i, n_tiles - 1), 0))

    def lead_spec(width):
        return pl.BlockSpec((BLOCK, width), lambda i: (0, 0))

    pos_spec = pl.BlockSpec((tm, RET_DK), lambda i: (i % tiles_per_seq, 0))
    out_widths = [D_MODEL, D_MODEL, 2 * D_MODEL, FOX_HEADS, RET_QK, RET_QK, D_MODEL, D_MODEL,
                  D_MODEL, D_MODEL]
    out_dtypes = [BF16, BF16, BF16, F32, BF16, BF16, BF16, BF16, BF16, BF16]
    lead_outs = [1, 2, 3, 5, 6]
    return pl.pallas_call(
        functools.partial(_proj_body, n_tiles=n_tiles, tiles_per_seq=tiles_per_seq),
        grid=(n_tiles + 1,),
        in_specs=[rows_spec(D_MODEL), lead_spec(D_MODEL)] + [_const_spec(a.shape) for a in consts]
        + [pos_spec, pos_spec, lead_spec(RET_DK), lead_spec(RET_DK), _const_spec(tri.shape),
           _const_spec(tri_lead.shape)],
        out_specs=[rows_spec(w) for w in out_widths] + [lead_spec(out_widths[j]) for j in lead_outs],
        out_shape=[jax.ShapeDtypeStruct((rows, w), dt) for w, dt in zip(out_widths, out_dtypes)]
        + [jax.ShapeDtypeStruct((BLOCK, out_widths[j]), out_dtypes[j]) for j in lead_outs],
        scratch_shapes=[pltpu.VMEM((1, V7X_LANES), F32)],
        compiler_params=_params(("arbitrary",)),
        name="proj",
    )(h, h_lead, *consts, cos[BLOCK:], sin[BLOCK:], cos, sin, tri, tri_lead)


def _fox_body(stab_ref, cs_ref, ce_ref, q_ref, cq_ref, k_ref, v_ref, ck_ref, kl_ref, vl_ref, cl_ref,
              *rest, n_cast, n_cast_short, short_steps):
    srcs, (o_ref, *dsts), (m_ref, acc_ref, e_ref) = (rest[:n_cast], rest[n_cast:2 * n_cast + 1],
                                                     rest[2 * n_cast + 1:])
    step = pl.program_id(0) * pl.num_programs(1) + pl.program_id(1)
    for src, dst in zip(srcs[:n_cast - n_cast_short], dsts):
        dst[...] = src[...].astype(BF16)

    @pl.when(step < short_steps)
    def _():
        for src, dst in zip(srcs[n_cast - n_cast_short:], dsts[n_cast - n_cast_short:]):
            dst[...] = src[...].astype(BF16)

    _fox_attend(stab_ref, cs_ref, ce_ref, q_ref, cq_ref, k_ref, v_ref, ck_ref, kl_ref, vl_ref, cl_ref,
                o_ref, m_ref, acc_ref, e_ref)


def _fox_attend(stab_ref, cs_ref, ce_ref, q_ref, cq_ref, k_ref, v_ref, ck_ref, kl_ref, vl_ref, cl_ref,
                o_ref, m_ref, acc_ref, e_ref):
    tq = m_ref.shape[1]
    tk = tq // 2
    tr = tq // 4
    lane = lax.broadcasted_iota(jnp.int32, (1, 2 * V7X_LANES), 1)
    first2 = (lane % V7X_LANES) < FOX_HD
    first = first2[:, :V7X_LANES]
    lead_ok = lax.broadcasted_iota(jnp.int32, (tq, BLOCK), 1) >= N_EMPTY
    row = lax.broadcasted_iota(jnp.int32, (tq, tk), 0)
    col = lax.broadcasted_iota(jnp.int32, (tq, tk), 1)
    causal = col <= row
    corner = causal[:tr, :tr]
    use_bound = stab_ref[0] > 0.5

    def keys(j):
        ks = pl.multiple_of(j * tk, tk)
        return k_ref[pl.ds(ks, tk), :], ck_ref[:, pl.ds(ks, tk)]

    def values(j):
        return v_ref[pl.ds(pl.multiple_of(j * tk, tk), tk), :]

    nq = q_ref.shape[0] // tq
    nk = k_ref.shape[0] // tk

    def query_block(i, carry):
        _fox_query_block(i, tq, tk, tr, nq, nk, first, first2, lead_ok, row, col, causal, corner,
                         use_bound, keys, values, stab_ref, cs_ref, ce_ref, q_ref, cq_ref, kl_ref,
                         vl_ref, cl_ref, o_ref, m_ref, acc_ref, e_ref)
        return carry

    lax.fori_loop(0, nq, query_block, 0)


def _fox_query_block(i, tq, tk, tr, nq, nk, first, first2, lead_ok, row, col, causal, corner,
                     use_bound, keys, values, stab_ref, cs_ref, ce_ref, q_ref, cq_ref, kl_ref,
                     vl_ref, cl_ref, o_ref, m_ref, acc_ref, e_ref):
    rows = pl.ds(pl.multiple_of(i * tq, tq), tq)
    n_full = 2 * i
    q = q_ref[rows, :]
    zero = jnp.zeros_like(q)
    q_heads = (jnp.where(first, q, zero), jnp.where(first, zero, q))
    acc_ref[...] = jnp.zeros_like(acc_ref)

    @pl.when(use_bound)
    def _():
        head = lax.broadcasted_iota(jnp.int32, (1, FOX_HEADS), 1) - 2 * pl.program_id(1)
        cq_blk = cq_ref[rows, :]
        cq = [jnp.sum(jnp.where(head == hh, cq_blk, 0.0), axis=1, keepdims=True) - stab_ref[1]
              for hh in range(2)]

        def exponents(k, ck, r0=0, r1=tq):
            return [_dot_nt(q_heads[hh][r0:r1], k) + cq[hh][r0:r1] - ck[hh:hh + 1, :] for hh in range(2)]

        def accumulate(e, v, allowed, r0=0):
            if allowed is not None:
                e = [jnp.where(allowed, x, NEG) for x in e]
            pvs = [_dot(jnp.exp2(x).astype(BF16), v) for x in e]
            r1 = r0 + e[0].shape[0]
            acc_ref[r0:r1, :] += jnp.where(first2, pvs[0], pvs[1])

        cutoff = -stab_ref[2]
        slot = 2 * pl.program_id(1)
        cs_base = (pl.program_id(0) * nq + i) * FOX_HEADS
        ce_base = pl.program_id(0) * nk * FOX_HEADS
        c_first = [cs_ref[cs_base + slot + hh] for hh in range(2)]
        skip_lead = jnp.maximum(c_first[0], c_first[1]) < cutoff
        j0 = jnp.int32(0)
        for j in range(nk - 2):
            gap = jnp.maximum(c_first[0] - ce_ref[ce_base + j * FOX_HEADS + slot],
                              c_first[1] - ce_ref[ce_base + j * FOX_HEADS + slot + 1])
            j0 += jnp.logical_and(gap < cutoff, j < n_full).astype(jnp.int32)

        e0 = exponents(*keys(j0))
        e_ref[0] = e0[0]
        e_ref[1] = e0[1]

        @pl.when(jnp.logical_not(skip_lead))
        def _():
            accumulate(exponents(kl_ref[...], cl_ref[...]), vl_ref[...], lead_ok)

        def step(j, carry):
            e_cur = [e_ref[0], e_ref[1]]
            e_next = exponents(*keys(j + 1))
            accumulate(e_cur, values(j), None)
            e_ref[0] = e_next[0]
            e_ref[1] = e_next[1]
            return carry

        def step_pair(m, carry):
            return step(2 * m + 1, step(2 * m, carry))

        @pl.when(j0 % 2 == 1)
        def _():
            step(j0, 0)

        first_pair = (j0 + 1) // 2

        @pl.when((i - first_pair) % 2 == 1)
        def _():
            step_pair(first_pair, 0)

        first_quad_pair = first_pair + (i - first_pair) % 2

        def step_quad(t, carry):
            m = first_quad_pair + 2 * t
            return step_pair(m + 1, step_pair(m, carry))

        lax.fori_loop(0, (i - first_quad_pair) // 2, step_quad, 0)

        kb, ckb = keys(n_full + 1)
        vb = values(n_full + 1)
        e_b1 = exponents(kb[:tr], ckb[:, :tr], 2 * tr, tq)
        e_b2 = exponents(kb[tr:], ckb[:, tr:], 3 * tr, tq)
        va = values(n_full)
        e_a = [e_ref[0], e_ref[1]]
        strips = [
            [([x[:tr, :tr] for x in e_a], va[:tr], corner)],
            [([x[tr:2 * tr] for x in e_a], va, causal[tr:2 * tr])],
            [([x[2 * tr:3 * tr] for x in e_a], va, None), ([x[:tr] for x in e_b1], vb[:tr], corner)],
            [([x[3 * tr:] for x in e_a], va, None), ([x[tr:] for x in e_b1], vb[:tr], None),
             (e_b2, vb[tr:], corner)],
        ]
        for s, parts in enumerate(strips):
            pv = [None, None]
            for e, v, allowed in parts:
                if allowed is not None:
                    e = [jnp.where(allowed, x, NEG) for x in e]
                for hh in range(2):
                    d = _dot(jnp.exp2(e[hh]).astype(BF16), v)
                    pv[hh] = d if pv[hh] is None else pv[hh] + d
            acc_ref[s * tr:(s + 1) * tr, :] += jnp.where(first2, pv[0], pv[1])

    @pl.when(jnp.logical_not(use_bound))
    def _():
        m_ref[...] = jnp.full_like(m_ref, NEG)

        def block(k, v, ck, allowed):
            pvs, alphas = [], []
            for hh in range(2):
                s = _dot_nt(q_heads[hh], k) - ck[hh:hh + 1, :]
                if allowed is not None:
                    s = jnp.where(allowed, s, NEG)
                m_old = m_ref[hh]
                m_new = jnp.maximum(m_old, jnp.max(s, axis=1, keepdims=True))
                m_ref[hh] = m_new
                pvs.append(_dot(jnp.exp2(s - m_new).astype(BF16), v))
                alphas.append(jnp.exp2(m_old - m_new))
            acc_ref[...] = (acc_ref[...] * jnp.where(first2, alphas[0], alphas[1])
                            + jnp.where(first2, pvs[0], pvs[1]))

        block(kl_ref[...], vl_ref[...], cl_ref[...], lead_ok)

        def full_block(j, carry):
            block(keys(j)[0], values(j), keys(j)[1], None)
            return carry

        lax.fori_loop(0, n_full, full_block, 0)
        block(keys(n_full)[0], values(n_full), keys(n_full)[1], causal)
        block(keys(n_full + 1)[0], values(n_full + 1), keys(n_full + 1)[1], col + tk <= row)

    acc = acc_ref[...]
    o_ref[rows, :] = (acc[:, :V7X_LANES] / acc[:, V7X_LANES:]).astype(o_ref.dtype)


def _fox(stab, cs, ce, q, cq, k, v, ck, kl, vl, cl, weights, short_weight, batch, seq):
    tq = ATT_BLOCK
    w = V7X_LANES
    steps = batch * FOX_PAIRS
    cast_specs, cast_shapes = [], []
    for a in weights:
        slab = a.shape[0] // steps
        assert slab * steps == a.shape[0] and slab % BF16_SUBLANES == 0
        cast_specs.append(pl.BlockSpec((slab, a.shape[1]), lambda b, j: (b * FOX_PAIRS + j, 0)))
        cast_shapes.append(jax.ShapeDtypeStruct(a.shape, BF16))
    short_steps = short_weight.shape[0] // FF_CHUNK
    assert short_steps * FF_CHUNK == short_weight.shape[0] and short_steps <= steps
    cast_specs.append(pl.BlockSpec(
        (FF_CHUNK, short_weight.shape[1]),
        lambda b, j: (jnp.minimum(b * FOX_PAIRS + j, short_steps - 1), 0)))
    cast_shapes.append(jax.ShapeDtypeStruct(short_weight.shape, BF16))
    n_cast = len(cast_specs)
    return pl.pallas_call(
        functools.partial(_fox_body, n_cast=n_cast, n_cast_short=1, short_steps=short_steps),
        grid=(batch, FOX_PAIRS),
        in_specs=[
            pl.BlockSpec(memory_space=pltpu.SMEM),
            pl.BlockSpec(memory_space=pltpu.SMEM),
            pl.BlockSpec(memory_space=pltpu.SMEM),
            pl.BlockSpec((None, seq, w), lambda b, j: (b, 0, j)),
            pl.BlockSpec((None, seq, FOX_HEADS), lambda b, j: (b, 0, 0)),
            pl.BlockSpec((None, seq, w), lambda b, j: (b, 0, j)),
            pl.BlockSpec((None, seq, 2 * w), lambda b, j: (b, 0, j)),
            pl.BlockSpec((None, None, 2, seq), lambda b, j: (b, j, 0, 0)),
            pl.BlockSpec((BLOCK, w), lambda b, j: (0, j)),
            pl.BlockSpec((BLOCK, 2 * w), lambda b, j: (0, j)),
            pl.BlockSpec((None, 2, BLOCK), lambda b, j: (j, 0, 0)),
        ] + cast_specs,
        out_specs=[pl.BlockSpec((None, seq, w), lambda b, j: (b, 0, j))] + cast_specs,
        out_shape=[jax.ShapeDtypeStruct((batch, seq, D_MODEL), BF16)] + cast_shapes,
        scratch_shapes=[pltpu.VMEM((2, tq, 1), F32), pltpu.VMEM((tq, 2 * w), F32),
                        pltpu.VMEM((2, tq, tq // 2), F32)],
        compiler_params=_params(("arbitrary", "arbitrary")),
        name="fox",
    )(stab, cs, ce, q, cq, k, v, ck, kl, vl, cl, *weights, short_weight)


def _ret_body(q_ref, k_ref, v_ref, sg_ref, kl_ref, vl_ref, gn_ref, cd_ref, o_ref, kv_ref, state_ref):
    n_chunks = q_ref.shape[0] // BLOCK
    cd = cd_ref[...]
    gn = gn_ref[...]
    causal = (lax.broadcasted_iota(jnp.int32, (BLOCK, BLOCK), 1)
              <= lax.broadcasted_iota(jnp.int32, (BLOCK, BLOCK), 0))

    def rows(c):
        return pl.ds(pl.multiple_of(c * BLOCK, BLOCK), BLOCK)

    def summarise(c, carry):
        kv_ref[c] = _dot_tn(k_ref[rows(c), :], v_ref[rows(c), :])
        return carry

    lax.fori_loop(0, n_chunks, summarise, 0, unroll=RET_UNROLL)

    def advance(c, state):
        state_ref[c] = state.astype(BF16)
        return cd * (state + kv_ref[c])

    lead_state = cd * _dot_tn(kl_ref[...], vl_ref[...])
    lax.fori_loop(0, n_chunks, advance, lead_state, unroll=RET_UNROLL)

    def emit(c, carry):
        q = q_ref[rows(c), :]
        v = v_ref[rows(c), :]
        scores = jnp.where(causal, _dot_nt(q, k_ref[rows(c), :]), 0.0).astype(BF16)
        o = _dot(jnp.concatenate([scores, q], axis=1), jnp.concatenate([v, state_ref[c]], axis=0))
        mu = jnp.mean(o, axis=-1, keepdims=True)
        d = o - mu
        var = jnp.mean(d * d, axis=-1, keepdims=True)
        yn = d * lax.rsqrt(var + GN_EPS) * gn
        o_ref[rows(c), :] = (sg_ref[rows(c), :].astype(F32) * yn).astype(o_ref.dtype)
        return carry

    lax.fori_loop(0, n_chunks, emit, 0, unroll=RET_UNROLL)


def _ret(rq, rk, rv, sg, rkl, rvl, gn, cd, batch, seq):
    head_qk = pl.BlockSpec((None, seq, RET_DK), lambda b, h: (b, 0, h))
    head_v = pl.BlockSpec((None, seq, RET_DV), lambda b, h: (b, 0, h))
    return pl.pallas_call(
        _ret_body,
        grid=(batch, RET_HEADS),
        in_specs=[head_qk, head_qk, head_v, head_v,
                  pl.BlockSpec((BLOCK, RET_DK), lambda b, h: (0, h)),
                  pl.BlockSpec((BLOCK, RET_DV), lambda b, h: (0, h)),
                  pl.BlockSpec((1, RET_DV), lambda b, h: (0, h)),
                  pl.BlockSpec((None, 1, RET_DV), lambda b, h: (h, 0, 0))],
        out_specs=head_v,
        out_shape=jax.ShapeDtypeStruct((batch, seq, D_MODEL), BF16),
        scratch_shapes=[pltpu.VMEM((seq // BLOCK, RET_DK, RET_DV), F32),
                        pltpu.VMEM((seq // BLOCK, RET_DK, RET_DV), BF16)],
        compiler_params=_params(("arbitrary", "arbitrary")),
        name="ret",
    )(rq, rk, rv, sg, rkl, rvl, gn, cd)


def _out_body(ya_ref, yb_ref, ga_ref, gb_ref, h_ref, wa_ref, wb_ref, wo_ref, g_ref, win_ref,
              wout_ref, o_ref):
    mixed = (ga_ref[...].astype(F32) * _dot(ya_ref[...], wa_ref[...])
             + gb_ref[...].astype(F32) * _dot(yb_ref[...], wb_ref[...]))
    h = h_ref[...] + _dot(mixed.astype(BF16), wo_ref[...])
    xn = _rms_rows(h, g_ref[...]).astype(BF16)
    o_ref[...] = h + 0.5 * _swiglu(xn, win_ref, wout_ref)


def _out(ya, yb, ga, gb, h, consts, tm):
    rows = h.shape[0]
    row_spec = pl.BlockSpec((tm, D_MODEL), lambda i: (i, 0))
    return pl.pallas_call(
        _out_body,
        grid=(rows // tm,),
        in_specs=[row_spec] * 5 + [_const_spec(a.shape) for a in consts],
        out_specs=row_spec,
        out_shape=jax.ShapeDtypeStruct((rows, D_MODEL), F32),
        compiler_params=_params(("arbitrary",)),
        name="out",
    )(ya, yb, ga, gb, h, *consts)


def _ffn_weights(w_in, w_out, dtype):
    return w_in.astype(dtype), w_out.astype(dtype).reshape(N_FF_CHUNKS, FF_CHUNK, D_MODEL)


def _position_tables(seq):
    half = RET_DK // 2
    pos = np.arange(BLOCK + seq, dtype=np.float64) - N_EMPTY
    inv = ROPE_BASE ** (-np.arange(half, dtype=np.float64) / half)
    ang = pos[:, None] * inv[None, :]
    cos = np.concatenate([np.cos(ang), np.cos(ang)], axis=1)
    sin = np.concatenate([-np.sin(ang), np.sin(ang)], axis=1)
    return jnp.asarray(cos, dtype=F32), jnp.asarray(sin, dtype=F32)


def _retention_tables(rows):
    log_gamma = np.log1p(-np.exp2(-5.0 - np.arange(RET_HEADS, dtype=np.float64)))
    n = (np.arange(rows) % BLOCK + 1.0)[None, :, None]
    lg = log_gamma[:, None, None]
    shape = (RET_HEADS, rows, RET_DK)
    q_scale = np.broadcast_to(np.exp(lg * n), shape)
    k_scale = np.broadcast_to(np.exp(-lg * n) * RET_DK ** -0.5, shape)
    cd = np.broadcast_to(np.exp(log_gamma * BLOCK)[:, None, None], (RET_HEADS, 1, RET_DV))
    return tuple(jnp.asarray(t, dtype=F32) for t in (q_scale, k_scale, cd))


def kernel(x, meta_tokens, norm_ffn1, w_ffn1_in, w_ffn1_out, norm_mix, w_in, b_forget, b_gate,
           fox_q_norm, fox_k_norm, w_o_fox, ret_gn, w_o_ret, w_out, norm_ffn2, w_ffn2_in,
           w_ffn2_out):
    batch, seq, d = x.shape
    assert d == D_MODEL and seq % ATT_BLOCK == 0 and seq % ROW_TILE == 0
    assert norm_ffn1.shape[0] == 1, "one layer"
    tiles_per_seq = seq // ROW_TILE

    win1, wout1 = _ffn_weights(w_ffn1_in[0], w_ffn1_out[0], F32)
    grp = np.arange(V7X_MXU_DIM) // FOX_HD
    gmat = jnp.asarray((grp[:, None] == grp[None, :]) / FOX_HD, dtype=BF16)
    gq = jnp.tile(fox_q_norm[0], FOX_HEADS)[None, :]
    gk = jnp.tile(fox_k_norm[0], FOX_HEADS)[None, :]
    bf = jnp.pad(b_forget[0], (0, V7X_LANES - FOX_HEADS))[None, :]
    bg = b_gate[0][None, :]
    cos, sin = _position_tables(seq)
    idx = np.arange(ROW_TILE)
    tri_incl = jnp.asarray(idx[None, :] <= idx[:, None], dtype=BF16)
    idx = np.arange(BLOCK)
    tri_after = jnp.asarray(idx[None, :] > idx[:, None], dtype=BF16)

    lead = jnp.concatenate([jnp.zeros((N_EMPTY, d), x.dtype), meta_tokens.astype(x.dtype)], axis=0)
    xr = x.reshape(batch * seq, d)

    assert w_in.shape[2] == COL_END
    h1, h1l, wt = _ffn_and_cast(xr, lead, norm_ffn1, win1, wout1, w_in[0].T, ROW_TILE)
    q_scale, k_scale, chunk_decay = _retention_tables(ROW_TILE)
    proj_consts = (norm_mix, wt, gmat, gq, gk, bf, bg, q_scale, k_scale)
    q, k, v, c, rq, rk, rv, sg, ga, gb, kl, vl, cl, rkl, rvl = _proj(
        h1, h1l, proj_consts, cos, sin, tri_incl, tri_after, ROW_TILE, tiles_per_seq)

    def b3(a):
        return a.reshape(batch, seq, a.shape[-1])

    ck = b3(c).transpose(0, 2, 1).reshape(batch, FOX_PAIRS, 2, seq)
    clt = cl.T.reshape(FOX_PAIRS, 2, BLOCK)
    bound = FOX_HD ** 0.5 * jnp.max(jnp.abs(fox_q_norm[0])) * jnp.max(jnp.abs(fox_k_norm[0]))
    bound2 = bound * LOG2E
    stab = jnp.stack([(bound <= SAFE_LOGIT_BOUND).astype(F32), bound2, F32_EXP2_ZERO + 2.0 * bound2])
    c3 = b3(c)
    cs = c3[:, ::ATT_BLOCK, :].reshape(-1)
    ce = c3[:, ATT_BLOCK // 2 - 1::ATT_BLOCK // 2, :].reshape(-1)
    ya, win2, wof, wor, wo, wout2 = _fox(
        stab, cs, ce, b3(q), c3, b3(k), b3(v), ck, kl, vl, clt,
        (w_ffn2_in[0], w_o_fox[0], w_o_ret[0], w_out[0]), w_ffn2_out[0], batch, seq)
    yb = _ret(b3(rq), b3(rk), b3(rv), b3(sg), rkl, rvl, ret_gn, chunk_decay, batch, seq)

    out_consts = (wof, wor, wo, norm_ffn2, win2, wout2.reshape(N_FF_CHUNKS, FF_CHUNK, D_MODEL))
    out = _out(ya.reshape(batch * seq, d), yb.reshape(batch * seq, d), ga, gb, h1, out_consts,
               ROW_TILE)
    return out.reshape(batch, seq, d)
```

```python
import functools
import math

import jax
import jax.numpy as jnp
import numpy as np
from jax import lax
from jax.experimental import pallas as pl
from jax.experimental.pallas import tpu as pltpu

F32 = jnp.float32
BF16 = jnp.bfloat16

D_MODEL = 1024
N_META = 16
BLOCK = 128
N_EMPTY = BLOCK - N_META
FOX_HD = 64
FOX_HEADS = D_MODEL // FOX_HD
FOX_PAIRS = FOX_HEADS // 2
RET_HEADS = 4
RET_DK = D_MODEL // (2 * RET_HEADS)
RET_DV = 2 * RET_DK
RET_QK = RET_HEADS * RET_DK
D_FF = ((8 * D_MODEL // 3 + 127) // 128) * 128
EPS = 1e-6
GN_EPS = 1e-5
ROPE_BASE = 10000.0
NEG = -1e30
LOG2E = math.log2(math.e)
SAFE_LOGIT_BOUND = 30.0
F32_EXP2_ZERO = 151.0

V7X_LANES = 128
BF16_SUBLANES = 16
V7X_MXU_DIM = 256
V7X_VMEM_BYTES = 64 * 2**20
VMEM_COMPILER_RESERVE = 8 * 2**20
VMEM_LIMIT = V7X_VMEM_BYTES - VMEM_COMPILER_RESERVE

ROW_TILE = 512
FF_CHUNK = V7X_MXU_DIM
N_FF_CHUNKS = D_FF // FF_CHUNK
RET_UNROLL = 32
ATT_BLOCK = 1024


def _dot(a, b):
    return jnp.dot(a, b, preferred_element_type=F32)


def _dot_nt(a, b):
    return lax.dot_general(a, b, (((1,), (1,)), ((), ())), preferred_element_type=F32)


def _dot_tn(a, b):
    return lax.dot_general(a, b, (((0,), (0,)), ((), ())), preferred_element_type=F32)


def _const_spec(shape):
    zeros = (0,) * len(shape)
    return pl.BlockSpec(shape, lambda *_: zeros, pipeline_mode=pl.Buffered(1))


def _params(semantics):
    return pltpu.CompilerParams(dimension_semantics=semantics, vmem_limit_bytes=VMEM_LIMIT)


def _rms_rows(x, g):
    ms = jnp.mean(x * x, axis=-1, keepdims=True)
    return x * lax.rsqrt(ms + EPS) * g


def _swiglu(xn, win_ref, wout_ref):
    acc = None
    for c in range(N_FF_CHUNKS):
        a = _dot(xn, win_ref[:, c * FF_CHUNK:(c + 1) * FF_CHUNK].astype(BF16))
        b = _dot(xn, win_ref[:, D_FF + c * FF_CHUNK:D_FF + (c + 1) * FF_CHUNK].astype(BF16))
        hm = (a * jax.nn.sigmoid(a) * b).astype(BF16)
        d = _dot(hm, wout_ref[c].astype(BF16))
        acc = d if acc is None else acc + d
    return acc


def _ffn_body(x_ref, g_ref, win_ref, wout_ref, o_ref):
    x = x_ref[...]
    xn = _rms_rows(x, g_ref[...]).astype(BF16)
    o_ref[...] = x + 0.5 * _swiglu(xn, win_ref, wout_ref)


def _ffn_cast_body(x_ref, lead_ref, g_ref, win_ref, wout_ref, w_src, o_ref, ol_ref, w_dst, *, n_tiles):
    i = pl.program_id(0)

    @pl.when(i < n_tiles)
    def _():
        _ffn_body(x_ref, g_ref, win_ref, wout_ref, o_ref)
        w_dst[...] = w_src[...].astype(BF16)

    @pl.when(i == n_tiles)
    def _():
        _ffn_body(lead_ref, g_ref, win_ref, wout_ref, ol_ref)


def _ffn_and_cast(x, lead, g, win, wout, w_t, tm):
    rows = x.shape[0]
    n_tiles = rows // tm
    n = w_t.shape[0]
    slab = min(s for s in range(BF16_SUBLANES, n + 1, BF16_SUBLANES) if n % s == 0 and n // s <= n_tiles)
    cast_steps = n // slab
    row_spec = pl.BlockSpec((tm, D_MODEL), lambda i: (jnp.minimum(i, n_tiles - 1), 0))
    lead_spec = pl.BlockSpec(lead.shape, lambda i: (0, 0))
    slab_spec = pl.BlockSpec((slab, w_t.shape[1]), lambda i: (jnp.minimum(i, cast_steps - 1), 0))
    return pl.pallas_call(
        functools.partial(_ffn_cast_body, n_tiles=n_tiles),
        grid=(n_tiles + 1,),
        in_specs=[row_spec, lead_spec, _const_spec(g.shape), _const_spec(win.shape),
                  _const_spec(wout.shape), slab_spec],
        out_specs=[row_spec, lead_spec, slab_spec],
        out_shape=[jax.ShapeDtypeStruct((rows, D_MODEL), F32), jax.ShapeDtypeStruct(lead.shape, F32),
                   jax.ShapeDtypeStruct(w_t.shape, BF16)],
        compiler_params=_params(("arbitrary",)),
        name="ffn_cast",
    )(x, lead, g, win, wout, w_t)


def _group_mean_sq(x, gmat_ref):
    sq = (x * x).astype(BF16)
    w = V7X_MXU_DIM
    parts = [_dot(sq[:, i * w:(i + 1) * w], gmat_ref[...]) for i in range(D_MODEL // w)]
    return jnp.concatenate(parts, axis=1)


def _rotary_tile(x, cos, sin_signed):
    return x * cos + pltpu.roll(x, RET_DK // 2, axis=1) * sin_signed


COL_FQ = 0
COL_FK = COL_FQ + D_MODEL
COL_FV = COL_FK + D_MODEL
COL_FF = COL_FV + D_MODEL
COL_RQ = COL_FF + FOX_HEADS
COL_RK = COL_RQ + RET_QK
COL_RV = COL_RK + RET_QK
COL_RG = COL_RV + D_MODEL
COL_GA = COL_RG + D_MODEL
COL_GB = COL_GA + D_MODEL
COL_END = COL_GB + D_MODEL


def _proj_rows(h, cos, sin, tri, q_scale, k_scale, gmix_ref, wt_ref, gmat_ref, gq_ref, gk_ref, bf_ref,
               bg_ref, carry_ref, outs, *, lead):
    q_ref, k_ref, v_ref, c_ref, rq_ref, rk_ref, rv_ref, sg_ref, ga_ref, gb_ref = outs
    tm = h.shape[0]
    u = _rms_rows(h, gmix_ref[...]).astype(BF16)

    def project(c0, c1):
        return _dot_nt(u, wt_ref[c0:c1, :])

    if lead:
        row = lax.broadcasted_iota(jnp.int32, (tm, 1), 0)
        valid = row >= N_EMPTY
        vmask = valid.astype(F32)


    z = project(COL_FF, COL_FF + V7X_LANES) + bf_ref[...]
    logf = -(jnp.maximum(-z, 0.0) + jnp.log1p(jnp.exp(-jnp.abs(z))))
    if lead:
        logf = jnp.where(valid, logf, 0.0)
    lane = lax.broadcasted_iota(jnp.int32, (1, V7X_LANES), 1)
    logf = jnp.where(lane < FOX_HEADS, logf, 0.0)
    p1 = logf.astype(BF16).astype(F32)
    r1 = logf - p1
    p2 = r1.astype(BF16).astype(F32)
    p3 = (r1 - p2).astype(BF16).astype(F32)
    packed = (p1 + pltpu.roll(p2, FOX_HEADS, axis=1) + pltpu.roll(p3, 2 * FOX_HEADS, axis=1)).astype(BF16)

    fq = project(COL_FQ, COL_FK) if q_ref is not None else None
    fk = project(COL_FK, COL_FV)
    rq = project(COL_RQ, COL_RK) if rq_ref is not None else None
    rk = project(COL_RK, COL_RV)

    if q_ref is not None:
        qn = fq * lax.rsqrt(_group_mean_sq(fq, gmat_ref) + EPS) * gq_ref[...]
        q_ref[...] = (qn * (FOX_HD ** -0.5 * LOG2E)).astype(BF16)
    kn = fk * lax.rsqrt(_group_mean_sq(fk, gmat_ref) + EPS) * gk_ref[...]
    k_ref[...] = kn.astype(BF16)

    r = _dot(tri, packed)
    cum = r + (pltpu.roll(r, V7X_LANES - FOX_HEADS, axis=1)
               + pltpu.roll(r, V7X_LANES - 2 * FOX_HEADS, axis=1))
    cum = jnp.where(lane < FOX_HEADS, cum, 0.0)
    if lead:
        c = -cum
    else:
        c = cum + carry_ref[...]
        carry_ref[...] = c[tm - 1:tm, :]
    c_ref[...] = (c * LOG2E)[:, :FOX_HEADS]

    for hh in range(RET_HEADS):
        sl = slice(hh * RET_DK, (hh + 1) * RET_DK)
        if rq_ref is not None:
            rq_ref[:, sl] = (_rotary_tile(rq[:, sl], cos, sin) * q_scale(hh)).astype(BF16)
        kt = _rotary_tile(rk[:, sl], cos, sin) * k_scale(hh)
        if lead:
            kt = kt * vmask
        rk_ref[:, sl] = kt.astype(BF16)
    rv = project(COL_RV, COL_RG)
    if lead:
        rv = rv * vmask
    rv_ref[...] = rv.astype(BF16)
    if sg_ref is not None:
        rg = project(COL_RG, COL_GA)
        sg_ref[...] = (rg * jax.nn.sigmoid(rg)).astype(BF16)

    if ga_ref is not None:
        bg = bg_ref[...]
        ga_ref[...] = jax.nn.sigmoid(project(COL_GA, COL_GB) + bg[:, :D_MODEL]).astype(BF16)
        gb_ref[...] = jax.nn.sigmoid(project(COL_GB, COL_END) + bg[:, D_MODEL:]).astype(BF16)

    fv = project(COL_FV, COL_FF).astype(BF16)
    ones = jnp.ones((tm, V7X_LANES), BF16)
    for j in range(FOX_PAIRS):
        v_ref[:, 2 * j * V7X_LANES:(2 * j + 1) * V7X_LANES] = fv[:, j * V7X_LANES:(j + 1) * V7X_LANES]
        v_ref[:, (2 * j + 1) * V7X_LANES:(2 * j + 2) * V7X_LANES] = ones


def _proj_body(h_ref, hl_ref, gmix_ref, wt_ref, gmat_ref, gq_ref, gk_ref, bf_ref, bg_ref, qs_ref, ks_ref,
               cos_ref, sin_ref, cosl_ref, sinl_ref, tri_ref, tril_ref,
               q_ref, k_ref, v_ref, c_ref, rq_ref, rk_ref, rv_ref, sg_ref, ga_ref, gb_ref,
               kl_ref, vl_ref, cl_ref, rkl_ref, rvl_ref, carry_ref, *, n_tiles, tiles_per_seq):
    i = pl.program_id(0)
    consts = (gmix_ref, wt_ref, gmat_ref, gq_ref, gk_ref, bf_ref, bg_ref, carry_ref)

    @pl.when(i % tiles_per_seq == 0)
    def _():
        carry_ref[...] = jnp.zeros_like(carry_ref)

    @pl.when(i < n_tiles)
    def _():
        outs = (q_ref, k_ref, v_ref, c_ref, rq_ref, rk_ref, rv_ref, sg_ref, ga_ref, gb_ref)
        _proj_rows(h_ref[...], cos_ref[...], sin_ref[...], tri_ref[...], lambda hh: qs_ref[hh],
                   lambda hh: ks_ref[hh], *consts, outs, lead=False)

    @pl.when(i == n_tiles)
    def _():
        outs = (None, kl_ref, vl_ref, cl_ref, None, rkl_ref, rvl_ref, None, None, None)
        _proj_rows(hl_ref[...], cosl_ref[...], sinl_ref[...], tril_ref[...], lambda hh: qs_ref[hh, :BLOCK],
                   lambda hh: ks_ref[hh, :BLOCK], *consts, outs, lead=True)


def _proj(h, h_lead, consts, cos, sin, tri, tri_lead, tm, tiles_per_seq):
    rows = h.shape[0]
    n_tiles = rows // tm
    assert tm % BLOCK == 0

    def rows_spec(width):
        return pl.BlockSpec((tm, width), lambda i: (jnp.minimum(i, n_tiles - 1), 0))

    def lead_spec(width):
        return pl.BlockSpec((BLOCK, width), lambda i: (0, 0))

    pos_spec = pl.BlockSpec((tm, RET_DK), lambda i: (i % tiles_per_seq, 0))
    out_widths = [D_MODEL, D_MODEL, 2 * D_MODEL, FOX_HEADS, RET_QK, RET_QK, D_MODEL, D_MODEL,
                  D_MODEL, D_MODEL]
    out_dtypes = [BF16, BF16, BF16, F32, BF16, BF16, BF16, BF16, BF16, BF16]
    lead_outs = [1, 2, 3, 5, 6]
    return pl.pallas_call(
        functools.partial(_proj_body, n_tiles=n_tiles, tiles_per_seq=tiles_per_seq),
        grid=(n_tiles + 1,),
        in_specs=[rows_spec(D_MODEL), lead_spec(D_MODEL)] + [_const_spec(a.shape) for a in consts]
        + [pos_spec, pos_spec, lead_spec(RET_DK), lead_spec(RET_DK), _const_spec(tri.shape),
           _const_spec(tri_lead.shape)],
        out_specs=[rows_spec(w) for w in out_widths] + [lead_spec(out_widths[j]) for j in lead_outs],
        out_shape=[jax.ShapeDtypeStruct((rows, w), dt) for w, dt in zip(out_widths, out_dtypes)]
        + [jax.ShapeDtypeStruct((BLOCK, out_widths[j]), out_dtypes[j]) for j in lead_outs],
        scratch_shapes=[pltpu.VMEM((1, V7X_LANES), F32)],
        compiler_params=_params(("arbitrary",)),
        name="proj",
    )(h, h_lead, *consts, cos[BLOCK:], sin[BLOCK:], cos, sin, tri, tri_lead)


def _fox_body(stab_ref, cs_ref, ce_ref, q_ref, cq_ref, k_ref, v_ref, ck_ref, kl_ref, vl_ref, cl_ref,
              *rest, n_cast, n_cast_short, short_steps):
    srcs, (o_ref, *dsts), (m_ref, acc_ref, e_ref) = (rest[:n_cast], rest[n_cast:2 * n_cast + 1],
                                                     rest[2 * n_cast + 1:])
    step = pl.program_id(0) * pl.num_programs(1) + pl.program_id(1)
    for src, dst in zip(srcs[:n_cast - n_cast_short], dsts):
        dst[...] = src[...].astype(BF16)

    @pl.when(step < short_steps)
    def _():
        for src, dst in zip(srcs[n_cast - n_cast_short:], dsts[n_cast - n_cast_short:]):
            dst[...] = src[...].astype(BF16)

    _fox_attend(stab_ref, cs_ref, ce_ref, q_ref, cq_ref, k_ref, v_ref, ck_ref, kl_ref, vl_ref, cl_ref,
                o_ref, m_ref, acc_ref, e_ref)


def _fox_attend(stab_ref, cs_ref, ce_ref, q_ref, cq_ref, k_ref, v_ref, ck_ref, kl_ref, vl_ref, cl_ref,
                o_ref, m_ref, acc_ref, e_ref):
    tq = m_ref.shape[1]
    tk = tq // 2
    tr = tq // 4
    lane = lax.broadcasted_iota(jnp.int32, (1, 2 * V7X_LANES), 1)
    first2 = (lane % V7X_LANES) < FOX_HD
    first = first2[:, :V7X_LANES]
    lead_ok = lax.broadcasted_iota(jnp.int32, (tq, BLOCK), 1) >= N_EMPTY
    row = lax.broadcasted_iota(jnp.int32, (tq, tk), 0)
    col = lax.broadcasted_iota(jnp.int32, (tq, tk), 1)
    causal = col <= row
    corner = causal[:tr, :tr]
    use_bound = stab_ref[0] > 0.5

    def keys(j):
        ks = pl.multiple_of(j * tk, tk)
        return k_ref[pl.ds(ks, tk), :], ck_ref[:, pl.ds(ks, tk)]

    def values(j):
        return v_ref[pl.ds(pl.multiple_of(j * tk, tk), tk), :]

    nq = q_ref.shape[0] // tq
    nk = k_ref.shape[0] // tk

    def query_block(i, carry):
        _fox_query_block(i, tq, tk, tr, nq, nk, first, first2, lead_ok, row, col, causal, corner,
                         use_bound, keys, values, stab_ref, cs_ref, ce_ref, q_ref, cq_ref, kl_ref,
                         vl_ref, cl_ref, o_ref, m_ref, acc_ref, e_ref)
        return carry

    lax.fori_loop(0, nq, query_block, 0)


def _fox_query_block(i, tq, tk, tr, nq, nk, first, first2, lead_ok, row, col, causal, corner,
                     use_bound, keys, values, stab_ref, cs_ref, ce_ref, q_ref, cq_ref, kl_ref,
                     vl_ref, cl_ref, o_ref, m_ref, acc_ref, e_ref):
    rows = pl.ds(pl.multiple_of(i * tq, tq), tq)
    n_full = 2 * i
    q = q_ref[rows, :]
    zero = jnp.zeros_like(q)
    q_heads = (jnp.where(first, q, zero), jnp.where(first, zero, q))
    acc_ref[...] = jnp.zeros_like(acc_ref)

    @pl.when(use_bound)
    def _():
        head = lax.broadcasted_iota(jnp.int32, (1, FOX_HEADS), 1) - 2 * pl.program_id(1)
        cq_blk = cq_ref[rows, :]
        cq = [jnp.sum(jnp.where(head == hh, cq_blk, 0.0), axis=1, keepdims=True) - stab_ref[1]
              for hh in range(2)]

        def exponents(k, ck, r0=0, r1=tq):
            return [_dot_nt(q_heads[hh][r0:r1], k) + cq[hh][r0:r1] - ck[hh:hh + 1, :] for hh in range(2)]

        def accumulate(e, v, allowed, r0=0):
            if allowed is not None:
                e = [jnp.where(allowed, x, NEG) for x in e]
            pvs = [_dot(jnp.exp2(x).astype(BF16), v) for x in e]
            r1 = r0 + e[0].shape[0]
            acc_ref[r0:r1, :] += jnp.where(first2, pvs[0], pvs[1])

        cutoff = -stab_ref[2]
        slot = 2 * pl.program_id(1)
        cs_base = (pl.program_id(0) * nq + i) * FOX_HEADS
        ce_base = pl.program_id(0) * nk * FOX_HEADS
        c_first = [cs_ref[cs_base + slot + hh] for hh in range(2)]
        skip_lead = jnp.maximum(c_first[0], c_first[1]) < cutoff
        j0 = jnp.int32(0)
        for j in range(nk - 2):
            gap = jnp.maximum(c_first[0] - ce_ref[ce_base + j * FOX_HEADS + slot],
                              c_first[1] - ce_ref[ce_base + j * FOX_HEADS + slot + 1])
            j0 += jnp.logical_and(gap < cutoff, j < n_full).astype(jnp.int32)

        e0 = exponents(*keys(j0))
        e_ref[0] = e0[0]
        e_ref[1] = e0[1]

        @pl.when(jnp.logical_not(skip_lead))
        def _():
            accumulate(exponents(kl_ref[...], cl_ref[...]), vl_ref[...], lead_ok)

        def step(j, carry):
            e_cur = [e_ref[0], e_ref[1]]
            e_next = exponents(*keys(j + 1))
            accumulate(e_cur, values(j), None)
            e_ref[0] = e_next[0]
            e_ref[1] = e_next[1]
            return carry

        def step_pair(m, carry):
            return step(2 * m + 1, step(2 * m, carry))

        @pl.when(j0 % 2 == 1)
        def _():
            step(j0, 0)

        lax.fori_loop((j0 + 1) // 2, i, step_pair, 0)

        kb, ckb = keys(n_full + 1)
        vb = values(n_full + 1)
        e_b1 = exponents(kb[:tr], ckb[:, :tr], 2 * tr, tq)
        e_b2 = exponents(kb[tr:], ckb[:, tr:], 3 * tr, tq)
        va = values(n_full)
        e_a = [e_ref[0], e_ref[1]]
        strips = [
            [([x[:tr, :tr] for x in e_a], va[:tr], corner)],
            [([x[tr:2 * tr] for x in e_a], va, causal[tr:2 * tr])],
            [([x[2 * tr:3 * tr] for x in e_a], va, None), ([x[:tr] for x in e_b1], vb[:tr], corner)],
            [([x[3 * tr:] for x in e_a], va, None), ([x[tr:] for x in e_b1], vb[:tr], None),
             (e_b2, vb[tr:], corner)],
        ]
        for s, parts in enumerate(strips):
            pv = [None, None]
            for e, v, allowed in parts:
                if allowed is not None:
                    e = [jnp.where(allowed, x, NEG) for x in e]
                for hh in range(2):
                    d = _dot(jnp.exp2(e[hh]).astype(BF16), v)
                    pv[hh] = d if pv[hh] is None else pv[hh] + d
            acc_ref[s * tr:(s + 1) * tr, :] += jnp.where(first2, pv[0], pv[1])

    @pl.when(jnp.logical_not(use_bound))
    def _():
        m_ref[...] = jnp.full_like(m_ref, NEG)

        def block(k, v, ck, allowed):
            pvs, alphas = [], []
            for hh in range(2):
                s = _dot_nt(q_heads[hh], k) - ck[hh:hh + 1, :]
                if allowed is not None:
                    s = jnp.where(allowed, s, NEG)
                m_old = m_ref[hh]
                m_new = jnp.maximum(m_old, jnp.max(s, axis=1, keepdims=True))
                m_ref[hh] = m_new
                pvs.append(_dot(jnp.exp2(s - m_new).astype(BF16), v))
                alphas.append(jnp.exp2(m_old - m_new))
            acc_ref[...] = (acc_ref[...] * jnp.where(first2, alphas[0], alphas[1])
                            + jnp.where(first2, pvs[0], pvs[1]))

        block(kl_ref[...], vl_ref[...], cl_ref[...], lead_ok)

        def full_block(j, carry):
            block(keys(j)[0], values(j), keys(j)[1], None)
            return carry

        lax.fori_loop(0, n_full, full_block, 0)
        block(keys(n_full)[0], values(n_full), keys(n_full)[1], causal)
        block(keys(n_full + 1)[0], values(n_full + 1), keys(n_full + 1)[1], col + tk <= row)

    acc = acc_ref[...]
    o_ref[rows, :] = (acc[:, :V7X_LANES] / acc[:, V7X_LANES:]).astype(o_ref.dtype)


def _fox(stab, cs, ce, q, cq, k, v, ck, kl, vl, cl, weights, short_weight, batch, seq):
    tq = ATT_BLOCK
    w = V7X_LANES
    steps = batch * FOX_PAIRS
    cast_specs, cast_shapes = [], []
    for a in weights:
        slab = a.shape[0] // steps
        assert slab * steps == a.shape[0] and slab % BF16_SUBLANES == 0
        cast_specs.append(pl.BlockSpec((slab, a.shape[1]), lambda b, j: (b * FOX_PAIRS + j, 0)))
        cast_shapes.append(jax.ShapeDtypeStruct(a.shape, BF16))
    short_steps = short_weight.shape[0] // FF_CHUNK
    assert short_steps * FF_CHUNK == short_weight.shape[0] and short_steps <= steps
    cast_specs.append(pl.BlockSpec(
        (FF_CHUNK, short_weight.shape[1]),
        lambda b, j: (jnp.minimum(b * FOX_PAIRS + j, short_steps - 1), 0)))
    cast_shapes.append(jax.ShapeDtypeStruct(short_weight.shape, BF16))
    n_cast = len(cast_specs)
    return pl.pallas_call(
        functools.partial(_fox_body, n_cast=n_cast, n_cast_short=1, short_steps=short_steps),
        grid=(batch, FOX_PAIRS),
        in_specs=[
            pl.BlockSpec(memory_space=pltpu.SMEM),
            pl.BlockSpec(memory_space=pltpu.SMEM),
            pl.BlockSpec(memory_space=pltpu.SMEM),
            pl.BlockSpec((None, seq, w), lambda b, j: (b, 0, j)),
            pl.BlockSpec((None, seq, FOX_HEADS), lambda b, j: (b, 0, 0)),
            pl.BlockSpec((None, seq, w), lambda b, j: (b, 0, j)),
            pl.BlockSpec((None, seq, 2 * w), lambda b, j: (b, 0, j)),
            pl.BlockSpec((None, None, 2, seq), lambda b, j: (b, j, 0, 0)),
            pl.BlockSpec((BLOCK, w), lambda b, j: (0, j)),
            pl.BlockSpec((BLOCK, 2 * w), lambda b, j: (0, j)),
            pl.BlockSpec((None, 2, BLOCK), lambda b, j: (j, 0, 0)),
        ] + cast_specs,
        out_specs=[pl.BlockSpec((None, seq, w), lambda b, j: (b, 0, j))] + cast_specs,
        out_shape=[jax.ShapeDtypeStruct((batch, seq, D_MODEL), BF16)] + cast_shapes,
        scratch_shapes=[pltpu.VMEM((2, tq, 1), F32), pltpu.VMEM((tq, 2 * w), F32),
                        pltpu.VMEM((2, tq, tq // 2), F32)],
        compiler_params=_params(("arbitrary", "arbitrary")),
        name="fox",
    )(stab, cs, ce, q, cq, k, v, ck, kl, vl, cl, *weights, short_weight)


def _ret_body(q_ref, k_ref, v_ref, sg_ref, kl_ref, vl_ref, gn_ref, cd_ref, o_ref, kv_ref, state_ref):
    n_chunks = q_ref.shape[0] // BLOCK
    cd = cd_ref[...]
    gn = gn_ref[...]
    causal = (lax.broadcasted_iota(jnp.int32, (BLOCK, BLOCK), 1)
              <= lax.broadcasted_iota(jnp.int32, (BLOCK, BLOCK), 0))

    def rows(c):
        return pl.ds(pl.multiple_of(c * BLOCK, BLOCK), BLOCK)

    def summarise(c, carry):
        kv_ref[c] = _dot_tn(k_ref[rows(c), :], v_ref[rows(c), :])
        return carry

    lax.fori_loop(0, n_chunks, summarise, 0, unroll=RET_UNROLL)

    def advance(c, state):
        state_ref[c] = state.astype(BF16)
        return cd * (state + kv_ref[c])

    lead_state = cd * _dot_tn(kl_ref[...], vl_ref[...])
    lax.fori_loop(0, n_chunks, advance, lead_state, unroll=RET_UNROLL)

    def emit(c, carry):
        q = q_ref[rows(c), :]
        v = v_ref[rows(c), :]
        scores = jnp.where(causal, _dot_nt(q, k_ref[rows(c), :]), 0.0).astype(BF16)
        o = _dot(jnp.concatenate([scores, q], axis=1), jnp.concatenate([v, state_ref[c]], axis=0))
        mu = jnp.mean(o, axis=-1, keepdims=True)
        d = o - mu
        var = jnp.mean(d * d, axis=-1, keepdims=True)
        yn = d * lax.rsqrt(var + GN_EPS) * gn
        o_ref[rows(c), :] = (sg_ref[rows(c), :].astype(F32) * yn).astype(o_ref.dtype)
        return carry

    lax.fori_loop(0, n_chunks, emit, 0, unroll=RET_UNROLL)


def _ret(rq, rk, rv, sg, rkl, rvl, gn, cd, batch, seq):
    head_qk = pl.BlockSpec((None, seq, RET_DK), lambda b, h: (b, 0, h))
    head_v = pl.BlockSpec((None, seq, RET_DV), lambda b, h: (b, 0, h))
    return pl.pallas_call(
        _ret_body,
        grid=(batch, RET_HEADS),
        in_specs=[head_qk, head_qk, head_v, head_v,
                  pl.BlockSpec((BLOCK, RET_DK), lambda b, h: (0, h)),
                  pl.BlockSpec((BLOCK, RET_DV), lambda b, h: (0, h)),
                  pl.BlockSpec((1, RET_DV), lambda b, h: (0, h)),
                  pl.BlockSpec((None, 1, RET_DV), lambda b, h: (h, 0, 0))],
        out_specs=head_v,
        out_shape=jax.ShapeDtypeStruct((batch, seq, D_MODEL), BF16),
        scratch_shapes=[pltpu.VMEM((seq // BLOCK, RET_DK, RET_DV), F32),
                        pltpu.VMEM((seq // BLOCK, RET_DK, RET_DV), BF16)],
        compiler_params=_params(("arbitrary", "arbitrary")),
        name="ret",
    )(rq, rk, rv, sg, rkl, rvl, gn, cd)


def _out_body(ya_ref, yb_ref, ga_ref, gb_ref, h_ref, wa_ref, wb_ref, wo_ref, g_ref, win_ref,
              wout_ref, o_ref):
    mixed = (ga_ref[...].astype(F32) * _dot(ya_ref[...], wa_ref[...])
             + gb_ref[...].astype(F32) * _dot(yb_ref[...], wb_ref[...]))
    h = h_ref[...] + _dot(mixed.astype(BF16), wo_ref[...])
    xn = _rms_rows(h, g_ref[...]).astype(BF16)
    o_ref[...] = h + 0.5 * _swiglu(xn, win_ref, wout_ref)


def _out(ya, yb, ga, gb, h, consts, tm):
    rows = h.shape[0]
    row_spec = pl.BlockSpec((tm, D_MODEL), lambda i: (i, 0))
    return pl.pallas_call(
        _out_body,
        grid=(rows // tm,),
        in_specs=[row_spec] * 5 + [_const_spec(a.shape) for a in consts],
        out_specs=row_spec,
        out_shape=jax.ShapeDtypeStruct((rows, D_MODEL), F32),
        compiler_params=_params(("arbitrary",)),
        name="out",
    )(ya, yb, ga, gb, h, *consts)


def _ffn_weights(w_in, w_out, dtype):
    return w_in.astype(dtype), w_out.astype(dtype).reshape(N_FF_CHUNKS, FF_CHUNK, D_MODEL)


def _position_tables(seq):
    half = RET_DK // 2
    pos = np.arange(BLOCK + seq, dtype=np.float64) - N_EMPTY
    inv = ROPE_BASE ** (-np.arange(half, dtype=np.float64) / half)
    ang = pos[:, None] * inv[None, :]
    cos = np.concatenate([np.cos(ang), np.cos(ang)], axis=1)
    sin = np.concatenate([-np.sin(ang), np.sin(ang)], axis=1)
    return jnp.asarray(cos, dtype=F32), jnp.asarray(sin, dtype=F32)


def _retention_tables(rows):
    log_gamma = np.log1p(-np.exp2(-5.0 - np.arange(RET_HEADS, dtype=np.float64)))
    n = (np.arange(rows) % BLOCK + 1.0)[None, :, None]
    lg = log_gamma[:, None, None]
    shape = (RET_HEADS, rows, RET_DK)
    q_scale = np.broadcast_to(np.exp(lg * n), shape)
    k_scale = np.broadcast_to(np.exp(-lg * n) * RET_DK ** -0.5, shape)
    cd = np.broadcast_to(np.exp(log_gamma * BLOCK)[:, None, None], (RET_HEADS, 1, RET_DV))
    return tuple(jnp.asarray(t, dtype=F32) for t in (q_scale, k_scale, cd))


def kernel(x, meta_tokens, norm_ffn1, w_ffn1_in, w_ffn1_out, norm_mix, w_in, b_forget, b_gate,
           fox_q_norm, fox_k_norm, w_o_fox, ret_gn, w_o_ret, w_out, norm_ffn2, w_ffn2_in,
           w_ffn2_out):
    batch, seq, d = x.shape
    assert d == D_MODEL and seq % ATT_BLOCK == 0 and seq % ROW_TILE == 0
    assert norm_ffn1.shape[0] == 1, "one layer"
    tiles_per_seq = seq // ROW_TILE

    win1, wout1 = _ffn_weights(w_ffn1_in[0], w_ffn1_out[0], F32)
    grp = np.arange(V7X_MXU_DIM) // FOX_HD
    gmat = jnp.asarray((grp[:, None] == grp[None, :]) / FOX_HD, dtype=BF16)
    gq = jnp.tile(fox_q_norm[0], FOX_HEADS)[None, :]
    gk = jnp.tile(fox_k_norm[0], FOX_HEADS)[None, :]
    bf = jnp.pad(b_forget[0], (0, V7X_LANES - FOX_HEADS))[None, :]
    bg = b_gate[0][None, :]
    cos, sin = _position_tables(seq)
    idx = np.arange(ROW_TILE)
    tri_incl = jnp.asarray(idx[None, :] <= idx[:, None], dtype=BF16)
    idx = np.arange(BLOCK)
    tri_after = jnp.asarray(idx[None, :] > idx[:, None], dtype=BF16)

    lead = jnp.concatenate([jnp.zeros((N_EMPTY, d), x.dtype), meta_tokens.astype(x.dtype)], axis=0)
    xr = x.reshape(batch * seq, d)

    assert w_in.shape[2] == COL_END
    h1, h1l, wt = _ffn_and_cast(xr, lead, norm_ffn1, win1, wout1, w_in[0].T, ROW_TILE)
    q_scale, k_scale, chunk_decay = _retention_tables(ROW_TILE)
    proj_consts = (norm_mix, wt, gmat, gq, gk, bf, bg, q_scale, k_scale)
    q, k, v, c, rq, rk, rv, sg, ga, gb, kl, vl, cl, rkl, rvl = _proj(
        h1, h1l, proj_consts, cos, sin, tri_incl, tri_after, ROW_TILE, tiles_per_seq)

    def b3(a):
        return a.reshape(batch, seq, a.shape[-1])

    ck = b3(c).transpose(0, 2, 1).reshape(batch, FOX_PAIRS, 2, seq)
    clt = cl.T.reshape(FOX_PAIRS, 2, BLOCK)
    bound = FOX_HD ** 0.5 * jnp.max(jnp.abs(fox_q_norm[0])) * jnp.max(jnp.abs(fox_k_norm[0]))
    bound2 = bound * LOG2E
    stab = jnp.stack([(bound <= SAFE_LOGIT_BOUND).astype(F32), bound2, F32_EXP2_ZERO + 2.0 * bound2])
    c3 = b3(c)
    cs = c3[:, ::ATT_BLOCK, :].reshape(-1)
    ce = c3[:, ATT_BLOCK // 2 - 1::ATT_BLOCK // 2, :].reshape(-1)
    ya, win2, wof, wor, wo, wout2 = _fox(
        stab, cs, ce, b3(q), c3, b3(k), b3(v), ck, kl, vl, clt,
        (w_ffn2_in[0], w_o_fox[0], w_o_ret[0], w_out[0]), w_ffn2_out[0], batch, seq)
    yb = _ret(b3(rq), b3(rk), b3(rv), b3(sg), rkl, rvl, ret_gn, chunk_decay, batch, seq)

    out_consts = (wof, wor, wo, norm_ffn2, win2, wout2.reshape(N_FF_CHUNKS, FF_CHUNK, D_MODEL))
    out = _out(ya.reshape(batch * seq, d), yb.reshape(batch * seq, d), ga, gb, h1, out_consts,
               ROW_TILE)
    return out.reshape(batch, seq, d)
```

```python
import functools
import math

import jax
import jax.numpy as jnp
import numpy as np
from jax import lax
from jax.experimental import pallas as pl
from jax.experimental.pallas import tpu as pltpu

F32 = jnp.float32
BF16 = jnp.bfloat16

D_MODEL = 1024
N_META = 16
BLOCK = 128
N_EMPTY = BLOCK - N_META
FOX_HD = 64
FOX_HEADS = D_MODEL // FOX_HD
FOX_PAIRS = FOX_HEADS // 2
RET_HEADS = 4
RET_DK = D_MODEL // (2 * RET_HEADS)
RET_DV = 2 * RET_DK
RET_QK = RET_HEADS * RET_DK
D_FF = ((8 * D_MODEL // 3 + 127) // 128) * 128
EPS = 1e-6
GN_EPS = 1e-5
ROPE_BASE = 10000.0
NEG = -1e30
LOG2E = math.log2(math.e)
SAFE_LOGIT_BOUND = 30.0
F32_EXP2_ZERO = 151.0

V7X_LANES = 128
BF16_SUBLANES = 16
V7X_MXU_DIM = 256
V7X_VMEM_BYTES = 64 * 2**20
VMEM_COMPILER_RESERVE = 8 * 2**20
VMEM_LIMIT = V7X_VMEM_BYTES - VMEM_COMPILER_RESERVE

ROW_TILE = 512
FF_CHUNK = V7X_MXU_DIM
N_FF_CHUNKS = D_FF // FF_CHUNK
RET_UNROLL = 32
ATT_BLOCK = 1024


def _dot(a, b):
    return jnp.dot(a, b, preferred_element_type=F32)


def _dot_nt(a, b):
    return lax.dot_general(a, b, (((1,), (1,)), ((), ())), preferred_element_type=F32)


def _dot_tn(a, b):
    return lax.dot_general(a, b, (((0,), (0,)), ((), ())), preferred_element_type=F32)


def _const_spec(shape):
    zeros = (0,) * len(shape)
    return pl.BlockSpec(shape, lambda *_: zeros, pipeline_mode=pl.Buffered(1))


def _params(semantics, vmem_limit=VMEM_LIMIT):
    return pltpu.CompilerParams(dimension_semantics=semantics, vmem_limit_bytes=vmem_limit)


def _rms_rows(x, g):
    ms = jnp.mean(x * x, axis=-1, keepdims=True)
    return x * lax.rsqrt(ms + EPS) * g


def _swiglu(xn, win_ref, wout_ref, before_chunk=None):
    acc = None
    for c in range(N_FF_CHUNKS):
        if before_chunk is not None:
            before_chunk(c)
        a = _dot(xn, win_ref[:, c * FF_CHUNK:(c + 1) * FF_CHUNK].astype(BF16))
        b = _dot(xn, win_ref[:, D_FF + c * FF_CHUNK:D_FF + (c + 1) * FF_CHUNK].astype(BF16))
        hm = (a * jax.nn.sigmoid(a) * b).astype(BF16)
        d = _dot(hm, wout_ref[c].astype(BF16))
        acc = d if acc is None else acc + d
    return acc


def _ffn_body(x_ref, g_ref, win_ref, wout_ref, o_ref, before_chunk=None):
    x = x_ref[...]
    xn = _rms_rows(x, g_ref[...]).astype(BF16)
    o_ref[...] = x + 0.5 * _swiglu(xn, win_ref, wout_ref, before_chunk)


def _ffn_cast_body(x_ref, lead_ref, g_ref, win_hbm, wout_hbm, w_src, o_ref, ol_ref, w_dst,
                   win_ref, wout_ref, sem, *, n_tiles):
    i = pl.program_id(0)

    def chunk_copies(c):
        cols_a = pl.ds(c * FF_CHUNK, FF_CHUNK)
        cols_b = pl.ds(D_FF + c * FF_CHUNK, FF_CHUNK)
        return (pltpu.make_async_copy(win_hbm.at[:, cols_a], win_ref.at[:, cols_a], sem.at[0, c]),
                pltpu.make_async_copy(win_hbm.at[:, cols_b], win_ref.at[:, cols_b], sem.at[1, c]),
                pltpu.make_async_copy(wout_hbm.at[c], wout_ref.at[c], sem.at[2, c]))

    def wait_chunk(c):
        for copy in chunk_copies(c):
            copy.wait()

    @pl.when(i == 0)
    def _():
        for c in range(N_FF_CHUNKS):
            for copy in chunk_copies(c):
                copy.start()
        _ffn_body(x_ref, g_ref, win_ref, wout_ref, o_ref, wait_chunk)
        w_dst[...] = w_src[...].astype(BF16)

    @pl.when(jnp.logical_and(i > 0, i < n_tiles))
    def _():
        _ffn_body(x_ref, g_ref, win_ref, wout_ref, o_ref)
        w_dst[...] = w_src[...].astype(BF16)

    @pl.when(i == n_tiles)
    def _():
        _ffn_body(lead_ref, g_ref, win_ref, wout_ref, ol_ref)


def _ffn_and_cast(x, lead, g, win, wout, w_t, tm):
    rows = x.shape[0]
    n_tiles = rows // tm
    n = w_t.shape[0]
    slab = min(s for s in range(BF16_SUBLANES, n + 1, BF16_SUBLANES) if n % s == 0 and n // s <= n_tiles)
    cast_steps = n // slab
    row_spec = pl.BlockSpec((tm, D_MODEL), lambda i: (jnp.minimum(i, n_tiles - 1), 0))
    lead_spec = pl.BlockSpec(lead.shape, lambda i: (0, 0))
    slab_spec = pl.BlockSpec((slab, w_t.shape[1]), lambda i: (jnp.minimum(i, cast_steps - 1), 0))
    return pl.pallas_call(
        functools.partial(_ffn_cast_body, n_tiles=n_tiles),
        grid=(n_tiles + 1,),
        in_specs=[row_spec, lead_spec, _const_spec(g.shape), pl.BlockSpec(memory_space=pl.ANY),
                  pl.BlockSpec(memory_space=pl.ANY), slab_spec],
        out_specs=[row_spec, lead_spec, slab_spec],
        out_shape=[jax.ShapeDtypeStruct((rows, D_MODEL), F32), jax.ShapeDtypeStruct(lead.shape, F32),
                   jax.ShapeDtypeStruct(w_t.shape, BF16)],
        scratch_shapes=[pltpu.VMEM(win.shape, win.dtype), pltpu.VMEM(wout.shape, wout.dtype),
                        pltpu.SemaphoreType.DMA((3, N_FF_CHUNKS))],
        compiler_params=_params(("arbitrary",), V7X_VMEM_BYTES - VMEM_COMPILER_RESERVE // 2),
        name="ffn_cast",
    )(x, lead, g, win, wout, w_t)


def _group_mean_sq(x, gmat_ref):
    sq = (x * x).astype(BF16)
    w = V7X_MXU_DIM
    parts = [_dot(sq[:, i * w:(i + 1) * w], gmat_ref[...]) for i in range(D_MODEL // w)]
    return jnp.concatenate(parts, axis=1)


def _rotary_tile(x, cos, sin_signed):
    return x * cos + pltpu.roll(x, RET_DK // 2, axis=1) * sin_signed


COL_FQ = 0
COL_FK = COL_FQ + D_MODEL
COL_FV = COL_FK + D_MODEL
COL_FF = COL_FV + D_MODEL
COL_RQ = COL_FF + FOX_HEADS
COL_RK = COL_RQ + RET_QK
COL_RV = COL_RK + RET_QK
COL_RG = COL_RV + D_MODEL
COL_GA = COL_RG + D_MODEL
COL_GB = COL_GA + D_MODEL
COL_END = COL_GB + D_MODEL


def _proj_rows(h, cos, sin, tri, q_scale, k_scale, gmix_ref, wt_ref, gmat_ref, gq_ref, gk_ref, bf_ref,
               bg_ref, carry_ref, outs, *, lead):
    q_ref, k_ref, v_ref, c_ref, rq_ref, rk_ref, rv_ref, sg_ref, ga_ref, gb_ref = outs
    tm = h.shape[0]
    u = _rms_rows(h, gmix_ref[...]).astype(BF16)

    def project(c0, c1):
        return _dot_nt(u, wt_ref[c0:c1, :])

    if lead:
        row = lax.broadcasted_iota(jnp.int32, (tm, 1), 0)
        valid = row >= N_EMPTY
        vmask = valid.astype(F32)


    z = project(COL_FF, COL_FF + V7X_LANES) + bf_ref[...]
    logf = -(jnp.maximum(-z, 0.0) + jnp.log1p(jnp.exp(-jnp.abs(z))))
    if lead:
        logf = jnp.where(valid, logf, 0.0)
    lane = lax.broadcasted_iota(jnp.int32, (1, V7X_LANES), 1)
    logf = jnp.where(lane < FOX_HEADS, logf, 0.0)
    p1 = logf.astype(BF16).astype(F32)
    r1 = logf - p1
    p2 = r1.astype(BF16).astype(F32)
    p3 = (r1 - p2).astype(BF16).astype(F32)
    packed = (p1 + pltpu.roll(p2, FOX_HEADS, axis=1) + pltpu.roll(p3, 2 * FOX_HEADS, axis=1)).astype(BF16)

    fq = project(COL_FQ, COL_FK) if q_ref is not None else None
    fk = project(COL_FK, COL_FV)
    rq = project(COL_RQ, COL_RK) if rq_ref is not None else None
    rk = project(COL_RK, COL_RV)

    if q_ref is not None:
        qn = fq * lax.rsqrt(_group_mean_sq(fq, gmat_ref) + EPS) * gq_ref[...]
        q_ref[...] = (qn * (FOX_HD ** -0.5 * LOG2E)).astype(BF16)
    kn = fk * lax.rsqrt(_group_mean_sq(fk, gmat_ref) + EPS) * gk_ref[...]
    k_ref[...] = kn.astype(BF16)

    r = _dot(tri, packed)
    cum = r + (pltpu.roll(r, V7X_LANES - FOX_HEADS, axis=1)
               + pltpu.roll(r, V7X_LANES - 2 * FOX_HEADS, axis=1))
    cum = jnp.where(lane < FOX_HEADS, cum, 0.0)
    if lead:
        c = -cum
    else:
        c = cum + carry_ref[...]
        carry_ref[...] = c[tm - 1:tm, :]
    c_ref[...] = (c * LOG2E)[:, :FOX_HEADS]

    for hh in range(RET_HEADS):
        sl = slice(hh * RET_DK, (hh + 1) * RET_DK)
        if rq_ref is not None:
            rq_ref[:, sl] = (_rotary_tile(rq[:, sl], cos, sin) * q_scale(hh)).astype(BF16)
        kt = _rotary_tile(rk[:, sl], cos, sin) * k_scale(hh)
        if lead:
            kt = kt * vmask
        rk_ref[:, sl] = kt.astype(BF16)
    rv = project(COL_RV, COL_RG)
    if lead:
        rv = rv * vmask
    rv_ref[...] = rv.astype(BF16)
    if sg_ref is not None:
        rg = project(COL_RG, COL_GA)
        sg_ref[...] = (rg * jax.nn.sigmoid(rg)).astype(BF16)

    if ga_ref is not None:
        bg = bg_ref[...]
        ga_ref[...] = jax.nn.sigmoid(project(COL_GA, COL_GB) + bg[:, :D_MODEL]).astype(BF16)
        gb_ref[...] = jax.nn.sigmoid(project(COL_GB, COL_END) + bg[:, D_MODEL:]).astype(BF16)

    fv = project(COL_FV, COL_FF).astype(BF16)
    ones = jnp.ones((tm, V7X_LANES), BF16)
    for j in range(FOX_PAIRS):
        v_ref[:, 2 * j * V7X_LANES:(2 * j + 1) * V7X_LANES] = fv[:, j * V7X_LANES:(j + 1) * V7X_LANES]
        v_ref[:, (2 * j + 1) * V7X_LANES:(2 * j + 2) * V7X_LANES] = ones


def _proj_body(h_ref, hl_ref, gmix_ref, wt_ref, gmat_ref, gq_ref, gk_ref, bf_ref, bg_ref, qs_ref, ks_ref,
               cos_ref, sin_ref, cosl_ref, sinl_ref, tri_ref, tril_ref,
               q_ref, k_ref, v_ref, c_ref, rq_ref, rk_ref, rv_ref, sg_ref, ga_ref, gb_ref,
               kl_ref, vl_ref, cl_ref, rkl_ref, rvl_ref, carry_ref, *, n_tiles, tiles_per_seq):
    i = pl.program_id(0)
    consts = (gmix_ref, wt_ref, gmat_ref, gq_ref, gk_ref, bf_ref, bg_ref, carry_ref)

    @pl.when(i % tiles_per_seq == 0)
    def _():
        carry_ref[...] = jnp.zeros_like(carry_ref)

    @pl.when(i < n_tiles)
    def _():
        outs = (q_ref, k_ref, v_ref, c_ref, rq_ref, rk_ref, rv_ref, sg_ref, ga_ref, gb_ref)
        _proj_rows(h_ref[...], cos_ref[...], sin_ref[...], tri_ref[...], lambda hh: qs_ref[hh],
                   lambda hh: ks_ref[hh], *consts, outs, lead=False)

    @pl.when(i == n_tiles)
    def _():
        outs = (None, kl_ref, vl_ref, cl_ref, None, rkl_ref, rvl_ref, None, None, None)
        _proj_rows(hl_ref[...], cosl_ref[...], sinl_ref[...], tril_ref[...], lambda hh: qs_ref[hh, :BLOCK],
                   lambda hh: ks_ref[hh, :BLOCK], *consts, outs, lead=True)


def _proj(h, h_lead, consts, cos, sin, tri, tri_lead, tm, tiles_per_seq):
    rows = h.shape[0]
    n_tiles = rows // tm
    assert tm % BLOCK == 0

    def rows_spec(width):
        return pl.BlockSpec((tm, width), lambda i: (jnp.minimum(i, n_tiles - 1), 0))

    def lead_spec(width):
        return pl.BlockSpec((BLOCK, width), lambda i: (0, 0))

    pos_spec = pl.BlockSpec((tm, RET_DK), lambda i: (i % tiles_per_seq, 0))
    out_widths = [D_MODEL, D_MODEL, 2 * D_MODEL, FOX_HEADS, RET_QK, RET_QK, D_MODEL, D_MODEL,
                  D_MODEL, D_MODEL]
    out_dtypes = [BF16, BF16, BF16, F32, BF16, BF16, BF16, BF16, BF16, BF16]
    lead_outs = [1, 2, 3, 5, 6]
    return pl.pallas_call(
        functools.partial(_proj_body, n_tiles=n_tiles, tiles_per_seq=tiles_per_seq),
        grid=(n_tiles + 1,),
        in_specs=[rows_spec(D_MODEL), lead_spec(D_MODEL)] + [_const_spec(a.shape) for a in consts]
        + [pos_spec, pos_spec, lead_spec(RET_DK), lead_spec(RET_DK), _const_spec(tri.shape),
           _const_spec(tri_lead.shape)],
        out_specs=[rows_spec(w) for w in out_widths] + [lead_spec(out_widths[j]) for j in lead_outs],
        out_shape=[jax.ShapeDtypeStruct((rows, w), dt) for w, dt in zip(out_widths, out_dtypes)]
        + [jax.ShapeDtypeStruct((BLOCK, out_widths[j]), out_dtypes[j]) for j in lead_outs],
        scratch_shapes=[pltpu.VMEM((1, V7X_LANES), F32)],
        compiler_params=_params(("arbitrary",)),
        name="proj",
    )(h, h_lead, *consts, cos[BLOCK:], sin[BLOCK:], cos, sin, tri, tri_lead)


def _fox_body(stab_ref, cs_ref, ce_ref, q_ref, cq_ref, k_ref, v_ref, ck_ref, kl_ref, vl_ref, cl_ref,
              *rest, n_cast, n_cast_short, short_steps):
    srcs, (o_ref, *dsts), (m_ref, acc_ref, e_ref) = (rest[:n_cast], rest[n_cast:2 * n_cast + 1],
                                                     rest[2 * n_cast + 1:])
    step = pl.program_id(0) * pl.num_programs(1) + pl.program_id(1)
    for src, dst in zip(srcs[:n_cast - n_cast_short], dsts):
        dst[...] = src[...].astype(BF16)

    @pl.when(step < short_steps)
    def _():
        for src, dst in zip(srcs[n_cast - n_cast_short:], dsts[n_cast - n_cast_short:]):
            dst[...] = src[...].astype(BF16)

    _fox_attend(stab_ref, cs_ref, ce_ref, q_ref, cq_ref, k_ref, v_ref, ck_ref, kl_ref, vl_ref, cl_ref,
                o_ref, m_ref, acc_ref, e_ref)


def _fox_attend(stab_ref, cs_ref, ce_ref, q_ref, cq_ref, k_ref, v_ref, ck_ref, kl_ref, vl_ref, cl_ref,
                o_ref, m_ref, acc_ref, e_ref):
    tq = m_ref.shape[1]
    tk = tq // 2
    tr = tq // 4
    lane = lax.broadcasted_iota(jnp.int32, (1, 2 * V7X_LANES), 1)
    first2 = (lane % V7X_LANES) < FOX_HD
    first = first2[:, :V7X_LANES]
    lead_ok = lax.broadcasted_iota(jnp.int32, (tq, BLOCK), 1) >= N_EMPTY
    row = lax.broadcasted_iota(jnp.int32, (tq, tk), 0)
    col = lax.broadcasted_iota(jnp.int32, (tq, tk), 1)
    causal = col <= row
    corner = causal[:tr, :tr]
    use_bound = stab_ref[0] > 0.5

    def keys(j):
        ks = pl.multiple_of(j * tk, tk)
        return k_ref[pl.ds(ks, tk), :], ck_ref[:, pl.ds(ks, tk)]

    def values(j):
        return v_ref[pl.ds(pl.multiple_of(j * tk, tk), tk), :]

    nq = q_ref.shape[0] // tq
    nk = k_ref.shape[0] // tk

    def query_block(i, carry):
        _fox_query_block(i, tq, tk, tr, nq, nk, first, first2, lead_ok, row, col, causal, corner,
                         use_bound, keys, values, stab_ref, cs_ref, ce_ref, q_ref, cq_ref, kl_ref,
                         vl_ref, cl_ref, o_ref, m_ref, acc_ref, e_ref)
        return carry

    lax.fori_loop(0, nq, query_block, 0)


def _fox_query_block(i, tq, tk, tr, nq, nk, first, first2, lead_ok, row, col, causal, corner,
                     use_bound, keys, values, stab_ref, cs_ref, ce_ref, q_ref, cq_ref, kl_ref,
                     vl_ref, cl_ref, o_ref, m_ref, acc_ref, e_ref):
    rows = pl.ds(pl.multiple_of(i * tq, tq), tq)
    n_full = 2 * i
    q = q_ref[rows, :]
    zero = jnp.zeros_like(q)
    q_heads = (jnp.where(first, q, zero), jnp.where(first, zero, q))
    acc_ref[...] = jnp.zeros_like(acc_ref)

    @pl.when(use_bound)
    def _():
        head = lax.broadcasted_iota(jnp.int32, (1, FOX_HEADS), 1) - 2 * pl.program_id(1)
        cq_blk = cq_ref[rows, :]
        cq = [jnp.sum(jnp.where(head == hh, cq_blk, 0.0), axis=1, keepdims=True) - stab_ref[1]
              for hh in range(2)]

        def exponents(k, ck, r0=0, r1=tq):
            return [_dot_nt(q_heads[hh][r0:r1], k) + cq[hh][r0:r1] - ck[hh:hh + 1, :] for hh in range(2)]

        def accumulate(e, v, allowed, r0=0):
            if allowed is not None:
                e = [jnp.where(allowed, x, NEG) for x in e]
            pvs = [_dot(jnp.exp2(x).astype(BF16), v) for x in e]
            r1 = r0 + e[0].shape[0]
            acc_ref[r0:r1, :] += jnp.where(first2, pvs[0], pvs[1])

        cutoff = -stab_ref[2]
        slot = 2 * pl.program_id(1)
        cs_base = (pl.program_id(0) * nq + i) * FOX_HEADS
        ce_base = pl.program_id(0) * nk * FOX_HEADS
        c_first = [cs_ref[cs_base + slot + hh] for hh in range(2)]
        skip_lead = jnp.maximum(c_first[0], c_first[1]) < cutoff
        j0 = jnp.int32(0)
        for j in range(nk - 2):
            gap = jnp.maximum(c_first[0] - ce_ref[ce_base + j * FOX_HEADS + slot],
                              c_first[1] - ce_ref[ce_base + j * FOX_HEADS + slot + 1])
            j0 += jnp.logical_and(gap < cutoff, j < n_full).astype(jnp.int32)

        e0 = exponents(*keys(j0))
        e_ref[0] = e0[0]
        e_ref[1] = e0[1]

        @pl.when(jnp.logical_not(skip_lead))
        def _():
            accumulate(exponents(kl_ref[...], cl_ref[...]), vl_ref[...], lead_ok)

        def step(j, carry):
            e_cur = [e_ref[0], e_ref[1]]
            e_next = exponents(*keys(j + 1))
            accumulate(e_cur, values(j), None)
            e_ref[0] = e_next[0]
            e_ref[1] = e_next[1]
            return carry

        def step_pair(m, carry):
            return step(2 * m + 1, step(2 * m, carry))

        @pl.when(j0 % 2 == 1)
        def _():
            step(j0, 0)

        lax.fori_loop((j0 + 1) // 2, i, step_pair, 0)

        kb, ckb = keys(n_full + 1)
        vb = values(n_full + 1)
        e_b1 = exponents(kb[:tr], ckb[:, :tr], 2 * tr, tq)
        e_b2 = exponents(kb[tr:], ckb[:, tr:], 3 * tr, tq)
        va = values(n_full)
        e_a = [e_ref[0], e_ref[1]]
        strips = [
            [([x[:tr, :tr] for x in e_a], va[:tr], corner)],
            [([x[tr:2 * tr] for x in e_a], va, causal[tr:2 * tr])],
            [([x[2 * tr:3 * tr] for x in e_a], va, None), ([x[:tr] for x in e_b1], vb[:tr], corner)],
            [([x[3 * tr:] for x in e_a], va, None), ([x[tr:] for x in e_b1], vb[:tr], None),
             (e_b2, vb[tr:], corner)],
        ]
        for s, parts in enumerate(strips):
            pv = [None, None]
            for e, v, allowed in parts:
                if allowed is not None:
                    e = [jnp.where(allowed, x, NEG) for x in e]
                for hh in range(2):
                    d = _dot(jnp.exp2(e[hh]).astype(BF16), v)
                    pv[hh] = d if pv[hh] is None else pv[hh] + d
            acc_ref[s * tr:(s + 1) * tr, :] += jnp.where(first2, pv[0], pv[1])

    @pl.when(jnp.logical_not(use_bound))
    def _():
        m_ref[...] = jnp.full_like(m_ref, NEG)

        def block(k, v, ck, allowed):
            pvs, alphas = [], []
            for hh in range(2):
                s = _dot_nt(q_heads[hh], k) - ck[hh:hh + 1, :]
                if allowed is not None:
                    s = jnp.where(allowed, s, NEG)
                m_old = m_ref[hh]
                m_new = jnp.maximum(m_old, jnp.max(s, axis=1, keepdims=True))
                m_ref[hh] = m_new
                pvs.append(_dot(jnp.exp2(s - m_new).astype(BF16), v))
                alphas.append(jnp.exp2(m_old - m_new))
            acc_ref[...] = (acc_ref[...] * jnp.where(first2, alphas[0], alphas[1])
                            + jnp.where(first2, pvs[0], pvs[1]))

        block(kl_ref[...], vl_ref[...], cl_ref[...], lead_ok)

        def full_block(j, carry):
            block(keys(j)[0], values(j), keys(j)[1], None)
            return carry

        lax.fori_loop(0, n_full, full_block, 0)
        block(keys(n_full)[0], values(n_full), keys(n_full)[1], causal)
        block(keys(n_full + 1)[0], values(n_full + 1), keys(n_full + 1)[1], col + tk <= row)

    acc = acc_ref[...]
    o_ref[rows, :] = (acc[:, :V7X_LANES] / acc[:, V7X_LANES:]).astype(o_ref.dtype)


def _fox(stab, cs, ce, q, cq, k, v, ck, kl, vl, cl, weights, short_weight, batch, seq):
    tq = ATT_BLOCK
    w = V7X_LANES
    steps = batch * FOX_PAIRS
    cast_specs, cast_shapes = [], []
    for a in weights:
        slab = a.shape[0] // steps
        assert slab * steps == a.shape[0] and slab % BF16_SUBLANES == 0
        cast_specs.append(pl.BlockSpec((slab, a.shape[1]), lambda b, j: (b * FOX_PAIRS + j, 0)))
        cast_shapes.append(jax.ShapeDtypeStruct(a.shape, BF16))
    short_steps = short_weight.shape[0] // FF_CHUNK
    assert short_steps * FF_CHUNK == short_weight.shape[0] and short_steps <= steps
    cast_specs.append(pl.BlockSpec(
        (FF_CHUNK, short_weight.shape[1]),
        lambda b, j: (jnp.minimum(b * FOX_PAIRS + j, short_steps - 1), 0)))
    cast_shapes.append(jax.ShapeDtypeStruct(short_weight.shape, BF16))
    n_cast = len(cast_specs)
    return pl.pallas_call(
        functools.partial(_fox_body, n_cast=n_cast, n_cast_short=1, short_steps=short_steps),
        grid=(batch, FOX_PAIRS),
        in_specs=[
            pl.BlockSpec(memory_space=pltpu.SMEM),
            pl.BlockSpec(memory_space=pltpu.SMEM),
            pl.BlockSpec(memory_space=pltpu.SMEM),
            pl.BlockSpec((None, seq, w), lambda b, j: (b, 0, j)),
            pl.BlockSpec((None, seq, FOX_HEADS), lambda b, j: (b, 0, 0)),
            pl.BlockSpec((None, seq, w), lambda b, j: (b, 0, j)),
            pl.BlockSpec((None, seq, 2 * w), lambda b, j: (b, 0, j)),
            pl.BlockSpec((None, None, 2, seq), lambda b, j: (b, j, 0, 0)),
            pl.BlockSpec((BLOCK, w), lambda b, j: (0, j)),
            pl.BlockSpec((BLOCK, 2 * w), lambda b, j: (0, j)),
            pl.BlockSpec((None, 2, BLOCK), lambda b, j: (j, 0, 0)),
        ] + cast_specs,
        out_specs=[pl.BlockSpec((None, seq, w), lambda b, j: (b, 0, j))] + cast_specs,
        out_shape=[jax.ShapeDtypeStruct((batch, seq, D_MODEL), BF16)] + cast_shapes,
        scratch_shapes=[pltpu.VMEM((2, tq, 1), F32), pltpu.VMEM((tq, 2 * w), F32),
                        pltpu.VMEM((2, tq, tq // 2), F32)],
        compiler_params=_params(("arbitrary", "arbitrary")),
        name="fox",
    )(stab, cs, ce, q, cq, k, v, ck, kl, vl, cl, *weights, short_weight)


def _ret_body(q_ref, k_ref, v_ref, sg_ref, kl_ref, vl_ref, gn_ref, cd_ref, o_ref, kv_ref, state_ref):
    n_chunks = q_ref.shape[0] // BLOCK
    cd = cd_ref[...]
    gn = gn_ref[...]
    causal = (lax.broadcasted_iota(jnp.int32, (BLOCK, BLOCK), 1)
              <= lax.broadcasted_iota(jnp.int32, (BLOCK, BLOCK), 0))

    def rows(c):
        return pl.ds(pl.multiple_of(c * BLOCK, BLOCK), BLOCK)

    def summarise(c, carry):
        kv_ref[c] = _dot_tn(k_ref[rows(c), :], v_ref[rows(c), :])
        return carry

    lax.fori_loop(0, n_chunks, summarise, 0, unroll=RET_UNROLL)

    def advance(c, state):
        state_ref[c] = state.astype(BF16)
        return cd * (state + kv_ref[c])

    lead_state = cd * _dot_tn(kl_ref[...], vl_ref[...])
    lax.fori_loop(0, n_chunks, advance, lead_state, unroll=RET_UNROLL)

    def emit(c, carry):
        q = q_ref[rows(c), :]
        v = v_ref[rows(c), :]
        scores = jnp.where(causal, _dot_nt(q, k_ref[rows(c), :]), 0.0).astype(BF16)
        o = _dot(jnp.concatenate([scores, q], axis=1), jnp.concatenate([v, state_ref[c]], axis=0))
        mu = jnp.mean(o, axis=-1, keepdims=True)
        d = o - mu
        var = jnp.mean(d * d, axis=-1, keepdims=True)
        yn = d * lax.rsqrt(var + GN_EPS) * gn
        o_ref[rows(c), :] = (sg_ref[rows(c), :].astype(F32) * yn).astype(o_ref.dtype)
        return carry

    lax.fori_loop(0, n_chunks, emit, 0, unroll=RET_UNROLL)


def _ret(rq, rk, rv, sg, rkl, rvl, gn, cd, batch, seq):
    head_qk = pl.BlockSpec((None, seq, RET_DK), lambda b, h: (b, 0, h))
    head_v = pl.BlockSpec((None, seq, RET_DV), lambda b, h: (b, 0, h))
    return pl.pallas_call(
        _ret_body,
        grid=(batch, RET_HEADS),
        in_specs=[head_qk, head_qk, head_v, head_v,
                  pl.BlockSpec((BLOCK, RET_DK), lambda b, h: (0, h)),
                  pl.BlockSpec((BLOCK, RET_DV), lambda b, h: (0, h)),
                  pl.BlockSpec((1, RET_DV), lambda b, h: (0, h)),
                  pl.BlockSpec((None, 1, RET_DV), lambda b, h: (h, 0, 0))],
        out_specs=head_v,
        out_shape=jax.ShapeDtypeStruct((batch, seq, D_MODEL), BF16),
        scratch_shapes=[pltpu.VMEM((seq // BLOCK, RET_DK, RET_DV), F32),
                        pltpu.VMEM((seq // BLOCK, RET_DK, RET_DV), BF16)],
        compiler_params=_params(("arbitrary", "arbitrary")),
        name="ret",
    )(rq, rk, rv, sg, rkl, rvl, gn, cd)


def _out_body(ya_ref, yb_ref, ga_ref, gb_ref, h_ref, wa_ref, wb_ref, wo_ref, g_ref, win_ref,
              wout_ref, o_ref):
    mixed = (ga_ref[...].astype(F32) * _dot(ya_ref[...], wa_ref[...])
             + gb_ref[...].astype(F32) * _dot(yb_ref[...], wb_ref[...]))
    h = h_ref[...] + _dot(mixed.astype(BF16), wo_ref[...])
    xn = _rms_rows(h, g_ref[...]).astype(BF16)
    o_ref[...] = h + 0.5 * _swiglu(xn, win_ref, wout_ref)


def _out(ya, yb, ga, gb, h, consts, tm):
    rows = h.shape[0]
    row_spec = pl.BlockSpec((tm, D_MODEL), lambda i: (i, 0))
    return pl.pallas_call(
        _out_body,
        grid=(rows // tm,),
        in_specs=[row_spec] * 5 + [_const_spec(a.shape) for a in consts],
        out_specs=row_spec,
        out_shape=jax.ShapeDtypeStruct((rows, D_MODEL), F32),
        compiler_params=_params(("arbitrary",)),
        name="out",
    )(ya, yb, ga, gb, h, *consts)


def _ffn_weights(w_in, w_out, dtype):
    return w_in.astype(dtype), w_out.astype(dtype).reshape(N_FF_CHUNKS, FF_CHUNK, D_MODEL)


def _position_tables(seq):
    half = RET_DK // 2
    pos = np.arange(BLOCK + seq, dtype=np.float64) - N_EMPTY
    inv = ROPE_BASE ** (-np.arange(half, dtype=np.float64) / half)
    ang = pos[:, None] * inv[None, :]
    cos = np.concatenate([np.cos(ang), np.cos(ang)], axis=1)
    sin = np.concatenate([-np.sin(ang), np.sin(ang)], axis=1)
    return jnp.asarray(cos, dtype=F32), jnp.asarray(sin, dtype=F32)


def _retention_tables(rows):
    log_gamma = np.log1p(-np.exp2(-5.0 - np.arange(RET_HEADS, dtype=np.float64)))
    n = (np.arange(rows) % BLOCK + 1.0)[None, :, None]
    lg = log_gamma[:, None, None]
    shape = (RET_HEADS, rows, RET_DK)
    q_scale = np.broadcast_to(np.exp(lg * n), shape)
    k_scale = np.broadcast_to(np.exp(-lg * n) * RET_DK ** -0.5, shape)
    cd = np.broadcast_to(np.exp(log_gamma * BLOCK)[:, None, None], (RET_HEADS, 1, RET_DV))
    return tuple(jnp.asarray(t, dtype=F32) for t in (q_scale, k_scale, cd))


def kernel(x, meta_tokens, norm_ffn1, w_ffn1_in, w_ffn1_out, norm_mix, w_in, b_forget, b_gate,
           fox_q_norm, fox_k_norm, w_o_fox, ret_gn, w_o_ret, w_out, norm_ffn2, w_ffn2_in,
           w_ffn2_out):
    batch, seq, d = x.shape
    assert d == D_MODEL and seq % ATT_BLOCK == 0 and seq % ROW_TILE == 0
    assert norm_ffn1.shape[0] == 1, "one layer"
    tiles_per_seq = seq // ROW_TILE

    win1, wout1 = _ffn_weights(w_ffn1_in[0], w_ffn1_out[0], F32)
    grp = np.arange(V7X_MXU_DIM) // FOX_HD
    gmat = jnp.asarray((grp[:, None] == grp[None, :]) / FOX_HD, dtype=BF16)
    gq = jnp.tile(fox_q_norm[0], FOX_HEADS)[None, :]
    gk = jnp.tile(fox_k_norm[0], FOX_HEADS)[None, :]
    bf = jnp.pad(b_forget[0], (0, V7X_LANES - FOX_HEADS))[None, :]
    bg = b_gate[0][None, :]
    cos, sin = _position_tables(seq)
    idx = np.arange(ROW_TILE)
    tri_incl = jnp.asarray(idx[None, :] <= idx[:, None], dtype=BF16)
    idx = np.arange(BLOCK)
    tri_after = jnp.asarray(idx[None, :] > idx[:, None], dtype=BF16)

    lead = jnp.concatenate([jnp.zeros((N_EMPTY, d), x.dtype), meta_tokens.astype(x.dtype)], axis=0)
    xr = x.reshape(batch * seq, d)

    assert w_in.shape[2] == COL_END
    h1, h1l, wt = _ffn_and_cast(xr, lead, norm_ffn1, win1, wout1, w_in[0].T, ROW_TILE)
    q_scale, k_scale, chunk_decay = _retention_tables(ROW_TILE)
    proj_consts = (norm_mix, wt, gmat, gq, gk, bf, bg, q_scale, k_scale)
    q, k, v, c, rq, rk, rv, sg, ga, gb, kl, vl, cl, rkl, rvl = _proj(
        h1, h1l, proj_consts, cos, sin, tri_incl, tri_after, ROW_TILE, tiles_per_seq)

    def b3(a):
        return a.reshape(batch, seq, a.shape[-1])

    ck = b3(c).transpose(0, 2, 1).reshape(batch, FOX_PAIRS, 2, seq)
    clt = cl.T.reshape(FOX_PAIRS, 2, BLOCK)
    bound = FOX_HD ** 0.5 * jnp.max(jnp.abs(fox_q_norm[0])) * jnp.max(jnp.abs(fox_k_norm[0]))
    bound2 = bound * LOG2E
    stab = jnp.stack([(bound <= SAFE_LOGIT_BOUND).astype(F32), bound2, F32_EXP2_ZERO + 2.0 * bound2])
    c3 = b3(c)
    cs = c3[:, ::ATT_BLOCK, :].reshape(-1)
    ce = c3[:, ATT_BLOCK // 2 - 1::ATT_BLOCK // 2, :].reshape(-1)
    ya, win2, wof, wor, wo, wout2 = _fox(
        stab, cs, ce, b3(q), c3, b3(k), b3(v), ck, kl, vl, clt,
        (w_ffn2_in[0], w_o_fox[0], w_o_ret[0], w_out[0]), w_ffn2_out[0], batch, seq)
    yb = _ret(b3(rq), b3(rk), b3(rv), b3(sg), rkl, rvl, ret_gn, chunk_decay, batch, seq)

    out_consts = (wof, wor, wo, norm_ffn2, win2, wout2.reshape(N_FF_CHUNKS, FF_CHUNK, D_MODEL))
    out = _out(ya.reshape(batch * seq, d), yb.reshape(batch * seq, d), ga, gb, h1, out_consts,
               ROW_TILE)
    return out.reshape(batch, seq, d)
```

```python
import functools
import math

import jax
import jax.numpy as jnp
import numpy as np
from jax import lax
from jax.experimental import pallas as pl
from jax.experimental.pallas import tpu as pltpu

F32 = jnp.float32
BF16 = jnp.bfloat16

D_MODEL = 1024
N_META = 16
BLOCK = 128
N_EMPTY = BLOCK - N_META
FOX_HD = 64
FOX_HEADS = D_MODEL // FOX_HD
FOX_PAIRS = FOX_HEADS // 2
RET_HEADS = 4
RET_DK = D_MODEL // (2 * RET_HEADS)
RET_DV = 2 * RET_DK
RET_QK = RET_HEADS * RET_DK
D_FF = ((8 * D_MODEL // 3 + 127) // 128) * 128
EPS = 1e-6
GN_EPS = 1e-5
ROPE_BASE = 10000.0
NEG = -1e30
LOG2E = math.log2(math.e)
SAFE_LOGIT_BOUND = 30.0
F32_EXP2_ZERO = 151.0

V7X_LANES = 128
BF16_SUBLANES = 16
V7X_MXU_DIM = 256
V7X_VMEM_BYTES = 64 * 2**20
VMEM_COMPILER_RESERVE = 8 * 2**20
VMEM_LIMIT = V7X_VMEM_BYTES - VMEM_COMPILER_RESERVE

ROW_TILE = 512
FF_CHUNK = V7X_MXU_DIM
N_FF_CHUNKS = D_FF // FF_CHUNK
RET_UNROLL = 32
ATT_BLOCK = 1024


def _dot(a, b):
    return jnp.dot(a, b, preferred_element_type=F32)


def _dot_nt(a, b):
    return lax.dot_general(a, b, (((1,), (1,)), ((), ())), preferred_element_type=F32)


def _dot_tn(a, b):
    return lax.dot_general(a, b, (((0,), (0,)), ((), ())), preferred_element_type=F32)


def _const_spec(shape):
    zeros = (0,) * len(shape)
    return pl.BlockSpec(shape, lambda *_: zeros, pipeline_mode=pl.Buffered(1))


def _params(semantics, vmem_limit=VMEM_LIMIT):
    return pltpu.CompilerParams(dimension_semantics=semantics, vmem_limit_bytes=vmem_limit)


def _rms_rows(x, g):
    ms = jnp.mean(x * x, axis=-1, keepdims=True)
    return x * lax.rsqrt(ms + EPS) * g


def _swiglu(xn, win_ref, wout_ref, before_chunk=None):
    acc = None
    for c in range(N_FF_CHUNKS):
        if before_chunk is not None:
            before_chunk(c)
        a = _dot(xn, win_ref[:, c * FF_CHUNK:(c + 1) * FF_CHUNK].astype(BF16))
        b = _dot(xn, win_ref[:, D_FF + c * FF_CHUNK:D_FF + (c + 1) * FF_CHUNK].astype(BF16))
        hm = (a * jax.nn.sigmoid(a) * b).astype(BF16)
        d = _dot(hm, wout_ref[c].astype(BF16))
        acc = d if acc is None else acc + d
    return acc


def _ffn_body(x_ref, g_ref, win_ref, wout_ref, o_ref, before_chunk=None):
    x = x_ref[...]
    xn = _rms_rows(x, g_ref[...]).astype(BF16)
    o_ref[...] = x + 0.5 * _swiglu(xn, win_ref, wout_ref, before_chunk)


def _ffn_cast_body(x_ref, lead_ref, g_ref, win_hbm, wout_hbm, w_src, o_ref, ol_ref, w_dst,
                   win_ref, wout_ref, sem, *, n_tiles):
    i = pl.program_id(0)

    def chunk_copies(c):
        cols_a = pl.ds(c * FF_CHUNK, FF_CHUNK)
        cols_b = pl.ds(D_FF + c * FF_CHUNK, FF_CHUNK)
        return (pltpu.make_async_copy(win_hbm.at[:, cols_a], win_ref.at[:, cols_a], sem.at[0, c]),
                pltpu.make_async_copy(win_hbm.at[:, cols_b], win_ref.at[:, cols_b], sem.at[1, c]),
                pltpu.make_async_copy(wout_hbm.at[c], wout_ref.at[c], sem.at[2, c]))

    def wait_chunk(c):
        for copy in chunk_copies(c):
            copy.wait()

    @pl.when(i == 0)
    def _():
        for c in range(N_FF_CHUNKS):
            for copy in chunk_copies(c):
                copy.start()
        _ffn_body(x_ref, g_ref, win_ref, wout_ref, o_ref, wait_chunk)
        w_dst[...] = w_src[...].astype(BF16)

    @pl.when(jnp.logical_and(i > 0, i < n_tiles))
    def _():
        _ffn_body(x_ref, g_ref, win_ref, wout_ref, o_ref)
        w_dst[...] = w_src[...].astype(BF16)

    @pl.when(i == n_tiles)
    def _():
        _ffn_body(lead_ref, g_ref, win_ref, wout_ref, ol_ref)


def _ffn_and_cast(x, lead, g, win, wout, w_t, tm):
    rows = x.shape[0]
    n_tiles = rows // tm
    n = w_t.shape[0]
    slab = min(s for s in range(BF16_SUBLANES, n + 1, BF16_SUBLANES) if n % s == 0 and n // s <= n_tiles)
    cast_steps = n // slab
    row_spec = pl.BlockSpec((tm, D_MODEL), lambda i: (jnp.minimum(i, n_tiles - 1), 0))
    lead_spec = pl.BlockSpec(lead.shape, lambda i: (0, 0))
    slab_spec = pl.BlockSpec((slab, w_t.shape[1]), lambda i: (jnp.minimum(i, cast_steps - 1), 0))
    return pl.pallas_call(
        functools.partial(_ffn_cast_body, n_tiles=n_tiles),
        grid=(n_tiles + 1,),
        in_specs=[row_spec, lead_spec, _const_spec(g.shape), pl.BlockSpec(memory_space=pl.ANY),
                  pl.BlockSpec(memory_space=pl.ANY), slab_spec],
        out_specs=[row_spec, lead_spec, slab_spec],
        out_shape=[jax.ShapeDtypeStruct((rows, D_MODEL), F32), jax.ShapeDtypeStruct(lead.shape, F32),
                   jax.ShapeDtypeStruct(w_t.shape, BF16)],
        scratch_shapes=[pltpu.VMEM(win.shape, win.dtype), pltpu.VMEM(wout.shape, wout.dtype),
                        pltpu.SemaphoreType.DMA((3, N_FF_CHUNKS))],
        compiler_params=_params(("arbitrary",), V7X_VMEM_BYTES - VMEM_COMPILER_RESERVE // 2),
        name="ffn_cast",
    )(x, lead, g, win, wout, w_t)


def _group_mean_sq(x, gmat_ref):
    sq = (x * x).astype(BF16)
    w = V7X_MXU_DIM
    parts = [_dot(sq[:, i * w:(i + 1) * w], gmat_ref[...]) for i in range(D_MODEL // w)]
    return jnp.concatenate(parts, axis=1)


def _rotary_tile(x, cos, sin_signed):
    return x * cos + pltpu.roll(x, RET_DK // 2, axis=1) * sin_signed


COL_FQ = 0
COL_FK = COL_FQ + D_MODEL
COL_FV = COL_FK + D_MODEL
COL_FF = COL_FV + D_MODEL
COL_RQ = COL_FF + FOX_HEADS
COL_RK = COL_RQ + RET_QK
COL_RV = COL_RK + RET_QK
COL_RG = COL_RV + D_MODEL
COL_GA = COL_RG + D_MODEL
COL_GB = COL_GA + D_MODEL
COL_END = COL_GB + D_MODEL


def _proj_rows(h, cos, sin, tri, q_scale, k_scale, gmix_ref, wt_ref, gmat_ref, gq_ref, gk_ref, bf_ref,
               bg_ref, carry_ref, outs, *, lead):
    q_ref, k_ref, v_ref, c_ref, rq_ref, rk_ref, rv_ref, sg_ref, ga_ref, gb_ref = outs
    tm = h.shape[0]
    u = _rms_rows(h, gmix_ref[...]).astype(BF16)

    def project(c0, c1):
        return _dot_nt(u, wt_ref[c0:c1, :])

    if lead:
        row = lax.broadcasted_iota(jnp.int32, (tm, 1), 0)
        valid = row >= N_EMPTY
        vmask = valid.astype(F32)


    z = project(COL_FF, COL_FF + V7X_LANES) + bf_ref[...]
    logf = -(jnp.maximum(-z, 0.0) + jnp.log1p(jnp.exp(-jnp.abs(z))))
    if lead:
        logf = jnp.where(valid, logf, 0.0)
    lane = lax.broadcasted_iota(jnp.int32, (1, V7X_LANES), 1)
    logf = jnp.where(lane < FOX_HEADS, logf, 0.0)
    p1 = logf.astype(BF16).astype(F32)
    r1 = logf - p1
    p2 = r1.astype(BF16).astype(F32)
    p3 = (r1 - p2).astype(BF16).astype(F32)
    packed = (p1 + pltpu.roll(p2, FOX_HEADS, axis=1) + pltpu.roll(p3, 2 * FOX_HEADS, axis=1)).astype(BF16)

    fq = project(COL_FQ, COL_FK) if q_ref is not None else None
    fk = project(COL_FK, COL_FV)
    rq = project(COL_RQ, COL_RK) if rq_ref is not None else None
    rk = project(COL_RK, COL_RV)

    if q_ref is not None:
        qn = fq * lax.rsqrt(_group_mean_sq(fq, gmat_ref) + EPS) * gq_ref[...]
        q_ref[...] = (qn * (FOX_HD ** -0.5 * LOG2E)).astype(BF16)
    kn = fk * lax.rsqrt(_group_mean_sq(fk, gmat_ref) + EPS) * gk_ref[...]
    k_ref[...] = kn.astype(BF16)

    r = _dot(tri, packed)
    cum = r + (pltpu.roll(r, V7X_LANES - FOX_HEADS, axis=1)
               + pltpu.roll(r, V7X_LANES - 2 * FOX_HEADS, axis=1))
    cum = jnp.where(lane < FOX_HEADS, cum, 0.0)
    if lead:
        c = -cum
    else:
        c = cum + carry_ref[...]
        carry_ref[...] = c[tm - 1:tm, :]
    c_ref[...] = (c * LOG2E)[:, :FOX_HEADS]

    for hh in range(RET_HEADS):
        sl = slice(hh * RET_DK, (hh + 1) * RET_DK)
        if rq_ref is not None:
            rq_ref[:, sl] = (_rotary_tile(rq[:, sl], cos, sin) * q_scale(hh)).astype(BF16)
        kt = _rotary_tile(rk[:, sl], cos, sin) * k_scale(hh)
        if lead:
            kt = kt * vmask
        rk_ref[:, sl] = kt.astype(BF16)
    rv = project(COL_RV, COL_RG)
    if lead:
        rv = rv * vmask
    rv_ref[...] = rv.astype(BF16)
    if sg_ref is not None:
        rg = project(COL_RG, COL_GA)
        sg_ref[...] = (rg * jax.nn.sigmoid(rg)).astype(BF16)

    if ga_ref is not None:
        bg = bg_ref[...]
        ga_ref[...] = jax.nn.sigmoid(project(COL_GA, COL_GB) + bg[:, :D_MODEL]).astype(BF16)
        gb_ref[...] = jax.nn.sigmoid(project(COL_GB, COL_END) + bg[:, D_MODEL:]).astype(BF16)

    fv = project(COL_FV, COL_FF).astype(BF16)
    ones = jnp.ones((tm, V7X_LANES), BF16)
    for j in range(FOX_PAIRS):
        v_ref[:, 2 * j * V7X_LANES:(2 * j + 1) * V7X_LANES] = fv[:, j * V7X_LANES:(j + 1) * V7X_LANES]
        v_ref[:, (2 * j + 1) * V7X_LANES:(2 * j + 2) * V7X_LANES] = ones


def _proj_body(h_ref, hl_ref, gmix_ref, wt_ref, gmat_ref, gq_ref, gk_ref, bf_ref, bg_ref, qs_ref, ks_ref,
               cos_ref, sin_ref, cosl_ref, sinl_ref, tri_ref, tril_ref,
               q_ref, k_ref, v_ref, c_ref, rq_ref, rk_ref, rv_ref, sg_ref, ga_ref, gb_ref,
               kl_ref, vl_ref, cl_ref, rkl_ref, rvl_ref, carry_ref, *, n_tiles, tiles_per_seq):
    i = pl.program_id(0)
    consts = (gmix_ref, wt_ref, gmat_ref, gq_ref, gk_ref, bf_ref, bg_ref, carry_ref)

    @pl.when(i % tiles_per_seq == 0)
    def _():
        carry_ref[...] = jnp.zeros_like(carry_ref)

    @pl.when(i < n_tiles)
    def _():
        outs = (q_ref, k_ref, v_ref, c_ref, rq_ref, rk_ref, rv_ref, sg_ref, ga_ref, gb_ref)
        _proj_rows(h_ref[...], cos_ref[...], sin_ref[...], tri_ref[...], lambda hh: qs_ref[hh],
                   lambda hh: ks_ref[hh], *consts, outs, lead=False)

    @pl.when(i == n_tiles)
    def _():
        outs = (None, kl_ref, vl_ref, cl_ref, None, rkl_ref, rvl_ref, None, None, None)
        _proj_rows(hl_ref[...], cosl_ref[...], sinl_ref[...], tril_ref[...], lambda hh: qs_ref[hh, :BLOCK],
                   lambda hh: ks_ref[hh, :BLOCK], *consts, outs, lead=True)


def _proj(h, h_lead, consts, cos, sin, tri, tri_lead, tm, tiles_per_seq):
    rows = h.shape[0]
    n_tiles = rows // tm
    assert tm % BLOCK == 0

    def rows_spec(width):
        return pl.BlockSpec((tm, width), lambda i: (jnp.minimum(i, n_tiles - 1), 0))

    def lead_spec(width):
        return pl.BlockSpec((BLOCK, width), lambda i: (0, 0))

    pos_spec = pl.BlockSpec((tm, RET_DK), lambda i: (i % tiles_per_seq, 0))
    out_widths = [D_MODEL, D_MODEL, 2 * D_MODEL, FOX_HEADS, RET_QK, RET_QK, D_MODEL, D_MODEL,
                  D_MODEL, D_MODEL]
    out_dtypes = [BF16, BF16, BF16, F32, BF16, BF16, BF16, BF16, BF16, BF16]
    lead_outs = [1, 2, 3, 5, 6]
    return pl.pallas_call(
        functools.partial(_proj_body, n_tiles=n_tiles, tiles_per_seq=tiles_per_seq),
        grid=(n_tiles + 1,),
        in_specs=[rows_spec(D_MODEL), lead_spec(D_MODEL)] + [_const_spec(a.shape) for a in consts]
        + [pos_spec, pos_spec, lead_spec(RET_DK), lead_spec(RET_DK), _const_spec(tri.shape),
           _const_spec(tri_lead.shape)],
        out_specs=[rows_spec(w) for w in out_widths] + [lead_spec(out_widths[j]) for j in lead_outs],
        out_shape=[jax.ShapeDtypeStruct((rows, w), dt) for w, dt in zip(out_widths, out_dtypes)]
        + [jax.ShapeDtypeStruct((BLOCK, out_widths[j]), out_dtypes[j]) for j in lead_outs],
        scratch_shapes=[pltpu.VMEM((1, V7X_LANES), F32)],
        compiler_params=_params(("arbitrary",)),
        name="proj",
    )(h, h_lead, *consts, cos[BLOCK:], sin[BLOCK:], cos, sin, tri, tri_lead)


def _fox_body(stab_ref, cs_ref, ce_ref, q_ref, cq_ref, k_ref, v_ref, ck_ref, kl_ref, vl_ref, cl_ref,
              *rest, n_cast, n_cast_short, short_steps):
    srcs, (o_ref, *dsts), (m_ref, acc_ref, e_ref) = (rest[:n_cast], rest[n_cast:2 * n_cast + 1],
                                                     rest[2 * n_cast + 1:])
    step = pl.program_id(0) * pl.num_programs(1) + pl.program_id(1)
    for src, dst in zip(srcs[:n_cast - n_cast_short], dsts):
        dst[...] = src[...].astype(BF16)

    @pl.when(step < short_steps)
    def _():
        for src, dst in zip(srcs[n_cast - n_cast_short:], dsts[n_cast - n_cast_short:]):
            dst[...] = src[...].astype(BF16)

    _fox_attend(stab_ref, cs_ref, ce_ref, q_ref, cq_ref, k_ref, v_ref, ck_ref, kl_ref, vl_ref, cl_ref,
                o_ref, m_ref, acc_ref, e_ref)


def _fox_attend(stab_ref, cs_ref, ce_ref, q_ref, cq_ref, k_ref, v_ref, ck_ref, kl_ref, vl_ref, cl_ref,
                o_ref, m_ref, acc_ref, e_ref):
    tq = m_ref.shape[1]
    tk = tq // 2
    tr = tq // 4
    lane = lax.broadcasted_iota(jnp.int32, (1, 2 * V7X_LANES), 1)
    first2 = (lane % V7X_LANES) < FOX_HD
    first = first2[:, :V7X_LANES]
    lead_ok = lax.broadcasted_iota(jnp.int32, (tq, BLOCK), 1) >= N_EMPTY
    row = lax.broadcasted_iota(jnp.int32, (tq, tk), 0)
    col = lax.broadcasted_iota(jnp.int32, (tq, tk), 1)
    causal = col <= row
    corner = causal[:tr, :tr]
    use_bound = stab_ref[0] > 0.5

    def keys(j):
        ks = pl.multiple_of(j * tk, tk)
        return k_ref[pl.ds(ks, tk), :], ck_ref[:, pl.ds(ks, tk)]

    def values(j):
        return v_ref[pl.ds(pl.multiple_of(j * tk, tk), tk), :]

    nq = q_ref.shape[0] // tq
    nk = k_ref.shape[0] // tk

    def query_block(i, carry):
        _fox_query_block(i, tq, tk, tr, nq, nk, first, first2, lead_ok, row, col, causal, corner,
                         use_bound, keys, values, stab_ref, cs_ref, ce_ref, q_ref, cq_ref, kl_ref,
                         vl_ref, cl_ref, o_ref, m_ref, acc_ref, e_ref)
        return carry

    lax.fori_loop(0, nq, query_block, 0)


def _fox_query_block(i, tq, tk, tr, nq, nk, first, first2, lead_ok, row, col, causal, corner,
                     use_bound, keys, values, stab_ref, cs_ref, ce_ref, q_ref, cq_ref, kl_ref,
                     vl_ref, cl_ref, o_ref, m_ref, acc_ref, e_ref):
    rows = pl.ds(pl.multiple_of(i * tq, tq), tq)
    n_full = 2 * i
    q = q_ref[rows, :]
    zero = jnp.zeros_like(q)
    q_heads = (jnp.where(first, q, zero), jnp.where(first, zero, q))
    acc_ref[...] = jnp.zeros_like(acc_ref)

    @pl.when(use_bound)
    def _():
        head = lax.broadcasted_iota(jnp.int32, (1, FOX_HEADS), 1) - 2 * pl.program_id(1)
        cq_blk = cq_ref[rows, :]
        cq = [jnp.sum(jnp.where(head == hh, cq_blk, 0.0), axis=1, keepdims=True) - stab_ref[1]
              for hh in range(2)]

        def exponents(k, ck, r0=0, r1=tq):
            return [_dot_nt(q_heads[hh][r0:r1], k) + cq[hh][r0:r1] - ck[hh:hh + 1, :] for hh in range(2)]

        def accumulate(e, v, allowed, r0=0):
            if allowed is not None:
                e = [jnp.where(allowed, x, NEG) for x in e]
            pvs = [_dot(jnp.exp2(x).astype(BF16), v) for x in e]
            r1 = r0 + e[0].shape[0]
            acc_ref[r0:r1, :] += jnp.where(first2, pvs[0], pvs[1])

        cutoff = -stab_ref[2]
        slot = 2 * pl.program_id(1)
        cs_base = (pl.program_id(0) * nq + i) * FOX_HEADS
        ce_base = pl.program_id(0) * nk * FOX_HEADS
        c_first = [cs_ref[cs_base + slot + hh] for hh in range(2)]
        skip_lead = jnp.maximum(c_first[0], c_first[1]) < cutoff
        j0 = jnp.int32(0)
        for j in range(nk - 2):
            gap = jnp.maximum(c_first[0] - ce_ref[ce_base + j * FOX_HEADS + slot],
                              c_first[1] - ce_ref[ce_base + j * FOX_HEADS + slot + 1])
            j0 += jnp.logical_and(gap < cutoff, j < n_full).astype(jnp.int32)

        e0 = exponents(*keys(j0))
        e_ref[0] = e0[0]
        e_ref[1] = e0[1]

        @pl.when(jnp.logical_not(skip_lead))
        def _():
            accumulate(exponents(kl_ref[...], cl_ref[...]), vl_ref[...], lead_ok)

        def step(j, carry):
            e_cur = [e_ref[0], e_ref[1]]
            e_next = exponents(*keys(j + 1))
            accumulate(e_cur, values(j), None)
            e_ref[0] = e_next[0]
            e_ref[1] = e_next[1]
            return carry

        def step_pair(m, carry):
            return step(2 * m + 1, step(2 * m, carry))

        @pl.when(j0 % 2 == 1)
        def _():
            step(j0, 0)

        lax.fori_loop((j0 + 1) // 2, i, step_pair, 0)

        kb, ckb = keys(n_full + 1)
        vb = values(n_full + 1)
        e_b1 = exponents(kb[:tr], ckb[:, :tr], 2 * tr, tq)
        e_b2 = exponents(kb[tr:], ckb[:, tr:], 3 * tr, tq)
        va = values(n_full)
        e_a = [e_ref[0], e_ref[1]]
        strips = [
            [([x[:tr, :tr] for x in e_a], va[:tr], corner)],
            [([x[tr:2 * tr] for x in e_a], va, causal[tr:2 * tr])],
            [([x[2 * tr:3 * tr] for x in e_a], va, None), ([x[:tr] for x in e_b1], vb[:tr], corner)],
            [([x[3 * tr:] for x in e_a], va, None), ([x[tr:] for x in e_b1], vb[:tr], None),
             (e_b2, vb[tr:], corner)],
        ]
        for s, parts in enumerate(strips):
            pv = [None, None]
            for e, v, allowed in parts:
                if allowed is not None:
                    e = [jnp.where(allowed, x, NEG) for x in e]
                for hh in range(2):
                    d = _dot(jnp.exp2(e[hh]).astype(BF16), v)
                    pv[hh] = d if pv[hh] is None else pv[hh] + d
            acc_ref[s * tr:(s + 1) * tr, :] += jnp.where(first2, pv[0], pv[1])

    @pl.when(jnp.logical_not(use_bound))
    def _():
        m_ref[...] = jnp.full_like(m_ref, NEG)

        def block(k, v, ck, allowed):
            pvs, alphas = [], []
            for hh in range(2):
                s = _dot_nt(q_heads[hh], k) - ck[hh:hh + 1, :]
                if allowed is not None:
                    s = jnp.where(allowed, s, NEG)
                m_old = m_ref[hh]
                m_new = jnp.maximum(m_old, jnp.max(s, axis=1, keepdims=True))
                m_ref[hh] = m_new
                pvs.append(_dot(jnp.exp2(s - m_new).astype(BF16), v))
                alphas.append(jnp.exp2(m_old - m_new))
            acc_ref[...] = (acc_ref[...] * jnp.where(first2, alphas[0], alphas[1])
                            + jnp.where(first2, pvs[0], pvs[1]))

        block(kl_ref[...], vl_ref[...], cl_ref[...], lead_ok)

        def full_block(j, carry):
            block(keys(j)[0], values(j), keys(j)[1], None)
            return carry

        lax.fori_loop(0, n_full, full_block, 0)
        block(keys(n_full)[0], values(n_full), keys(n_full)[1], causal)
        block(keys(n_full + 1)[0], values(n_full + 1), keys(n_full + 1)[1], col + tk <= row)

    acc = acc_ref[...]
    o_ref[rows, :] = (acc[:, :V7X_LANES] / acc[:, V7X_LANES:]).astype(o_ref.dtype)


def _fox(stab, cs, ce, q, cq, k, v, ck, kl, vl, cl, weights, short_weight, batch, seq):
    tq = ATT_BLOCK
    w = V7X_LANES
    steps = batch * FOX_PAIRS
    cast_specs, cast_shapes = [], []
    for a in weights:
        slab = a.shape[0] // steps
        assert slab * steps == a.shape[0] and slab % BF16_SUBLANES == 0
        cast_specs.append(pl.BlockSpec((slab, a.shape[1]), lambda b, j: (b * FOX_PAIRS + j, 0)))
        cast_shapes.append(jax.ShapeDtypeStruct(a.shape, BF16))
    short_steps = short_weight.shape[0] // FF_CHUNK
    assert short_steps * FF_CHUNK == short_weight.shape[0] and short_steps <= steps
    cast_specs.append(pl.BlockSpec(
        (FF_CHUNK, short_weight.shape[1]),
        lambda b, j: (jnp.minimum(b * FOX_PAIRS + j, short_steps - 1), 0)))
    cast_shapes.append(jax.ShapeDtypeStruct(short_weight.shape, BF16))
    n_cast = len(cast_specs)
    return pl.pallas_call(
        functools.partial(_fox_body, n_cast=n_cast, n_cast_short=1, short_steps=short_steps),
        grid=(batch, FOX_PAIRS),
        in_specs=[
            pl.BlockSpec(memory_space=pltpu.SMEM),
            pl.BlockSpec(memory_space=pltpu.SMEM),
            pl.BlockSpec(memory_space=pltpu.SMEM),
            pl.BlockSpec((None, seq, w), lambda b, j: (b, 0, j)),
            pl.BlockSpec((None, seq, FOX_HEADS), lambda b, j: (b, 0, 0)),
            pl.BlockSpec((None, seq, w), lambda b, j: (b, 0, j)),
            pl.BlockSpec((None, seq, 2 * w), lambda b, j: (b, 0, j)),
            pl.BlockSpec((None, None, 2, seq), lambda b, j: (b, j, 0, 0)),
            pl.BlockSpec((BLOCK, w), lambda b, j: (0, j)),
            pl.BlockSpec((BLOCK, 2 * w), lambda b, j: (0, j)),
            pl.BlockSpec((None, 2, BLOCK), lambda b, j: (j, 0, 0)),
        ] + cast_specs,
        out_specs=[pl.BlockSpec((None, seq, w), lambda b, j: (b, 0, j))] + cast_specs,
        out_shape=[jax.ShapeDtypeStruct((batch, seq, D_MODEL), BF16)] + cast_shapes,
        scratch_shapes=[pltpu.VMEM((2, tq, 1), F32), pltpu.VMEM((tq, 2 * w), F32),
                        pltpu.VMEM((2, tq, tq // 2), F32)],
        compiler_params=_params(("arbitrary", "arbitrary")),
        name="fox",
    )(stab, cs, ce, q, cq, k, v, ck, kl, vl, cl, *weights, short_weight)


def _ret_body(q_ref, k_ref, v_ref, sg_ref, kl_ref, vl_ref, gn_ref, cd_ref, o_ref, kv_ref, state_ref):
    n_chunks = q_ref.shape[0] // BLOCK
    cd = cd_ref[...]
    gn = gn_ref[...]
    causal = (lax.broadcasted_iota(jnp.int32, (BLOCK, BLOCK), 1)
              <= lax.broadcasted_iota(jnp.int32, (BLOCK, BLOCK), 0))

    def rows(c):
        return pl.ds(pl.multiple_of(c * BLOCK, BLOCK), BLOCK)

    def summarise(c, carry):
        kv_ref[c] = _dot_tn(k_ref[rows(c), :], v_ref[rows(c), :])
        return carry

    lax.fori_loop(0, n_chunks, summarise, 0, unroll=RET_UNROLL)

    def advance(c, state):
        state_ref[c] = state.astype(BF16)
        return cd * (state + kv_ref[c])

    lead_state = cd * _dot_tn(kl_ref[...], vl_ref[...])
    lax.fori_loop(0, n_chunks, advance, lead_state, unroll=RET_UNROLL)

    def emit(c, carry):
        q = q_ref[rows(c), :]
        v = v_ref[rows(c), :]
        scores = jnp.where(causal, _dot_nt(q, k_ref[rows(c), :]), 0.0).astype(BF16)
        o = _dot(jnp.concatenate([scores, q], axis=1), jnp.concatenate([v, state_ref[c]], axis=0))
        mu = jnp.mean(o, axis=-1, keepdims=True)
        d = o - mu
        var = jnp.mean(d * d, axis=-1, keepdims=True)
        yn = d * lax.rsqrt(var + GN_EPS) * gn
        o_ref[rows(c), :] = (sg_ref[rows(c), :].astype(F32) * yn).astype(o_ref.dtype)
        return carry

    lax.fori_loop(0, n_chunks, emit, 0, unroll=RET_UNROLL)


def _ret(rq, rk, rv, sg, rkl, rvl, gn, cd, batch, seq):
    head_qk = pl.BlockSpec((None, seq, RET_DK), lambda b, h: (b, 0, h))
    head_v = pl.BlockSpec((None, seq, RET_DV), lambda b, h: (b, 0, h))
    return pl.pallas_call(
        _ret_body,
        grid=(batch, RET_HEADS),
        in_specs=[head_qk, head_qk, head_v, head_v,
                  pl.BlockSpec((BLOCK, RET_DK), lambda b, h: (0, h)),
                  pl.BlockSpec((BLOCK, RET_DV), lambda b, h: (0, h)),
                  pl.BlockSpec((1, RET_DV), lambda b, h: (0, h)),
                  pl.BlockSpec((None, 1, RET_DV), lambda b, h: (h, 0, 0))],
        out_specs=head_v,
        out_shape=jax.ShapeDtypeStruct((batch, seq, D_MODEL), BF16),
        scratch_shapes=[pltpu.VMEM((seq // BLOCK, RET_DK, RET_DV), F32),
                        pltpu.VMEM((seq // BLOCK, RET_DK, RET_DV), BF16)],
        compiler_params=_params(("arbitrary", "arbitrary")),
        name="ret",
    )(rq, rk, rv, sg, rkl, rvl, gn, cd)


def _out_body(ya_ref, yb_ref, ga_ref, gb_ref, h_ref, wa_hbm, wb_hbm, wo_hbm, g_ref, win_hbm,
              wout_hbm, o_ref, wa_ref, wb_ref, wo_ref, win_ref, wout_ref, sem):
    def square_copies():
        return [pltpu.make_async_copy(src, dst, sem.at[3, n])
                for n, (src, dst) in enumerate(((wa_hbm, wa_ref), (wb_hbm, wb_ref), (wo_hbm, wo_ref)))]

    def chunk_copies(c):
        cols_a = pl.ds(c * FF_CHUNK, FF_CHUNK)
        cols_b = pl.ds(D_FF + c * FF_CHUNK, FF_CHUNK)
        return (pltpu.make_async_copy(win_hbm.at[:, cols_a], win_ref.at[:, cols_a], sem.at[0, c]),
                pltpu.make_async_copy(win_hbm.at[:, cols_b], win_ref.at[:, cols_b], sem.at[1, c]),
                pltpu.make_async_copy(wout_hbm.at[c], wout_ref.at[c], sem.at[2, c]))

    def rows(first_step):
        def arrived(copies):
            if first_step:
                for copy in copies:
                    copy.wait()

        arrived(square_copies()[:2])
        mixed = (ga_ref[...].astype(F32) * _dot(ya_ref[...], wa_ref[...])
                 + gb_ref[...].astype(F32) * _dot(yb_ref[...], wb_ref[...]))
        arrived(square_copies()[2:])
        h = h_ref[...] + _dot(mixed.astype(BF16), wo_ref[...])
        xn = _rms_rows(h, g_ref[...]).astype(BF16)
        before_chunk = (lambda c: arrived(chunk_copies(c))) if first_step else None
        o_ref[...] = h + 0.5 * _swiglu(xn, win_ref, wout_ref, before_chunk)

    @pl.when(pl.program_id(0) == 0)
    def _():
        for copy in square_copies():
            copy.start()
        for c in range(N_FF_CHUNKS):
            for copy in chunk_copies(c):
                copy.start()
        rows(True)

    @pl.when(pl.program_id(0) > 0)
    def _():
        rows(False)


def _out(ya, yb, ga, gb, h, consts, tm):
    wa, wb, wo, g, win, wout = consts
    rows = h.shape[0]
    row_spec = pl.BlockSpec((tm, D_MODEL), lambda i: (i, 0))
    hbm = pl.BlockSpec(memory_space=pl.ANY)
    return pl.pallas_call(
        _out_body,
        grid=(rows // tm,),
        in_specs=[row_spec] * 5 + [hbm, hbm, hbm, _const_spec(g.shape), hbm, hbm],
        out_specs=row_spec,
        out_shape=jax.ShapeDtypeStruct((rows, D_MODEL), F32),
        scratch_shapes=[pltpu.VMEM(a.shape, a.dtype) for a in (wa, wb, wo, win, wout)]
        + [pltpu.SemaphoreType.DMA((4, N_FF_CHUNKS))],
        compiler_params=_params(("arbitrary",)),
        name="out",
    )(ya, yb, ga, gb, h, *consts)


def _ffn_weights(w_in, w_out, dtype):
    return w_in.astype(dtype), w_out.astype(dtype).reshape(N_FF_CHUNKS, FF_CHUNK, D_MODEL)


def _position_tables(seq):
    half = RET_DK // 2
    pos = np.arange(BLOCK + seq, dtype=np.float64) - N_EMPTY
    inv = ROPE_BASE ** (-np.arange(half, dtype=np.float64) / half)
    ang = pos[:, None] * inv[None, :]
    cos = np.concatenate([np.cos(ang), np.cos(ang)], axis=1)
    sin = np.concatenate([-np.sin(ang), np.sin(ang)], axis=1)
    return jnp.asarray(cos, dtype=F32), jnp.asarray(sin, dtype=F32)


def _retention_tables(rows):
    log_gamma = np.log1p(-np.exp2(-5.0 - np.arange(RET_HEADS, dtype=np.float64)))
    n = (np.arange(rows) % BLOCK + 1.0)[None, :, None]
    lg = log_gamma[:, None, None]
    shape = (RET_HEADS, rows, RET_DK)
    q_scale = np.broadcast_to(np.exp(lg * n), shape)
    k_scale = np.broadcast_to(np.exp(-lg * n) * RET_DK ** -0.5, shape)
    cd = np.broadcast_to(np.exp(log_gamma * BLOCK)[:, None, None], (RET_HEADS, 1, RET_DV))
    return tuple(jnp.asarray(t, dtype=F32) for t in (q_scale, k_scale, cd))


def kernel(x, meta_tokens, norm_ffn1, w_ffn1_in, w_ffn1_out, norm_mix, w_in, b_forget, b_gate,
           fox_q_norm, fox_k_norm, w_o_fox, ret_gn, w_o_ret, w_out, norm_ffn2, w_ffn2_in,
           w_ffn2_out):
    batch, seq, d = x.shape
    assert d == D_MODEL and seq % ATT_BLOCK == 0 and seq % ROW_TILE == 0
    assert norm_ffn1.shape[0] == 1, "one layer"
    tiles_per_seq = seq // ROW_TILE

    win1, wout1 = _ffn_weights(w_ffn1_in[0], w_ffn1_out[0], F32)
    grp = np.arange(V7X_MXU_DIM) // FOX_HD
    gmat = jnp.asarray((grp[:, None] == grp[None, :]) / FOX_HD, dtype=BF16)
    gq = jnp.tile(fox_q_norm[0], FOX_HEADS)[None, :]
    gk = jnp.tile(fox_k_norm[0], FOX_HEADS)[None, :]
    bf = jnp.pad(b_forget[0], (0, V7X_LANES - FOX_HEADS))[None, :]
    bg = b_gate[0][None, :]
    cos, sin = _position_tables(seq)
    idx = np.arange(ROW_TILE)
    tri_incl = jnp.asarray(idx[None, :] <= idx[:, None], dtype=BF16)
    idx = np.arange(BLOCK)
    tri_after = jnp.asarray(idx[None, :] > idx[:, None], dtype=BF16)

    lead = jnp.concatenate([jnp.zeros((N_EMPTY, d), x.dtype), meta_tokens.astype(x.dtype)], axis=0)
    xr = x.reshape(batch * seq, d)

    assert w_in.shape[2] == COL_END
    h1, h1l, wt = _ffn_and_cast(xr, lead, norm_ffn1, win1, wout1, w_in[0].T, ROW_TILE)
    q_scale, k_scale, chunk_decay = _retention_tables(ROW_TILE)
    proj_consts = (norm_mix, wt, gmat, gq, gk, bf, bg, q_scale, k_scale)
    q, k, v, c, rq, rk, rv, sg, ga, gb, kl, vl, cl, rkl, rvl = _proj(
        h1, h1l, proj_consts, cos, sin, tri_incl, tri_after, ROW_TILE, tiles_per_seq)

    def b3(a):
        return a.reshape(batch, seq, a.shape[-1])

    ck = b3(c).transpose(0, 2, 1).reshape(batch, FOX_PAIRS, 2, seq)
    clt = cl.T.reshape(FOX_PAIRS, 2, BLOCK)
    bound = FOX_HD ** 0.5 * jnp.max(jnp.abs(fox_q_norm[0])) * jnp.max(jnp.abs(fox_k_norm[0]))
    bound2 = bound * LOG2E
    stab = jnp.stack([(bound <= SAFE_LOGIT_BOUND).astype(F32), bound2, F32_EXP2_ZERO + 2.0 * bound2])
    c3 = b3(c)
    cs = c3[:, ::ATT_BLOCK, :].reshape(-1)
    ce = c3[:, ATT_BLOCK // 2 - 1::ATT_BLOCK // 2, :].reshape(-1)
    ya, win2, wof, wor, wo, wout2 = _fox(
        stab, cs, ce, b3(q), c3, b3(k), b3(v), ck, kl, vl, clt,
        (w_ffn2_in[0], w_o_fox[0], w_o_ret[0], w_out[0]), w_ffn2_out[0], batch, seq)
    yb = _ret(b3(rq), b3(rk), b3(rv), b3(sg), rkl, rvl, ret_gn, chunk_decay, batch, seq)

    out_consts = (wof, wor, wo, norm_ffn2, win2, wout2.reshape(N_FF_CHUNKS, FF_CHUNK, D_MODEL))
    out = _out(ya.reshape(batch * seq, d), yb.reshape(batch * seq, d), ga, gb, h1, out_consts,
               ROW_TILE)
    return out.reshape(batch, seq, d)
```

```python
import functools
import math

import jax
import jax.numpy as jnp
import numpy as np
from jax import lax
from jax.experimental import pallas as pl
from jax.experimental.pallas import tpu as pltpu

F32 = jnp.float32
BF16 = jnp.bfloat16

D_MODEL = 1024
N_META = 16
BLOCK = 128
N_EMPTY = BLOCK - N_META
FOX_HD = 64
FOX_HEADS = D_MODEL // FOX_HD
FOX_PAIRS = FOX_HEADS // 2
RET_HEADS = 4
RET_DK = D_MODEL // (2 * RET_HEADS)
RET_DV = 2 * RET_DK
RET_QK = RET_HEADS * RET_DK
D_FF = ((8 * D_MODEL // 3 + 127) // 128) * 128
EPS = 1e-6
GN_EPS = 1e-5
ROPE_BASE = 10000.0
NEG = -1e30
LOG2E = math.log2(math.e)
SAFE_LOGIT_BOUND = 30.0
F32_EXP2_ZERO = 151.0

V7X_LANES = 128
BF16_SUBLANES = 16
V7X_MXU_DIM = 256
V7X_VMEM_BYTES = 64 * 2**20
VMEM_COMPILER_RESERVE = 8 * 2**20
VMEM_LIMIT = V7X_VMEM_BYTES - VMEM_COMPILER_RESERVE

ROW_TILE = 512
FF_CHUNK = V7X_MXU_DIM
N_FF_CHUNKS = D_FF // FF_CHUNK
RET_UNROLL = 32
ATT_BLOCK = 1024


def _dot(a, b):
    return jnp.dot(a, b, preferred_element_type=F32)


def _dot_nt(a, b):
    return lax.dot_general(a, b, (((1,), (1,)), ((), ())), preferred_element_type=F32)


def _dot_tn(a, b):
    return lax.dot_general(a, b, (((0,), (0,)), ((), ())), preferred_element_type=F32)


def _const_spec(shape):
    zeros = (0,) * len(shape)
    return pl.BlockSpec(shape, lambda *_: zeros, pipeline_mode=pl.Buffered(1))


def _params(semantics, vmem_limit=VMEM_LIMIT):
    return pltpu.CompilerParams(dimension_semantics=semantics, vmem_limit_bytes=vmem_limit)


def _sigmoid(x):
    return 0.5 * jnp.tanh(0.5 * x) + 0.5


def _rms_rows(x, g):
    ms = jnp.mean(x * x, axis=-1, keepdims=True)
    return x * lax.rsqrt(ms + EPS) * g


def _swiglu(xn, win_ref, wout_ref, before_chunk=None):
    acc = None
    for c in range(N_FF_CHUNKS):
        if before_chunk is not None:
            before_chunk(c)
        a = _dot(xn, win_ref[:, c * FF_CHUNK:(c + 1) * FF_CHUNK].astype(BF16))
        b = _dot(xn, win_ref[:, D_FF + c * FF_CHUNK:D_FF + (c + 1) * FF_CHUNK].astype(BF16))
        hm = (a * _sigmoid(a) * b).astype(BF16)
        d = _dot(hm, wout_ref[c].astype(BF16))
        acc = d if acc is None else acc + d
    return acc


def _ffn_body(x_ref, g_ref, win_ref, wout_ref, o_ref, before_chunk=None):
    x = x_ref[...]
    xn = _rms_rows(x, g_ref[...]).astype(BF16)
    o_ref[...] = x + 0.5 * _swiglu(xn, win_ref, wout_ref, before_chunk)


def _ffn_cast_body(x_ref, lead_ref, g_ref, win_hbm, wout_hbm, w_src, o_ref, ol_ref, w_dst,
                   win_ref, wout_ref, sem, *, n_tiles):
    i = pl.program_id(0)

    def chunk_copies(c):
        cols_a = pl.ds(c * FF_CHUNK, FF_CHUNK)
        cols_b = pl.ds(D_FF + c * FF_CHUNK, FF_CHUNK)
        return (pltpu.make_async_copy(win_hbm.at[:, cols_a], win_ref.at[:, cols_a], sem.at[0, c]),
                pltpu.make_async_copy(win_hbm.at[:, cols_b], win_ref.at[:, cols_b], sem.at[1, c]),
                pltpu.make_async_copy(wout_hbm.at[c], wout_ref.at[c], sem.at[2, c]))

    def wait_chunk(c):
        for copy in chunk_copies(c):
            copy.wait()

    @pl.when(i == 0)
    def _():
        for c in range(N_FF_CHUNKS):
            for copy in chunk_copies(c):
                copy.start()
        _ffn_body(x_ref, g_ref, win_ref, wout_ref, o_ref, wait_chunk)
        w_dst[...] = w_src[...].astype(BF16)

    @pl.when(jnp.logical_and(i > 0, i < n_tiles))
    def _():
        _ffn_body(x_ref, g_ref, win_ref, wout_ref, o_ref)
        w_dst[...] = w_src[...].astype(BF16)

    @pl.when(i == n_tiles)
    def _():
        _ffn_body(lead_ref, g_ref, win_ref, wout_ref, ol_ref)


def _ffn_and_cast(x, lead, g, win, wout, w_t, tm):
    rows = x.shape[0]
    n_tiles = rows // tm
    n = w_t.shape[0]
    slab = min(s for s in range(BF16_SUBLANES, n + 1, BF16_SUBLANES) if n % s == 0 and n // s <= n_tiles)
    cast_steps = n // slab
    row_spec = pl.BlockSpec((tm, D_MODEL), lambda i: (jnp.minimum(i, n_tiles - 1), 0))
    lead_spec = pl.BlockSpec(lead.shape, lambda i: (0, 0))
    slab_spec = pl.BlockSpec((slab, w_t.shape[1]), lambda i: (jnp.minimum(i, cast_steps - 1), 0))
    return pl.pallas_call(
        functools.partial(_ffn_cast_body, n_tiles=n_tiles),
        grid=(n_tiles + 1,),
        in_specs=[row_spec, lead_spec, _const_spec(g.shape), pl.BlockSpec(memory_space=pl.ANY),
                  pl.BlockSpec(memory_space=pl.ANY), slab_spec],
        out_specs=[row_spec, lead_spec, slab_spec],
        out_shape=[jax.ShapeDtypeStruct((rows, D_MODEL), F32), jax.ShapeDtypeStruct(lead.shape, F32),
                   jax.ShapeDtypeStruct(w_t.shape, BF16)],
        scratch_shapes=[pltpu.VMEM(win.shape, win.dtype), pltpu.VMEM(wout.shape, wout.dtype),
                        pltpu.SemaphoreType.DMA((3, N_FF_CHUNKS))],
        compiler_params=_params(("arbitrary",), V7X_VMEM_BYTES - VMEM_COMPILER_RESERVE // 2),
        name="ffn_cast",
    )(x, lead, g, win, wout, w_t)


def _group_mean_sq(x, gmat_ref):
    sq = (x * x).astype(BF16)
    w = V7X_MXU_DIM
    parts = [_dot(sq[:, i * w:(i + 1) * w], gmat_ref[...]) for i in range(D_MODEL // w)]
    return jnp.concatenate(parts, axis=1)


def _rotary_tile(x, cos, sin_signed):
    return x * cos + pltpu.roll(x, RET_DK // 2, axis=1) * sin_signed


COL_FQ = 0
COL_FK = COL_FQ + D_MODEL
COL_FV = COL_FK + D_MODEL
COL_FF = COL_FV + D_MODEL
COL_RQ = COL_FF + FOX_HEADS
COL_RK = COL_RQ + RET_QK
COL_RV = COL_RK + RET_QK
COL_RG = COL_RV + D_MODEL
COL_GA = COL_RG + D_MODEL
COL_GB = COL_GA + D_MODEL
COL_END = COL_GB + D_MODEL


def _proj_rows(h, cos, sin, tri, q_scale, k_scale, gmix_ref, wt_ref, gmat_ref, gq_ref, gk_ref, bf_ref,
               bg_ref, carry_ref, outs, *, lead):
    q_ref, k_ref, v_ref, c_ref, rq_ref, rk_ref, rv_ref, sg_ref, ga_ref, gb_ref = outs
    tm = h.shape[0]
    u = _rms_rows(h, gmix_ref[...]).astype(BF16)

    def project(c0, c1):
        return _dot_nt(u, wt_ref[c0:c1, :])

    if lead:
        row = lax.broadcasted_iota(jnp.int32, (tm, 1), 0)
        valid = row >= N_EMPTY
        vmask = valid.astype(F32)


    z = project(COL_FF, COL_FF + V7X_LANES) + bf_ref[...]
    logf = -(jnp.maximum(-z, 0.0) + jnp.log1p(jnp.exp(-jnp.abs(z))))
    if lead:
        logf = jnp.where(valid, logf, 0.0)
    lane = lax.broadcasted_iota(jnp.int32, (1, V7X_LANES), 1)
    logf = jnp.where(lane < FOX_HEADS, logf, 0.0)
    p1 = logf.astype(BF16).astype(F32)
    r1 = logf - p1
    p2 = r1.astype(BF16).astype(F32)
    p3 = (r1 - p2).astype(BF16).astype(F32)
    packed = (p1 + pltpu.roll(p2, FOX_HEADS, axis=1) + pltpu.roll(p3, 2 * FOX_HEADS, axis=1)).astype(BF16)

    fq = project(COL_FQ, COL_FK) if q_ref is not None else None
    fk = project(COL_FK, COL_FV)
    rq = project(COL_RQ, COL_RK) if rq_ref is not None else None
    rk = project(COL_RK, COL_RV)

    if q_ref is not None:
        qn = fq * lax.rsqrt(_group_mean_sq(fq, gmat_ref) + EPS) * gq_ref[...]
        q_ref[...] = (qn * (FOX_HD ** -0.5 * LOG2E)).astype(BF16)
    kn = fk * lax.rsqrt(_group_mean_sq(fk, gmat_ref) + EPS) * gk_ref[...]
    k_ref[...] = kn.astype(BF16)

    r = _dot(tri, packed)
    cum = r + (pltpu.roll(r, V7X_LANES - FOX_HEADS, axis=1)
               + pltpu.roll(r, V7X_LANES - 2 * FOX_HEADS, axis=1))
    cum = jnp.where(lane < FOX_HEADS, cum, 0.0)
    if lead:
        c = -cum
    else:
        c = cum + carry_ref[...]
        carry_ref[...] = c[tm - 1:tm, :]
    c_ref[...] = (c * LOG2E)[:, :FOX_HEADS]

    for hh in range(RET_HEADS):
        sl = slice(hh * RET_DK, (hh + 1) * RET_DK)
        if rq_ref is not None:
            rq_ref[:, sl] = (_rotary_tile(rq[:, sl], cos, sin) * q_scale(hh)).astype(BF16)
        kt = _rotary_tile(rk[:, sl], cos, sin) * k_scale(hh)
        if lead:
            kt = kt * vmask
        rk_ref[:, sl] = kt.astype(BF16)
    rv = project(COL_RV, COL_RG)
    if lead:
        rv = rv * vmask
    rv_ref[...] = rv.astype(BF16)
    if sg_ref is not None:
        rg = project(COL_RG, COL_GA)
        sg_ref[...] = (rg * _sigmoid(rg)).astype(BF16)

    if ga_ref is not None:
        bg = bg_ref[...]
        ga_ref[...] = _sigmoid(project(COL_GA, COL_GB) + bg[:, :D_MODEL]).astype(BF16)
        gb_ref[...] = _sigmoid(project(COL_GB, COL_END) + bg[:, D_MODEL:]).astype(BF16)

    fv = project(COL_FV, COL_FF).astype(BF16)
    ones = jnp.ones((tm, V7X_LANES), BF16)
    for j in range(FOX_PAIRS):
        v_ref[:, 2 * j * V7X_LANES:(2 * j + 1) * V7X_LANES] = fv[:, j * V7X_LANES:(j + 1) * V7X_LANES]
        v_ref[:, (2 * j + 1) * V7X_LANES:(2 * j + 2) * V7X_LANES] = ones


def _proj_body(h_ref, hl_ref, gmix_ref, wt_ref, gmat_ref, gq_ref, gk_ref, bf_ref, bg_ref, qs_ref, ks_ref,
               cos_ref, sin_ref, cosl_ref, sinl_ref, tri_ref, tril_ref,
               q_ref, k_ref, v_ref, c_ref, rq_ref, rk_ref, rv_ref, sg_ref, ga_ref, gb_ref,
               kl_ref, vl_ref, cl_ref, rkl_ref, rvl_ref, carry_ref, *, n_tiles, tiles_per_seq):
    i = pl.program_id(0)
    consts = (gmix_ref, wt_ref, gmat_ref, gq_ref, gk_ref, bf_ref, bg_ref, carry_ref)

    @pl.when(i % tiles_per_seq == 0)
    def _():
        carry_ref[...] = jnp.zeros_like(carry_ref)

    @pl.when(i < n_tiles)
    def _():
        outs = (q_ref, k_ref, v_ref, c_ref, rq_ref, rk_ref, rv_ref, sg_ref, ga_ref, gb_ref)
        _proj_rows(h_ref[...], cos_ref[...], sin_ref[...], tri_ref[...], lambda hh: qs_ref[hh],
                   lambda hh: ks_ref[hh], *consts, outs, lead=False)

    @pl.when(i == n_tiles)
    def _():
        outs = (None, kl_ref, vl_ref, cl_ref, None, rkl_ref, rvl_ref, None, None, None)
        _proj_rows(hl_ref[...], cosl_ref[...], sinl_ref[...], tril_ref[...], lambda hh: qs_ref[hh, :BLOCK],
                   lambda hh: ks_ref[hh, :BLOCK], *consts, outs, lead=True)


def _proj(h, h_lead, consts, cos, sin, tri, tri_lead, tm, tiles_per_seq):
    rows = h.shape[0]
    n_tiles = rows // tm
    assert tm % BLOCK == 0

    def rows_spec(width):
        return pl.BlockSpec((tm, width), lambda i: (jnp.minimum(i, n_tiles - 1), 0))

    def lead_spec(width):
        return pl.BlockSpec((BLOCK, width), lambda i: (0, 0))

    pos_spec = pl.BlockSpec((tm, RET_DK), lambda i: (i % tiles_per_seq, 0))
    out_widths = [D_MODEL, D_MODEL, 2 * D_MODEL, FOX_HEADS, RET_QK, RET_QK, D_MODEL, D_MODEL,
                  D_MODEL, D_MODEL]
    out_dtypes = [BF16, BF16, BF16, F32, BF16, BF16, BF16, BF16, BF16, BF16]
    lead_outs = [1, 2, 3, 5, 6]
    return pl.pallas_call(
        functools.partial(_proj_body, n_tiles=n_tiles, tiles_per_seq=tiles_per_seq),
        grid=(n_tiles + 1,),
        in_specs=[rows_spec(D_MODEL), lead_spec(D_MODEL)] + [_const_spec(a.shape) for a in consts]
        + [pos_spec, pos_spec, lead_spec(RET_DK), lead_spec(RET_DK), _const_spec(tri.shape),
           _const_spec(tri_lead.shape)],
        out_specs=[rows_spec(w) for w in out_widths] + [lead_spec(out_widths[j]) for j in lead_outs],
        out_shape=[jax.ShapeDtypeStruct((rows, w), dt) for w, dt in zip(out_widths, out_dtypes)]
        + [jax.ShapeDtypeStruct((BLOCK, out_widths[j]), out_dtypes[j]) for j in lead_outs],
        scratch_shapes=[pltpu.VMEM((1, V7X_LANES), F32)],
        compiler_params=_params(("arbitrary",)),
        name="proj",
    )(h, h_lead, *consts, cos[BLOCK:], sin[BLOCK:], cos, sin, tri, tri_lead)


def _fox_body(stab_ref, cs_ref, ce_ref, q_ref, cq_ref, k_ref, v_ref, ck_ref, kl_ref, vl_ref, cl_ref,
              *rest, n_cast, n_cast_short, short_steps):
    srcs, (o_ref, *dsts), (m_ref, acc_ref, e_ref) = (rest[:n_cast], rest[n_cast:2 * n_cast + 1],
                                                     rest[2 * n_cast + 1:])
    step = pl.program_id(0) * pl.num_programs(1) + pl.program_id(1)
    for src, dst in zip(srcs[:n_cast - n_cast_short], dsts):
        dst[...] = src[...].astype(BF16)

    @pl.when(step < short_steps)
    def _():
        for src, dst in zip(srcs[n_cast - n_cast_short:], dsts[n_cast - n_cast_short:]):
            dst[...] = src[...].astype(BF16)

    _fox_attend(stab_ref, cs_ref, ce_ref, q_ref, cq_ref, k_ref, v_ref, ck_ref, kl_ref, vl_ref, cl_ref,
                o_ref, m_ref, acc_ref, e_ref)


def _fox_attend(stab_ref, cs_ref, ce_ref, q_ref, cq_ref, k_ref, v_ref, ck_ref, kl_ref, vl_ref, cl_ref,
                o_ref, m_ref, acc_ref, e_ref):
    tq = m_ref.shape[1]
    tk = tq // 2
    tr = tq // 4
    lane = lax.broadcasted_iota(jnp.int32, (1, 2 * V7X_LANES), 1)
    first2 = (lane % V7X_LANES) < FOX_HD
    first = first2[:, :V7X_LANES]
    lead_ok = lax.broadcasted_iota(jnp.int32, (tq, BLOCK), 1) >= N_EMPTY
    row = lax.broadcasted_iota(jnp.int32, (tq, tk), 0)
    col = lax.broadcasted_iota(jnp.int32, (tq, tk), 1)
    causal = col <= row
    corner = causal[:tr, :tr]
    use_bound = stab_ref[0] > 0.5

    def keys(j):
        ks = pl.multiple_of(j * tk, tk)
        return k_ref[pl.ds(ks, tk), :], ck_ref[:, pl.ds(ks, tk)]

    def values(j):
        return v_ref[pl.ds(pl.multiple_of(j * tk, tk), tk), :]

    nq = q_ref.shape[0] // tq
    nk = k_ref.shape[0] // tk

    def query_block(i, carry):
        _fox_query_block(i, tq, tk, tr, nq, nk, first, first2, lead_ok, row, col, causal, corner,
                         use_bound, keys, values, stab_ref, cs_ref, ce_ref, q_ref, cq_ref, kl_ref,
                         vl_ref, cl_ref, o_ref, m_ref, acc_ref, e_ref)
        return carry

    lax.fori_loop(0, nq, query_block, 0)


def _fox_query_block(i, tq, tk, tr, nq, nk, first, first2, lead_ok, row, col, causal, corner,
                     use_bound, keys, values, stab_ref, cs_ref, ce_ref, q_ref, cq_ref, kl_ref,
                     vl_ref, cl_ref, o_ref, m_ref, acc_ref, e_ref):
    rows = pl.ds(pl.multiple_of(i * tq, tq), tq)
    n_full = 2 * i
    q = q_ref[rows, :]
    zero = jnp.zeros_like(q)
    q_heads = (jnp.where(first, q, zero), jnp.where(first, zero, q))
    acc_ref[...] = jnp.zeros_like(acc_ref)

    @pl.when(use_bound)
    def _():
        head = lax.broadcasted_iota(jnp.int32, (1, FOX_HEADS), 1) - 2 * pl.program_id(1)
        cq_blk = cq_ref[rows, :]
        cq = [jnp.sum(jnp.where(head == hh, cq_blk, 0.0), axis=1, keepdims=True) - stab_ref[1]
              for hh in range(2)]

        def exponents(k, ck, r0=0, r1=tq):
            return [_dot_nt(q_heads[hh][r0:r1], k) + cq[hh][r0:r1] - ck[hh:hh + 1, :] for hh in range(2)]

        def accumulate(e, v, allowed, r0=0):
            if allowed is not None:
                e = [jnp.where(allowed, x, NEG) for x in e]
            pvs = [_dot(jnp.exp2(x).astype(BF16), v) for x in e]
            r1 = r0 + e[0].shape[0]
            acc_ref[r0:r1, :] += jnp.where(first2, pvs[0], pvs[1])

        cutoff = -stab_ref[2]
        slot = 2 * pl.program_id(1)
        cs_base = (pl.program_id(0) * nq + i) * FOX_HEADS
        ce_base = pl.program_id(0) * nk * FOX_HEADS
        c_first = [cs_ref[cs_base + slot + hh] for hh in range(2)]
        skip_lead = jnp.maximum(c_first[0], c_first[1]) < cutoff
        j0 = jnp.int32(0)
        for j in range(nk - 2):
            gap = jnp.maximum(c_first[0] - ce_ref[ce_base + j * FOX_HEADS + slot],
                              c_first[1] - ce_ref[ce_base + j * FOX_HEADS + slot + 1])
            j0 += jnp.logical_and(gap < cutoff, j < n_full).astype(jnp.int32)

        e0 = exponents(*keys(j0))
        e_ref[0] = e0[0]
        e_ref[1] = e0[1]

        @pl.when(jnp.logical_not(skip_lead))
        def _():
            accumulate(exponents(kl_ref[...], cl_ref[...]), vl_ref[...], lead_ok)

        def step(j, carry):
            e_cur = [e_ref[0], e_ref[1]]
            e_next = exponents(*keys(j + 1))
            accumulate(e_cur, values(j), None)
            e_ref[0] = e_next[0]
            e_ref[1] = e_next[1]
            return carry

        def step_pair(m, carry):
            return step(2 * m + 1, step(2 * m, carry))

        @pl.when(j0 % 2 == 1)
        def _():
            step(j0, 0)

        lax.fori_loop((j0 + 1) // 2, i, step_pair, 0)

        kb, ckb = keys(n_full + 1)
        vb = values(n_full + 1)
        e_b1 = exponents(kb[:tr], ckb[:, :tr], 2 * tr, tq)
        e_b2 = exponents(kb[tr:], ckb[:, tr:], 3 * tr, tq)
        va = values(n_full)
        e_a = [e_ref[0], e_ref[1]]
        strips = [
            [([x[:tr, :tr] for x in e_a], va[:tr], corner)],
            [([x[tr:2 * tr] for x in e_a], va, causal[tr:2 * tr])],
            [([x[2 * tr:3 * tr] for x in e_a], va, None), ([x[:tr] for x in e_b1], vb[:tr], corner)],
            [([x[3 * tr:] for x in e_a], va, None), ([x[tr:] for x in e_b1], vb[:tr], None),
             (e_b2, vb[tr:], corner)],
        ]
        for s, parts in enumerate(strips):
            pv = [None, None]
            for e, v, allowed in parts:
                if allowed is not None:
                    e = [jnp.where(allowed, x, NEG) for x in e]
                for hh in range(2):
                    d = _dot(jnp.exp2(e[hh]).astype(BF16), v)
                    pv[hh] = d if pv[hh] is None else pv[hh] + d
            acc_ref[s * tr:(s + 1) * tr, :] += jnp.where(first2, pv[0], pv[1])

    @pl.when(jnp.logical_not(use_bound))
    def _():
        m_ref[...] = jnp.full_like(m_ref, NEG)

        def block(k, v, ck, allowed):
            pvs, alphas = [], []
            for hh in range(2):
                s = _dot_nt(q_heads[hh], k) - ck[hh:hh + 1, :]
                if allowed is not None:
                    s = jnp.where(allowed, s, NEG)
                m_old = m_ref[hh]
                m_new = jnp.maximum(m_old, jnp.max(s, axis=1, keepdims=True))
                m_ref[hh] = m_new
                pvs.append(_dot(jnp.exp2(s - m_new).astype(BF16), v))
                alphas.append(jnp.exp2(m_old - m_new))
            acc_ref[...] = (acc_ref[...] * jnp.where(first2, alphas[0], alphas[1])
                            + jnp.where(first2, pvs[0], pvs[1]))

        block(kl_ref[...], vl_ref[...], cl_ref[...], lead_ok)

        def full_block(j, carry):
            block(keys(j)[0], values(j), keys(j)[1], None)
            return carry

        lax.fori_loop(0, n_full, full_block, 0)
        block(keys(n_full)[0], values(n_full), keys(n_full)[1], causal)
        block(keys(n_full + 1)[0], values(n_full + 1), keys(n_full + 1)[1], col + tk <= row)

    acc = acc_ref[...]
    o_ref[rows, :] = (acc[:, :V7X_LANES] / acc[:, V7X_LANES:]).astype(o_ref.dtype)


def _fox(stab, cs, ce, q, cq, k, v, ck, kl, vl, cl, weights, short_weight, batch, seq):
    tq = ATT_BLOCK
    w = V7X_LANES
    steps = batch * FOX_PAIRS
    cast_specs, cast_shapes = [], []
    for a in weights:
        slab = a.shape[0] // steps
        assert slab * steps == a.shape[0] and slab % BF16_SUBLANES == 0
        cast_specs.append(pl.BlockSpec((slab, a.shape[1]), lambda b, j: (b * FOX_PAIRS + j, 0)))
        cast_shapes.append(jax.ShapeDtypeStruct(a.shape, BF16))
    short_steps = short_weight.shape[0] // FF_CHUNK
    assert short_steps * FF_CHUNK == short_weight.shape[0] and short_steps <= steps
    cast_specs.append(pl.BlockSpec(
        (FF_CHUNK, short_weight.shape[1]),
        lambda b, j: (jnp.minimum(b * FOX_PAIRS + j, short_steps - 1), 0)))
    cast_shapes.append(jax.ShapeDtypeStruct(short_weight.shape, BF16))
    n_cast = len(cast_specs)
    return pl.pallas_call(
        functools.partial(_fox_body, n_cast=n_cast, n_cast_short=1, short_steps=short_steps),
        grid=(batch, FOX_PAIRS),
        in_specs=[
            pl.BlockSpec(memory_space=pltpu.SMEM),
            pl.BlockSpec(memory_space=pltpu.SMEM),
            pl.BlockSpec(memory_space=pltpu.SMEM),
            pl.BlockSpec((None, seq, w), lambda b, j: (b, 0, j)),
            pl.BlockSpec((None, seq, FOX_HEADS), lambda b, j: (b, 0, 0)),
            pl.BlockSpec((None, seq, w), lambda b, j: (b, 0, j)),
            pl.BlockSpec((None, seq, 2 * w), lambda b, j: (b, 0, j)),
            pl.BlockSpec((None, None, 2, seq), lambda b, j: (b, j, 0, 0)),
            pl.BlockSpec((BLOCK, w), lambda b, j: (0, j)),
            pl.BlockSpec((BLOCK, 2 * w), lambda b, j: (0, j)),
            pl.BlockSpec((None, 2, BLOCK), lambda b, j: (j, 0, 0)),
        ] + cast_specs,
        out_specs=[pl.BlockSpec((None, seq, w), lambda b, j: (b, 0, j))] + cast_specs,
        out_shape=[jax.ShapeDtypeStruct((batch, seq, D_MODEL), BF16)] + cast_shapes,
        scratch_shapes=[pltpu.VMEM((2, tq, 1), F32), pltpu.VMEM((tq, 2 * w), F32),
                        pltpu.VMEM((2, tq, tq // 2), F32)],
        compiler_params=_params(("arbitrary", "arbitrary")),
        name="fox",
    )(stab, cs, ce, q, cq, k, v, ck, kl, vl, cl, *weights, short_weight)


def _ret_body(q_ref, k_ref, v_ref, sg_ref, kl_ref, vl_ref, gn_ref, cd_ref, o_ref, kv_ref, state_ref):
    n_chunks = q_ref.shape[0] // BLOCK
    cd = cd_ref[...]
    gn = gn_ref[...]
    causal = (lax.broadcasted_iota(jnp.int32, (BLOCK, BLOCK), 1)
              <= lax.broadcasted_iota(jnp.int32, (BLOCK, BLOCK), 0))

    def rows(c):
        return pl.ds(pl.multiple_of(c * BLOCK, BLOCK), BLOCK)

    def summarise(c, carry):
        kv_ref[c] = _dot_tn(k_ref[rows(c), :], v_ref[rows(c), :])
        return carry

    lax.fori_loop(0, n_chunks, summarise, 0, unroll=RET_UNROLL)

    def advance(c, state):
        state_ref[c] = state.astype(BF16)
        return cd * (state + kv_ref[c])

    lead_state = cd * _dot_tn(kl_ref[...], vl_ref[...])
    lax.fori_loop(0, n_chunks, advance, lead_state, unroll=RET_UNROLL)

    def emit(c, carry):
        q = q_ref[rows(c), :]
        v = v_ref[rows(c), :]
        scores = jnp.where(causal, _dot_nt(q, k_ref[rows(c), :]), 0.0).astype(BF16)
        o = _dot(jnp.concatenate([scores, q], axis=1), jnp.concatenate([v, state_ref[c]], axis=0))
        mu = jnp.mean(o, axis=-1, keepdims=True)
        d = o - mu
        var = jnp.mean(d * d, axis=-1, keepdims=True)
        yn = d * lax.rsqrt(var + GN_EPS) * gn
        o_ref[rows(c), :] = (sg_ref[rows(c), :].astype(F32) * yn).astype(o_ref.dtype)
        return carry

    lax.fori_loop(0, n_chunks, emit, 0, unroll=RET_UNROLL)


def _ret(rq, rk, rv, sg, rkl, rvl, gn, cd, batch, seq):
    head_qk = pl.BlockSpec((None, seq, RET_DK), lambda b, h: (b, 0, h))
    head_v = pl.BlockSpec((None, seq, RET_DV), lambda b, h: (b, 0, h))
    return pl.pallas_call(
        _ret_body,
        grid=(batch, RET_HEADS),
        in_specs=[head_qk, head_qk, head_v, head_v,
                  pl.BlockSpec((BLOCK, RET_DK), lambda b, h: (0, h)),
                  pl.BlockSpec((BLOCK, RET_DV), lambda b, h: (0, h)),
                  pl.BlockSpec((1, RET_DV), lambda b, h: (0, h)),
                  pl.BlockSpec((None, 1, RET_DV), lambda b, h: (h, 0, 0))],
        out_specs=head_v,
        out_shape=jax.ShapeDtypeStruct((batch, seq, D_MODEL), BF16),
        scratch_shapes=[pltpu.VMEM((seq // BLOCK, RET_DK, RET_DV), F32),
                        pltpu.VMEM((seq // BLOCK, RET_DK, RET_DV), BF16)],
        compiler_params=_params(("arbitrary", "arbitrary")),
        name="ret",
    )(rq, rk, rv, sg, rkl, rvl, gn, cd)


def _out_body(ya_ref, yb_ref, ga_ref, gb_ref, h_ref, wa_hbm, wb_hbm, wo_hbm, g_ref, win_hbm,
              wout_hbm, o_ref, wa_ref, wb_ref, wo_ref, win_ref, wout_ref, sem):
    def square_copies():
        return [pltpu.make_async_copy(src, dst, sem.at[3, n])
                for n, (src, dst) in enumerate(((wa_hbm, wa_ref), (wb_hbm, wb_ref), (wo_hbm, wo_ref)))]

    def chunk_copies(c):
        cols_a = pl.ds(c * FF_CHUNK, FF_CHUNK)
        cols_b = pl.ds(D_FF + c * FF_CHUNK, FF_CHUNK)
        return (pltpu.make_async_copy(win_hbm.at[:, cols_a], win_ref.at[:, cols_a], sem.at[0, c]),
                pltpu.make_async_copy(win_hbm.at[:, cols_b], win_ref.at[:, cols_b], sem.at[1, c]),
                pltpu.make_async_copy(wout_hbm.at[c], wout_ref.at[c], sem.at[2, c]))

    def rows(first_step):
        def arrived(copies):
            if first_step:
                for copy in copies:
                    copy.wait()

        arrived(square_copies()[:2])
        mixed = (ga_ref[...].astype(F32) * _dot(ya_ref[...], wa_ref[...])
                 + gb_ref[...].astype(F32) * _dot(yb_ref[...], wb_ref[...]))
        arrived(square_copies()[2:])
        h = h_ref[...] + _dot(mixed.astype(BF16), wo_ref[...])
        xn = _rms_rows(h, g_ref[...]).astype(BF16)
        before_chunk = (lambda c: arrived(chunk_copies(c))) if first_step else None
        o_ref[...] = h + 0.5 * _swiglu(xn, win_ref, wout_ref, before_chunk)

    @pl.when(pl.program_id(0) == 0)
    def _():
        for copy in square_copies():
            copy.start()
        for c in range(N_FF_CHUNKS):
            for copy in chunk_copies(c):
                copy.start()
        rows(True)

    @pl.when(pl.program_id(0) > 0)
    def _():
        rows(False)


def _out(ya, yb, ga, gb, h, consts, tm):
    wa, wb, wo, g, win, wout = consts
    rows = h.shape[0]
    row_spec = pl.BlockSpec((tm, D_MODEL), lambda i: (i, 0))
    hbm = pl.BlockSpec(memory_space=pl.ANY)
    return pl.pallas_call(
        _out_body,
        grid=(rows // tm,),
        in_specs=[row_spec] * 5 + [hbm, hbm, hbm, _const_spec(g.shape), hbm, hbm],
        out_specs=row_spec,
        out_shape=jax.ShapeDtypeStruct((rows, D_MODEL), F32),
        scratch_shapes=[pltpu.VMEM(a.shape, a.dtype) for a in (wa, wb, wo, win, wout)]
        + [pltpu.SemaphoreType.DMA((4, N_FF_CHUNKS))],
        compiler_params=_params(("arbitrary",)),
        name="out",
    )(ya, yb, ga, gb, h, *consts)


def _ffn_weights(w_in, w_out, dtype):
    return w_in.astype(dtype), w_out.astype(dtype).reshape(N_FF_CHUNKS, FF_CHUNK, D_MODEL)


def _position_tables(seq):
    half = RET_DK // 2
    pos = np.arange(BLOCK + seq, dtype=np.float64) - N_EMPTY
    inv = ROPE_BASE ** (-np.arange(half, dtype=np.float64) / half)
    ang = pos[:, None] * inv[None, :]
    cos = np.concatenate([np.cos(ang), np.cos(ang)], axis=1)
    sin = np.concatenate([-np.sin(ang), np.sin(ang)], axis=1)
    return jnp.asarray(cos, dtype=F32), jnp.asarray(sin, dtype=F32)


def _retention_tables(rows):
    log_gamma = np.log1p(-np.exp2(-5.0 - np.arange(RET_HEADS, dtype=np.float64)))
    n = (np.arange(rows) % BLOCK + 1.0)[None, :, None]
    lg = log_gamma[:, None, None]
    shape = (RET_HEADS, rows, RET_DK)
    q_scale = np.broadcast_to(np.exp(lg * n), shape)
    k_scale = np.broadcast_to(np.exp(-lg * n) * RET_DK ** -0.5, shape)
    cd = np.broadcast_to(np.exp(log_gamma * BLOCK)[:, None, None], (RET_HEADS, 1, RET_DV))
    return tuple(jnp.asarray(t, dtype=F32) for t in (q_scale, k_scale, cd))


def kernel(x, meta_tokens, norm_ffn1, w_ffn1_in, w_ffn1_out, norm_mix, w_in, b_forget, b_gate,
           fox_q_norm, fox_k_norm, w_o_fox, ret_gn, w_o_ret, w_out, norm_ffn2, w_ffn2_in,
           w_ffn2_out):
    batch, seq, d = x.shape
    assert d == D_MODEL and seq % ATT_BLOCK == 0 and seq % ROW_TILE == 0
    assert norm_ffn1.shape[0] == 1, "one layer"
    tiles_per_seq = seq // ROW_TILE

    win1, wout1 = _ffn_weights(w_ffn1_in[0], w_ffn1_out[0], F32)
    grp = np.arange(V7X_MXU_DIM) // FOX_HD
    gmat = jnp.asarray((grp[:, None] == grp[None, :]) / FOX_HD, dtype=BF16)
    gq = jnp.tile(fox_q_norm[0], FOX_HEADS)[None, :]
    gk = jnp.tile(fox_k_norm[0], FOX_HEADS)[None, :]
    bf = jnp.pad(b_forget[0], (0, V7X_LANES - FOX_HEADS))[None, :]
    bg = b_gate[0][None, :]
    cos, sin = _position_tables(seq)
    idx = np.arange(ROW_TILE)
    tri_incl = jnp.asarray(idx[None, :] <= idx[:, None], dtype=BF16)
    idx = np.arange(BLOCK)
    tri_after = jnp.asarray(idx[None, :] > idx[:, None], dtype=BF16)

    lead = jnp.concatenate([jnp.zeros((N_EMPTY, d), x.dtype), meta_tokens.astype(x.dtype)], axis=0)
    xr = x.reshape(batch * seq, d)

    assert w_in.shape[2] == COL_END
    h1, h1l, wt = _ffn_and_cast(xr, lead, norm_ffn1, win1, wout1, w_in[0].T, ROW_TILE)
    q_scale, k_scale, chunk_decay = _retention_tables(ROW_TILE)
    proj_consts = (norm_mix, wt, gmat, gq, gk, bf, bg, q_scale, k_scale)
    q, k, v, c, rq, rk, rv, sg, ga, gb, kl, vl, cl, rkl, rvl = _proj(
        h1, h1l, proj_consts, cos, sin, tri_incl, tri_after, ROW_TILE, tiles_per_seq)

    def b3(a):
        return a.reshape(batch, seq, a.shape[-1])

    ck = b3(c).transpose(0, 2, 1).reshape(batch, FOX_PAIRS, 2, seq)
    clt = cl.T.reshape(FOX_PAIRS, 2, BLOCK)
    bound = FOX_HD ** 0.5 * jnp.max(jnp.abs(fox_q_norm[0])) * jnp.max(jnp.abs(fox_k_norm[0]))
    bound2 = bound * LOG2E
    stab = jnp.stack([(bound <= SAFE_LOGIT_BOUND).astype(F32), bound2, F32_EXP2_ZERO + 2.0 * bound2])
    c3 = b3(c)
    cs = c3[:, ::ATT_BLOCK, :].reshape(-1)
    ce = c3[:, ATT_BLOCK // 2 - 1::ATT_BLOCK // 2, :].reshape(-1)
    ya, win2, wof, wor, wo, wout2 = _fox(
        stab, cs, ce, b3(q), c3, b3(k), b3(v), ck, kl, vl, clt,
        (w_ffn2_in[0], w_o_fox[0], w_o_ret[0], w_out[0]), w_ffn2_out[0], batch, seq)
    yb = _ret(b3(rq), b3(rk), b3(rv), b3(sg), rkl, rvl, ret_gn, chunk_decay, batch, seq)

    out_consts = (wof, wor, wo, norm_ffn2, win2, wout2.reshape(N_FF_CHUNKS, FF_CHUNK, D_MODEL))
    out = _out(ya.reshape(batch * seq, d), yb.reshape(batch * seq, d), ga, gb, h1, out_consts,
               ROW_TILE)
    return out.reshape(batch, seq, d)
```
